```python
import math
import jax, jax.numpy as jnp
from jax import lax
import numpy as np

D_MODEL = 1024
BATCH = 2
SEQ = 8192
DEPTH = 1
DEC_BATCH = 32
DEC_SEQ = 1
PAST_LEN = 16384
PAGE_SIZE = 128

MIX_WIDTH = D_MODEL
HG_WIDTH = MIX_WIDTH // 2
HG_HEAD_DIM = 128
HG_HEADS = HG_WIDTH // HG_HEAD_DIM
ATT_WIDTH = MIX_WIDTH - HG_WIDTH
ATT_HEAD_DIM = 64
ATT_HEADS = ATT_WIDTH // ATT_HEAD_DIM
ROPE_DIM = ATT_HEAD_DIM // 4
ROPE_THETA = 500000.0
DILATED_PATTERNS = ((128, 1), (512, 4), (2048, 16))
MAX_WINDOW = 2048
HG_CHUNK = 64
N_GROUPS = 8
EXPERTS_PER_GROUP = 8
N_EXPERTS = N_GROUPS * EXPERTS_PER_GROUP
TOP_K_INNER = 2
D_FF_EXPERT = 512
MOE_BLOCK = 128
IN_COLS = 4 * HG_WIDTH + 3 * ATT_WIDTH
RMS_EPS = 1e-6

kernel_name = 'hymba_hgrn2_longnet_hmoe_step'


def rmsnorm(x, g):
    xf = x.astype(jnp.float32)
    y = xf * lax.rsqrt(jnp.mean(xf * xf, axis=-1, keepdims=True) + RMS_EPS)
    return (y * g.astype(jnp.float32)).astype(x.dtype)


def rope_partial(x, pos):
    half = ROPE_DIM // 2
    inv_freq = ROPE_THETA ** (-jnp.arange(half, dtype=jnp.float32) / half)
    ang = pos.astype(jnp.float32)[:, None] * inv_freq[None, :]
    cos = jnp.cos(ang)[None, :, None, :]
    sin = jnp.sin(ang)[None, :, None, :]
    xr = x[..., :ROPE_DIM].astype(jnp.float32)
    x1, x2 = xr[..., :half], xr[..., half:]
    rot = jnp.concatenate([x1 * cos - x2 * sin, x2 * cos + x1 * sin], axis=-1)
    return jnp.concatenate([rot.astype(x.dtype), x[..., ROPE_DIM:]], axis=-1)


def mixer_inputs(h, w_in_l, lb_l, pos):
    B, T, _ = h.shape
    z = h @ w_in_l
    cuts = [HG_WIDTH, 2 * HG_WIDTH, 3 * HG_WIDTH, 4 * HG_WIDTH,
            4 * HG_WIDTH + ATT_WIDTH, 4 * HG_WIDTH + 2 * ATT_WIDTH]
    zq, zf, zi, zg, aq, ak, av = jnp.split(z, cuts, axis=-1)
    hg = lambda a: a.reshape(B, T, HG_HEADS, HG_HEAD_DIM)
    at = lambda a: a.reshape(B, T, ATT_HEADS, ATT_HEAD_DIM)
    f = lb_l + (1.0 - lb_l) * jax.nn.sigmoid(zf.astype(jnp.float32))
    hq = hg(jax.nn.silu(zq.astype(jnp.float32)))
    hk = hg(1.0 - f)
    hv = hg(zi.astype(jnp.float32))
    hlogf = hg(jnp.log(f))
    return hq, hk, hv, hlogf, zg, rope_partial(at(aq), pos), rope_partial(at(ak), pos), at(av)


def hgrn2_recurrence(q, k, v, logf, s0):
    B, T, H, K = q.shape
    V = v.shape[-1]
    c = min(HG_CHUNK, T)
    n = -(-T // c)
    pad = n * c - T
    padf = lambda a: jnp.pad(a, ((0, 0), (0, pad), (0, 0), (0, 0)))
    chunks = lambda a: padf(a).reshape(B, n, c, H, a.shape[-1]).swapaxes(0, 1)
    causal = jnp.tril(jnp.ones((c, c), dtype=bool))[None, :, :, None, None]

    def step(s, inp):
        qc, kc, vc, gc = inp
        b = jnp.cumsum(gc, axis=1)
        diff = b[:, :, None] - b[:, None, :]
        decay = jnp.exp(jnp.where(causal, diff, -jnp.inf))
        a = jnp.einsum('btshk,bshk->bhts', qc[:, :, None] * decay, kc)
        o = (jnp.einsum('bhts,bshv->bthv', a, vc)
             + jnp.einsum('bthk,bhkv->bthv', qc * jnp.exp(b), s))
        b_last = b[:, -1]
        s_new = (jnp.exp(b_last)[..., None] * s
                 + jnp.einsum('bshk,bshv->bhkv', kc * jnp.exp(b_last[:, None] - b), vc))
        return s_new, o

    s_T, o = lax.scan(step, s0.astype(jnp.float32),
                      (chunks(q), chunks(k), chunks(v), chunks(logf)))
    o = o.swapaxes(0, 1).reshape(B, n * c, H, V)[:, :T]
    return o, s_T


def dilated_attention_prompt(q, k, v, window, dil):
    B, T, H, E = q.shape
    n = window // dil
    span = n * dil
    nb = -(-T // span)
    pad = nb * span - T

    def blocks(a):
        a = jnp.pad(a.astype(jnp.float32), ((0, 0), (0, pad), (0, 0), (0, 0)))
        return a.reshape(B, nb, n, dil, H, E)

    qb, kb, vb = blocks(q), blocks(k), blocks(v)
    with_prev = lambda a: jnp.concatenate(
        [jnp.pad(a, ((0, 0), (1, 0), (0, 0), (0, 0), (0, 0), (0, 0)))[:, :-1], a], axis=2)
    kk, vv = with_prev(kb), with_prev(vb)
    s = jnp.einsum('bnirhe,bnjrhe->bnrhij', qb, kk) * (E ** -0.5)
    i = jnp.arange(n)[:, None]
    j = jnp.arange(2 * n)[None, :]
    blk = jnp.arange(nb)[:, None, None]
    steps = i + n - j
    valid = (steps >= 0) & (steps <= n) & (blk * n + j - n >= 0)
    s = jnp.where(valid[None, :, None, None], s, -jnp.inf)
    m = jnp.max(s, axis=-1, keepdims=True)
    p = jnp.exp(s - m)
    den = jnp.sum(p, axis=-1)
    o = jnp.einsum('bnrhij,bnjrhe->bnirhe', p, vv) / jnp.transpose(den, (0, 1, 4, 2, 3))[..., None]
    lse = jnp.transpose(m[..., 0] + jnp.log(den), (0, 1, 4, 2, 3))
    return o.reshape(B, nb * span, H, E)[:, :T], lse.reshape(B, nb * span, H)[:, :T]


def dilated_attention_step(q, k_rows, v_rows, window, dil):
    B, S, H, E = q.shape
    L = k_rows.shape[1] - S
    n = window // dil
    idx = L + jnp.arange(S)[:, None] - dil * jnp.arange(n + 1)[None, :]
    valid = idx >= 0
    idx = jnp.maximum(idx, 0)
    kg = k_rows[:, idx].astype(jnp.float32)
    vg = v_rows[:, idx].astype(jnp.float32)
    s = jnp.einsum('bshe,bsjhe->bshj', q.astype(jnp.float32), kg) * (E ** -0.5)
    s = jnp.where(valid[None, :, None, :], s, -jnp.inf)
    m = jnp.max(s, axis=-1, keepdims=True)
    p = jnp.exp(s - m)
    den = jnp.sum(p, axis=-1)
    o = jnp.einsum('bshj,bsjhe->bshe', p, vg) / den[..., None]
    return o, m[..., 0] + jnp.log(den)


def merge_dilations(outs, lses):
    w = jax.nn.softmax(jnp.stack(lses, axis=0), axis=0)
    return jnp.einsum('pbth,pbthe->bthe', w, jnp.stack(outs, axis=0))


def mixer_output(hg_o, zg, att_o, hg_norm_g_l, w_out_l, dtype):
    B, T = zg.shape[:2]
    g = hg_norm_g_l.reshape(HG_HEADS, HG_HEAD_DIM).astype(jnp.float32)
    hn = hg_o * lax.rsqrt(jnp.mean(hg_o * hg_o, axis=-1, keepdims=True) + RMS_EPS) * g
    hn = hn.reshape(B, T, HG_WIDTH) * jax.nn.silu(zg.astype(jnp.float32))
    cat = jnp.concatenate([hn, att_o.reshape(B, T, ATT_WIDTH)], axis=-1).astype(dtype)
    return cat @ w_out_l


def hier_route(h, w_rg, w_re):
    hf = h.astype(jnp.float32)
    lg = hf @ w_rg.astype(jnp.float32)
    pg = jax.nn.softmax(lg, axis=-1)
    _, gi = lax.top_k(lg, 1)
    p_grp = jnp.take_along_axis(pg, gi, axis=-1)[:, 0]
    le_all = jnp.einsum('td,gde->tge', hf, w_re.astype(jnp.float32))
    le = jnp.take_along_axis(le_all, gi[:, :, None], axis=1)[:, 0]
    tv, ti = lax.top_k(jax.nn.softmax(le, axis=-1), TOP_K_INNER)
    gates = p_grp[:, None] * tv / jnp.sum(tv, axis=-1, keepdims=True)
    return gi * EXPERTS_PER_GROUP + ti, gates


def moe_apply(h, experts, gates, w_g, w_u, w_d):
    T, D = h.shape
    A = T * TOP_K_INNER
    blk = max(1, min(MOE_BLOCK, A // N_EXPERTS))
    nblk = (A + N_EXPERTS * (blk - 1)) // blk + 1
    flat_e = experts.reshape(-1)
    order = jnp.argsort(flat_e)
    e_sorted = flat_e[order]
    tok_sorted = order // TOP_K_INNER
    counts = jnp.zeros((N_EXPERTS,), jnp.int32).at[flat_e].add(1)
    padded = (counts + blk - 1) // blk * blk
    pad_end = jnp.cumsum(padded)
    pad_start = pad_end - padded
    start = jnp.cumsum(counts) - counts
    slot = pad_start[e_sorted] + jnp.arange(A, dtype=jnp.int32) - start[e_sorted]
    slot_tok = jnp.zeros((nblk * blk,), jnp.int32).at[slot].set(tok_sorted)
    blk_expert = jnp.minimum(
        jnp.searchsorted(pad_end, jnp.arange(nblk, dtype=jnp.int32) * blk, side='right'),
        N_EXPERTS - 1)
    xb = h[slot_tok].reshape(nblk, blk, D)

    def run(args):
        xs, e = args
        return (jax.nn.silu(xs @ w_g[e]) * (xs @ w_u[e])) @ w_d[e]

    yb = lax.map(run, (xb, blk_expert)).reshape(nblk * blk, D)
    contrib = (yb[slot].astype(jnp.float32) * gates.reshape(-1)[order][:, None]).astype(h.dtype)
    return jnp.zeros_like(h).at[tok_sorted].add(contrib)


def channel_mixer(h, w_rg, w_re, w_g, w_u, w_d):
    B, T, D = h.shape
    hf = h.reshape(B * T, D)
    experts, gates = hier_route(hf, w_rg, w_re)
    return moe_apply(hf, experts, gates, w_g, w_u, w_d).reshape(B, T, D)


def setup_inputs(seed: int = 0) -> dict:
    key = jax.random.key(seed)
    ks = jax.random.split(key, 17)
    f32 = jnp.float32
    win_rows = min(MAX_WINDOW, PAST_LEN)

    def nrm(k, shape, scale):
        return scale * jax.random.normal(k, shape, f32)

    return {
        'x_prompt': nrm(ks[0], (BATCH, SEQ, D_MODEL), 1.0),
        'x_sample': nrm(ks[1], (DEC_BATCH, DEC_SEQ, D_MODEL), 1.0),
        'cache_attn_k': nrm(ks[2], (DEPTH, DEC_BATCH, win_rows, ATT_HEADS, ATT_HEAD_DIM), 1.0),
        'cache_attn_v': nrm(ks[3], (DEPTH, DEC_BATCH, win_rows, ATT_HEADS, ATT_HEAD_DIM), 1.0),
        'state_hgrn': nrm(ks[4], (DEPTH, DEC_BATCH, HG_HEADS, HG_HEAD_DIM, HG_HEAD_DIM), 0.1),
        'w_in': nrm(ks[5], (DEPTH, D_MODEL, IN_COLS), D_MODEL ** -0.5),
        'w_out': nrm(ks[6], (DEPTH, MIX_WIDTH, D_MODEL), MIX_WIDTH ** -0.5),
        'hg_lb_logits': nrm(ks[7], (DEPTH + 1, HG_WIDTH), 0.5),
        'hg_norm_g': 1.0 + nrm(ks[8], (DEPTH, HG_WIDTH), 0.02),
        'norm_mix_g': 1.0 + nrm(ks[9], (DEPTH, D_MODEL), 0.02),
        'norm_ffn_g': 1.0 + nrm(ks[10], (DEPTH, D_MODEL), 0.02),
        'norm_final_g': 1.0 + nrm(ks[11], (D_MODEL,), 0.02),
        'w_route_group': nrm(ks[12], (DEPTH, D_MODEL, N_GROUPS), D_MODEL ** -0.5),
        'w_route_expert': nrm(ks[13], (DEPTH, N_GROUPS, D_MODEL, EXPERTS_PER_GROUP), D_MODEL ** -0.5),
        'w_expert_gate': nrm(ks[14], (DEPTH, N_EXPERTS, D_MODEL, D_FF_EXPERT), D_MODEL ** -0.5),
        'w_expert_up': nrm(ks[15], (DEPTH, N_EXPERTS, D_MODEL, D_FF_EXPERT), D_MODEL ** -0.5),
        'w_expert_down': nrm(ks[16], (DEPTH, N_EXPERTS, D_FF_EXPERT, D_MODEL), D_FF_EXPERT ** -0.5),
    }


def reference(x_prompt, x_sample, cache_attn_k, cache_attn_v, state_hgrn, w_in, w_out,
              hg_lb_logits, hg_norm_g, norm_mix_g, norm_ffn_g, norm_final_g,
              w_route_group, w_route_expert, w_expert_gate, w_expert_up, w_expert_down):
    lb_all = jnp.cumsum(jax.nn.softmax(hg_lb_logits.astype(jnp.float32), axis=0), axis=0)
    Bp, Tp = x_prompt.shape[:2]
    Ts = x_sample.shape[1]
    pos_p = jnp.arange(Tp, dtype=jnp.int32)
    pos_s = PAST_LEN + jnp.arange(Ts, dtype=jnp.int32)
    keep_p = min(MAX_WINDOW, Tp)
    xp, xs = x_prompt, x_sample
    kp, vp, sp, ksl, vsl, ssl = [], [], [], [], [], []
    for l in range(DEPTH):
        h = rmsnorm(xp, norm_mix_g[l])
        hq, hk, hv, hlf, zg, aq, ak, av = mixer_inputs(h, w_in[l], lb_all[l], pos_p)
        s0 = jnp.zeros((Bp, HG_HEADS, HG_HEAD_DIM, HG_HEAD_DIM), jnp.float32)
        hg_o, s_fin = hgrn2_recurrence(hq, hk, hv, hlf, s0)
        res = [dilated_attention_prompt(aq, ak, av, w, d) for w, d in DILATED_PATTERNS]
        att_o = merge_dilations([r[0] for r in res], [r[1] for r in res])
        xp = xp + mixer_output(hg_o, zg, att_o, hg_norm_g[l], w_out[l], xp.dtype)
        xp = xp + channel_mixer(rmsnorm(xp, norm_ffn_g[l]), w_route_group[l], w_route_expert[l],
                                w_expert_gate[l], w_expert_up[l], w_expert_down[l])
        kp.append(ak[:, Tp - keep_p:].astype(cache_attn_k.dtype))
        vp.append(av[:, Tp - keep_p:].astype(cache_attn_v.dtype))
        sp.append(s_fin.astype(state_hgrn.dtype))
        h = rmsnorm(xs, norm_mix_g[l])
        hq, hk, hv, hlf, zg, aq, ak, av = mixer_inputs(h, w_in[l], lb_all[l], pos_s)
        hg_o, s_new = hgrn2_recurrence(hq, hk, hv, hlf, state_hgrn[l])
        k_rows = jnp.concatenate([cache_attn_k[l], ak.astype(cache_attn_k.dtype)], axis=1)
        v_rows = jnp.concatenate([cache_attn_v[l], av.astype(cache_attn_v.dtype)], axis=1)
        res = [dilated_attention_step(aq, k_rows, v_rows, w, d) for w, d in DILATED_PATTERNS]
        att_o = merge_dilations([r[0] for r in res], [r[1] for r in res])
        xs = xs + mixer_output(hg_o, zg, att_o, hg_norm_g[l], w_out[l], xs.dtype)
        xs = xs + channel_mixer(rmsnorm(xs, norm_ffn_g[l]), w_route_group[l], w_route_expert[l],
                                w_expert_gate[l], w_expert_up[l], w_expert_down[l])
        n_rows = k_rows.shape[1]
        keep_s = min(MAX_WINDOW, n_rows)
        ksl.append(k_rows[:, n_rows - keep_s:])
        vsl.append(v_rows[:, n_rows - keep_s:])
        ssl.append(s_new.astype(state_hgrn.dtype))
    y_prompt = rmsnorm(xp, norm_final_g)
    y_sample = rmsnorm(xs, norm_final_g)
    return (y_prompt, y_sample, jnp.stack(kp), jnp.stack(vp), jnp.stack(sp),
            jnp.stack(ksl), jnp.stack(vsl), jnp.stack(ssl))
```

```python
import functools

import jax
import jax.numpy as jnp
from jax import lax
from jax.experimental import pallas as pl
from jax.experimental.pallas import tpu as pltpu

F32 = jnp.float32
BF16 = jnp.bfloat16

D_MODEL = 1024
HG_WIDTH = 512
HG_HEAD_DIM = 128
HG_HEADS = 4
ATT_WIDTH = 512
ATT_HEAD_DIM = 64
ATT_HEADS = 8
ROPE_DIM = 16
ROPE_THETA = 500000.0
DILATED_PATTERNS = ((128, 1), (512, 4), (2048, 16))
MAX_WINDOW = 2048
PAST_LEN = 16384
N_GROUPS = 8
EXPERTS_PER_GROUP = 8
N_EXPERTS = 64
TOP_K_INNER = 2
D_FF_EXPERT = 512
MOE_BLOCK = 128
IN_COLS = 4 * HG_WIDTH + 3 * ATT_WIDTH
RMS_EPS = 1e-6

LANES = 128
VMEM_LIMIT = 56 * 1024 * 1024
NEG = -1e30
HIGHEST = lax.Precision.HIGHEST
NT_DIMS = (((1,), (1,)), ((), ()))


def _sigmoid(z):
    return 1.0 / (1.0 + jnp.exp(-z))


def _cparams(sem):
    return pltpu.CompilerParams(dimension_semantics=sem, vmem_limit_bytes=VMEM_LIMIT)


def _inproj_kernel(x_ref, g_ref, w_ref, lbl_ref, cos_ref, sa_ref, sb_ref,
                   hq_ref, hk_ref, hv_ref, lf_ref, zg_ref, aq_ref, ak_ref, av_ref):
    x = x_ref[...]
    ms = jnp.mean(x * x, axis=-1, keepdims=True)
    h = (x * lax.rsqrt(ms + RMS_EPS) * g_ref[...]).astype(BF16)

    def mm(c0):
        return jnp.dot(h, w_ref[:, c0:c0 + HG_WIDTH], preferred_element_type=F32)

    lbl = lbl_ref[...]
    le = jnp.exp(lbl - jnp.max(lbl, axis=0, keepdims=True))
    lb = le[0:1, :] / jnp.sum(le, axis=0, keepdims=True)

    zq = mm(0)
    hq_ref[...] = zq * _sigmoid(zq)
    zf = mm(HG_WIDTH)
    f = lb + (1.0 - lb) * _sigmoid(zf)
    hk_ref[...] = 1.0 - f
    lf_ref[...] = jnp.log(f)
    hv_ref[...] = mm(2 * HG_WIDTH)
    zg_ref[...] = mm(3 * HG_WIDTH)

    cos, sa, sb = cos_ref[...], sa_ref[...], sb_ref[...]

    def rope(a, out_ref):
        for j in range(ATT_WIDTH // LANES):
            xj = a[:, j * LANES:(j + 1) * LANES]
            up = pltpu.roll(xj, LANES - ROPE_DIM // 2, 1)
            dn = pltpu.roll(xj, ROPE_DIM // 2, 1)
            out_ref[:, j * LANES:(j + 1) * LANES] = xj * cos + up * sa + dn * sb

    rope(mm(4 * HG_WIDTH), aq_ref)
    rope(mm(4 * HG_WIDTH + ATT_WIDTH), ak_ref)
    av_ref[...] = mm(4 * HG_WIDTH + 2 * ATT_WIDTH)


def _rope_tables(pos):
    half = ROPE_DIM // 2
    inv_freq = ROPE_THETA ** (-jnp.arange(half, dtype=F32) / half)
    ang = pos.astype(F32)[:, None] * inv_freq[None, :]
    cos, sin = jnp.cos(ang), jnp.sin(ang)
    m = pos.shape[0]
    rest = ATT_HEAD_DIM - ROPE_DIM
    one, zero, z8 = jnp.ones((m, rest), F32), jnp.zeros((m, rest), F32), jnp.zeros((m, half), F32)
    reps = LANES // ATT_HEAD_DIM
    c = jnp.tile(jnp.concatenate([cos, cos, one], axis=-1), (1, reps))
    sa = jnp.tile(jnp.concatenate([-sin, z8, zero], axis=-1), (1, reps))
    sb = jnp.tile(jnp.concatenate([z8, sin, zero], axis=-1), (1, reps))
    return c, sa, sb


def _inproj(x2d, g, w_bf, lb_logits, pos, tm):
    m = x2d.shape[0]
    cos, sa, sb = _rope_tables(pos)
    row = lambda i: (i, 0)
    const = lambda i: (0, 0)
    outs = [jax.ShapeDtypeStruct((m, HG_WIDTH), F32)] * 8
    return pl.pallas_call(
        _inproj_kernel,
        grid=(m // tm,),
        in_specs=[pl.BlockSpec((tm, D_MODEL), row),
                  pl.BlockSpec((1, D_MODEL), const),
                  pl.BlockSpec((D_MODEL, IN_COLS), const),
                  pl.BlockSpec(lb_logits.shape, const),
                  pl.BlockSpec((tm, LANES), row),
                  pl.BlockSpec((tm, LANES), row),
                  pl.BlockSpec((tm, LANES), row)],
        out_specs=[pl.BlockSpec((tm, HG_WIDTH), row)] * 8,
        out_shape=outs,
        compiler_params=_cparams(("parallel",)),
        name="inproj",
    )(x2d, g.reshape(1, D_MODEL), w_bf, lb_logits, cos, sa, sb)


HG_C = 128
HG_SB = 16


def _hgrn_kernel(q_ref, k_ref, v_ref, lf_ref, zg_ref, g_ref, hn_ref, sfin_ref, st_ref, *, n_chunks):
    t = pl.program_id(2)

    @pl.when(t == 0)
    def _init():
        st_ref[...] = jnp.zeros_like(st_ref)

    ri = lax.broadcasted_iota(jnp.int32, (HG_C, HG_C), 0)
    ci = lax.broadcasted_iota(jnp.int32, (HG_C, HG_C), 1)
    ltri = (ri >= ci).astype(F32)
    ones_b = jnp.ones((LANES, LANES), BF16)
    row_sb = lax.broadcasted_iota(jnp.int32, (HG_SB, LANES), 0)
    col_sb = lax.broadcasted_iota(jnp.int32, (HG_SB, HG_C), 1)
    g = g_ref[...]

    def chunk(c, carry):
        r0 = pl.multiple_of(c * HG_C, HG_C)
        q = q_ref[pl.ds(r0, HG_C), :]
        k = k_ref[pl.ds(r0, HG_C), :]
        v = v_ref[pl.ds(r0, HG_C), :]
        lf = lf_ref[pl.ds(r0, HG_C), :]
        b = jnp.dot(ltri, lf, precision=HIGHEST, preferred_element_type=F32)
        st = st_ref[...]
        vb = v.astype(BF16)
        qb = (q * jnp.exp(b)).astype(BF16)
        o_inter = lax.dot_general(qb, st.astype(BF16), NT_DIMS, preferred_element_type=F32)
        parts = []
        for i in range(HG_C // HG_SB):
            lo = i * HG_SB
            b_i, q_i, k_i, v_i = b[lo:lo + HG_SB], q[lo:lo + HG_SB], k[lo:lo + HG_SB], v[lo:lo + HG_SB]
            ps = []
            for s in range(HG_SB):
                d = jnp.where(row_sb >= s, b_i - b_i[s:s + 1, :], NEG)
                ps.append(q_i * jnp.exp(d) * k_i[s:s + 1, :])
            p_all = jnp.concatenate(ps, axis=0).astype(BF16)
            r_all = jnp.dot(p_all, ones_b, preferred_element_type=F32)
            o_i = o_inter[lo:lo + HG_SB]
            for s in range(HG_SB):
                o_i = o_i + r_all[s * HG_SB:(s + 1) * HG_SB] * v_i[s:s + 1, :]
            if i > 0:
                b_ref = b[lo - 1:lo, :]
                qs = (q_i * jnp.exp(b_i - b_ref)).astype(BF16)
                ks = (k * jnp.exp(jnp.minimum(b_ref - b, 0.0))).astype(BF16)
                a = lax.dot_general(qs, ks, NT_DIMS, preferred_element_type=F32)
                a = jnp.where(col_sb < lo, a, 0.0).astype(BF16)
                o_i = o_i + jnp.dot(a, vb, preferred_element_type=F32)
            parts.append(o_i)
        o = jnp.concatenate(parts, axis=0)
        b_last = b[HG_C - 1:HG_C, :]
        kdec = (k * jnp.exp(b_last - b)).astype(BF16)
        st_ref[...] = st * jnp.exp(b_last) + jnp.dot(v.T.astype(BF16), kdec, preferred_element_type=F32)
        ms = jnp.mean(o * o, axis=-1, keepdims=True)
        zg = zg_ref[pl.ds(r0, HG_C), :]
        hn_ref[pl.ds(r0, HG_C), :] = o * lax.rsqrt(ms + RMS_EPS) * g * (zg * _sigmoid(zg))
        return carry

    lax.fori_loop(0, n_chunks, chunk, 0)

    @pl.when(t == pl.num_programs(2) - 1)
    def _fin():
        sfin_ref[...] = st_ref[...].T


def _hgrn_prompt(hq, hk, hv, lf, zg, g_hg, tb=1024):
    bsz, t, _ = hq.shape
    seq = pl.BlockSpec((None, tb, HG_HEAD_DIM), lambda b, h, i: (b, i, h))
    return pl.pallas_call(
        functools.partial(_hgrn_kernel, n_chunks=tb // HG_C),
        grid=(bsz, HG_HEADS, t // tb),
        in_specs=[seq, seq, seq, seq, seq,
                  pl.BlockSpec((1, HG_HEAD_DIM), lambda b, h, i: (0, h))],
        out_specs=[seq,
                   pl.BlockSpec((None, None, HG_HEAD_DIM, HG_HEAD_DIM), lambda b, h, i: (b, h, 0, 0))],
        out_shape=[jax.ShapeDtypeStruct((bsz, t, HG_WIDTH), F32),
                   jax.ShapeDtypeStruct((bsz, HG_HEADS, HG_HEAD_DIM, HG_HEAD_DIM), F32)],
        scratch_shapes=[pltpu.VMEM((HG_HEAD_DIM, HG_HEAD_DIM), F32)],
        compiler_params=_cparams(("parallel", "parallel", "arbitrary")),
        name="hgrn_prompt",
    )(hq, hk, hv, lf, zg, g_hg.reshape(1, HG_WIDTH))


def _hgrn_step_kernel(q_ref, k_ref, v_ref, lf_ref, zg_ref, g_ref, s_ref, hn_ref, snew_ref):
    row = slice(None)
    zeros = jnp.zeros((HG_HEAD_DIM - 3, HG_HEAD_DIM), F32)
    for h in range(HG_HEADS):
        cs = slice(h * HG_HEAD_DIM, (h + 1) * HG_HEAD_DIM)
        q, k, v = q_ref[row, cs], k_ref[row, cs], v_ref[row, cs]
        f = jnp.exp(lf_ref[row, cs])
        cols = jnp.concatenate([f, k, q, zeros], axis=0).T
        s_new = cols[:, 0:1] * s_ref[h] + cols[:, 1:2] * v
        snew_ref[h] = s_new
        o = jnp.sum(cols[:, 2:3] * s_new, axis=0, keepdims=True)
        ms = jnp.mean(o * o, axis=-1, keepdims=True)
        zg = zg_ref[row, cs]
        hn_ref[row, cs] = o * lax.rsqrt(ms + RMS_EPS) * g_ref[:, cs] * (zg * _sigmoid(zg))


def _hgrn_step(hq, hk, hv, lf, zg, g_hg, state):
    bsz = hq.shape[0]
    one = pl.BlockSpec((None, 1, HG_WIDTH), lambda b: (b, 0, 0))
    st = pl.BlockSpec((None, HG_HEADS, HG_HEAD_DIM, HG_HEAD_DIM), lambda b: (b, 0, 0, 0))
    r3 = lambda a: a.reshape(bsz, 1, HG_WIDTH)
    hn, s_new = pl.pallas_call(
        _hgrn_step_kernel,
        grid=(bsz,),
        in_specs=[one, one, one, one, one, pl.BlockSpec((1, HG_WIDTH), lambda b: (0, 0)), st],
        out_specs=[one, st],
        out_shape=[jax.ShapeDtypeStruct((bsz, 1, HG_WIDTH), F32),
                   jax.ShapeDtypeStruct(state.shape, F32)],
        compiler_params=_cparams(("parallel",)),
        name="hgrn_step",
    )(r3(hq), r3(hk), r3(hv), r3(lf), r3(zg), g_hg.reshape(1, HG_WIDTH), state)
    return hn.reshape(bsz, HG_WIDTH), s_new


ATT_N = 128
ATT_SUPER = 2048


def _attn_prompt_kernel(q_ref, k_ref, v_ref, o_ref, osc, lsc, *, seq_len):
    lane = lax.broadcasted_iota(jnp.int32, (ATT_N, LANES), 1)
    rowi = lax.broadcasted_iota(jnp.int32, (ATT_N, LANES), 0)
    head0 = lane < ATT_HEAD_DIM
    m_cur = lane <= rowi
    m_prev = lane >= rowi
    ones_b = jnp.ones((LANES, LANES), BF16)
    scale = ATT_HEAD_DIM ** -0.5

    def do_block(p, d, qs, off):
        span = ATT_N * d
        q = q_ref[pl.ds(qs, ATT_N, stride=d), :] * scale
        kc = k_ref[pl.ds(qs, ATT_N, stride=d), :].astype(BF16)
        vc = v_ref[pl.ds(qs, ATT_N, stride=d), :].astype(BF16)
        prev = jnp.maximum(qs - span, 0)
        pen = jnp.where(qs >= span, 0.0, NEG)
        kp = k_ref[pl.ds(prev, ATT_N, stride=d), :].astype(BF16)
        vp = v_ref[pl.ds(prev, ATT_N, stride=d), :].astype(BF16)
        res = []
        for hm in (head0, jnp.logical_not(head0)):
            qh = jnp.where(hm, q, 0.0).astype(BF16)
            sc = lax.dot_general(qh, kc, NT_DIMS, preferred_element_type=F32)
            sp = lax.dot_general(qh, kp, NT_DIMS, preferred_element_type=F32)
            sc = jnp.where(m_cur, sc, NEG)
            sp = jnp.where(m_prev, sp, NEG) + pen
            m = jnp.max(jnp.maximum(sc, sp), axis=-1, keepdims=True)
            pc = jnp.exp(sc - m).astype(BF16)
            pp = jnp.exp(sp - m).astype(BF16)
            den = (jnp.dot(pc, ones_b, preferred_element_type=F32)
                   + jnp.dot(pp, ones_b, preferred_element_type=F32))
            o = (jnp.dot(pc, vc, preferred_element_type=F32)
                 + jnp.dot(pp, vp, preferred_element_type=F32))
            res.append((o / den, m + jnp.log(den)))
        osc[p, pl.ds(off, ATT_N, stride=d), :] = jnp.where(head0, res[0][0], res[1][0])
        lsc[p, pl.ds(off, ATT_N, stride=d), :] = jnp.where(head0, res[0][1], res[1][1])

    def superblock(sb, carry):
        base = sb * ATT_SUPER

        def blocks(i, c2):
            for p, (w, d) in enumerate(DILATED_PATTERNS):
                span = ATT_N * d
                off = (i // d) * span + (i % d)
                do_block(p, d, base + off, off)
            return c2

        lax.fori_loop(0, ATT_SUPER // ATT_N, blocks, 0)

        piece = 256

        def merge(j, c2):
            r = pl.ds(pl.multiple_of(j * piece, piece), piece)
            ls = [lsc[p, r, :] for p in range(len(DILATED_PATTERNS))]
            mx = jnp.maximum(jnp.maximum(ls[0], ls[1]), ls[2])
            ws = [jnp.exp(l - mx) for l in ls]
            num = ws[0] * osc[0, r, :] + ws[1] * osc[1, r, :] + ws[2] * osc[2, r, :]
            o_ref[pl.ds(pl.multiple_of(base + j * piece, piece), piece), :] = num / (ws[0] + ws[1] + ws[2])
            return c2

        lax.fori_loop(0, ATT_SUPER // piece, merge, 0)
        return carry

    lax.fori_loop(0, seq_len // ATT_SUPER, superblock, 0)


def _attn_prompt(aq, ak, av):
    bsz, t, _ = aq.shape
    spec = pl.BlockSpec((None, t, LANES), lambda b, p: (b, 0, p))
    n_pat = len(DILATED_PATTERNS)
    return pl.pallas_call(
        functools.partial(_attn_prompt_kernel, seq_len=t),
        grid=(bsz, ATT_WIDTH // LANES),
        in_specs=[spec, spec, spec],
        out_specs=spec,
        out_shape=jax.ShapeDtypeStruct((bsz, t, ATT_WIDTH), F32),
        scratch_shapes=[pltpu.VMEM((n_pat, ATT_SUPER, LANES), F32),
                        pltpu.VMEM((n_pat, ATT_SUPER, LANES), F32)],
        compiler_params=_cparams(("parallel", "parallel")),
        name="attn_prompt",
    )(aq, ak, av)


def _attn_step_kernel(q_ref, kn_ref, vn_ref, ck_ref, cv_ref, o_ref, nk_ref, nv_ref):
    win = ck_ref.shape[0]
    kc, vc = ck_ref[...], cv_ref[...]
    q, kn, vn = q_ref[...], kn_ref[...], vn_ref[...]

    hrow = lax.broadcasted_iota(jnp.int32, (ATT_HEADS, ATT_WIDTH), 0)
    hcol = lax.broadcasted_iota(jnp.int32, (ATT_HEADS, ATT_WIDTH), 1) // ATT_HEAD_DIM
    own = hrow == hcol
    qm = jnp.where(own, q, 0.0) * (ATT_HEAD_DIM ** -0.5)
    s_all = lax.dot_general(qm.astype(BF16), kc.astype(BF16), NT_DIMS, preferred_element_type=F32)
    s_new = jnp.sum(qm * kn, axis=-1, keepdims=True)
    dist = win - lax.broadcasted_iota(jnp.int32, (ATT_HEADS, win), 1)

    ps, pnews, lses = [], [], []
    for w, d in DILATED_PATTERNS:
        on_stride = (dist & (d - 1)) == 0 if d & (d - 1) == 0 else dist % d == 0
        valid = jnp.logical_and(dist <= w, on_stride)
        sm = jnp.where(valid, s_all, NEG)
        m = jnp.maximum(jnp.max(sm, axis=-1, keepdims=True), s_new)
        p = jnp.exp(sm - m)
        pn = jnp.exp(s_new - m)
        den = jnp.sum(p, axis=-1, keepdims=True) + pn
        ps.append(p / den)
        pnews.append(pn / den)
        lses.append(m + jnp.log(den))
    mx = jnp.maximum(jnp.maximum(lses[0], lses[1]), lses[2])
    ws = [jnp.exp(l - mx) for l in lses]
    wsum = ws[0] + ws[1] + ws[2]
    p_tot = (ws[0] * ps[0] + ws[1] * ps[1] + ws[2] * ps[2]) / wsum
    pn_tot = (ws[0] * pnews[0] + ws[1] * pnews[1] + ws[2] * pnews[2]) / wsum
    o = jnp.dot(p_tot.astype(BF16), vc.astype(BF16), preferred_element_type=F32) + pn_tot * vn
    o_ref[...] = jnp.sum(jnp.where(own, o, 0.0), axis=0, keepdims=True)

    last = lax.broadcasted_iota(jnp.int32, (win, ATT_WIDTH), 0) == win - 1
    nk_ref[...] = jnp.where(last, kn, pltpu.roll(kc, win - 1, 0))
    nv_ref[...] = jnp.where(last, vn, pltpu.roll(vc, win - 1, 0))


def _attn_step(aq, ak, av, cache_k, cache_v):
    bsz, win, _ = cache_k.shape
    one = pl.BlockSpec((None, 1, ATT_WIDTH), lambda b: (b, 0, 0))
    cache = pl.BlockSpec((None, win, ATT_WIDTH), lambda b: (b, 0, 0))
    r3 = lambda a: a.reshape(bsz, 1, ATT_WIDTH)
    att, new_k, new_v = pl.pallas_call(
        _attn_step_kernel,
        grid=(bsz,),
        in_specs=[one, one, one, cache, cache],
        out_specs=[one, cache, cache],
        out_shape=[jax.ShapeDtypeStruct((bsz, 1, ATT_WIDTH), F32),
                   jax.ShapeDtypeStruct(cache_k.shape, F32),
                   jax.ShapeDtypeStruct(cache_v.shape, F32)],
        compiler_params=_cparams(("parallel",)),
        name="attn_step",
    )(r3(aq), r3(ak), r3(av), cache_k, cache_v)
    return att.reshape(bsz, ATT_WIDTH), new_k, new_v


def _outproj_kernel(hn_ref, att_ref, x_ref, w_ref, g_ref, wr_ref, xmid_ref, h2_ref, lg_ref):
    y = (jnp.dot(hn_ref[...].astype(BF16), w_ref[0:HG_WIDTH, :], preferred_element_type=F32)
         + jnp.dot(att_ref[...].astype(BF16), w_ref[HG_WIDTH:, :], preferred_element_type=F32))
    xm = x_ref[...] + y
    xmid_ref[...] = xm
    ms = jnp.mean(xm * xm, axis=-1, keepdims=True)
    h2 = xm * lax.rsqrt(ms + RMS_EPS) * g_ref[...]
    h2_ref[...] = h2.astype(BF16)
    lg_ref[...] = jnp.dot(h2, wr_ref[...], precision=HIGHEST, preferred_element_type=F32)


def _outproj(hn, att, x2d, w_out_bf, g_ffn, w_router, tm):
    m = x2d.shape[0]
    row = lambda i: (i, 0)
    const = lambda i: (0, 0)
    return pl.pallas_call(
        _outproj_kernel,
        grid=(m // tm,),
        in_specs=[pl.BlockSpec((tm, HG_WIDTH), row),
                  pl.BlockSpec((tm, ATT_WIDTH), row),
                  pl.BlockSpec((tm, D_MODEL), row),
                  pl.BlockSpec((D_MODEL, D_MODEL), const),
                  pl.BlockSpec((1, D_MODEL), const),
                  pl.BlockSpec((D_MODEL, LANES), const)],
        out_specs=[pl.BlockSpec((tm, D_MODEL), row),
                   pl.BlockSpec((tm, D_MODEL), row),
                   pl.BlockSpec((tm, LANES), row)],
        out_shape=[jax.ShapeDtypeStruct((m, D_MODEL), F32),
                   jax.ShapeDtypeStruct((m, D_MODEL), BF16),
                   jax.ShapeDtypeStruct((m, LANES), F32)],
        compiler_params=_cparams(("parallel",)),
        name="outproj",
    )(hn, att, x2d, w_out_bf, g_ffn.reshape(1, D_MODEL), w_router)


def _expert_kernel(be_ref, nu_ref, x_ref, gate_ref, wg_ref, wu_ref, wd_ref, y_ref, wgb, wub, wdb):
    i = pl.program_id(0)
    e = be_ref[i]
    e_prev = be_ref[jnp.maximum(i - 1, 0)]

    @pl.when(jnp.logical_or(i == 0, e != e_prev))
    def _cast():
        wgb[...] = wg_ref[...].astype(BF16)
        wub[...] = wu_ref[...].astype(BF16)
        wdb[...] = wd_ref[...].astype(BF16)

    @pl.when(i < nu_ref[0])
    def _run():
        x = x_ref[...]
        a = jnp.dot(x, wgb[...], preferred_element_type=F32)
        u = jnp.dot(x, wub[...], preferred_element_type=F32)
        mid = (a * _sigmoid(a) * u).astype(BF16)
        y_ref[...] = jnp.dot(mid, wdb[...], preferred_element_type=F32) * gate_ref[...]

    @pl.when(i >= nu_ref[0])
    def _skip():
        y_ref[...] = jnp.zeros_like(y_ref)


def _experts(xb, slot_gate, blk_expert, n_used, w_g, w_u, w_d, blk):
    nblk = blk_expert.shape[0]
    grid_spec = pltpu.PrefetchScalarGridSpec(
        num_scalar_prefetch=2,
        grid=(nblk,),
        in_specs=[pl.BlockSpec((blk, D_MODEL), lambda i, be, nu: (i, 0)),
                  pl.BlockSpec((blk, 1), lambda i, be, nu: (i, 0)),
                  pl.BlockSpec((None, D_MODEL, D_FF_EXPERT), lambda i, be, nu: (be[i], 0, 0)),
                  pl.BlockSpec((None, D_MODEL, D_FF_EXPERT), lambda i, be, nu: (be[i], 0, 0)),
                  pl.BlockSpec((None, D_FF_EXPERT, D_MODEL), lambda i, be, nu: (be[i], 0, 0))],
        out_specs=pl.BlockSpec((blk, D_MODEL), lambda i, be, nu: (i, 0)),
        scratch_shapes=[pltpu.VMEM((D_MODEL, D_FF_EXPERT), BF16),
                        pltpu.VMEM((D_MODEL, D_FF_EXPERT), BF16),
                        pltpu.VMEM((D_FF_EXPERT, D_MODEL), BF16)],
    )
    return pl.pallas_call(
        _expert_kernel,
        grid_spec=grid_spec,
        out_shape=jax.ShapeDtypeStruct((nblk * blk, D_MODEL), F32),
        compiler_params=_cparams(("arbitrary",)),
        name="experts",
    )(blk_expert, n_used, xb, slot_gate, w_g, w_u, w_d)


def _route(logits):
    t = logits.shape[0]
    lg = logits[:, :N_GROUPS]
    gi = jnp.argmax(lg, axis=-1)
    p_grp = 1.0 / jnp.sum(jnp.exp(lg - jnp.max(lg, axis=-1, keepdims=True)), axis=-1)
    le_all = logits[:, N_GROUPS:N_GROUPS + N_EXPERTS].reshape(t, N_GROUPS, EXPERTS_PER_GROUP)
    le = jnp.take_along_axis(le_all, gi[:, None, None], axis=1)[:, 0]
    tv, ti = lax.top_k(jax.nn.softmax(le, axis=-1), TOP_K_INNER)
    gates = p_grp[:, None] * tv / jnp.sum(tv, axis=-1, keepdims=True)
    return (gi[:, None] * EXPERTS_PER_GROUP + ti).astype(jnp.int32), gates


def _dispatch(experts, gates, blk):
    t = experts.shape[0]
    a = t * TOP_K_INNER
    nblk = (a + N_EXPERTS * (blk - 1)) // blk + 1
    flat_e = experts.reshape(-1)
    order = jnp.argsort(flat_e).astype(jnp.int32)
    e_sorted = flat_e[order]
    counts = jnp.zeros((N_EXPERTS,), jnp.int32).at[flat_e].add(1)
    padded = (counts + blk - 1) // blk * blk
    pad_end = jnp.cumsum(padded)
    pad_start = pad_end - padded
    start = jnp.cumsum(counts) - counts
    slot = pad_start[e_sorted] + jnp.arange(a, dtype=jnp.int32) - start[e_sorted]
    slot_tok = jnp.zeros((nblk * blk,), jnp.int32).at[slot].set(order // TOP_K_INNER)
    slot_gate = jnp.zeros((nblk * blk,), F32).at[slot].set(gates.reshape(-1)[order])
    blk_expert = jnp.minimum(
        jnp.searchsorted(pad_end, jnp.arange(nblk, dtype=jnp.int32) * blk, side='right'),
        N_EXPERTS - 1).astype(jnp.int32)
    n_used = (pad_end[-1:] // blk).astype(jnp.int32)
    slot_of = jnp.zeros((a,), jnp.int32).at[order].set(slot).reshape(t, TOP_K_INNER)
    return slot_tok, slot_gate.reshape(-1, 1), blk_expert, n_used, slot_of


def _final_kernel(x_ref, y0_ref, y1_ref, g_ref, o_ref):
    x = x_ref[...] + (y0_ref[...] + y1_ref[...])
    ms = jnp.mean(x * x, axis=-1, keepdims=True)
    o_ref[...] = x * lax.rsqrt(ms + RMS_EPS) * g_ref[...]


def _final(xmid, y0, y1, g_final, tm):
    m = xmid.shape[0]
    row = pl.BlockSpec((tm, D_MODEL), lambda i: (i, 0))
    return pl.pallas_call(
        _final_kernel,
        grid=(m // tm,),
        in_specs=[row, row, row, pl.BlockSpec((1, D_MODEL), lambda i: (0, 0))],
        out_specs=row,
        out_shape=jax.ShapeDtypeStruct((m, D_MODEL), F32),
        compiler_params=_cparams(("parallel",)),
        name="final",
    )(xmid, y0, y1, g_final.reshape(1, D_MODEL))


def _ffn_and_final(xmid, h2, logits, w_g, w_u, w_d, g_final, blk, tm):
    experts, gates = _route(logits)
    slot_tok, slot_gate, blk_expert, n_used, slot_of = _dispatch(experts, gates, blk)
    yb = _experts(h2[slot_tok], slot_gate, blk_expert, n_used, w_g, w_u, w_d, blk)
    return _final(xmid, yb[slot_of[:, 0]], yb[slot_of[:, 1]], g_final, tm)


def kernel(x_prompt, x_sample, cache_attn_k, cache_attn_v, state_hgrn, w_in, w_out, hg_lb_logits,
           hg_norm_g, norm_mix_g, norm_ffn_g, norm_final_g, w_route_group, w_route_expert,
           w_expert_gate, w_expert_up, w_expert_down):
    bp, tp, _ = x_prompt.shape
    bs = x_sample.shape[0]
    l = 0
    w_in_bf = w_in[l].astype(BF16)
    w_out_bf = w_out[l].astype(BF16)
    w_router = jnp.concatenate(
        [w_route_group[l],
         jnp.transpose(w_route_expert[l], (1, 0, 2)).reshape(D_MODEL, N_EXPERTS),
         jnp.zeros((D_MODEL, LANES - N_GROUPS - N_EXPERTS), F32)], axis=-1)
    w_g, w_u, w_d = w_expert_gate[l], w_expert_up[l], w_expert_down[l]

    xp = x_prompt.reshape(bp * tp, D_MODEL)
    pos_p = jnp.tile(jnp.arange(tp, dtype=jnp.int32), bp)
    hq, hk, hv, lf, zg, aq, ak, av = _inproj(xp, norm_mix_g[l], w_in_bf, hg_lb_logits, pos_p, 256)
    seq3 = lambda a: a.reshape(bp, tp, HG_WIDTH)
    hn, s_fin = _hgrn_prompt(seq3(hq), seq3(hk), seq3(hv), seq3(lf), seq3(zg), hg_norm_g[l])
    att = _attn_prompt(seq3(aq), seq3(ak), seq3(av))
    xmid, h2, logits = _outproj(hn.reshape(bp * tp, HG_WIDTH), att.reshape(bp * tp, ATT_WIDTH), xp,
                                w_out_bf, norm_ffn_g[l], w_router, 256)
    y_prompt = _ffn_and_final(xmid, h2, logits, w_g, w_u, w_d, norm_final_g, MOE_BLOCK, 256)
    keep = min(MAX_WINDOW, tp)
    heads = lambda a: a.reshape(1, bp, keep, ATT_HEADS, ATT_HEAD_DIM)
    new_k_p = heads(seq3(ak)[:, tp - keep:])
    new_v_p = heads(seq3(av)[:, tp - keep:])

    xs = x_sample.reshape(bs, D_MODEL)
    pos_s = jnp.full((bs,), PAST_LEN, jnp.int32)
    hq, hk, hv, lf, zg, aq, ak, av = _inproj(xs, norm_mix_g[l], w_in_bf, hg_lb_logits, pos_s, bs)
    hn, s_new = _hgrn_step(hq, hk, hv, lf, zg, hg_norm_g[l], state_hgrn[l])
    win = cache_attn_k.shape[2]
    att, new_k_s, new_v_s = _attn_step(aq, ak, av,
                                       cache_attn_k[l].reshape(bs, win, ATT_WIDTH),
                                       cache_attn_v[l].reshape(bs, win, ATT_WIDTH))
    xmid, h2, logits = _outproj(hn, att, xs, w_out_bf, norm_ffn_g[l], w_router, bs)
    y_sample = _ffn_and_final(xmid, h2, logits, w_g, w_u, w_d, norm_final_g, 16, bs)
    cache5 = lambda a: a.reshape(1, bs, win, ATT_HEADS, ATT_HEAD_DIM)

    return (y_prompt.reshape(bp, tp, D_MODEL), y_sample.reshape(bs, 1, D_MODEL),
            new_k_p, new_v_p, s_fin[None], cache5(new_k_s), cache5(new_v_s), s_new[None])
```

```python
import functools

import jax
import jax.numpy as jnp
from jax import lax
from jax.experimental import pallas as pl
from jax.experimental.pallas import tpu as pltpu

F32 = jnp.float32
BF16 = jnp.bfloat16

D_MODEL = 1024
HG_WIDTH = 512
HG_HEAD_DIM = 128
HG_HEADS = 4
ATT_WIDTH = 512
ATT_HEAD_DIM = 64
ATT_HEADS = 8
ROPE_DIM = 16
ROPE_THETA = 500000.0
DILATED_PATTERNS = ((128, 1), (512, 4), (2048, 16))
MAX_WINDOW = 2048
PAST_LEN = 16384
N_GROUPS = 8
EXPERTS_PER_GROUP = 8
N_EXPERTS = 64
TOP_K_INNER = 2
D_FF_EXPERT = 512
MOE_BLOCK = 128
IN_COLS = 4 * HG_WIDTH + 3 * ATT_WIDTH
RMS_EPS = 1e-6

LANES = 128
VMEM_LIMIT = 56 * 1024 * 1024
NEG = -1e30
HIGHEST = lax.Precision.HIGHEST
NT_DIMS = (((1,), (1,)), ((), ()))


def _sigmoid(z):
    return 1.0 / (1.0 + jnp.exp(-z))


def _cparams(sem):
    return pltpu.CompilerParams(dimension_semantics=sem, vmem_limit_bytes=VMEM_LIMIT)


def _inproj_kernel(x_ref, g_ref, w_ref, lbl_ref, cos_ref, sa_ref, sb_ref,
                   hq_ref, hk_ref, hv_ref, lf_ref, zg_ref, aq_ref, ak_ref, av_ref):
    x = x_ref[...]
    ms = jnp.mean(x * x, axis=-1, keepdims=True)
    h = (x * lax.rsqrt(ms + RMS_EPS) * g_ref[...]).astype(BF16)

    def mm(c0):
        return jnp.dot(h, w_ref[:, c0:c0 + HG_WIDTH], preferred_element_type=F32)

    lbl = lbl_ref[...]
    le = jnp.exp(lbl - jnp.max(lbl, axis=0, keepdims=True))
    lb = le[0:1, :] / jnp.sum(le, axis=0, keepdims=True)

    zq = mm(0)
    hq_ref[...] = zq * _sigmoid(zq)
    zf = mm(HG_WIDTH)
    f = lb + (1.0 - lb) * _sigmoid(zf)
    hk_ref[...] = 1.0 - f
    lf_ref[...] = jnp.log(f)
    hv_ref[...] = mm(2 * HG_WIDTH)
    zg_ref[...] = mm(3 * HG_WIDTH)

    cos, sa, sb = cos_ref[...], sa_ref[...], sb_ref[...]

    def rope(a, out_ref):
        for j in range(ATT_WIDTH // LANES):
            xj = a[:, j * LANES:(j + 1) * LANES]
            up = pltpu.roll(xj, LANES - ROPE_DIM // 2, 1)
            dn = pltpu.roll(xj, ROPE_DIM // 2, 1)
            out_ref[:, j * LANES:(j + 1) * LANES] = xj * cos + up * sa + dn * sb

    rope(mm(4 * HG_WIDTH), aq_ref)
    rope(mm(4 * HG_WIDTH + ATT_WIDTH), ak_ref)
    av_ref[...] = mm(4 * HG_WIDTH + 2 * ATT_WIDTH)


def _rope_tables(pos):
    half = ROPE_DIM // 2
    inv_freq = ROPE_THETA ** (-jnp.arange(half, dtype=F32) / half)
    ang = pos.astype(F32)[:, None] * inv_freq[None, :]
    cos, sin = jnp.cos(ang), jnp.sin(ang)
    m = pos.shape[0]
    rest = ATT_HEAD_DIM - ROPE_DIM
    one, zero, z8 = jnp.ones((m, rest), F32), jnp.zeros((m, rest), F32), jnp.zeros((m, half), F32)
    reps = LANES // ATT_HEAD_DIM
    c = jnp.tile(jnp.concatenate([cos, cos, one], axis=-1), (1, reps))
    sa = jnp.tile(jnp.concatenate([-sin, z8, zero], axis=-1), (1, reps))
    sb = jnp.tile(jnp.concatenate([z8, sin, zero], axis=-1), (1, reps))
    return c, sa, sb


def _inproj(x2d, g, w_bf, lb_logits, pos, tm):
    m = x2d.shape[0]
    cos, sa, sb = _rope_tables(pos)
    row = lambda i: (i, 0)
    const = lambda i: (0, 0)
    outs = [jax.ShapeDtypeStruct((m, HG_WIDTH), F32)] * 8
    return pl.pallas_call(
        _inproj_kernel,
        grid=(m // tm,),
        in_specs=[pl.BlockSpec((tm, D_MODEL), row),
                  pl.BlockSpec((1, D_MODEL), const),
                  pl.BlockSpec((D_MODEL, IN_COLS), const),
                  pl.BlockSpec(lb_logits.shape, const),
                  pl.BlockSpec((tm, LANES), row),
                  pl.BlockSpec((tm, LANES), row),
                  pl.BlockSpec((tm, LANES), row)],
        out_specs=[pl.BlockSpec((tm, HG_WIDTH), row)] * 8,
        out_shape=outs,
        compiler_params=_cparams(("parallel",)),
        name="inproj",
    )(x2d, g.reshape(1, D_MODEL), w_bf, lb_logits, cos, sa, sb)


HG_C = 128
HG_SB = 16


def _hgrn_kernel(q_ref, k_ref, v_ref, lf_ref, zg_ref, g_ref, hn_ref, sfin_ref, st_ref, *, n_chunks):
    t = pl.program_id(2)

    @pl.when(t == 0)
    def _init():
        st_ref[...] = jnp.zeros_like(st_ref)

    ri = lax.broadcasted_iota(jnp.int32, (HG_C, HG_C), 0)
    ci = lax.broadcasted_iota(jnp.int32, (HG_C, HG_C), 1)
    ltri = (ri >= ci).astype(F32)
    ones_b = jnp.ones((LANES, LANES), BF16)
    row_sb = lax.broadcasted_iota(jnp.int32, (HG_SB, LANES), 0)
    col_sb = lax.broadcasted_iota(jnp.int32, (HG_SB, HG_C), 1)
    g = g_ref[...]

    def chunk(c, carry):
        r0 = pl.multiple_of(c * HG_C, HG_C)
        q = q_ref[pl.ds(r0, HG_C), :]
        k = k_ref[pl.ds(r0, HG_C), :]
        v = v_ref[pl.ds(r0, HG_C), :]
        lf = lf_ref[pl.ds(r0, HG_C), :]
        b = jnp.dot(ltri, lf, precision=HIGHEST, preferred_element_type=F32)
        st = st_ref[...]
        vb = v.astype(BF16)
        qb = (q * jnp.exp(b)).astype(BF16)
        o_inter = lax.dot_general(qb, st.astype(BF16), NT_DIMS, preferred_element_type=F32)
        parts = []
        for i in range(HG_C // HG_SB):
            lo = i * HG_SB
            b_i, q_i, k_i, v_i = b[lo:lo + HG_SB], q[lo:lo + HG_SB], k[lo:lo + HG_SB], v[lo:lo + HG_SB]
            ps = []
            for s in range(HG_SB):
                d = jnp.where(row_sb >= s, b_i - b_i[s:s + 1, :], NEG)
                ps.append(q_i * jnp.exp(d) * k_i[s:s + 1, :])
            p_all = jnp.concatenate(ps, axis=0).astype(BF16)
            r_all = jnp.dot(p_all, ones_b, preferred_element_type=F32)
            o_i = o_inter[lo:lo + HG_SB]
            for s in range(HG_SB):
                o_i = o_i + r_all[s * HG_SB:(s + 1) * HG_SB] * v_i[s:s + 1, :]
            if i > 0:
                b_ref = b[lo - 1:lo, :]
                qs = (q_i * jnp.exp(b_i - b_ref)).astype(BF16)
                ks = (k * jnp.exp(jnp.minimum(b_ref - b, 0.0))).astype(BF16)
                a = lax.dot_general(qs, ks, NT_DIMS, preferred_element_type=F32)
                a = jnp.where(col_sb < lo, a, 0.0).astype(BF16)
                o_i = o_i + jnp.dot(a, vb, preferred_element_type=F32)
            parts.append(o_i)
        o = jnp.concatenate(parts, axis=0)
        b_last = b[HG_C - 1:HG_C, :]
        kdec = (k * jnp.exp(b_last - b)).astype(BF16)
        st_ref[...] = st * jnp.exp(b_last) + jnp.dot(v.T.astype(BF16), kdec, preferred_element_type=F32)
        ms = jnp.mean(o * o, axis=-1, keepdims=True)
        zg = zg_ref[pl.ds(r0, HG_C), :]
        hn_ref[pl.ds(r0, HG_C), :] = o * lax.rsqrt(ms + RMS_EPS) * g * (zg * _sigmoid(zg))
        return carry

    lax.fori_loop(0, n_chunks, chunk, 0)

    @pl.when(t == pl.num_programs(2) - 1)
    def _fin():
        sfin_ref[...] = st_ref[...].T


def _hgrn_prompt(hq, hk, hv, lf, zg, g_hg, tb=1024):
    bsz, t, _ = hq.shape
    seq = pl.BlockSpec((None, tb, HG_HEAD_DIM), lambda b, h, i: (b, i, h))
    return pl.pallas_call(
        functools.partial(_hgrn_kernel, n_chunks=tb // HG_C),
        grid=(bsz, HG_HEADS, t // tb),
        in_specs=[seq, seq, seq, seq, seq,
                  pl.BlockSpec((1, HG_HEAD_DIM), lambda b, h, i: (0, h))],
        out_specs=[seq,
                   pl.BlockSpec((None, None, HG_HEAD_DIM, HG_HEAD_DIM), lambda b, h, i: (b, h, 0, 0))],
        out_shape=[jax.ShapeDtypeStruct((bsz, t, HG_WIDTH), F32),
                   jax.ShapeDtypeStruct((bsz, HG_HEADS, HG_HEAD_DIM, HG_HEAD_DIM), F32)],
        scratch_shapes=[pltpu.VMEM((HG_HEAD_DIM, HG_HEAD_DIM), F32)],
        compiler_params=_cparams(("parallel", "parallel", "arbitrary")),
        name="hgrn_prompt",
    )(hq, hk, hv, lf, zg, g_hg.reshape(1, HG_WIDTH))


def _hgrn_step_kernel(q_ref, k_ref, v_ref, lf_ref, zg_ref, g_ref, s_ref, hn_ref, snew_ref):
    row = slice(None)
    zeros = jnp.zeros((HG_HEAD_DIM - 3, HG_HEAD_DIM), F32)
    for h in range(HG_HEADS):
        cs = slice(h * HG_HEAD_DIM, (h + 1) * HG_HEAD_DIM)
        q, k, v = q_ref[row, cs], k_ref[row, cs], v_ref[row, cs]
        f = jnp.exp(lf_ref[row, cs])
        cols = jnp.concatenate([f, k, q, zeros], axis=0).T
        s_new = cols[:, 0:1] * s_ref[h] + cols[:, 1:2] * v
        snew_ref[h] = s_new
        o = jnp.sum(cols[:, 2:3] * s_new, axis=0, keepdims=True)
        ms = jnp.mean(o * o, axis=-1, keepdims=True)
        zg = zg_ref[row, cs]
        hn_ref[row, cs] = o * lax.rsqrt(ms + RMS_EPS) * g_ref[:, cs] * (zg * _sigmoid(zg))


def _hgrn_step(hq, hk, hv, lf, zg, g_hg, state):
    bsz = hq.shape[0]
    one = pl.BlockSpec((None, 1, HG_WIDTH), lambda b: (b, 0, 0))
    st = pl.BlockSpec((None, HG_HEADS, HG_HEAD_DIM, HG_HEAD_DIM), lambda b: (b, 0, 0, 0))
    r3 = lambda a: a.reshape(bsz, 1, HG_WIDTH)
    hn, s_new = pl.pallas_call(
        _hgrn_step_kernel,
        grid=(bsz,),
        in_specs=[one, one, one, one, one, pl.BlockSpec((1, HG_WIDTH), lambda b: (0, 0)), st],
        out_specs=[one, st],
        out_shape=[jax.ShapeDtypeStruct((bsz, 1, HG_WIDTH), F32),
                   jax.ShapeDtypeStruct(state.shape, F32)],
        compiler_params=_cparams(("parallel",)),
        name="hgrn_step",
    )(r3(hq), r3(hk), r3(hv), r3(lf), r3(zg), g_hg.reshape(1, HG_WIDTH), state)
    return hn.reshape(bsz, HG_WIDTH), s_new


ATT_N = 128
ATT_SUPER = 2048


def _attn_prompt_kernel(q_ref, k_ref, v_ref, o_ref, osc, lsc, *, seq_len):
    lane = lax.broadcasted_iota(jnp.int32, (ATT_N, LANES), 1)
    rowi = lax.broadcasted_iota(jnp.int32, (ATT_N, LANES), 0)
    head0 = lane < ATT_HEAD_DIM
    m_cur = lane <= rowi
    m_prev = lane >= rowi
    ones_b = jnp.ones((LANES, LANES), BF16)
    scale = ATT_HEAD_DIM ** -0.5

    def do_block(p, d, qs, off):
        span = ATT_N * d
        q = q_ref[pl.ds(qs, ATT_N, stride=d), :] * scale
        kc = k_ref[pl.ds(qs, ATT_N, stride=d), :].astype(BF16)
        vc = v_ref[pl.ds(qs, ATT_N, stride=d), :].astype(BF16)
        prev = jnp.maximum(qs - span, 0)
        pen = jnp.where(qs >= span, 0.0, NEG)
        kp = k_ref[pl.ds(prev, ATT_N, stride=d), :].astype(BF16)
        vp = v_ref[pl.ds(prev, ATT_N, stride=d), :].astype(BF16)
        res = []
        for hm in (head0, jnp.logical_not(head0)):
            qh = jnp.where(hm, q, 0.0).astype(BF16)
            sc = lax.dot_general(qh, kc, NT_DIMS, preferred_element_type=F32)
            sp = lax.dot_general(qh, kp, NT_DIMS, preferred_element_type=F32)
            sc = jnp.where(m_cur, sc, NEG)
            sp = jnp.where(m_prev, sp, NEG) + pen
            m = jnp.max(jnp.maximum(sc, sp), axis=-1, keepdims=True)
            pc = jnp.exp(sc - m).astype(BF16)
            pp = jnp.exp(sp - m).astype(BF16)
            den = (jnp.dot(pc, ones_b, preferred_element_type=F32)
                   + jnp.dot(pp, ones_b, preferred_element_type=F32))
            o = (jnp.dot(pc, vc, preferred_element_type=F32)
                 + jnp.dot(pp, vp, preferred_element_type=F32))
            res.append((o / den, m + jnp.log(den)))
        osc[p, pl.ds(off, ATT_N, stride=d), :] = jnp.where(head0, res[0][0], res[1][0])
        lsc[p, pl.ds(off, ATT_N, stride=d), :] = jnp.where(head0, res[0][1], res[1][1])

    def superblock(sb, carry):
        base = sb * ATT_SUPER

        def blocks(i, c2):
            for p, (w, d) in enumerate(DILATED_PATTERNS):
                span = ATT_N * d
                off = (i // d) * span + (i % d)
                do_block(p, d, base + off, off)
            return c2

        lax.fori_loop(0, ATT_SUPER // ATT_N, blocks, 0)

        piece = 256

        def merge(j, c2):
            r = pl.ds(pl.multiple_of(j * piece, piece), piece)
            ls = [lsc[p, r, :] for p in range(len(DILATED_PATTERNS))]
            mx = jnp.maximum(jnp.maximum(ls[0], ls[1]), ls[2])
            ws = [jnp.exp(l - mx) for l in ls]
            num = ws[0] * osc[0, r, :] + ws[1] * osc[1, r, :] + ws[2] * osc[2, r, :]
            o_ref[pl.ds(pl.multiple_of(base + j * piece, piece), piece), :] = num / (ws[0] + ws[1] + ws[2])
            return c2

        lax.fori_loop(0, ATT_SUPER // piece, merge, 0)
        return carry

    lax.fori_loop(0, seq_len // ATT_SUPER, superblock, 0)


def _attn_prompt(aq, ak, av):
    bsz, t, _ = aq.shape
    spec = pl.BlockSpec((None, t, LANES), lambda b, p: (b, 0, p))
    n_pat = len(DILATED_PATTERNS)
    return pl.pallas_call(
        functools.partial(_attn_prompt_kernel, seq_len=t),
        grid=(bsz, ATT_WIDTH // LANES),
        in_specs=[spec, spec, spec],
        out_specs=spec,
        out_shape=jax.ShapeDtypeStruct((bsz, t, ATT_WIDTH), F32),
        scratch_shapes=[pltpu.VMEM((n_pat, ATT_SUPER, LANES), F32),
                        pltpu.VMEM((n_pat, ATT_SUPER, LANES), F32)],
        compiler_params=_cparams(("parallel", "parallel")),
        name="attn_prompt",
    )(aq, ak, av)


def _attn_step_kernel(q_ref, kn_ref, vn_ref, ck_ref, cv_ref, o_ref, nk_ref, nv_ref):
    win = ck_ref.shape[0]
    kc, vc = ck_ref[...], cv_ref[...]
    q, kn, vn = q_ref[...], kn_ref[...], vn_ref[...]

    hrow = lax.broadcasted_iota(jnp.int32, (ATT_HEADS, ATT_WIDTH), 0)
    hcol = lax.broadcasted_iota(jnp.int32, (ATT_HEADS, ATT_WIDTH), 1) // ATT_HEAD_DIM
    own = hrow == hcol
    qm = jnp.where(own, q, 0.0) * (ATT_HEAD_DIM ** -0.5)
    s_all = lax.dot_general(qm.astype(BF16), kc.astype(BF16), NT_DIMS, preferred_element_type=F32)
    s_new = jnp.sum(qm * kn, axis=-1, keepdims=True)
    dist = win - lax.broadcasted_iota(jnp.int32, (ATT_HEADS, win), 1)

    ps, pnews, lses = [], [], []
    for w, d in DILATED_PATTERNS:
        on_stride = (dist & (d - 1)) == 0 if d & (d - 1) == 0 else dist % d == 0
        valid = jnp.logical_and(dist <= w, on_stride)
        sm = jnp.where(valid, s_all, NEG)
        m = jnp.maximum(jnp.max(sm, axis=-1, keepdims=True), s_new)
        p = jnp.exp(sm - m)
        pn = jnp.exp(s_new - m)
        den = jnp.sum(p, axis=-1, keepdims=True) + pn
        ps.append(p / den)
        pnews.append(pn / den)
        lses.append(m + jnp.log(den))
    mx = jnp.maximum(jnp.maximum(lses[0], lses[1]), lses[2])
    ws = [jnp.exp(l - mx) for l in lses]
    wsum = ws[0] + ws[1] + ws[2]
    p_tot = (ws[0] * ps[0] + ws[1] * ps[1] + ws[2] * ps[2]) / wsum
    pn_tot = (ws[0] * pnews[0] + ws[1] * pnews[1] + ws[2] * pnews[2]) / wsum
    o = jnp.dot(p_tot.astype(BF16), vc.astype(BF16), preferred_element_type=F32) + pn_tot * vn
    o_ref[...] = jnp.sum(jnp.where(own, o, 0.0), axis=0, keepdims=True)

    last = lax.broadcasted_iota(jnp.int32, (win, ATT_WIDTH), 0) == win - 1
    nk_ref[...] = jnp.where(last, kn, pltpu.roll(kc, win - 1, 0))
    nv_ref[...] = jnp.where(last, vn, pltpu.roll(vc, win - 1, 0))


def _attn_step(aq, ak, av, cache_k, cache_v):
    bsz, win, _ = cache_k.shape
    one = pl.BlockSpec((None, 1, ATT_WIDTH), lambda b: (b, 0, 0))
    cache = pl.BlockSpec((None, win, ATT_WIDTH), lambda b: (b, 0, 0))
    r3 = lambda a: a.reshape(bsz, 1, ATT_WIDTH)
    att, new_k, new_v = pl.pallas_call(
        _attn_step_kernel,
        grid=(bsz,),
        in_specs=[one, one, one, cache, cache],
        out_specs=[one, cache, cache],
        out_shape=[jax.ShapeDtypeStruct((bsz, 1, ATT_WIDTH), F32),
                   jax.ShapeDtypeStruct(cache_k.shape, F32),
                   jax.ShapeDtypeStruct(cache_v.shape, F32)],
        compiler_params=_cparams(("parallel",)),
        name="attn_step",
    )(r3(aq), r3(ak), r3(av), cache_k, cache_v)
    return att.reshape(bsz, ATT_WIDTH), new_k, new_v


ROUTE_TM = 256


def _outproj_kernel(hn_ref, att_ref, x_ref, hn_s_ref, att_s_ref, x_s_ref, w_ref, g_ref, wr_ref,
                    xmid_ref, h2_ref, route_ref):
    is_prompt = pl.program_id(0) < pl.num_programs(0) - 1
    hn = jnp.where(is_prompt, hn_ref[...], hn_s_ref[...])
    att = jnp.where(is_prompt, att_ref[...], att_s_ref[...])
    y = (jnp.dot(hn.astype(BF16), w_ref[0:HG_WIDTH, :], preferred_element_type=F32)
         + jnp.dot(att.astype(BF16), w_ref[HG_WIDTH:, :], preferred_element_type=F32))
    xm = jnp.where(is_prompt, x_ref[...], x_s_ref[...]) + y
    xmid_ref[...] = xm
    ms = jnp.mean(xm * xm, axis=-1, keepdims=True)
    h2 = xm * lax.rsqrt(ms + RMS_EPS) * g_ref[...]
    h2_ref[...] = h2
    lg = jnp.dot(h2, wr_ref[...], precision=HIGHEST, preferred_element_type=F32)
    lane = lax.broadcasted_iota(jnp.int32, lg.shape, 1).astype(F32)
    big = float(LANES)
    gmask = lane < N_GROUPS
    lgg = jnp.where(gmask, lg, NEG)
    mg = jnp.max(lgg, axis=-1, keepdims=True)
    gi = jnp.min(jnp.where(lgg == mg, lane, big), axis=-1, keepdims=True)
    p_grp = 1.0 / jnp.sum(jnp.exp(lgg - mg), axis=-1, keepdims=True)
    lo = N_GROUPS + gi * EXPERTS_PER_GROUP
    emask = jnp.logical_and(lane >= lo, lane < lo + EXPERTS_PER_GROUP)
    le1 = jnp.where(emask, lg, NEG)
    m1 = jnp.max(le1, axis=-1, keepdims=True)
    i1 = jnp.min(jnp.where(le1 == m1, lane, big), axis=-1, keepdims=True)
    le2 = jnp.where(lane == i1, NEG, le1)
    m2 = jnp.max(le2, axis=-1, keepdims=True)
    i2 = jnp.min(jnp.where(le2 == m2, lane, big), axis=-1, keepdims=True)
    r = jnp.exp(m2 - m1)
    g1 = p_grp / (1.0 + r)
    g2 = p_grp * r / (1.0 + r)
    route_ref[...] = jnp.where(lane == 0, i1 - N_GROUPS,
                               jnp.where(lane == 1, i2 - N_GROUPS,
                                         jnp.where(lane == 2, g1, jnp.where(lane == 3, g2, 0.0))))


def _outproj(hn_p, att_p, x_p, hn_s, att_s, x_s, w_out_bf, g_ffn, w_router):
    tm = ROUTE_TM
    n_p = x_p.shape[0]
    n_tiles = n_p // tm + 1
    pad = lambda a: jnp.pad(a, ((0, tm - a.shape[0]), (0, 0)))
    row_p = lambda i: (jnp.minimum(i, n_tiles - 2), 0)
    row = lambda i: (i, 0)
    const = lambda i: (0, 0)
    n_rows = n_tiles * tm
    return pl.pallas_call(
        _outproj_kernel,
        grid=(n_tiles,),
        in_specs=[pl.BlockSpec((tm, HG_WIDTH), row_p),
                  pl.BlockSpec((tm, ATT_WIDTH), row_p),
                  pl.BlockSpec((tm, D_MODEL), row_p),
                  pl.BlockSpec((tm, HG_WIDTH), const),
                  pl.BlockSpec((tm, ATT_WIDTH), const),
                  pl.BlockSpec((tm, D_MODEL), const),
                  pl.BlockSpec((D_MODEL, D_MODEL), const),
                  pl.BlockSpec((1, D_MODEL), const),
                  pl.BlockSpec((D_MODEL, LANES), const)],
        out_specs=[pl.BlockSpec((tm, D_MODEL), row),
                   pl.BlockSpec((tm, D_MODEL), row),
                   pl.BlockSpec((tm, LANES), row)],
        out_shape=[jax.ShapeDtypeStruct((n_rows, D_MODEL), F32),
                   jax.ShapeDtypeStruct((n_rows, D_MODEL), F32),
                   jax.ShapeDtypeStruct((n_rows, LANES), F32)],
        compiler_params=_cparams(("arbitrary",)),
        name="outproj",
    )(hn_p, att_p, x_p, pad(hn_s), pad(att_s), pad(x_s), w_out_bf, g_ffn.reshape(1, D_MODEL), w_router)


def _rank_kernel(route_ref, slot_ref, meta_ref, cnt_ref, base_ref, *, n_tok, blk):
    ph, i = pl.program_id(0), pl.program_id(1)
    tm = route_ref.shape[0]
    lane = lax.broadcasted_iota(jnp.int32, (tm, LANES), 1).astype(F32)
    rowg = i * tm + lax.broadcasted_iota(jnp.int32, (tm, LANES), 0)
    valid = rowg < n_tok
    r = route_ref[...]
    oh0 = jnp.where(jnp.logical_and(valid, lane == r[:, 0:1]), 1.0, 0.0)
    oh1 = jnp.where(jnp.logical_and(valid, lane == r[:, 1:2]), 1.0, 0.0)
    oh = oh0 + oh1
    colsum = jnp.sum(oh, axis=0, keepdims=True)

    @pl.when(jnp.logical_and(ph == 0, i == 0))
    def _zero():
        cnt_ref[...] = jnp.zeros_like(cnt_ref)

    @pl.when(ph == 0)
    def _count():
        cnt_ref[...] += colsum

    @pl.when(jnp.logical_and(ph == 1, i == 0))
    def _starts():
        cnt = cnt_ref[...].astype(jnp.int32)
        shift = blk.bit_length() - 1
        padded = (((cnt + (blk - 1)) >> shift) << shift).astype(F32)
        up = (lax.broadcasted_iota(jnp.int32, (LANES, LANES), 0)
              < lax.broadcasted_iota(jnp.int32, (LANES, LANES), 1)).astype(F32)
        start = jnp.dot(jnp.broadcast_to(padded, (8, LANES)), up, precision=HIGHEST,
                        preferred_element_type=F32)[0:1]
        base_ref[...] = start
        nb = meta_ref.shape[0]
        lane_b = lax.broadcasted_iota(jnp.int32, (nb, LANES), 1)
        blk_start = (lax.broadcasted_iota(jnp.int32, (nb, LANES), 0) * blk).astype(F32)
        ended = jnp.logical_and(start + padded <= blk_start, lane_b < N_EXPERTS)
        be = jnp.minimum(jnp.sum(jnp.where(ended, 1.0, 0.0), axis=-1, keepdims=True), N_EXPERTS - 1.0)
        n_used = jnp.sum(padded, axis=-1, keepdims=True) * (1.0 / blk)
        meta_ref[...] = jnp.where(lane_b == 0, be, jnp.where(lane_b == 1, n_used, 0.0)).astype(jnp.int32)

    @pl.when(ph == 1)
    def _slots():
        before = (lax.broadcasted_iota(jnp.int32, (tm, tm), 1)
                  < lax.broadcasted_iota(jnp.int32, (tm, tm), 0)).astype(BF16)
        pre = jnp.dot(before, oh.astype(BF16), preferred_element_type=F32) + base_ref[...]
        s0 = jnp.sum(oh0 * pre, axis=-1, keepdims=True)
        s1 = jnp.sum(oh1 * pre, axis=-1, keepdims=True)
        slot_ref[...] = jnp.where(lane == 0, s0, jnp.where(lane == 1, s1, 0.0)).astype(jnp.int32)
        base_ref[...] += colsum


def _rank(route, n_tok, blk, nblk):
    n_rows = route.shape[0]
    nb = (nblk + 7) // 8 * 8
    slots, meta = pl.pallas_call(
        functools.partial(_rank_kernel, n_tok=n_tok, blk=blk),
        grid=(2, n_rows // ROUTE_TM),
        in_specs=[pl.BlockSpec((ROUTE_TM, LANES), lambda ph, i: (i, 0))],
        out_specs=[pl.BlockSpec((ROUTE_TM, LANES), lambda ph, i: (i * ph, 0)),
                   pl.BlockSpec((nb, LANES), lambda ph, i: (0, 0))],
        out_shape=[jax.ShapeDtypeStruct((n_rows, LANES), jnp.int32),
                   jax.ShapeDtypeStruct((nb, LANES), jnp.int32)],
        scratch_shapes=[pltpu.VMEM((1, LANES), F32), pltpu.VMEM((1, LANES), F32)],
        compiler_params=_cparams(("arbitrary", "arbitrary")),
        name="rank",
    )(route)
    return slots[:n_tok, :TOP_K_INNER].reshape(-1), meta[:nblk, 0], meta[0:1, 1]


def _invert_kernel(slot_ref, inv_ref):
    def clear(j, c):
        inv_ref[j] = -1
        return c

    lax.fori_loop(0, inv_ref.shape[0], clear, 0)

    def put(a, c):
        inv_ref[slot_ref[a]] = a
        return c

    lax.fori_loop(0, slot_ref.shape[0], put, 0)


def _invert(slot_flat, n_slots):
    return pl.pallas_call(
        _invert_kernel,
        in_specs=[pl.BlockSpec(memory_space=pltpu.SMEM)],
        out_specs=pl.BlockSpec(memory_space=pltpu.SMEM),
        out_shape=jax.ShapeDtypeStruct((n_slots,), jnp.int32),
        name="invert",
    )(slot_flat)


def _expert_kernel(be_ref, nu_ref, inv_ref, h2_hbm, wg_ref, wu_ref, wd_ref, y_ref,
                   xbuf, sem, wgb, wub, wdb, *, blk):
    i = pl.program_id(0)
    n_used = nu_ref[0]
    e = be_ref[i]
    e_prev = be_ref[jnp.maximum(i - 1, 0)]

    def gather(j, buf):
        def body(r, c):
            a = inv_ref[j * blk + r]
            tok = jnp.maximum(a, 0) >> 1
            pltpu.make_async_copy(h2_hbm.at[pl.ds(tok, 1)], xbuf.at[buf, pl.ds(r, 1)], sem.at[buf]).start()
            return c
        lax.fori_loop(0, blk, body, 0)

    @pl.when(i == 0)
    def _first():
        gather(0, 0)

    @pl.when(i + 1 < n_used)
    def _next():
        gather(i + 1, (i + 1) % 2)

    @pl.when(jnp.logical_or(i == 0, e != e_prev))
    def _cast():
        wgb[...] = wg_ref[...].astype(BF16)
        wub[...] = wu_ref[...].astype(BF16)
        wdb[...] = wd_ref[...].astype(BF16)

    @pl.when(i < n_used)
    def _run():
        buf = i % 2
        pltpu.make_async_copy(h2_hbm.at[pl.ds(0, blk)], xbuf.at[buf], sem.at[buf]).wait()
        x = xbuf[buf].astype(BF16)
        a = jnp.dot(x, wgb[...], preferred_element_type=F32)
        u = jnp.dot(x, wub[...], preferred_element_type=F32)
        mid = (a * _sigmoid(a) * u).astype(BF16)
        y_ref[...] = jnp.dot(mid, wdb[...], preferred_element_type=F32)

    @pl.when(i >= n_used)
    def _skip():
        y_ref[...] = jnp.zeros_like(y_ref)


def _experts(h2, inv, blk_expert, n_used, w_g, w_u, w_d, blk):
    nblk = blk_expert.shape[0]
    wmap = lambda i, be, nu, inv: (be[i], 0, 0)
    grid_spec = pltpu.PrefetchScalarGridSpec(
        num_scalar_prefetch=3,
        grid=(nblk,),
        in_specs=[pl.BlockSpec(memory_space=pl.ANY),
                  pl.BlockSpec((None, D_MODEL, D_FF_EXPERT), wmap),
                  pl.BlockSpec((None, D_MODEL, D_FF_EXPERT), wmap),
                  pl.BlockSpec((None, D_FF_EXPERT, D_MODEL), wmap)],
        out_specs=pl.BlockSpec((blk, D_MODEL), lambda i, be, nu, inv: (i, 0)),
        scratch_shapes=[pltpu.VMEM((2, blk, D_MODEL), F32),
                        pltpu.SemaphoreType.DMA((2,)),
                        pltpu.VMEM((D_MODEL, D_FF_EXPERT), BF16),
                        pltpu.VMEM((D_MODEL, D_FF_EXPERT), BF16),
                        pltpu.VMEM((D_FF_EXPERT, D_MODEL), BF16)],
    )
    return pl.pallas_call(
        functools.partial(_expert_kernel, blk=blk),
        grid_spec=grid_spec,
        out_shape=jax.ShapeDtypeStruct((nblk * blk, D_MODEL), F32),
        compiler_params=_cparams(("arbitrary",)),
        name="experts",
    )(blk_expert, n_used, inv, h2, w_g, w_u, w_d)


def _final_kernel(slot_ref, x_ref, route_ref, g_ref, yb_hbm, o_ref, ybuf, sem, *, tok0):
    i = pl.program_id(0)
    tm = x_ref.shape[0]

    def gather(j, buf):
        def body(r, c):
            a = (tok0 + j * tm + r) * TOP_K_INNER
            for k in range(TOP_K_INNER):
                pltpu.make_async_copy(yb_hbm.at[pl.ds(slot_ref[a + k], 1)], ybuf.at[buf, k, pl.ds(r, 1)],
                                      sem.at[buf]).start()
            return c
        lax.fori_loop(0, tm, body, 0)

    @pl.when(i == 0)
    def _first():
        gather(0, 0)

    @pl.when(i + 1 < pl.num_programs(0))
    def _next():
        gather(i + 1, (i + 1) % 2)

    buf = i % 2
    for k in range(TOP_K_INNER):
        pltpu.make_async_copy(yb_hbm.at[pl.ds(0, tm)], ybuf.at[buf, k], sem.at[buf]).wait()
    route = route_ref[...]
    x = x_ref[...] + (ybuf[buf, 0] * route[:, 2:3] + ybuf[buf, 1] * route[:, 3:4])
    ms = jnp.mean(x * x, axis=-1, keepdims=True)
    o_ref[...] = x * lax.rsqrt(ms + RMS_EPS) * g_ref[...]


def _final(slot_flat, xmid, route, g_final, yb, tok0, n_out, tm):
    blk0 = tok0 // tm
    grid_spec = pltpu.PrefetchScalarGridSpec(
        num_scalar_prefetch=1,
        grid=(n_out // tm,),
        in_specs=[pl.BlockSpec((tm, D_MODEL), lambda i, s: (i + blk0, 0)),
                  pl.BlockSpec((tm, LANES), lambda i, s: (i + blk0, 0)),
                  pl.BlockSpec((1, D_MODEL), lambda i, s: (0, 0)),
                  pl.BlockSpec(memory_space=pl.ANY)],
        out_specs=pl.BlockSpec((tm, D_MODEL), lambda i, s: (i, 0)),
        scratch_shapes=[pltpu.VMEM((2, TOP_K_INNER, tm, D_MODEL), F32),
                        pltpu.SemaphoreType.DMA((2,))],
    )
    return pl.pallas_call(
        functools.partial(_final_kernel, tok0=tok0),
        grid_spec=grid_spec,
        out_shape=jax.ShapeDtypeStruct((n_out, D_MODEL), F32),
        compiler_params=_cparams(("arbitrary",)),
        name="final",
    )(slot_flat, xmid, route, g_final.reshape(1, D_MODEL), yb)


def kernel(x_prompt, x_sample, cache_attn_k, cache_attn_v, state_hgrn, w_in, w_out, hg_lb_logits,
           hg_norm_g, norm_mix_g, norm_ffn_g, norm_final_g, w_route_group, w_route_expert,
           w_expert_gate, w_expert_up, w_expert_down):
    bp, tp, _ = x_prompt.shape
    bs = x_sample.shape[0]
    l = 0
    w_in_bf = w_in[l].astype(BF16)
    w_out_bf = w_out[l].astype(BF16)
    w_router = jnp.concatenate(
        [w_route_group[l],
         jnp.transpose(w_route_expert[l], (1, 0, 2)).reshape(D_MODEL, N_EXPERTS),
         jnp.zeros((D_MODEL, LANES - N_GROUPS - N_EXPERTS), F32)], axis=-1)

    n_p = bp * tp
    xp = x_prompt.reshape(n_p, D_MODEL)
    pos_p = jnp.tile(jnp.arange(tp, dtype=jnp.int32), bp)
    hq, hk, hv, lf, zg, aq, ak, av = _inproj(xp, norm_mix_g[l], w_in_bf, hg_lb_logits, pos_p, 256)
    seq3 = lambda a: a.reshape(bp, tp, HG_WIDTH)
    hn_p, s_fin = _hgrn_prompt(seq3(hq), seq3(hk), seq3(hv), seq3(lf), seq3(zg), hg_norm_g[l])
    att_p = _attn_prompt(seq3(aq), seq3(ak), seq3(av))
    keep = min(MAX_WINDOW, tp)
    heads = lambda a: a.reshape(1, bp, keep, ATT_HEADS, ATT_HEAD_DIM)
    new_k_p = heads(seq3(ak)[:, tp - keep:])
    new_v_p = heads(seq3(av)[:, tp - keep:])

    xs = x_sample.reshape(bs, D_MODEL)
    pos_s = jnp.full((bs,), PAST_LEN, jnp.int32)
    hq, hk, hv, lf, zg, aq, ak, av = _inproj(xs, norm_mix_g[l], w_in_bf, hg_lb_logits, pos_s, bs)
    hn_s, s_new = _hgrn_step(hq, hk, hv, lf, zg, hg_norm_g[l], state_hgrn[l])
    win = cache_attn_k.shape[2]
    att_s, new_k_s, new_v_s = _attn_step(aq, ak, av,
                                         cache_attn_k[l].reshape(bs, win, ATT_WIDTH),
                                         cache_attn_v[l].reshape(bs, win, ATT_WIDTH))
    cache5 = lambda a: a.reshape(1, bs, win, ATT_HEADS, ATT_HEAD_DIM)

    assert n_p % ROUTE_TM == 0 and bs <= ROUTE_TM
    n_tok = n_p + bs
    xmid, h2, route = _outproj(hn_p.reshape(n_p, HG_WIDTH), att_p.reshape(n_p, ATT_WIDTH), xp,
                               hn_s, att_s, xs, w_out_bf, norm_ffn_g[l], w_router)
    blk = MOE_BLOCK
    nblk = (n_tok * TOP_K_INNER + N_EXPERTS * (blk - 1)) // blk + 1
    slot_flat, blk_expert, n_used = _rank(route, n_tok, blk, nblk)
    inv = _invert(slot_flat, nblk * blk)
    yb = _experts(h2, inv, blk_expert, n_used, w_expert_gate[l], w_expert_up[l], w_expert_down[l], blk)
    y_prompt = _final(slot_flat, xmid, route, norm_final_g, yb, 0, n_p, 256)
    y_sample = _final(slot_flat, xmid, route, norm_final_g, yb, n_p, bs, bs)

    return (y_prompt.reshape(bp, tp, D_MODEL), y_sample.reshape(bs, 1, D_MODEL),
            new_k_p, new_v_p, s_fin[None], cache5(new_k_s), cache5(new_v_s), s_new[None])
```

```python
import functools

import jax
import jax.numpy as jnp
from jax import lax
from jax.experimental import pallas as pl
from jax.experimental.pallas import tpu as pltpu

F32 = jnp.float32
BF16 = jnp.bfloat16

D_MODEL = 1024
HG_WIDTH = 512
HG_HEAD_DIM = 128
HG_HEADS = 4
ATT_WIDTH = 512
ATT_HEAD_DIM = 64
ATT_HEADS = 8
ROPE_DIM = 16
ROPE_THETA = 500000.0
DILATED_PATTERNS = ((128, 1), (512, 4), (2048, 16))
MAX_WINDOW = 2048
PAST_LEN = 16384
N_GROUPS = 8
EXPERTS_PER_GROUP = 8
N_EXPERTS = 64
TOP_K_INNER = 2
D_FF_EXPERT = 512
MOE_BLOCK = 128
IN_COLS = 4 * HG_WIDTH + 3 * ATT_WIDTH
RMS_EPS = 1e-6

LANES = 128
VMEM_LIMIT = 56 * 1024 * 1024
NEG = -1e30
HIGHEST = lax.Precision.HIGHEST
NT_DIMS = (((1,), (1,)), ((), ()))


def _sigmoid(z):
    return 1.0 / (1.0 + jnp.exp(-z))


def _cparams(sem):
    return pltpu.CompilerParams(dimension_semantics=sem, vmem_limit_bytes=VMEM_LIMIT)


def _inproj_kernel(x_ref, g_ref, w_ref, lbl_ref, cos_ref, sa_ref, sb_ref,
                   hq_ref, hk_ref, hv_ref, lf_ref, zg_ref, aq_ref, ak_ref, av_ref):
    x = x_ref[...]
    ms = jnp.mean(x * x, axis=-1, keepdims=True)
    h = (x * lax.rsqrt(ms + RMS_EPS) * g_ref[...]).astype(BF16)

    def mm(c0):
        return jnp.dot(h, w_ref[:, c0:c0 + HG_WIDTH], preferred_element_type=F32)

    lbl = lbl_ref[...]
    le = jnp.exp(lbl - jnp.max(lbl, axis=0, keepdims=True))
    lb = le[0:1, :] / jnp.sum(le, axis=0, keepdims=True)

    zq = mm(0)
    hq_ref[...] = zq * _sigmoid(zq)
    zf = mm(HG_WIDTH)
    f = lb + (1.0 - lb) * _sigmoid(zf)
    hk_ref[...] = 1.0 - f
    lf_ref[...] = jnp.log(f)
    hv_ref[...] = mm(2 * HG_WIDTH)
    zg_ref[...] = mm(3 * HG_WIDTH)

    cos, sa, sb = cos_ref[...], sa_ref[...], sb_ref[...]

    def rope(a, out_ref):
        for j in range(ATT_WIDTH // LANES):
            xj = a[:, j * LANES:(j + 1) * LANES]
            up = pltpu.roll(xj, LANES - ROPE_DIM // 2, 1)
            dn = pltpu.roll(xj, ROPE_DIM // 2, 1)
            out_ref[:, j * LANES:(j + 1) * LANES] = xj * cos + up * sa + dn * sb

    rope(mm(4 * HG_WIDTH), aq_ref)
    rope(mm(4 * HG_WIDTH + ATT_WIDTH), ak_ref)
    av_ref[...] = mm(4 * HG_WIDTH + 2 * ATT_WIDTH)


def _rope_tables(pos):
    half = ROPE_DIM // 2
    inv_freq = ROPE_THETA ** (-jnp.arange(half, dtype=F32) / half)
    ang = pos.astype(F32)[:, None] * inv_freq[None, :]
    cos, sin = jnp.cos(ang), jnp.sin(ang)
    m = pos.shape[0]
    rest = ATT_HEAD_DIM - ROPE_DIM
    one, zero, z8 = jnp.ones((m, rest), F32), jnp.zeros((m, rest), F32), jnp.zeros((m, half), F32)
    reps = LANES // ATT_HEAD_DIM
    c = jnp.tile(jnp.concatenate([cos, cos, one], axis=-1), (1, reps))
    sa = jnp.tile(jnp.concatenate([-sin, z8, zero], axis=-1), (1, reps))
    sb = jnp.tile(jnp.concatenate([z8, sin, zero], axis=-1), (1, reps))
    return c, sa, sb


def _inproj(x2d, g, w_bf, lb_logits, pos, tm):
    m = x2d.shape[0]
    cos, sa, sb = _rope_tables(pos)
    row = lambda i: (i, 0)
    const = lambda i: (0, 0)
    outs = [jax.ShapeDtypeStruct((m, HG_WIDTH), F32)] * 8
    return pl.pallas_call(
        _inproj_kernel,
        grid=(m // tm,),
        in_specs=[pl.BlockSpec((tm, D_MODEL), row),
                  pl.BlockSpec((1, D_MODEL), const),
                  pl.BlockSpec((D_MODEL, IN_COLS), const),
                  pl.BlockSpec(lb_logits.shape, const),
                  pl.BlockSpec((tm, LANES), row),
                  pl.BlockSpec((tm, LANES), row),
                  pl.BlockSpec((tm, LANES), row)],
        out_specs=[pl.BlockSpec((tm, HG_WIDTH), row)] * 8,
        out_shape=outs,
        compiler_params=_cparams(("parallel",)),
        name="inproj",
    )(x2d, g.reshape(1, D_MODEL), w_bf, lb_logits, cos, sa, sb)


HG_C = 128
HG_SB = 16


def _hgrn_kernel(q_ref, k_ref, v_ref, lf_ref, zg_ref, g_ref, hn_ref, sfin_ref, st_ref, *, n_chunks):
    t = pl.program_id(2)

    @pl.when(t == 0)
    def _init():
        st_ref[...] = jnp.zeros_like(st_ref)

    ri = lax.broadcasted_iota(jnp.int32, (HG_C, HG_C), 0)
    ci = lax.broadcasted_iota(jnp.int32, (HG_C, HG_C), 1)
    ltri = (ri >= ci).astype(F32)
    ones_b = jnp.ones((LANES, LANES), BF16)
    row_sb = lax.broadcasted_iota(jnp.int32, (HG_SB, LANES), 0)
    col_sb = lax.broadcasted_iota(jnp.int32, (HG_SB, HG_C), 1)
    g = g_ref[...]

    def chunk(c, carry):
        r0 = pl.multiple_of(c * HG_C, HG_C)
        q = q_ref[pl.ds(r0, HG_C), :]
        k = k_ref[pl.ds(r0, HG_C), :]
        v = v_ref[pl.ds(r0, HG_C), :]
        lf = lf_ref[pl.ds(r0, HG_C), :]
        b = jnp.dot(ltri, lf, precision=HIGHEST, preferred_element_type=F32)
        st = st_ref[...]
        vb = v.astype(BF16)
        qb = (q * jnp.exp(b)).astype(BF16)
        o_inter = lax.dot_general(qb, st.astype(BF16), NT_DIMS, preferred_element_type=F32)
        parts = []
        for i in range(HG_C // HG_SB):
            lo = i * HG_SB
            b_i, q_i, k_i, v_i = b[lo:lo + HG_SB], q[lo:lo + HG_SB], k[lo:lo + HG_SB], v[lo:lo + HG_SB]
            ps = []
            for s in range(HG_SB):
                d = jnp.where(row_sb >= s, b_i - b_i[s:s + 1, :], NEG)
                ps.append(q_i * jnp.exp(d) * k_i[s:s + 1, :])
            p_all = jnp.concatenate(ps, axis=0).astype(BF16)
            r_all = jnp.dot(p_all, ones_b, preferred_element_type=F32)
            o_i = o_inter[lo:lo + HG_SB]
            for s in range(HG_SB):
                o_i = o_i + r_all[s * HG_SB:(s + 1) * HG_SB] * v_i[s:s + 1, :]
            if i > 0:
                b_ref = b[lo - 1:lo, :]
                qs = (q_i * jnp.exp(b_i - b_ref)).astype(BF16)
                ks = (k * jnp.exp(jnp.minimum(b_ref - b, 0.0))).astype(BF16)
                a = lax.dot_general(qs, ks, NT_DIMS, preferred_element_type=F32)
                a = jnp.where(col_sb < lo, a, 0.0).astype(BF16)
                o_i = o_i + jnp.dot(a, vb, preferred_element_type=F32)
            parts.append(o_i)
        o = jnp.concatenate(parts, axis=0)
        b_last = b[HG_C - 1:HG_C, :]
        kdec = (k * jnp.exp(b_last - b)).astype(BF16)
        st_ref[...] = st * jnp.exp(b_last) + jnp.dot(v.T.astype(BF16), kdec, preferred_element_type=F32)
        ms = jnp.mean(o * o, axis=-1, keepdims=True)
        zg = zg_ref[pl.ds(r0, HG_C), :]
        hn_ref[pl.ds(r0, HG_C), :] = o * lax.rsqrt(ms + RMS_EPS) * g * (zg * _sigmoid(zg))
        return carry

    lax.fori_loop(0, n_chunks, chunk, 0)

    @pl.when(t == pl.num_programs(2) - 1)
    def _fin():
        sfin_ref[...] = st_ref[...].T


def _hgrn_prompt(hq, hk, hv, lf, zg, g_hg, tb=1024):
    bsz, t, _ = hq.shape
    seq = pl.BlockSpec((None, tb, HG_HEAD_DIM), lambda b, h, i: (b, i, h))
    return pl.pallas_call(
        functools.partial(_hgrn_kernel, n_chunks=tb // HG_C),
        grid=(bsz, HG_HEADS, t // tb),
        in_specs=[seq, seq, seq, seq, seq,
                  pl.BlockSpec((1, HG_HEAD_DIM), lambda b, h, i: (0, h))],
        out_specs=[seq,
                   pl.BlockSpec((None, None, HG_HEAD_DIM, HG_HEAD_DIM), lambda b, h, i: (b, h, 0, 0))],
        out_shape=[jax.ShapeDtypeStruct((bsz, t, HG_WIDTH), F32),
                   jax.ShapeDtypeStruct((bsz, HG_HEADS, HG_HEAD_DIM, HG_HEAD_DIM), F32)],
        scratch_shapes=[pltpu.VMEM((HG_HEAD_DIM, HG_HEAD_DIM), F32)],
        compiler_params=_cparams(("parallel", "parallel", "arbitrary")),
        name="hgrn_prompt",
    )(hq, hk, hv, lf, zg, g_hg.reshape(1, HG_WIDTH))


def _hgrn_step_kernel(q_ref, k_ref, v_ref, lf_ref, zg_ref, g_ref, s_ref, hn_ref, snew_ref):
    row = slice(None)
    zeros = jnp.zeros((HG_HEAD_DIM - 3, HG_HEAD_DIM), F32)
    for h in range(HG_HEADS):
        cs = slice(h * HG_HEAD_DIM, (h + 1) * HG_HEAD_DIM)
        q, k, v = q_ref[row, cs], k_ref[row, cs], v_ref[row, cs]
        f = jnp.exp(lf_ref[row, cs])
        cols = jnp.concatenate([f, k, q, zeros], axis=0).T
        s_new = cols[:, 0:1] * s_ref[h] + cols[:, 1:2] * v
        snew_ref[h] = s_new
        o = jnp.sum(cols[:, 2:3] * s_new, axis=0, keepdims=True)
        ms = jnp.mean(o * o, axis=-1, keepdims=True)
        zg = zg_ref[row, cs]
        hn_ref[row, cs] = o * lax.rsqrt(ms + RMS_EPS) * g_ref[:, cs] * (zg * _sigmoid(zg))


def _hgrn_step(hq, hk, hv, lf, zg, g_hg, state):
    bsz = hq.shape[0]
    one = pl.BlockSpec((None, 1, HG_WIDTH), lambda b: (b, 0, 0))
    st = pl.BlockSpec((None, HG_HEADS, HG_HEAD_DIM, HG_HEAD_DIM), lambda b: (b, 0, 0, 0))
    r3 = lambda a: a.reshape(bsz, 1, HG_WIDTH)
    hn, s_new = pl.pallas_call(
        _hgrn_step_kernel,
        grid=(bsz,),
        in_specs=[one, one, one, one, one, pl.BlockSpec((1, HG_WIDTH), lambda b: (0, 0)), st],
        out_specs=[one, st],
        out_shape=[jax.ShapeDtypeStruct((bsz, 1, HG_WIDTH), F32),
                   jax.ShapeDtypeStruct(state.shape, F32)],
        compiler_params=_cparams(("parallel",)),
        name="hgrn_step",
    )(r3(hq), r3(hk), r3(hv), r3(lf), r3(zg), g_hg.reshape(1, HG_WIDTH), state)
    return hn.reshape(bsz, HG_WIDTH), s_new


ATT_N = 128
ATT_SUPER = 2048


def _attn_prompt_kernel(q_ref, k_ref, v_ref, o_ref, osc, lsc, *, seq_len):
    lane = lax.broadcasted_iota(jnp.int32, (ATT_N, LANES), 1)
    rowi = lax.broadcasted_iota(jnp.int32, (ATT_N, LANES), 0)
    head0 = lane < ATT_HEAD_DIM
    m_cur = lane <= rowi
    m_prev = lane >= rowi
    ones_b = jnp.ones((LANES, LANES), BF16)
    scale = ATT_HEAD_DIM ** -0.5

    def do_block(p, d, qs, off):
        span = ATT_N * d
        q = q_ref[pl.ds(qs, ATT_N, stride=d), :] * scale
        kc = k_ref[pl.ds(qs, ATT_N, stride=d), :].astype(BF16)
        vc = v_ref[pl.ds(qs, ATT_N, stride=d), :].astype(BF16)
        prev = jnp.maximum(qs - span, 0)
        pen = jnp.where(qs >= span, 0.0, NEG)
        kp = k_ref[pl.ds(prev, ATT_N, stride=d), :].astype(BF16)
        vp = v_ref[pl.ds(prev, ATT_N, stride=d), :].astype(BF16)
        res = []
        for hm in (head0, jnp.logical_not(head0)):
            qh = jnp.where(hm, q, 0.0).astype(BF16)
            sc = lax.dot_general(qh, kc, NT_DIMS, preferred_element_type=F32)
            sp = lax.dot_general(qh, kp, NT_DIMS, preferred_element_type=F32)
            sc = jnp.where(m_cur, sc, NEG)
            sp = jnp.where(m_prev, sp, NEG) + pen
            m = jnp.max(jnp.maximum(sc, sp), axis=-1, keepdims=True)
            pc = jnp.exp(sc - m).astype(BF16)
            pp = jnp.exp(sp - m).astype(BF16)
            den = (jnp.dot(pc, ones_b, preferred_element_type=F32)
                   + jnp.dot(pp, ones_b, preferred_element_type=F32))
            o = (jnp.dot(pc, vc, preferred_element_type=F32)
                 + jnp.dot(pp, vp, preferred_element_type=F32))
            res.append((o / den, m + jnp.log(den)))
        osc[p, pl.ds(off, ATT_N, stride=d), :] = jnp.where(head0, res[0][0], res[1][0])
        lsc[p, pl.ds(off, ATT_N, stride=d), :] = jnp.where(head0, res[0][1], res[1][1])

    def superblock(sb, carry):
        base = sb * ATT_SUPER

        def blocks(i, c2):
            for p, (w, d) in enumerate(DILATED_PATTERNS):
                span = ATT_N * d
                off = (i // d) * span + (i % d)
                do_block(p, d, base + off, off)
            return c2

        lax.fori_loop(0, ATT_SUPER // ATT_N, blocks, 0)

        piece = 256

        def merge(j, c2):
            r = pl.ds(pl.multiple_of(j * piece, piece), piece)
            ls = [lsc[p, r, :] for p in range(len(DILATED_PATTERNS))]
            mx = jnp.maximum(jnp.maximum(ls[0], ls[1]), ls[2])
            ws = [jnp.exp(l - mx) for l in ls]
            num = ws[0] * osc[0, r, :] + ws[1] * osc[1, r, :] + ws[2] * osc[2, r, :]
            o_ref[pl.ds(pl.multiple_of(base + j * piece, piece), piece), :] = num / (ws[0] + ws[1] + ws[2])
            return c2

        lax.fori_loop(0, ATT_SUPER // piece, merge, 0)
        return carry

    lax.fori_loop(0, seq_len // ATT_SUPER, superblock, 0)


def _attn_prompt(aq, ak, av):
    bsz, t, _ = aq.shape
    spec = pl.BlockSpec((None, t, LANES), lambda b, p: (b, 0, p))
    n_pat = len(DILATED_PATTERNS)
    return pl.pallas_call(
        functools.partial(_attn_prompt_kernel, seq_len=t),
        grid=(bsz, ATT_WIDTH // LANES),
        in_specs=[spec, spec, spec],
        out_specs=spec,
        out_shape=jax.ShapeDtypeStruct((bsz, t, ATT_WIDTH), F32),
        scratch_shapes=[pltpu.VMEM((n_pat, ATT_SUPER, LANES), F32),
                        pltpu.VMEM((n_pat, ATT_SUPER, LANES), F32)],
        compiler_params=_cparams(("parallel", "parallel")),
        name="attn_prompt",
    )(aq, ak, av)


def _attn_step_kernel(q_ref, kn_ref, vn_ref, ck_ref, cv_ref, o_ref, nk_ref, nv_ref):
    win = ck_ref.shape[0]
    kc, vc = ck_ref[...], cv_ref[...]
    q, kn, vn = q_ref[...], kn_ref[...], vn_ref[...]

    hrow = lax.broadcasted_iota(jnp.int32, (ATT_HEADS, ATT_WIDTH), 0)
    hcol = lax.broadcasted_iota(jnp.int32, (ATT_HEADS, ATT_WIDTH), 1) // ATT_HEAD_DIM
    own = hrow == hcol
    qm = jnp.where(own, q, 0.0) * (ATT_HEAD_DIM ** -0.5)
    s_all = lax.dot_general(qm.astype(BF16), kc.astype(BF16), NT_DIMS, preferred_element_type=F32)
    s_new = jnp.sum(qm * kn, axis=-1, keepdims=True)
    dist = win - lax.broadcasted_iota(jnp.int32, (ATT_HEADS, win), 1)

    ps, pnews, lses = [], [], []
    for w, d in DILATED_PATTERNS:
        on_stride = (dist & (d - 1)) == 0 if d & (d - 1) == 0 else dist % d == 0
        valid = jnp.logical_and(dist <= w, on_stride)
        sm = jnp.where(valid, s_all, NEG)
        m = jnp.maximum(jnp.max(sm, axis=-1, keepdims=True), s_new)
        p = jnp.exp(sm - m)
        pn = jnp.exp(s_new - m)
        den = jnp.sum(p, axis=-1, keepdims=True) + pn
        ps.append(p / den)
        pnews.append(pn / den)
        lses.append(m + jnp.log(den))
    mx = jnp.maximum(jnp.maximum(lses[0], lses[1]), lses[2])
    ws = [jnp.exp(l - mx) for l in lses]
    wsum = ws[0] + ws[1] + ws[2]
    p_tot = (ws[0] * ps[0] + ws[1] * ps[1] + ws[2] * ps[2]) / wsum
    pn_tot = (ws[0] * pnews[0] + ws[1] * pnews[1] + ws[2] * pnews[2]) / wsum
    o = jnp.dot(p_tot.astype(BF16), vc.astype(BF16), preferred_element_type=F32) + pn_tot * vn
    o_ref[...] = jnp.sum(jnp.where(own, o, 0.0), axis=0, keepdims=True)

    last = lax.broadcasted_iota(jnp.int32, (win, ATT_WIDTH), 0) == win - 1
    nk_ref[...] = jnp.where(last, kn, pltpu.roll(kc, win - 1, 0))
    nv_ref[...] = jnp.where(last, vn, pltpu.roll(vc, win - 1, 0))


def _attn_step(aq, ak, av, cache_k, cache_v):
    bsz, win, _ = cache_k.shape
    one = pl.BlockSpec((None, 1, ATT_WIDTH), lambda b: (b, 0, 0))
    cache = pl.BlockSpec((None, win, ATT_WIDTH), lambda b: (b, 0, 0))
    r3 = lambda a: a.reshape(bsz, 1, ATT_WIDTH)
    att, new_k, new_v = pl.pallas_call(
        _attn_step_kernel,
        grid=(bsz,),
        in_specs=[one, one, one, cache, cache],
        out_specs=[one, cache, cache],
        out_shape=[jax.ShapeDtypeStruct((bsz, 1, ATT_WIDTH), F32),
                   jax.ShapeDtypeStruct(cache_k.shape, F32),
                   jax.ShapeDtypeStruct(cache_v.shape, F32)],
        compiler_params=_cparams(("parallel",)),
        name="attn_step",
    )(r3(aq), r3(ak), r3(av), cache_k, cache_v)
    return att.reshape(bsz, ATT_WIDTH), new_k, new_v


ROUTE_TM = 256


def _outproj_kernel(hn_ref, att_ref, x_ref, hn_s_ref, att_s_ref, x_s_ref, w_ref, g_ref, wr_ref,
                    xmid_ref, h2_ref, route_ref):
    is_prompt = pl.program_id(0) < pl.num_programs(0) - 1
    hn = jnp.where(is_prompt, hn_ref[...], hn_s_ref[...])
    att = jnp.where(is_prompt, att_ref[...], att_s_ref[...])
    y = (jnp.dot(hn.astype(BF16), w_ref[0:HG_WIDTH, :], preferred_element_type=F32)
         + jnp.dot(att.astype(BF16), w_ref[HG_WIDTH:, :], preferred_element_type=F32))
    xm = jnp.where(is_prompt, x_ref[...], x_s_ref[...]) + y
    xmid_ref[...] = xm
    ms = jnp.mean(xm * xm, axis=-1, keepdims=True)
    h2 = xm * lax.rsqrt(ms + RMS_EPS) * g_ref[...]
    h2_ref[...] = h2
    lg = jnp.dot(h2, wr_ref[...], precision=HIGHEST, preferred_element_type=F32)
    lane = lax.broadcasted_iota(jnp.int32, lg.shape, 1).astype(F32)
    big = float(LANES)
    gmask = lane < N_GROUPS
    lgg = jnp.where(gmask, lg, NEG)
    mg = jnp.max(lgg, axis=-1, keepdims=True)
    gi = jnp.min(jnp.where(lgg == mg, lane, big), axis=-1, keepdims=True)
    p_grp = 1.0 / jnp.sum(jnp.exp(lgg - mg), axis=-1, keepdims=True)
    lo = N_GROUPS + gi * EXPERTS_PER_GROUP
    emask = jnp.logical_and(lane >= lo, lane < lo + EXPERTS_PER_GROUP)
    le1 = jnp.where(emask, lg, NEG)
    m1 = jnp.max(le1, axis=-1, keepdims=True)
    i1 = jnp.min(jnp.where(le1 == m1, lane, big), axis=-1, keepdims=True)
    le2 = jnp.where(lane == i1, NEG, le1)
    m2 = jnp.max(le2, axis=-1, keepdims=True)
    i2 = jnp.min(jnp.where(le2 == m2, lane, big), axis=-1, keepdims=True)
    r = jnp.exp(m2 - m1)
    g1 = p_grp / (1.0 + r)
    g2 = p_grp * r / (1.0 + r)
    route_ref[...] = jnp.where(lane == 0, i1 - N_GROUPS,
                               jnp.where(lane == 1, i2 - N_GROUPS,
                                         jnp.where(lane == 2, g1, jnp.where(lane == 3, g2, 0.0))))


def _outproj(hn_p, att_p, x_p, hn_s, att_s, x_s, w_out_bf, g_ffn, w_router):
    tm = ROUTE_TM
    n_p = x_p.shape[0]
    n_tiles = n_p // tm + 1
    pad = lambda a: jnp.pad(a, ((0, tm - a.shape[0]), (0, 0)))
    row_p = lambda i: (jnp.minimum(i, n_tiles - 2), 0)
    row = lambda i: (i, 0)
    const = lambda i: (0, 0)
    n_rows = n_tiles * tm
    return pl.pallas_call(
        _outproj_kernel,
        grid=(n_tiles,),
        in_specs=[pl.BlockSpec((tm, HG_WIDTH), row_p),
                  pl.BlockSpec((tm, ATT_WIDTH), row_p),
                  pl.BlockSpec((tm, D_MODEL), row_p),
                  pl.BlockSpec((tm, HG_WIDTH), const),
                  pl.BlockSpec((tm, ATT_WIDTH), const),
                  pl.BlockSpec((tm, D_MODEL), const),
                  pl.BlockSpec((D_MODEL, D_MODEL), const),
                  pl.BlockSpec((1, D_MODEL), const),
                  pl.BlockSpec((D_MODEL, LANES), const)],
        out_specs=[pl.BlockSpec((tm, D_MODEL), row),
                   pl.BlockSpec((tm, D_MODEL), row),
                   pl.BlockSpec((tm, LANES), row)],
        out_shape=[jax.ShapeDtypeStruct((n_rows, D_MODEL), F32),
                   jax.ShapeDtypeStruct((n_rows, D_MODEL), F32),
                   jax.ShapeDtypeStruct((n_rows, LANES), F32)],
        compiler_params=_cparams(("arbitrary",)),
        name="outproj",
    )(hn_p, att_p, x_p, pad(hn_s), pad(att_s), pad(x_s), w_out_bf, g_ffn.reshape(1, D_MODEL), w_router)


def _rank_kernel(route_ref, slot_ref, meta_ref, cnt_ref, base_ref, *, n_tok, blk):
    ph, i = pl.program_id(0), pl.program_id(1)
    tm = route_ref.shape[0]
    lane = lax.broadcasted_iota(jnp.int32, (tm, LANES), 1).astype(F32)
    rowg = i * tm + lax.broadcasted_iota(jnp.int32, (tm, LANES), 0)
    valid = rowg < n_tok
    r = route_ref[...]
    oh0 = jnp.where(jnp.logical_and(valid, lane == r[:, 0:1]), 1.0, 0.0)
    oh1 = jnp.where(jnp.logical_and(valid, lane == r[:, 1:2]), 1.0, 0.0)
    oh = oh0 + oh1
    colsum = jnp.sum(oh, axis=0, keepdims=True)

    @pl.when(jnp.logical_and(ph == 0, i == 0))
    def _zero():
        cnt_ref[...] = jnp.zeros_like(cnt_ref)

    @pl.when(ph == 0)
    def _count():
        cnt_ref[...] += colsum

    @pl.when(jnp.logical_and(ph == 1, i == 0))
    def _starts():
        cnt = cnt_ref[...].astype(jnp.int32)
        shift = blk.bit_length() - 1
        padded = (((cnt + (blk - 1)) >> shift) << shift).astype(F32)
        up = (lax.broadcasted_iota(jnp.int32, (LANES, LANES), 0)
              < lax.broadcasted_iota(jnp.int32, (LANES, LANES), 1)).astype(F32)
        start = jnp.dot(jnp.broadcast_to(padded, (8, LANES)), up, precision=HIGHEST,
                        preferred_element_type=F32)[0:1]
        base_ref[...] = start
        nb = meta_ref.shape[0]
        lane_b = lax.broadcasted_iota(jnp.int32, (nb, LANES), 1)
        blk_start = (lax.broadcasted_iota(jnp.int32, (nb, LANES), 0) * blk).astype(F32)
        ended = jnp.logical_and(start + padded <= blk_start, lane_b < N_EXPERTS)
        be = jnp.minimum(jnp.sum(jnp.where(ended, 1.0, 0.0), axis=-1, keepdims=True), N_EXPERTS - 1.0)
        n_used = jnp.sum(padded, axis=-1, keepdims=True) * (1.0 / blk)
        meta_ref[...] = jnp.where(lane_b == 0, be, jnp.where(lane_b == 1, n_used, 0.0)).astype(jnp.int32)

    @pl.when(ph == 1)
    def _slots():
        before = (lax.broadcasted_iota(jnp.int32, (tm, tm), 1)
                  < lax.broadcasted_iota(jnp.int32, (tm, tm), 0)).astype(BF16)
        pre = jnp.dot(before, oh.astype(BF16), preferred_element_type=F32) + base_ref[...]
        s0 = jnp.sum(oh0 * pre, axis=-1, keepdims=True)
        s1 = jnp.sum(oh1 * pre, axis=-1, keepdims=True)
        slot_ref[...] = jnp.where(lane == 0, s0, jnp.where(lane == 1, s1, 0.0)).astype(jnp.int32)
        base_ref[...] += colsum


def _rank(route, n_tok, blk, nblk):
    n_rows = route.shape[0]
    nb = (nblk + 7) // 8 * 8
    slots, meta = pl.pallas_call(
        functools.partial(_rank_kernel, n_tok=n_tok, blk=blk),
        grid=(2, n_rows // ROUTE_TM),
        in_specs=[pl.BlockSpec((ROUTE_TM, LANES), lambda ph, i: (i, 0))],
        out_specs=[pl.BlockSpec((ROUTE_TM, LANES), lambda ph, i: (i * ph, 0)),
                   pl.BlockSpec((nb, LANES), lambda ph, i: (0, 0))],
        out_shape=[jax.ShapeDtypeStruct((n_rows, LANES), jnp.int32),
                   jax.ShapeDtypeStruct((nb, LANES), jnp.int32)],
        scratch_shapes=[pltpu.VMEM((1, LANES), F32), pltpu.VMEM((1, LANES), F32)],
        compiler_params=_cparams(("arbitrary", "arbitrary")),
        name="rank",
    )(route)
    return slots[:n_tok, :TOP_K_INNER].reshape(-1), meta[:nblk, 0], meta[0:1, 1]


DMA_UNROLL = 8


def _dispatch_kernel(slot_ref, h2_ref, xb_in, xb_out, sem, *, n_tok):
    del xb_in
    i = pl.program_id(0)
    tm = h2_ref.shape[0]
    tail = n_tok % tm

    def push(rows):
        def body(r, c):
            a = (i * tm + r) * TOP_K_INNER
            for k in range(TOP_K_INNER):
                pltpu.make_async_copy(h2_ref.at[pl.ds(r, 1)], xb_out.at[pl.ds(slot_ref[a + k], 1)], sem).start()
            return c
        lax.fori_loop(0, rows, body, 0, unroll=DMA_UNROLL)
        for k in range(TOP_K_INNER):
            pltpu.make_async_copy(h2_ref.at[pl.ds(0, rows)], xb_out.at[pl.ds(0, rows)], sem).wait()

    last = pl.num_programs(0) - 1
    if tail == 0:
        push(tm)
    else:
        @pl.when(i < last)
        def _full():
            push(tm)

        @pl.when(i == last)
        def _tail():
            push(tail)


def _dispatch(slot_flat, h2, n_tok, n_slots):
    tm = ROUTE_TM
    grid_spec = pltpu.PrefetchScalarGridSpec(
        num_scalar_prefetch=1,
        grid=(h2.shape[0] // tm,),
        in_specs=[pl.BlockSpec((tm, D_MODEL), lambda i, s: (i, 0)),
                  pl.BlockSpec(memory_space=pl.ANY)],
        out_specs=pl.BlockSpec(memory_space=pl.ANY),
        scratch_shapes=[pltpu.SemaphoreType.DMA(())],
    )
    return pl.pallas_call(
        functools.partial(_dispatch_kernel, n_tok=n_tok),
        grid_spec=grid_spec,
        out_shape=jax.ShapeDtypeStruct((n_slots, D_MODEL), F32),
        input_output_aliases={2: 0},
        compiler_params=_cparams(("arbitrary",)),
        name="dispatch",
    )(slot_flat, h2, jnp.zeros((n_slots, D_MODEL), F32))


def _expert_kernel(be_ref, nu_ref, x_ref, wg_ref, wu_ref, wd_ref, y_ref, wgb, wub, wdb):
    i = pl.program_id(0)
    e = be_ref[i]
    e_prev = be_ref[jnp.maximum(i - 1, 0)]

    @pl.when(jnp.logical_or(i == 0, e != e_prev))
    def _cast():
        wgb[...] = wg_ref[...].astype(BF16)
        wub[...] = wu_ref[...].astype(BF16)
        wdb[...] = wd_ref[...].astype(BF16)

    @pl.when(i < nu_ref[0])
    def _run():
        x = x_ref[...].astype(BF16)
        a = jnp.dot(x, wgb[...], preferred_element_type=F32)
        u = jnp.dot(x, wub[...], preferred_element_type=F32)
        mid = (a * _sigmoid(a) * u).astype(BF16)
        y_ref[...] = jnp.dot(mid, wdb[...], preferred_element_type=F32)

    @pl.when(i >= nu_ref[0])
    def _skip():
        y_ref[...] = jnp.zeros_like(y_ref)


def _experts(xb, blk_expert, n_used, w_g, w_u, w_d, blk):
    nblk = blk_expert.shape[0]
    wmap = lambda i, be, nu: (be[i], 0, 0)
    grid_spec = pltpu.PrefetchScalarGridSpec(
        num_scalar_prefetch=2,
        grid=(nblk,),
        in_specs=[pl.BlockSpec((blk, D_MODEL), lambda i, be, nu: (jnp.minimum(i, nu[0] - 1), 0)),
                  pl.BlockSpec((None, D_MODEL, D_FF_EXPERT), wmap),
                  pl.BlockSpec((None, D_MODEL, D_FF_EXPERT), wmap),
                  pl.BlockSpec((None, D_FF_EXPERT, D_MODEL), wmap)],
        out_specs=pl.BlockSpec((blk, D_MODEL), lambda i, be, nu: (i, 0)),
        scratch_shapes=[pltpu.VMEM((D_MODEL, D_FF_EXPERT), BF16),
                        pltpu.VMEM((D_MODEL, D_FF_EXPERT), BF16),
                        pltpu.VMEM((D_FF_EXPERT, D_MODEL), BF16)],
    )
    return pl.pallas_call(
        _expert_kernel,
        grid_spec=grid_spec,
        out_shape=jax.ShapeDtypeStruct((nblk * blk, D_MODEL), F32),
        compiler_params=_cparams(("arbitrary",)),
        name="experts",
    )(blk_expert, n_used, xb, w_g, w_u, w_d)


def _final_kernel(slot_ref, x_ref, route_ref, g_ref, yb_hbm, o_ref, ybuf, sem, *, tok0):
    i = pl.program_id(0)
    tm = x_ref.shape[0]

    def gather(j, buf):
        def body(r, c):
            a = (tok0 + j * tm + r) * TOP_K_INNER
            for k in range(TOP_K_INNER):
                pltpu.make_async_copy(yb_hbm.at[pl.ds(slot_ref[a + k], 1)], ybuf.at[buf, k, pl.ds(r, 1)],
                                      sem.at[buf]).start()
            return c
        lax.fori_loop(0, tm, body, 0, unroll=DMA_UNROLL)

    @pl.when(i == 0)
    def _first():
        gather(0, 0)

    @pl.when(i + 1 < pl.num_programs(0))
    def _next():
        gather(i + 1, (i + 1) % 2)

    buf = i % 2
    for k in range(TOP_K_INNER):
        pltpu.make_async_copy(yb_hbm.at[pl.ds(0, tm)], ybuf.at[buf, k], sem.at[buf]).wait()
    route = route_ref[...]
    x = x_ref[...] + (ybuf[buf, 0] * route[:, 2:3] + ybuf[buf, 1] * route[:, 3:4])
    ms = jnp.mean(x * x, axis=-1, keepdims=True)
    o_ref[...] = x * lax.rsqrt(ms + RMS_EPS) * g_ref[...]


def _final(slot_flat, xmid, route, g_final, yb, tok0, n_out, tm):
    blk0 = tok0 // tm
    grid_spec = pltpu.PrefetchScalarGridSpec(
        num_scalar_prefetch=1,
        grid=(n_out // tm,),
        in_specs=[pl.BlockSpec((tm, D_MODEL), lambda i, s: (i + blk0, 0)),
                  pl.BlockSpec((tm, LANES), lambda i, s: (i + blk0, 0)),
                  pl.BlockSpec((1, D_MODEL), lambda i, s: (0, 0)),
                  pl.BlockSpec(memory_space=pl.ANY)],
        out_specs=pl.BlockSpec((tm, D_MODEL), lambda i, s: (i, 0)),
        scratch_shapes=[pltpu.VMEM((2, TOP_K_INNER, tm, D_MODEL), F32),
                        pltpu.SemaphoreType.DMA((2,))],
    )
    return pl.pallas_call(
        functools.partial(_final_kernel, tok0=tok0),
        grid_spec=grid_spec,
        out_shape=jax.ShapeDtypeStruct((n_out, D_MODEL), F32),
        compiler_params=_cparams(("arbitrary",)),
        name="final",
    )(slot_flat, xmid, route, g_final.reshape(1, D_MODEL), yb)


def kernel(x_prompt, x_sample, cache_attn_k, cache_attn_v, state_hgrn, w_in, w_out, hg_lb_logits,
           hg_norm_g, norm_mix_g, norm_ffn_g, norm_final_g, w_route_group, w_route_expert,
           w_expert_gate, w_expert_up, w_expert_down):
    bp, tp, _ = x_prompt.shape
    bs = x_sample.shape[0]
    l = 0
    w_in_bf = w_in[l].astype(BF16)
    w_out_bf = w_out[l].astype(BF16)
    w_router = jnp.concatenate(
        [w_route_group[l],
         jnp.transpose(w_route_expert[l], (1, 0, 2)).reshape(D_MODEL, N_EXPERTS),
         jnp.zeros((D_MODEL, LANES - N_GROUPS - N_EXPERTS), F32)], axis=-1)

    n_p = bp * tp
    xp = x_prompt.reshape(n_p, D_MODEL)
    pos_p = jnp.tile(jnp.arange(tp, dtype=jnp.int32), bp)
    hq, hk, hv, lf, zg, aq, ak, av = _inproj(xp, norm_mix_g[l], w_in_bf, hg_lb_logits, pos_p, 256)
    seq3 = lambda a: a.reshape(bp, tp, HG_WIDTH)
    hn_p, s_fin = _hgrn_prompt(seq3(hq), seq3(hk), seq3(hv), seq3(lf), seq3(zg), hg_norm_g[l])
    att_p = _attn_prompt(seq3(aq), seq3(ak), seq3(av))
    keep = min(MAX_WINDOW, tp)
    heads = lambda a: a.reshape(1, bp, keep, ATT_HEADS, ATT_HEAD_DIM)
    new_k_p = heads(seq3(ak)[:, tp - keep:])
    new_v_p = heads(seq3(av)[:, tp - keep:])

    xs = x_sample.reshape(bs, D_MODEL)
    pos_s = jnp.full((bs,), PAST_LEN, jnp.int32)
    hq, hk, hv, lf, zg, aq, ak, av = _inproj(xs, norm_mix_g[l], w_in_bf, hg_lb_logits, pos_s, bs)
    hn_s, s_new = _hgrn_step(hq, hk, hv, lf, zg, hg_norm_g[l], state_hgrn[l])
    win = cache_attn_k.shape[2]
    att_s, new_k_s, new_v_s = _attn_step(aq, ak, av,
                                         cache_attn_k[l].reshape(bs, win, ATT_WIDTH),
                                         cache_attn_v[l].reshape(bs, win, ATT_WIDTH))
    cache5 = lambda a: a.reshape(1, bs, win, ATT_HEADS, ATT_HEAD_DIM)

    assert n_p % ROUTE_TM == 0 and bs <= ROUTE_TM
    n_tok = n_p + bs
    xmid, h2, route = _outproj(hn_p.reshape(n_p, HG_WIDTH), att_p.reshape(n_p, ATT_WIDTH), xp,
                               hn_s, att_s, xs, w_out_bf, norm_ffn_g[l], w_router)
    blk = MOE_BLOCK
    nblk = (n_tok * TOP_K_INNER + N_EXPERTS * (blk - 1)) // blk + 1
    slot_flat, blk_expert, n_used = _rank(route, n_tok, blk, nblk)
    xb = _dispatch(slot_flat, h2, n_tok, nblk * blk)
    yb = _experts(xb, blk_expert, n_used, w_expert_gate[l], w_expert_up[l], w_expert_down[l], blk)
    y_prompt = _final(slot_flat, xmid, route, norm_final_g, yb, 0, n_p, 256)
    y_sample = _final(slot_flat, xmid, route, norm_final_g, yb, n_p, bs, bs)

    return (y_prompt.reshape(bp, tp, D_MODEL), y_sample.reshape(bs, 1, D_MODEL),
            new_k_p, new_v_p, s_fin[None], cache5(new_k_s), cache5(new_v_s), s_new[None])
```

```python
import functools

import jax
import jax.numpy as jnp
from jax import lax
from jax.experimental import pallas as pl
from jax.experimental.pallas import tpu as pltpu

F32 = jnp.float32
BF16 = jnp.bfloat16

D_MODEL = 1024
HG_WIDTH = 512
HG_HEAD_DIM = 128
HG_HEADS = 4
ATT_WIDTH = 512
ATT_HEAD_DIM = 64
ATT_HEADS = 8
ROPE_DIM = 16
ROPE_THETA = 500000.0
DILATED_PATTERNS = ((128, 1), (512, 4), (2048, 16))
MAX_WINDOW = 2048
PAST_LEN = 16384
N_GROUPS = 8
EXPERTS_PER_GROUP = 8
N_EXPERTS = 64
TOP_K_INNER = 2
D_FF_EXPERT = 512
MOE_BLOCK = 128
IN_COLS = 4 * HG_WIDTH + 3 * ATT_WIDTH
RMS_EPS = 1e-6

LANES = 128
VMEM_LIMIT = 56 * 1024 * 1024
NEG = -1e30
HIGHEST = lax.Precision.HIGHEST
NT_DIMS = (((1,), (1,)), ((), ()))


def _sigmoid(z):
    return 1.0 / (1.0 + jnp.exp(-z))


def _cparams(sem):
    return pltpu.CompilerParams(dimension_semantics=sem, vmem_limit_bytes=VMEM_LIMIT)


def _inproj_kernel(x_ref, g_ref, w_ref, lbl_ref, cos_ref, sa_ref, sb_ref,
                   hq_ref, hk_ref, hv_ref, lf_ref, zg_ref, aq_ref, ak_ref, av_ref):
    x = x_ref[...]
    ms = jnp.mean(x * x, axis=-1, keepdims=True)
    h = (x * lax.rsqrt(ms + RMS_EPS) * g_ref[...]).astype(BF16)

    def mm(c0):
        return jnp.dot(h, w_ref[:, c0:c0 + HG_WIDTH], preferred_element_type=F32)

    lbl = lbl_ref[...]
    le = jnp.exp(lbl - jnp.max(lbl, axis=0, keepdims=True))
    lb = le[0:1, :] / jnp.sum(le, axis=0, keepdims=True)

    zq = mm(0)
    hq_ref[...] = zq * _sigmoid(zq)
    zf = mm(HG_WIDTH)
    f = lb + (1.0 - lb) * _sigmoid(zf)
    hk_ref[...] = 1.0 - f
    lf_ref[...] = jnp.log(f)
    hv_ref[...] = mm(2 * HG_WIDTH)
    zg_ref[...] = mm(3 * HG_WIDTH)

    cos, sa, sb = cos_ref[...], sa_ref[...], sb_ref[...]

    def rope(a, out_ref):
        for j in range(ATT_WIDTH // LANES):
            xj = a[:, j * LANES:(j + 1) * LANES]
            up = pltpu.roll(xj, LANES - ROPE_DIM // 2, 1)
            dn = pltpu.roll(xj, ROPE_DIM // 2, 1)
            out_ref[:, j * LANES:(j + 1) * LANES] = xj * cos + up * sa + dn * sb

    rope(mm(4 * HG_WIDTH), aq_ref)
    rope(mm(4 * HG_WIDTH + ATT_WIDTH), ak_ref)
    av_ref[...] = mm(4 * HG_WIDTH + 2 * ATT_WIDTH)


def _rope_tables(pos):
    half = ROPE_DIM // 2
    inv_freq = ROPE_THETA ** (-jnp.arange(half, dtype=F32) / half)
    ang = pos.astype(F32)[:, None] * inv_freq[None, :]
    cos, sin = jnp.cos(ang), jnp.sin(ang)
    m = pos.shape[0]
    rest = ATT_HEAD_DIM - ROPE_DIM
    one, zero, z8 = jnp.ones((m, rest), F32), jnp.zeros((m, rest), F32), jnp.zeros((m, half), F32)
    reps = LANES // ATT_HEAD_DIM
    c = jnp.tile(jnp.concatenate([cos, cos, one], axis=-1), (1, reps))
    sa = jnp.tile(jnp.concatenate([-sin, z8, zero], axis=-1), (1, reps))
    sb = jnp.tile(jnp.concatenate([z8, sin, zero], axis=-1), (1, reps))
    return c, sa, sb


def _inproj(x2d, g, w_bf, lb_logits, pos, tm):
    m = x2d.shape[0]
    cos, sa, sb = _rope_tables(pos)
    row = lambda i: (i, 0)
    const = lambda i: (0, 0)
    outs = [jax.ShapeDtypeStruct((m, HG_WIDTH), F32)] * 8
    return pl.pallas_call(
        _inproj_kernel,
        grid=(m // tm,),
        in_specs=[pl.BlockSpec((tm, D_MODEL), row),
                  pl.BlockSpec((1, D_MODEL), const),
                  pl.BlockSpec((D_MODEL, IN_COLS), const),
                  pl.BlockSpec(lb_logits.shape, const),
                  pl.BlockSpec((tm, LANES), row),
                  pl.BlockSpec((tm, LANES), row),
                  pl.BlockSpec((tm, LANES), row)],
        out_specs=[pl.BlockSpec((tm, HG_WIDTH), row)] * 8,
        out_shape=outs,
        compiler_params=_cparams(("parallel",)),
        name="inproj",
    )(x2d, g.reshape(1, D_MODEL), w_bf, lb_logits, cos, sa, sb)


HG_C = 128
HG_SB = 16


def _hgrn_kernel(q_ref, k_ref, v_ref, lf_ref, zg_ref, g_ref, hn_ref, sfin_ref, st_ref, *, n_chunks):
    t = pl.program_id(2)

    @pl.when(t == 0)
    def _init():
        st_ref[...] = jnp.zeros_like(st_ref)

    ri = lax.broadcasted_iota(jnp.int32, (HG_C, HG_C), 0)
    ci = lax.broadcasted_iota(jnp.int32, (HG_C, HG_C), 1)
    ltri = (ri >= ci).astype(F32)
    ones_b = jnp.ones((LANES, LANES), BF16)
    n_sb = HG_C // HG_SB
    row_sb = lax.broadcasted_iota(jnp.int32, (n_sb, HG_SB, LANES), 1)
    col_sb = lax.broadcasted_iota(jnp.int32, (n_sb, HG_SB, HG_C), 2)
    lo_sb = lax.broadcasted_iota(jnp.int32, (n_sb, HG_SB, HG_C), 0) * HG_SB
    g = g_ref[...]

    def chunk(c, carry):
        r0 = pl.multiple_of(c * HG_C, HG_C)
        q = q_ref[pl.ds(r0, HG_C), :]
        k = k_ref[pl.ds(r0, HG_C), :]
        v = v_ref[pl.ds(r0, HG_C), :]
        lf = lf_ref[pl.ds(r0, HG_C), :]
        b = jnp.dot(ltri, lf, precision=HIGHEST, preferred_element_type=F32)
        st = st_ref[...]
        vb = v.astype(BF16)
        qb = (q * jnp.exp(b)).astype(BF16)
        o_inter = lax.dot_general(qb, st.astype(BF16), NT_DIMS, preferred_element_type=F32)
        b3, q3, k3, v3 = (a.reshape(n_sb, HG_SB, LANES) for a in (b, q, k, v))
        ps = []
        for s in range(HG_SB):
            d = jnp.where(row_sb >= s, b3 - b3[:, s:s + 1, :], NEG)
            ps.append(q3 * jnp.exp(d) * k3[:, s:s + 1, :])
        p_all = jnp.concatenate(ps, axis=1).reshape(n_sb * HG_SB * HG_SB, LANES).astype(BF16)
        r_all = jnp.dot(p_all, ones_b, preferred_element_type=F32)
        r_all = r_all.reshape(n_sb, HG_SB * HG_SB, LANES)
        o3 = o_inter.reshape(n_sb, HG_SB, LANES)
        for s in range(HG_SB):
            o3 = o3 + r_all[:, s * HG_SB:(s + 1) * HG_SB, :] * v3[:, s:s + 1, :]
        b_ref = jnp.concatenate([b3[0:1, 0:1], b3[:n_sb - 1, HG_SB - 1:HG_SB]], axis=0)
        qs = (q3 * jnp.exp(jnp.minimum(b3 - b_ref, 0.0))).astype(BF16)
        ks = (k[None] * jnp.exp(jnp.minimum(b_ref - b[None], 0.0))).astype(BF16)
        a = lax.dot_general(qs, ks, (((2,), (2,)), ((0,), (0,))), preferred_element_type=F32)
        a = jnp.where(col_sb < lo_sb, a, 0.0).astype(BF16).reshape(HG_C, HG_C)
        o = o3.reshape(HG_C, LANES) + jnp.dot(a, vb, preferred_element_type=F32)
        b_last = b[HG_C - 1:HG_C, :]
        kdec = (k * jnp.exp(b_last - b)).astype(BF16)
        st_ref[...] = st * jnp.exp(b_last) + jnp.dot(v.T.astype(BF16), kdec, preferred_element_type=F32)
        ms = jnp.mean(o * o, axis=-1, keepdims=True)
        zg = zg_ref[pl.ds(r0, HG_C), :]
        hn_ref[pl.ds(r0, HG_C), :] = o * lax.rsqrt(ms + RMS_EPS) * g * (zg * _sigmoid(zg))
        return carry

    lax.fori_loop(0, n_chunks, chunk, 0, unroll=2)

    @pl.when(t == pl.num_programs(2) - 1)
    def _fin():
        sfin_ref[...] = st_ref[...].T


def _hgrn_prompt(hq, hk, hv, lf, zg, g_hg, tb=1024):
    bsz, t, _ = hq.shape
    seq = pl.BlockSpec((None, tb, HG_HEAD_DIM), lambda b, h, i: (b, i, h))
    return pl.pallas_call(
        functools.partial(_hgrn_kernel, n_chunks=tb // HG_C),
        grid=(bsz, HG_HEADS, t // tb),
        in_specs=[seq, seq, seq, seq, seq,
                  pl.BlockSpec((1, HG_HEAD_DIM), lambda b, h, i: (0, h))],
        out_specs=[seq,
                   pl.BlockSpec((None, None, HG_HEAD_DIM, HG_HEAD_DIM), lambda b, h, i: (b, h, 0, 0))],
        out_shape=[jax.ShapeDtypeStruct((bsz, t, HG_WIDTH), F32),
                   jax.ShapeDtypeStruct((bsz, HG_HEADS, HG_HEAD_DIM, HG_HEAD_DIM), F32)],
        scratch_shapes=[pltpu.VMEM((HG_HEAD_DIM, HG_HEAD_DIM), F32)],
        compiler_params=_cparams(("parallel", "parallel", "arbitrary")),
        name="hgrn_prompt",
    )(hq, hk, hv, lf, zg, g_hg.reshape(1, HG_WIDTH))


def _hgrn_step_kernel(q_ref, k_ref, v_ref, lf_ref, zg_ref, g_ref, s_ref, hn_ref, snew_ref):
    row = slice(None)
    zeros = jnp.zeros((HG_HEAD_DIM - 3, HG_HEAD_DIM), F32)
    for h in range(HG_HEADS):
        cs = slice(h * HG_HEAD_DIM, (h + 1) * HG_HEAD_DIM)
        q, k, v = q_ref[row, cs], k_ref[row, cs], v_ref[row, cs]
        f = jnp.exp(lf_ref[row, cs])
        cols = jnp.concatenate([f, k, q, zeros], axis=0).T
        s_new = cols[:, 0:1] * s_ref[h] + cols[:, 1:2] * v
        snew_ref[h] = s_new
        o = jnp.sum(cols[:, 2:3] * s_new, axis=0, keepdims=True)
        ms = jnp.mean(o * o, axis=-1, keepdims=True)
        zg = zg_ref[row, cs]
        hn_ref[row, cs] = o * lax.rsqrt(ms + RMS_EPS) * g_ref[:, cs] * (zg * _sigmoid(zg))


def _hgrn_step(hq, hk, hv, lf, zg, g_hg, state):
    bsz = hq.shape[0]
    one = pl.BlockSpec((None, 1, HG_WIDTH), lambda b: (b, 0, 0))
    st = pl.BlockSpec((None, HG_HEADS, HG_HEAD_DIM, HG_HEAD_DIM), lambda b: (b, 0, 0, 0))
    r3 = lambda a: a.reshape(bsz, 1, HG_WIDTH)
    hn, s_new = pl.pallas_call(
        _hgrn_step_kernel,
        grid=(bsz,),
        in_specs=[one, one, one, one, one, pl.BlockSpec((1, HG_WIDTH), lambda b: (0, 0)), st],
        out_specs=[one, st],
        out_shape=[jax.ShapeDtypeStruct((bsz, 1, HG_WIDTH), F32),
                   jax.ShapeDtypeStruct(state.shape, F32)],
        compiler_params=_cparams(("parallel",)),
        name="hgrn_step",
    )(r3(hq), r3(hk), r3(hv), r3(lf), r3(zg), g_hg.reshape(1, HG_WIDTH), state)
    return hn.reshape(bsz, HG_WIDTH), s_new


ATT_N = 128
ATT_SUPER = 2048
ATT_G = 4


def _attn_prompt_kernel(q_ref, k_ref, v_ref, o_ref, osc, lsc, *, seq_len):
    lane = lax.broadcasted_iota(jnp.int32, (ATT_N, LANES), 1)
    rowi = lax.broadcasted_iota(jnp.int32, (ATT_N, LANES), 0)
    head0 = lane < ATT_HEAD_DIM
    kidx = lax.broadcasted_iota(jnp.int32, (ATT_N, 2 * ATT_N), 1)
    qidx = lax.broadcasted_iota(jnp.int32, (ATT_N, 2 * ATT_N), 0)
    band = jnp.logical_and(kidx >= qidx, kidx <= qidx + ATT_N)
    in_prev = kidx < ATT_N
    scale = ATT_HEAD_DIM ** -0.5

    bidx = lax.broadcasted_iota(jnp.int32, (ATT_G, ATT_N, 2 * ATT_N), 0)
    bqk = (((2,), (2,)), ((0,), (0,)))
    bkd = (((2,), (1,)), ((0,), (0,)))
    rows_g = ATT_G * ATT_N

    def do_group(p, d, base, g):
        span = ATT_N * d
        if d == 1:
            off = g * rows_g
            start = base + off

            def cur(ref):
                return ref[pl.ds(start, rows_g), :].reshape(ATT_G, ATT_N, LANES)

            def prv(ref, c):
                before = ref[pl.ds(jnp.maximum(start - ATT_N, 0), ATT_N), :].astype(BF16)
                return jnp.concatenate([before[None], c[:ATT_G - 1]], axis=0)

            pen = jnp.where(jnp.logical_and(jnp.logical_and(bidx == 0, in_prev), start == 0), NEG, 0.0)
        else:
            per_blk = d // ATT_G
            off = (g // per_blk) * span + (g % per_blk) * ATT_G
            start = base + off
            prev = jnp.maximum(start - span, 0)

            def cur(ref):
                return jnp.stack([ref[pl.ds(start + r, ATT_N, stride=d), :] for r in range(ATT_G)])

            def prv(ref, c):
                return jnp.stack([ref[pl.ds(prev + r, ATT_N, stride=d), :] for r in range(ATT_G)]).astype(BF16)

            pen = jnp.where(jnp.logical_and(in_prev, start < span), NEG, 0.0)
        q = cur(q_ref) * scale
        kc = cur(k_ref).astype(BF16)
        vc = cur(v_ref).astype(BF16)
        kk = jnp.concatenate([prv(k_ref, kc), kc], axis=1)
        vv = jnp.concatenate([prv(v_ref, vc), vc], axis=1)
        res = []
        for hm in (head0, jnp.logical_not(head0)):
            qh = jnp.where(hm, q, 0.0).astype(BF16)
            s = lax.dot_general(qh, kk, bqk, preferred_element_type=F32)
            s = jnp.where(band, s, NEG) + pen
            m = jnp.max(s, axis=-1, keepdims=True)
            pr = jnp.exp(s - m)
            den = jnp.sum(pr, axis=-1, keepdims=True)
            o = lax.dot_general(pr.astype(BF16), vv, bkd, preferred_element_type=F32)
            res.append((o / den, m + jnp.log(den)))
        o = jnp.where(head0, res[0][0], res[1][0])
        lse = jnp.where(head0, res[0][1], res[1][1])
        if d == 1:
            osc[p, pl.ds(off, rows_g), :] = o.reshape(rows_g, LANES)
            lsc[p, pl.ds(off, rows_g), :] = lse.reshape(rows_g, LANES)
        else:
            for r in range(ATT_G):
                osc[p, pl.ds(off + r, ATT_N, stride=d), :] = o[r]
                lsc[p, pl.ds(off + r, ATT_N, stride=d), :] = lse[r]

    def superblock(sb, carry):
        base = sb * ATT_SUPER

        def groups(g, c2):
            for p, (w, d) in enumerate(DILATED_PATTERNS):
                do_group(p, d, base, g)
            return c2

        lax.fori_loop(0, ATT_SUPER // rows_g, groups, 0)

        piece = 256

        def merge(j, c2):
            r = pl.ds(pl.multiple_of(j * piece, piece), piece)
            ls = [lsc[p, r, :] for p in range(len(DILATED_PATTERNS))]
            mx = jnp.maximum(jnp.maximum(ls[0], ls[1]), ls[2])
            ws = [jnp.exp(l - mx) for l in ls]
            num = ws[0] * osc[0, r, :] + ws[1] * osc[1, r, :] + ws[2] * osc[2, r, :]
            o_ref[pl.ds(pl.multiple_of(base + j * piece, piece), piece), :] = num / (ws[0] + ws[1] + ws[2])
            return c2

        lax.fori_loop(0, ATT_SUPER // piece, merge, 0)
        return carry

    lax.fori_loop(0, seq_len // ATT_SUPER, superblock, 0)


def _attn_prompt(aq, ak, av):
    bsz, t, _ = aq.shape
    spec = pl.BlockSpec((None, t, LANES), lambda b, p: (b, 0, p))
    n_pat = len(DILATED_PATTERNS)
    return pl.pallas_call(
        functools.partial(_attn_prompt_kernel, seq_len=t),
        grid=(bsz, ATT_WIDTH // LANES),
        in_specs=[spec, spec, spec],
        out_specs=spec,
        out_shape=jax.ShapeDtypeStruct((bsz, t, ATT_WIDTH), F32),
        scratch_shapes=[pltpu.VMEM((n_pat, ATT_SUPER, LANES), F32),
                        pltpu.VMEM((n_pat, ATT_SUPER, LANES), F32)],
        compiler_params=_cparams(("parallel", "parallel")),
        name="attn_prompt",
    )(aq, ak, av)


def _attn_step_kernel(q_ref, kn_ref, vn_ref, ck_ref, cv_ref, o_ref, nk_ref, nv_ref):
    win = ck_ref.shape[0]
    kc, vc = ck_ref[...], cv_ref[...]
    q, kn, vn = q_ref[...], kn_ref[...], vn_ref[...]

    hrow = lax.broadcasted_iota(jnp.int32, (ATT_HEADS, ATT_WIDTH), 0)
    hcol = lax.broadcasted_iota(jnp.int32, (ATT_HEADS, ATT_WIDTH), 1) // ATT_HEAD_DIM
    own = hrow == hcol
    qm = jnp.where(own, q, 0.0) * (ATT_HEAD_DIM ** -0.5)
    s_all = lax.dot_general(qm.astype(BF16), kc.astype(BF16), NT_DIMS, preferred_element_type=F32)
    s_new = jnp.sum(qm * kn, axis=-1, keepdims=True)
    dist = win - lax.broadcasted_iota(jnp.int32, (ATT_HEADS, win), 1)

    ps, pnews, lses = [], [], []
    for w, d in DILATED_PATTERNS:
        on_stride = (dist & (d - 1)) == 0 if d & (d - 1) == 0 else dist % d == 0
        valid = jnp.logical_and(dist <= w, on_stride)
        sm = jnp.where(valid, s_all, NEG)
        m = jnp.maximum(jnp.max(sm, axis=-1, keepdims=True), s_new)
        p = jnp.exp(sm - m)
        pn = jnp.exp(s_new - m)
        den = jnp.sum(p, axis=-1, keepdims=True) + pn
        ps.append(p / den)
        pnews.append(pn / den)
        lses.append(m + jnp.log(den))
    mx = jnp.maximum(jnp.maximum(lses[0], lses[1]), lses[2])
    ws = [jnp.exp(l - mx) for l in lses]
    wsum = ws[0] + ws[1] + ws[2]
    p_tot = (ws[0] * ps[0] + ws[1] * ps[1] + ws[2] * ps[2]) / wsum
    pn_tot = (ws[0] * pnews[0] + ws[1] * pnews[1] + ws[2] * pnews[2]) / wsum
    o = jnp.dot(p_tot.astype(BF16), vc.astype(BF16), preferred_element_type=F32) + pn_tot * vn
    o_ref[...] = jnp.sum(jnp.where(own, o, 0.0), axis=0, keepdims=True)

    last = lax.broadcasted_iota(jnp.int32, (win, ATT_WIDTH), 0) == win - 1
    nk_ref[...] = jnp.where(last, kn, pltpu.roll(kc, win - 1, 0))
    nv_ref[...] = jnp.where(last, vn, pltpu.roll(vc, win - 1, 0))


def _attn_step(aq, ak, av, cache_k, cache_v):
    bsz, win, _ = cache_k.shape
    one = pl.BlockSpec((None, 1, ATT_WIDTH), lambda b: (b, 0, 0))
    cache = pl.BlockSpec((None, win, ATT_WIDTH), lambda b: (b, 0, 0))
    r3 = lambda a: a.reshape(bsz, 1, ATT_WIDTH)
    att, new_k, new_v = pl.pallas_call(
        _attn_step_kernel,
        grid=(bsz,),
        in_specs=[one, one, one, cache, cache],
        out_specs=[one, cache, cache],
        out_shape=[jax.ShapeDtypeStruct((bsz, 1, ATT_WIDTH), F32),
                   jax.ShapeDtypeStruct(cache_k.shape, F32),
                   jax.ShapeDtypeStruct(cache_v.shape, F32)],
        compiler_params=_cparams(("parallel",)),
        name="attn_step",
    )(r3(aq), r3(ak), r3(av), cache_k, cache_v)
    return att.reshape(bsz, ATT_WIDTH), new_k, new_v


ROUTE_TM = 256


def _outproj_kernel(hn_ref, att_ref, x_ref, hn_s_ref, att_s_ref, x_s_ref, w_ref, g_ref, wr_ref,
                    xmid_ref, h2_ref, route_ref):
    is_prompt = pl.program_id(0) < pl.num_programs(0) - 1
    hn = jnp.where(is_prompt, hn_ref[...], hn_s_ref[...])
    att = jnp.where(is_prompt, att_ref[...], att_s_ref[...])
    y = (jnp.dot(hn.astype(BF16), w_ref[0:HG_WIDTH, :], preferred_element_type=F32)
         + jnp.dot(att.astype(BF16), w_ref[HG_WIDTH:, :], preferred_element_type=F32))
    xm = jnp.where(is_prompt, x_ref[...], x_s_ref[...]) + y
    xmid_ref[...] = xm
    ms = jnp.mean(xm * xm, axis=-1, keepdims=True)
    h2 = xm * lax.rsqrt(ms + RMS_EPS) * g_ref[...]
    h2_ref[...] = h2
    lg = jnp.dot(h2, wr_ref[...], precision=HIGHEST, preferred_element_type=F32)
    lane = lax.broadcasted_iota(jnp.int32, lg.shape, 1).astype(F32)
    big = float(LANES)
    gmask = lane < N_GROUPS
    lgg = jnp.where(gmask, lg, NEG)
    mg = jnp.max(lgg, axis=-1, keepdims=True)
    gi = jnp.min(jnp.where(lgg == mg, lane, big), axis=-1, keepdims=True)
    p_grp = 1.0 / jnp.sum(jnp.exp(lgg - mg), axis=-1, keepdims=True)
    lo = N_GROUPS + gi * EXPERTS_PER_GROUP
    emask = jnp.logical_and(lane >= lo, lane < lo + EXPERTS_PER_GROUP)
    le1 = jnp.where(emask, lg, NEG)
    m1 = jnp.max(le1, axis=-1, keepdims=True)
    i1 = jnp.min(jnp.where(le1 == m1, lane, big), axis=-1, keepdims=True)
    le2 = jnp.where(lane == i1, NEG, le1)
    m2 = jnp.max(le2, axis=-1, keepdims=True)
    i2 = jnp.min(jnp.where(le2 == m2, lane, big), axis=-1, keepdims=True)
    r = jnp.exp(m2 - m1)
    g1 = p_grp / (1.0 + r)
    g2 = p_grp * r / (1.0 + r)
    route_ref[...] = jnp.where(lane == 0, i1 - N_GROUPS,
                               jnp.where(lane == 1, i2 - N_GROUPS,
                                         jnp.where(lane == 2, g1, jnp.where(lane == 3, g2, 0.0))))


def _outproj(hn_p, att_p, x_p, hn_s, att_s, x_s, w_out_bf, g_ffn, w_router):
    tm = ROUTE_TM
    n_p = x_p.shape[0]
    n_tiles = n_p // tm + 1
    pad = lambda a: jnp.pad(a, ((0, tm - a.shape[0]), (0, 0)))
    row_p = lambda i: (jnp.minimum(i, n_tiles - 2), 0)
    row = lambda i: (i, 0)
    const = lambda i: (0, 0)
    n_rows = n_tiles * tm
    return pl.pallas_call(
        _outproj_kernel,
        grid=(n_tiles,),
        in_specs=[pl.BlockSpec((tm, HG_WIDTH), row_p),
                  pl.BlockSpec((tm, ATT_WIDTH), row_p),
                  pl.BlockSpec((tm, D_MODEL), row_p),
                  pl.BlockSpec((tm, HG_WIDTH), const),
                  pl.BlockSpec((tm, ATT_WIDTH), const),
                  pl.BlockSpec((tm, D_MODEL), const),
                  pl.BlockSpec((D_MODEL, D_MODEL), const),
                  pl.BlockSpec((1, D_MODEL), const),
                  pl.BlockSpec((D_MODEL, LANES), const)],
        out_specs=[pl.BlockSpec((tm, D_MODEL), row),
                   pl.BlockSpec((tm, D_MODEL), row),
                   pl.BlockSpec((tm, LANES), row)],
        out_shape=[jax.ShapeDtypeStruct((n_rows, D_MODEL), F32),
                   jax.ShapeDtypeStruct((n_rows, D_MODEL), F32),
                   jax.ShapeDtypeStruct((n_rows, LANES), F32)],
        compiler_params=_cparams(("arbitrary",)),
        name="outproj",
    )(hn_p, att_p, x_p, pad(hn_s), pad(att_s), pad(x_s), w_out_bf, g_ffn.reshape(1, D_MODEL), w_router)


def _rank_kernel(route_ref, slot_ref, meta_ref, cnt_ref, base_ref, *, n_tok, blk):
    ph, i = pl.program_id(0), pl.program_id(1)
    tm = route_ref.shape[0]
    lane = lax.broadcasted_iota(jnp.int32, (tm, LANES), 1).astype(F32)
    rowg = i * tm + lax.broadcasted_iota(jnp.int32, (tm, LANES), 0)
    valid = rowg < n_tok
    r = route_ref[...]
    oh0 = jnp.where(jnp.logical_and(valid, lane == r[:, 0:1]), 1.0, 0.0)
    oh1 = jnp.where(jnp.logical_and(valid, lane == r[:, 1:2]), 1.0, 0.0)
    oh = oh0 + oh1
    colsum = jnp.sum(oh, axis=0, keepdims=True)

    @pl.when(jnp.logical_and(ph == 0, i == 0))
    def _zero():
        cnt_ref[...] = jnp.zeros_like(cnt_ref)

    @pl.when(ph == 0)
    def _count():
        cnt_ref[...] += colsum

    @pl.when(jnp.logical_and(ph == 1, i == 0))
    def _starts():
        cnt = cnt_ref[...].astype(jnp.int32)
        shift = blk.bit_length() - 1
        padded = (((cnt + (blk - 1)) >> shift) << shift).astype(F32)
        up = (lax.broadcasted_iota(jnp.int32, (LANES, LANES), 0)
              < lax.broadcasted_iota(jnp.int32, (LANES, LANES), 1)).astype(F32)
        start = jnp.dot(jnp.broadcast_to(padded, (8, LANES)), up, precision=HIGHEST,
                        preferred_element_type=F32)[0:1]
        base_ref[...] = start
        nb = meta_ref.shape[0]
        lane_b = lax.broadcasted_iota(jnp.int32, (nb, LANES), 1)
        blk_start = (lax.broadcasted_iota(jnp.int32, (nb, LANES), 0) * blk).astype(F32)
        ended = jnp.logical_and(start + padded <= blk_start, lane_b < N_EXPERTS)
        be = jnp.minimum(jnp.sum(jnp.where(ended, 1.0, 0.0), axis=-1, keepdims=True), N_EXPERTS - 1.0)
        n_used = jnp.sum(padded, axis=-1, keepdims=True) * (1.0 / blk)
        meta_ref[...] = jnp.where(lane_b == 0, be, jnp.where(lane_b == 1, n_used, 0.0)).astype(jnp.int32)

    @pl.when(ph == 1)
    def _slots():
        before = (lax.broadcasted_iota(jnp.int32, (tm, tm), 1)
                  < lax.broadcasted_iota(jnp.int32, (tm, tm), 0)).astype(BF16)
        pre = jnp.dot(before, oh.astype(BF16), preferred_element_type=F32) + base_ref[...]
        s0 = jnp.sum(oh0 * pre, axis=-1, keepdims=True)
        s1 = jnp.sum(oh1 * pre, axis=-1, keepdims=True)
        slot_ref[...] = jnp.where(lane == 0, s0, jnp.where(lane == 1, s1, 0.0)).astype(jnp.int32)
        base_ref[...] += colsum


def _rank(route, n_tok, blk, nblk):
    n_rows = route.shape[0]
    nb = (nblk + 7) // 8 * 8
    slots, meta = pl.pallas_call(
        functools.partial(_rank_kernel, n_tok=n_tok, blk=blk),
        grid=(2, n_rows // ROUTE_TM),
        in_specs=[pl.BlockSpec((ROUTE_TM, LANES), lambda ph, i: (i, 0))],
        out_specs=[pl.BlockSpec((ROUTE_TM, LANES), lambda ph, i: (i * ph, 0)),
                   pl.BlockSpec((nb, LANES), lambda ph, i: (0, 0))],
        out_shape=[jax.ShapeDtypeStruct((n_rows, LANES), jnp.int32),
                   jax.ShapeDtypeStruct((nb, LANES), jnp.int32)],
        scratch_shapes=[pltpu.VMEM((1, LANES), F32), pltpu.VMEM((1, LANES), F32)],
        compiler_params=_cparams(("arbitrary", "arbitrary")),
        name="rank",
    )(route)
    return slots[:n_tok, :TOP_K_INNER].reshape(-1), meta[:nblk, 0], meta[0:1, 1]


DMA_UNROLL = 8


def _dispatch_kernel(slot_ref, h2_ref, xb_in, xb_out, sem, *, n_tok):
    del xb_in
    i = pl.program_id(0)
    tm = h2_ref.shape[0]
    tail = n_tok % tm

    def push(rows):
        def body(r, c):
            a = (i * tm + r) * TOP_K_INNER
            for k in range(TOP_K_INNER):
                pltpu.make_async_copy(h2_ref.at[pl.ds(r, 1)], xb_out.at[pl.ds(slot_ref[a + k], 1)], sem).start()
            return c
        lax.fori_loop(0, rows, body, 0, unroll=DMA_UNROLL)
        for k in range(TOP_K_INNER):
            pltpu.make_async_copy(h2_ref.at[pl.ds(0, rows)], xb_out.at[pl.ds(0, rows)], sem).wait()

    last = pl.num_programs(0) - 1
    if tail == 0:
        push(tm)
    else:
        @pl.when(i < last)
        def _full():
            push(tm)

        @pl.when(i == last)
        def _tail():
            push(tail)


def _dispatch(slot_flat, h2, n_tok, n_slots):
    tm = ROUTE_TM
    grid_spec = pltpu.PrefetchScalarGridSpec(
        num_scalar_prefetch=1,
        grid=(h2.shape[0] // tm,),
        in_specs=[pl.BlockSpec((tm, D_MODEL), lambda i, s: (i, 0)),
                  pl.BlockSpec(memory_space=pl.ANY)],
        out_specs=pl.BlockSpec(memory_space=pl.ANY),
        scratch_shapes=[pltpu.SemaphoreType.DMA(())],
    )
    return pl.pallas_call(
        functools.partial(_dispatch_kernel, n_tok=n_tok),
        grid_spec=grid_spec,
        out_shape=jax.ShapeDtypeStruct((n_slots, D_MODEL), F32),
        input_output_aliases={2: 0},
        compiler_params=_cparams(("arbitrary",)),
        name="dispatch",
    )(slot_flat, h2, jnp.zeros((n_slots, D_MODEL), F32))


def _expert_kernel(be_ref, nu_ref, x_ref, wg_ref, wu_ref, wd_ref, y_ref, wgb, wub, wdb):
    i = pl.program_id(0)
    e = be_ref[i]
    e_prev = be_ref[jnp.maximum(i - 1, 0)]

    @pl.when(jnp.logical_or(i == 0, e != e_prev))
    def _cast():
        wgb[...] = wg_ref[...].astype(BF16)
        wub[...] = wu_ref[...].astype(BF16)
        wdb[...] = wd_ref[...].astype(BF16)

    @pl.when(i < nu_ref[0])
    def _run():
        x = x_ref[...].astype(BF16)
        a = jnp.dot(x, wgb[...], preferred_element_type=F32)
        u = jnp.dot(x, wub[...], preferred_element_type=F32)
        mid = (a * _sigmoid(a) * u).astype(BF16)
        y_ref[...] = jnp.dot(mid, wdb[...], preferred_element_type=F32)

    @pl.when(i >= nu_ref[0])
    def _skip():
        y_ref[...] = jnp.zeros_like(y_ref)


def _experts(xb, blk_expert, n_used, w_g, w_u, w_d, blk):
    nblk = blk_expert.shape[0]
    wmap = lambda i, be, nu: (be[i], 0, 0)
    grid_spec = pltpu.PrefetchScalarGridSpec(
        num_scalar_prefetch=2,
        grid=(nblk,),
        in_specs=[pl.BlockSpec((blk, D_MODEL), lambda i, be, nu: (jnp.minimum(i, nu[0] - 1), 0)),
                  pl.BlockSpec((None, D_MODEL, D_FF_EXPERT), wmap),
                  pl.BlockSpec((None, D_MODEL, D_FF_EXPERT), wmap),
                  pl.BlockSpec((None, D_FF_EXPERT, D_MODEL), wmap)],
        out_specs=pl.BlockSpec((blk, D_MODEL), lambda i, be, nu: (i, 0)),
        scratch_shapes=[pltpu.VMEM((D_MODEL, D_FF_EXPERT), BF16),
                        pltpu.VMEM((D_MODEL, D_FF_EXPERT), BF16),
                        pltpu.VMEM((D_FF_EXPERT, D_MODEL), BF16)],
    )
    return pl.pallas_call(
        _expert_kernel,
        grid_spec=grid_spec,
        out_shape=jax.ShapeDtypeStruct((nblk * blk, D_MODEL), F32),
        compiler_params=_cparams(("arbitrary",)),
        name="experts",
    )(blk_expert, n_used, xb, w_g, w_u, w_d)


def _final_kernel(slot_ref, x_ref, route_ref, g_ref, yb_hbm, o_ref, ybuf, sem, *, tok0):
    i = pl.program_id(0)
    tm = x_ref.shape[0]

    def gather(j, buf):
        def body(r, c):
            a = (tok0 + j * tm + r) * TOP_K_INNER
            for k in range(TOP_K_INNER):
                pltpu.make_async_copy(yb_hbm.at[pl.ds(slot_ref[a + k], 1)], ybuf.at[buf, k, pl.ds(r, 1)],
                                      sem.at[buf]).start()
            return c
        lax.fori_loop(0, tm, body, 0, unroll=DMA_UNROLL)

    @pl.when(i == 0)
    def _first():
        gather(0, 0)

    @pl.when(i + 1 < pl.num_programs(0))
    def _next():
        gather(i + 1, (i + 1) % 2)

    buf = i % 2
    for k in range(TOP_K_INNER):
        pltpu.make_async_copy(yb_hbm.at[pl.ds(0, tm)], ybuf.at[buf, k], sem.at[buf]).wait()
    route = route_ref[...]
    x = x_ref[...] + (ybuf[buf, 0] * route[:, 2:3] + ybuf[buf, 1] * route[:, 3:4])
    ms = jnp.mean(x * x, axis=-1, keepdims=True)
    o_ref[...] = x * lax.rsqrt(ms + RMS_EPS) * g_ref[...]


def _final(slot_flat, xmid, route, g_final, yb, tok0, n_out, tm):
    blk0 = tok0 // tm
    grid_spec = pltpu.PrefetchScalarGridSpec(
        num_scalar_prefetch=1,
        grid=(n_out // tm,),
        in_specs=[pl.BlockSpec((tm, D_MODEL), lambda i, s: (i + blk0, 0)),
                  pl.BlockSpec((tm, LANES), lambda i, s: (i + blk0, 0)),
                  pl.BlockSpec((1, D_MODEL), lambda i, s: (0, 0)),
                  pl.BlockSpec(memory_space=pl.ANY)],
        out_specs=pl.BlockSpec((tm, D_MODEL), lambda i, s: (i, 0)),
        scratch_shapes=[pltpu.VMEM((2, TOP_K_INNER, tm, D_MODEL), F32),
                        pltpu.SemaphoreType.DMA((2,))],
    )
    return pl.pallas_call(
        functools.partial(_final_kernel, tok0=tok0),
        grid_spec=grid_spec,
        out_shape=jax.ShapeDtypeStruct((n_out, D_MODEL), F32),
        compiler_params=_cparams(("arbitrary",)),
        name="final",
    )(slot_flat, xmid, route, g_final.reshape(1, D_MODEL), yb)


def kernel(x_prompt, x_sample, cache_attn_k, cache_attn_v, state_hgrn, w_in, w_out, hg_lb_logits,
           hg_norm_g, norm_mix_g, norm_ffn_g, norm_final_g, w_route_group, w_route_expert,
           w_expert_gate, w_expert_up, w_expert_down):
    bp, tp, _ = x_prompt.shape
    bs = x_sample.shape[0]
    l = 0
    w_in_bf = w_in[l].astype(BF16)
    w_out_bf = w_out[l].astype(BF16)
    w_router = jnp.concatenate(
        [w_route_group[l],
         jnp.transpose(w_route_expert[l], (1, 0, 2)).reshape(D_MODEL, N_EXPERTS),
         jnp.zeros((D_MODEL, LANES - N_GROUPS - N_EXPERTS), F32)], axis=-1)

    n_p = bp * tp
    xp = x_prompt.reshape(n_p, D_MODEL)
    pos_p = jnp.tile(jnp.arange(tp, dtype=jnp.int32), bp)
    hq, hk, hv, lf, zg, aq, ak, av = _inproj(xp, norm_mix_g[l], w_in_bf, hg_lb_logits, pos_p, 256)
    seq3 = lambda a: a.reshape(bp, tp, HG_WIDTH)
    hn_p, s_fin = _hgrn_prompt(seq3(hq), seq3(hk), seq3(hv), seq3(lf), seq3(zg), hg_norm_g[l])
    att_p = _attn_prompt(seq3(aq), seq3(ak), seq3(av))
    keep = min(MAX_WINDOW, tp)
    heads = lambda a: a.reshape(1, bp, keep, ATT_HEADS, ATT_HEAD_DIM)
    new_k_p = heads(seq3(ak)[:, tp - keep:])
    new_v_p = heads(seq3(av)[:, tp - keep:])

    xs = x_sample.reshape(bs, D_MODEL)
    pos_s = jnp.full((bs,), PAST_LEN, jnp.int32)
    hq, hk, hv, lf, zg, aq, ak, av = _inproj(xs, norm_mix_g[l], w_in_bf, hg_lb_logits, pos_s, bs)
    hn_s, s_new = _hgrn_step(hq, hk, hv, lf, zg, hg_norm_g[l], state_hgrn[l])
    win = cache_attn_k.shape[2]
    att_s, new_k_s, new_v_s = _attn_step(aq, ak, av,
                                         cache_attn_k[l].reshape(bs, win, ATT_WIDTH),
                                         cache_attn_v[l].reshape(bs, win, ATT_WIDTH))
    cache5 = lambda a: a.reshape(1, bs, win, ATT_HEADS, ATT_HEAD_DIM)

    assert n_p % ROUTE_TM == 0 and bs <= ROUTE_TM
    n_tok = n_p + bs
    xmid, h2, route = _outproj(hn_p.reshape(n_p, HG_WIDTH), att_p.reshape(n_p, ATT_WIDTH), xp,
                               hn_s, att_s, xs, w_out_bf, norm_ffn_g[l], w_router)
    blk = MOE_BLOCK
    nblk = (n_tok * TOP_K_INNER + N_EXPERTS * (blk - 1)) // blk + 1
    slot_flat, blk_expert, n_used = _rank(route, n_tok, blk, nblk)
    xb = _dispatch(slot_flat, h2, n_tok, nblk * blk)
    yb = _experts(xb, blk_expert, n_used, w_expert_gate[l], w_expert_up[l], w_expert_down[l], blk)
    y_prompt = _final(slot_flat, xmid, route, norm_final_g, yb, 0, n_p, 256)
    y_sample = _final(slot_flat, xmid, route, norm_final_g, yb, n_p, bs, bs)

    return (y_prompt.reshape(bp, tp, D_MODEL), y_sample.reshape(bs, 1, D_MODEL),
            new_k_p, new_v_p, s_fin[None], cache5(new_k_s), cache5(new_v_s), s_new[None])
```

```python
import functools

import jax
import jax.numpy as jnp
from jax import lax
from jax.experimental import pallas as pl
from jax.experimental.pallas import tpu as pltpu

F32 = jnp.float32
BF16 = jnp.bfloat16

D_MODEL = 1024
HG_WIDTH = 512
HG_HEAD_DIM = 128
HG_HEADS = 4
ATT_WIDTH = 512
ATT_HEAD_DIM = 64
ATT_HEADS = 8
ROPE_DIM = 16
ROPE_THETA = 500000.0
DILATED_PATTERNS = ((128, 1), (512, 4), (2048, 16))
MAX_WINDOW = 2048
PAST_LEN = 16384
N_GROUPS = 8
EXPERTS_PER_GROUP = 8
N_EXPERTS = 64
TOP_K_INNER = 2
D_FF_EXPERT = 512
MOE_BLOCK = 128
IN_COLS = 4 * HG_WIDTH + 3 * ATT_WIDTH
RMS_EPS = 1e-6

LANES = 128
VMEM_LIMIT = 56 * 1024 * 1024
NEG = -1e30
HIGHEST = lax.Precision.HIGHEST
NT_DIMS = (((1,), (1,)), ((), ()))


def _sigmoid(z):
    return 1.0 / (1.0 + jnp.exp(-z))


def _cparams(sem):
    return pltpu.CompilerParams(dimension_semantics=sem, vmem_limit_bytes=VMEM_LIMIT)


def _inproj_kernel(x_ref, g_ref, w_ref, lbl_ref, cos_ref, sa_ref, sb_ref,
                   hq_ref, hk_ref, hv_ref, lf_ref, zg_ref, aq_ref, ak_ref, av_ref):
    x = x_ref[...]
    ms = jnp.mean(x * x, axis=-1, keepdims=True)
    h = (x * lax.rsqrt(ms + RMS_EPS) * g_ref[...]).astype(BF16)

    def mm(c0):
        return jnp.dot(h, w_ref[:, c0:c0 + HG_WIDTH], preferred_element_type=F32)

    lbl = lbl_ref[...]
    le = jnp.exp(lbl - jnp.max(lbl, axis=0, keepdims=True))
    lb = le[0:1, :] / jnp.sum(le, axis=0, keepdims=True)

    zq = mm(0)
    hq_ref[...] = zq * _sigmoid(zq)
    zf = mm(HG_WIDTH)
    f = lb + (1.0 - lb) * _sigmoid(zf)
    hk_ref[...] = 1.0 - f
    lf_ref[...] = jnp.log(f)
    hv_ref[...] = mm(2 * HG_WIDTH)
    zg_ref[...] = mm(3 * HG_WIDTH)

    cos, sa, sb = cos_ref[...], sa_ref[...], sb_ref[...]

    def rope(a, out_ref):
        for j in range(ATT_WIDTH // LANES):
            xj = a[:, j * LANES:(j + 1) * LANES]
            up = pltpu.roll(xj, LANES - ROPE_DIM // 2, 1)
            dn = pltpu.roll(xj, ROPE_DIM // 2, 1)
            out_ref[:, j * LANES:(j + 1) * LANES] = xj * cos + up * sa + dn * sb

    rope(mm(4 * HG_WIDTH), aq_ref)
    rope(mm(4 * HG_WIDTH + ATT_WIDTH), ak_ref)
    av_ref[...] = mm(4 * HG_WIDTH + 2 * ATT_WIDTH)


def _rope_tables(pos):
    half = ROPE_DIM // 2
    inv_freq = ROPE_THETA ** (-jnp.arange(half, dtype=F32) / half)
    ang = pos.astype(F32)[:, None] * inv_freq[None, :]
    cos, sin = jnp.cos(ang), jnp.sin(ang)
    m = pos.shape[0]
    rest = ATT_HEAD_DIM - ROPE_DIM
    one, zero, z8 = jnp.ones((m, rest), F32), jnp.zeros((m, rest), F32), jnp.zeros((m, half), F32)
    reps = LANES // ATT_HEAD_DIM
    c = jnp.tile(jnp.concatenate([cos, cos, one], axis=-1), (1, reps))
    sa = jnp.tile(jnp.concatenate([-sin, z8, zero], axis=-1), (1, reps))
    sb = jnp.tile(jnp.concatenate([z8, sin, zero], axis=-1), (1, reps))
    return c, sa, sb


def _inproj(x2d, g, w_bf, lb_logits, pos, tm):
    m = x2d.shape[0]
    cos, sa, sb = _rope_tables(pos)
    row = lambda i: (i, 0)
    const = lambda i: (0, 0)
    outs = [jax.ShapeDtypeStruct((m, HG_WIDTH), F32)] * 8
    return pl.pallas_call(
        _inproj_kernel,
        grid=(m // tm,),
        in_specs=[pl.BlockSpec((tm, D_MODEL), row),
                  pl.BlockSpec((1, D_MODEL), const),
                  pl.BlockSpec((D_MODEL, IN_COLS), const),
                  pl.BlockSpec(lb_logits.shape, const),
                  pl.BlockSpec((tm, LANES), row),
                  pl.BlockSpec((tm, LANES), row),
                  pl.BlockSpec((tm, LANES), row)],
        out_specs=[pl.BlockSpec((tm, HG_WIDTH), row)] * 8,
        out_shape=outs,
        compiler_params=_cparams(("parallel",)),
        name="inproj",
    )(x2d, g.reshape(1, D_MODEL), w_bf, lb_logits, cos, sa, sb)


HG_C = 128
HG_SB = 16


def _hgrn_kernel(q_ref, k_ref, v_ref, lf_ref, zg_ref, g_ref, hn_ref, sfin_ref, st_ref, *, n_chunks):
    t = pl.program_id(2)

    @pl.when(t == 0)
    def _init():
        st_ref[...] = jnp.zeros_like(st_ref)

    ri = lax.broadcasted_iota(jnp.int32, (HG_C, HG_C), 0)
    ci = lax.broadcasted_iota(jnp.int32, (HG_C, HG_C), 1)
    ltri = (ri >= ci).astype(F32)
    ones_b = jnp.ones((LANES, LANES), BF16)
    n_sb = HG_C // HG_SB
    row_sb = lax.broadcasted_iota(jnp.int32, (n_sb, HG_SB, LANES), 1)
    col_sb = lax.broadcasted_iota(jnp.int32, (n_sb, HG_SB, HG_C), 2)
    lo_sb = lax.broadcasted_iota(jnp.int32, (n_sb, HG_SB, HG_C), 0) * HG_SB
    g = g_ref[...]

    def chunk(c, carry):
        r0 = pl.multiple_of(c * HG_C, HG_C)
        q = q_ref[pl.ds(r0, HG_C), :]
        k = k_ref[pl.ds(r0, HG_C), :]
        v = v_ref[pl.ds(r0, HG_C), :]
        lf = lf_ref[pl.ds(r0, HG_C), :]
        b = jnp.dot(ltri, lf, precision=HIGHEST, preferred_element_type=F32)
        st = st_ref[...]
        vb = v.astype(BF16)
        qb = (q * jnp.exp(b)).astype(BF16)
        o_inter = lax.dot_general(qb, st.astype(BF16), NT_DIMS, preferred_element_type=F32)
        b3, q3, k3, v3 = (a.reshape(n_sb, HG_SB, LANES) for a in (b, q, k, v))
        ps = []
        for s in range(HG_SB):
            d = jnp.where(row_sb >= s, b3 - b3[:, s:s + 1, :], NEG)
            ps.append(q3 * jnp.exp(d) * k3[:, s:s + 1, :])
        p_all = jnp.concatenate(ps, axis=1).reshape(n_sb * HG_SB * HG_SB, LANES).astype(BF16)
        r_all = jnp.dot(p_all, ones_b, preferred_element_type=F32)
        r_all = r_all.reshape(n_sb, HG_SB * HG_SB, LANES)
        o3 = o_inter.reshape(n_sb, HG_SB, LANES)
        for s in range(HG_SB):
            o3 = o3 + r_all[:, s * HG_SB:(s + 1) * HG_SB, :] * v3[:, s:s + 1, :]
        b_ref = jnp.concatenate([b3[0:1, 0:1], b3[:n_sb - 1, HG_SB - 1:HG_SB]], axis=0)
        qs = (q3 * jnp.exp(jnp.minimum(b3 - b_ref, 0.0))).astype(BF16)
        ks = (k[None] * jnp.exp(jnp.minimum(b_ref - b[None], 0.0))).astype(BF16)
        a = lax.dot_general(qs, ks, (((2,), (2,)), ((0,), (0,))), preferred_element_type=F32)
        a = jnp.where(col_sb < lo_sb, a, 0.0).astype(BF16).reshape(HG_C, HG_C)
        o = o3.reshape(HG_C, LANES) + jnp.dot(a, vb, preferred_element_type=F32)
        b_last = b[HG_C - 1:HG_C, :]
        kdec = (k * jnp.exp(b_last - b)).astype(BF16)
        st_ref[...] = st * jnp.exp(b_last) + jnp.dot(v.T.astype(BF16), kdec, preferred_element_type=F32)
        ms = jnp.mean(o * o, axis=-1, keepdims=True)
        zg = zg_ref[pl.ds(r0, HG_C), :]
        hn_ref[pl.ds(r0, HG_C), :] = o * lax.rsqrt(ms + RMS_EPS) * g * (zg * _sigmoid(zg))
        return carry

    lax.fori_loop(0, n_chunks, chunk, 0, unroll=2)

    @pl.when(t == pl.num_programs(2) - 1)
    def _fin():
        sfin_ref[...] = st_ref[...].T


def _hgrn_prompt(hq, hk, hv, lf, zg, g_hg, tb=1024):
    bsz, t, _ = hq.shape
    seq = pl.BlockSpec((None, tb, HG_HEAD_DIM), lambda b, h, i: (b, i, h))
    return pl.pallas_call(
        functools.partial(_hgrn_kernel, n_chunks=tb // HG_C),
        grid=(bsz, HG_HEADS, t // tb),
        in_specs=[seq, seq, seq, seq, seq,
                  pl.BlockSpec((1, HG_HEAD_DIM), lambda b, h, i: (0, h))],
        out_specs=[seq,
                   pl.BlockSpec((None, None, HG_HEAD_DIM, HG_HEAD_DIM), lambda b, h, i: (b, h, 0, 0))],
        out_shape=[jax.ShapeDtypeStruct((bsz, t, HG_WIDTH), F32),
                   jax.ShapeDtypeStruct((bsz, HG_HEADS, HG_HEAD_DIM, HG_HEAD_DIM), F32)],
        scratch_shapes=[pltpu.VMEM((HG_HEAD_DIM, HG_HEAD_DIM), F32)],
        compiler_params=_cparams(("parallel", "parallel", "arbitrary")),
        name="hgrn_prompt",
    )(hq, hk, hv, lf, zg, g_hg.reshape(1, HG_WIDTH))


def _hgrn_step_kernel(q_ref, k_ref, v_ref, lf_ref, zg_ref, g_ref, s_ref, hn_ref, snew_ref):
    row = slice(None)
    zeros = jnp.zeros((HG_HEAD_DIM - 3, HG_HEAD_DIM), F32)
    for h in range(HG_HEADS):
        cs = slice(h * HG_HEAD_DIM, (h + 1) * HG_HEAD_DIM)
        q, k, v = q_ref[row, cs], k_ref[row, cs], v_ref[row, cs]
        f = jnp.exp(lf_ref[row, cs])
        cols = jnp.concatenate([f, k, q, zeros], axis=0).T
        s_new = cols[:, 0:1] * s_ref[h] + cols[:, 1:2] * v
        snew_ref[h] = s_new
        o = jnp.sum(cols[:, 2:3] * s_new, axis=0, keepdims=True)
        ms = jnp.mean(o * o, axis=-1, keepdims=True)
        zg = zg_ref[row, cs]
        hn_ref[row, cs] = o * lax.rsqrt(ms + RMS_EPS) * g_ref[:, cs] * (zg * _sigmoid(zg))


def _hgrn_step(hq, hk, hv, lf, zg, g_hg, state):
    bsz = hq.shape[0]
    one = pl.BlockSpec((None, 1, HG_WIDTH), lambda b: (b, 0, 0))
    st = pl.BlockSpec((None, HG_HEADS, HG_HEAD_DIM, HG_HEAD_DIM), lambda b: (b, 0, 0, 0))
    r3 = lambda a: a.reshape(bsz, 1, HG_WIDTH)
    hn, s_new = pl.pallas_call(
        _hgrn_step_kernel,
        grid=(bsz,),
        in_specs=[one, one, one, one, one, pl.BlockSpec((1, HG_WIDTH), lambda b: (0, 0)), st],
        out_specs=[one, st],
        out_shape=[jax.ShapeDtypeStruct((bsz, 1, HG_WIDTH), F32),
                   jax.ShapeDtypeStruct(state.shape, F32)],
        compiler_params=_cparams(("parallel",)),
        name="hgrn_step",
    )(r3(hq), r3(hk), r3(hv), r3(lf), r3(zg), g_hg.reshape(1, HG_WIDTH), state)
    return hn.reshape(bsz, HG_WIDTH), s_new


ATT_N = 128
ATT_SUPER = 2048
ATT_G = 4


def _attn_prompt_kernel(q_ref, k_ref, v_ref, o_ref, osc, lsc, *, seq_len):
    lane = lax.broadcasted_iota(jnp.int32, (ATT_N, LANES), 1)
    rowi = lax.broadcasted_iota(jnp.int32, (ATT_N, LANES), 0)
    head0 = lane < ATT_HEAD_DIM
    kidx = lax.broadcasted_iota(jnp.int32, (ATT_N, 2 * ATT_N), 1)
    qidx = lax.broadcasted_iota(jnp.int32, (ATT_N, 2 * ATT_N), 0)
    band = jnp.logical_and(kidx >= qidx, kidx <= qidx + ATT_N)
    in_prev = kidx < ATT_N
    scale = ATT_HEAD_DIM ** -0.5

    bidx = lax.broadcasted_iota(jnp.int32, (ATT_G, ATT_N, 2 * ATT_N), 0)
    bqk = (((2,), (2,)), ((0,), (0,)))
    bkd = (((2,), (1,)), ((0,), (0,)))
    rows_g = ATT_G * ATT_N

    def do_group(p, d, base, g):
        span = ATT_N * d
        if d == 1:
            off = g * rows_g
            start = base + off

            def cur(ref):
                return ref[pl.ds(start, rows_g), :].reshape(ATT_G, ATT_N, LANES)

            def prv(ref, c):
                before = ref[pl.ds(jnp.maximum(start - ATT_N, 0), ATT_N), :].astype(BF16)
                return jnp.concatenate([before[None], c[:ATT_G - 1]], axis=0)

            pen = jnp.where(jnp.logical_and(jnp.logical_and(bidx == 0, in_prev), start == 0), NEG, 0.0)
        else:
            per_blk = d // ATT_G
            off = (g // per_blk) * span + (g % per_blk) * ATT_G
            start = base + off
            prev = jnp.maximum(start - span, 0)

            def cur(ref):
                return jnp.stack([ref[pl.ds(start + r, ATT_N, stride=d), :] for r in range(ATT_G)])

            def prv(ref, c):
                return jnp.stack([ref[pl.ds(prev + r, ATT_N, stride=d), :] for r in range(ATT_G)]).astype(BF16)

            pen = jnp.where(jnp.logical_and(in_prev, start < span), NEG, 0.0)
        q = cur(q_ref) * scale
        kc = cur(k_ref).astype(BF16)
        vc = cur(v_ref).astype(BF16)
        kk = jnp.concatenate([prv(k_ref, kc), kc], axis=1)
        vv = jnp.concatenate([prv(v_ref, vc), vc], axis=1)
        res = []
        for hm in (head0, jnp.logical_not(head0)):
            qh = jnp.where(hm, q, 0.0).astype(BF16)
            s = lax.dot_general(qh, kk, bqk, preferred_element_type=F32)
            s = jnp.where(band, s, NEG) + pen
            m = jnp.max(s, axis=-1, keepdims=True)
            pr = jnp.exp(s - m)
            den = jnp.sum(pr, axis=-1, keepdims=True)
            o = lax.dot_general(pr.astype(BF16), vv, bkd, preferred_element_type=F32)
            res.append((o / den, m + jnp.log(den)))
        o = jnp.where(head0, res[0][0], res[1][0])
        lse = jnp.where(head0, res[0][1], res[1][1])
        if d == 1:
            osc[p, pl.ds(off, rows_g), :] = o.reshape(rows_g, LANES)
            lsc[p, pl.ds(off, rows_g), :] = lse.reshape(rows_g, LANES)
        else:
            for r in range(ATT_G):
                osc[p, pl.ds(off + r, ATT_N, stride=d), :] = o[r]
                lsc[p, pl.ds(off + r, ATT_N, stride=d), :] = lse[r]

    def superblock(sb, carry):
        base = sb * ATT_SUPER

        def groups(g, c2):
            for p, (w, d) in enumerate(DILATED_PATTERNS):
                do_group(p, d, base, g)
            return c2

        lax.fori_loop(0, ATT_SUPER // rows_g, groups, 0)

        piece = 256

        def merge(j, c2):
            r = pl.ds(pl.multiple_of(j * piece, piece), piece)
            ls = [lsc[p, r, :] for p in range(len(DILATED_PATTERNS))]
            mx = jnp.maximum(jnp.maximum(ls[0], ls[1]), ls[2])
            ws = [jnp.exp(l - mx) for l in ls]
            num = ws[0] * osc[0, r, :] + ws[1] * osc[1, r, :] + ws[2] * osc[2, r, :]
            o_ref[pl.ds(pl.multiple_of(base + j * piece, piece), piece), :] = num / (ws[0] + ws[1] + ws[2])
            return c2

        lax.fori_loop(0, ATT_SUPER // piece, merge, 0)
        return carry

    lax.fori_loop(0, seq_len // ATT_SUPER, superblock, 0)


def _attn_prompt(aq, ak, av):
    bsz, t, _ = aq.shape
    spec = pl.BlockSpec((None, t, LANES), lambda b, p: (b, 0, p))
    n_pat = len(DILATED_PATTERNS)
    return pl.pallas_call(
        functools.partial(_attn_prompt_kernel, seq_len=t),
        grid=(bsz, ATT_WIDTH // LANES),
        in_specs=[spec, spec, spec],
        out_specs=spec,
        out_shape=jax.ShapeDtypeStruct((bsz, t, ATT_WIDTH), F32),
        scratch_shapes=[pltpu.VMEM((n_pat, ATT_SUPER, LANES), F32),
                        pltpu.VMEM((n_pat, ATT_SUPER, LANES), F32)],
        compiler_params=_cparams(("parallel", "parallel")),
        name="attn_prompt",
    )(aq, ak, av)


def _attn_step_kernel(q_ref, kn_ref, vn_ref, ck_ref, cv_ref, o_ref, nk_ref, nv_ref):
    win = ck_ref.shape[-1]
    kt, vt = ck_ref[...], cv_ref[...]
    q = q_ref[...] * (ATT_HEAD_DIM ** -0.5)
    kn, vn = kn_ref[...], vn_ref[...]
    s_all = jnp.sum(kt * q, axis=1, keepdims=True)
    s_new = jnp.sum(kn * q, axis=1, keepdims=True)
    dist = win - lax.broadcasted_iota(jnp.int32, (1, 1, win), 2)

    ps, pnews, lses = [], [], []
    for w, d in DILATED_PATTERNS:
        on_stride = (dist & (d - 1)) == 0 if d & (d - 1) == 0 else dist % d == 0
        valid = jnp.logical_and(dist <= w, on_stride)
        sm = jnp.where(valid, s_all, NEG)
        m = jnp.maximum(jnp.max(sm, axis=-1, keepdims=True), s_new)
        p = jnp.exp(sm - m)
        pn = jnp.exp(s_new - m)
        den = jnp.sum(p, axis=-1, keepdims=True) + pn
        ps.append(p / den)
        pnews.append(pn / den)
        lses.append(m + jnp.log(den))
    mx = jnp.maximum(jnp.maximum(lses[0], lses[1]), lses[2])
    ws = [jnp.exp(l - mx) for l in lses]
    wsum = ws[0] + ws[1] + ws[2]
    p_tot = (ws[0] * ps[0] + ws[1] * ps[1] + ws[2] * ps[2]) / wsum
    pn_tot = (ws[0] * pnews[0] + ws[1] * pnews[1] + ws[2] * pnews[2]) / wsum
    o_ref[...] = jnp.sum(vt * p_tot, axis=-1, keepdims=True) + pn_tot * vn

    last = lax.broadcasted_iota(jnp.int32, (1, 1, win), 2) == win - 1
    nk_ref[...] = jnp.where(last, kn, pltpu.roll(kt, win - 1, 2))
    nv_ref[...] = jnp.where(last, vn, pltpu.roll(vt, win - 1, 2))


def _attn_step(aq, ak, av, cache_k, cache_v):
    bsz, _, _, win = cache_k.shape
    one = pl.BlockSpec((None, ATT_HEADS, ATT_HEAD_DIM, 1), lambda b: (b, 0, 0, 0))
    cache = pl.BlockSpec((None, ATT_HEADS, ATT_HEAD_DIM, win), lambda b: (b, 0, 0, 0))
    col = lambda a: a.reshape(bsz, ATT_HEADS, ATT_HEAD_DIM, 1)
    att, new_k, new_v = pl.pallas_call(
        _attn_step_kernel,
        grid=(bsz,),
        in_specs=[one, one, one, cache, cache],
        out_specs=[one, cache, cache],
        out_shape=[jax.ShapeDtypeStruct((bsz, ATT_HEADS, ATT_HEAD_DIM, 1), F32),
                   jax.ShapeDtypeStruct(cache_k.shape, F32),
                   jax.ShapeDtypeStruct(cache_v.shape, F32)],
        compiler_params=_cparams(("parallel",)),
        name="attn_step",
    )(col(aq), col(ak), col(av), cache_k, cache_v)
    return att.reshape(bsz, ATT_WIDTH), new_k, new_v


ROUTE_TM = 256


def _outproj_kernel(hn_ref, att_ref, x_ref, hn_s_ref, att_s_ref, x_s_ref, w_ref, g_ref, wr_ref,
                    xmid_ref, h2_ref, route_ref):
    is_prompt = pl.program_id(0) < pl.num_programs(0) - 1
    hn = jnp.where(is_prompt, hn_ref[...], hn_s_ref[...])
    att = jnp.where(is_prompt, att_ref[...], att_s_ref[...])
    y = (jnp.dot(hn.astype(BF16), w_ref[0:HG_WIDTH, :], preferred_element_type=F32)
         + jnp.dot(att.astype(BF16), w_ref[HG_WIDTH:, :], preferred_element_type=F32))
    xm = jnp.where(is_prompt, x_ref[...], x_s_ref[...]) + y
    xmid_ref[...] = xm
    ms = jnp.mean(xm * xm, axis=-1, keepdims=True)
    h2 = xm * lax.rsqrt(ms + RMS_EPS) * g_ref[...]
    h2_ref[...] = h2
    lg = jnp.dot(h2, wr_ref[...], precision=HIGHEST, preferred_element_type=F32)
    lane = lax.broadcasted_iota(jnp.int32, lg.shape, 1).astype(F32)
    big = float(LANES)
    gmask = lane < N_GROUPS
    lgg = jnp.where(gmask, lg, NEG)
    mg = jnp.max(lgg, axis=-1, keepdims=True)
    gi = jnp.min(jnp.where(lgg == mg, lane, big), axis=-1, keepdims=True)
    p_grp = 1.0 / jnp.sum(jnp.exp(lgg - mg), axis=-1, keepdims=True)
    lo = N_GROUPS + gi * EXPERTS_PER_GROUP
    emask = jnp.logical_and(lane >= lo, lane < lo + EXPERTS_PER_GROUP)
    le1 = jnp.where(emask, lg, NEG)
    m1 = jnp.max(le1, axis=-1, keepdims=True)
    i1 = jnp.min(jnp.where(le1 == m1, lane, big), axis=-1, keepdims=True)
    le2 = jnp.where(lane == i1, NEG, le1)
    m2 = jnp.max(le2, axis=-1, keepdims=True)
    i2 = jnp.min(jnp.where(le2 == m2, lane, big), axis=-1, keepdims=True)
    r = jnp.exp(m2 - m1)
    g1 = p_grp / (1.0 + r)
    g2 = p_grp * r / (1.0 + r)
    route_ref[...] = jnp.where(lane == 0, i1 - N_GROUPS,
                               jnp.where(lane == 1, i2 - N_GROUPS,
                                         jnp.where(lane == 2, g1, jnp.where(lane == 3, g2, 0.0))))


def _outproj(hn_p, att_p, x_p, hn_s, att_s, x_s, w_out_bf, g_ffn, w_router):
    tm = ROUTE_TM
    n_p = x_p.shape[0]
    n_tiles = n_p // tm + 1
    pad = lambda a: jnp.pad(a, ((0, tm - a.shape[0]), (0, 0)))
    row_p = lambda i: (jnp.minimum(i, n_tiles - 2), 0)
    row = lambda i: (i, 0)
    const = lambda i: (0, 0)
    n_rows = n_tiles * tm
    return pl.pallas_call(
        _outproj_kernel,
        grid=(n_tiles,),
        in_specs=[pl.BlockSpec((tm, HG_WIDTH), row_p),
                  pl.BlockSpec((tm, ATT_WIDTH), row_p),
                  pl.BlockSpec((tm, D_MODEL), row_p),
                  pl.BlockSpec((tm, HG_WIDTH), const),
                  pl.BlockSpec((tm, ATT_WIDTH), const),
                  pl.BlockSpec((tm, D_MODEL), const),
                  pl.BlockSpec((D_MODEL, D_MODEL), const),
                  pl.BlockSpec((1, D_MODEL), const),
                  pl.BlockSpec((D_MODEL, LANES), const)],
        out_specs=[pl.BlockSpec((tm, D_MODEL), row),
                   pl.BlockSpec((tm, D_MODEL), row),
                   pl.BlockSpec((tm, LANES), row)],
        out_shape=[jax.ShapeDtypeStruct((n_rows, D_MODEL), F32),
                   jax.ShapeDtypeStruct((n_rows, D_MODEL), F32),
                   jax.ShapeDtypeStruct((n_rows, LANES), F32)],
        compiler_params=_cparams(("arbitrary",)),
        name="outproj",
    )(hn_p, att_p, x_p, pad(hn_s), pad(att_s), pad(x_s), w_out_bf, g_ffn.reshape(1, D_MODEL), w_router)


def _rank_kernel(route_ref, slot_ref, meta_ref, cnt_ref, base_ref, *, n_tok, blk):
    ph, i = pl.program_id(0), pl.program_id(1)
    tm = route_ref.shape[0]
    lane = lax.broadcasted_iota(jnp.int32, (tm, LANES), 1).astype(F32)
    rowg = i * tm + lax.broadcasted_iota(jnp.int32, (tm, LANES), 0)
    valid = rowg < n_tok
    r = route_ref[...]
    oh0 = jnp.where(jnp.logical_and(valid, lane == r[:, 0:1]), 1.0, 0.0)
    oh1 = jnp.where(jnp.logical_and(valid, lane == r[:, 1:2]), 1.0, 0.0)
    oh = oh0 + oh1
    colsum = jnp.sum(oh, axis=0, keepdims=True)

    @pl.when(jnp.logical_and(ph == 0, i == 0))
    def _zero():
        cnt_ref[...] = jnp.zeros_like(cnt_ref)

    @pl.when(ph == 0)
    def _count():
        cnt_ref[...] += colsum

    @pl.when(jnp.logical_and(ph == 1, i == 0))
    def _starts():
        cnt = cnt_ref[...].astype(jnp.int32)
        shift = blk.bit_length() - 1
        padded = (((cnt + (blk - 1)) >> shift) << shift).astype(F32)
        up = (lax.broadcasted_iota(jnp.int32, (LANES, LANES), 0)
              < lax.broadcasted_iota(jnp.int32, (LANES, LANES), 1)).astype(F32)
        start = jnp.dot(jnp.broadcast_to(padded, (8, LANES)), up, precision=HIGHEST,
                        preferred_element_type=F32)[0:1]
        base_ref[...] = start
        nb = meta_ref.shape[0]
        lane_b = lax.broadcasted_iota(jnp.int32, (nb, LANES), 1)
        blk_start = (lax.broadcasted_iota(jnp.int32, (nb, LANES), 0) * blk).astype(F32)
        ended = jnp.logical_and(start + padded <= blk_start, lane_b < N_EXPERTS)
        be = jnp.minimum(jnp.sum(jnp.where(ended, 1.0, 0.0), axis=-1, keepdims=True), N_EXPERTS - 1.0)
        n_used = jnp.sum(padded, axis=-1, keepdims=True) * (1.0 / blk)
        meta_ref[...] = jnp.where(lane_b == 0, be, jnp.where(lane_b == 1, n_used, 0.0)).astype(jnp.int32)

    @pl.when(ph == 1)
    def _slots():
        before = (lax.broadcasted_iota(jnp.int32, (tm, tm), 1)
                  < lax.broadcasted_iota(jnp.int32, (tm, tm), 0)).astype(BF16)
        pre = jnp.dot(before, oh.astype(BF16), preferred_element_type=F32) + base_ref[...]
        s0 = jnp.sum(oh0 * pre, axis=-1, keepdims=True)
        s1 = jnp.sum(oh1 * pre, axis=-1, keepdims=True)
        slot_ref[...] = jnp.where(lane == 0, s0, jnp.where(lane == 1, s1, 0.0)).astype(jnp.int32)
        base_ref[...] += colsum


def _rank(route, n_tok, blk, nblk):
    n_rows = route.shape[0]
    nb = (nblk + 7) // 8 * 8
    slots, meta = pl.pallas_call(
        functools.partial(_rank_kernel, n_tok=n_tok, blk=blk),
        grid=(2, n_rows // ROUTE_TM),
        in_specs=[pl.BlockSpec((ROUTE_TM, LANES), lambda ph, i: (i, 0))],
        out_specs=[pl.BlockSpec((ROUTE_TM, LANES), lambda ph, i: (i * ph, 0)),
                   pl.BlockSpec((nb, LANES), lambda ph, i: (0, 0))],
        out_shape=[jax.ShapeDtypeStruct((n_rows, LANES), jnp.int32),
                   jax.ShapeDtypeStruct((nb, LANES), jnp.int32)],
        scratch_shapes=[pltpu.VMEM((1, LANES), F32), pltpu.VMEM((1, LANES), F32)],
        compiler_params=_cparams(("arbitrary", "arbitrary")),
        name="rank",
    )(route)
    return slots[:n_tok, :TOP_K_INNER].reshape(-1), meta[:nblk, 0], meta[0:1, 1]


DMA_UNROLL = 8


def _dispatch_kernel(slot_ref, h2_ref, xb_in, xb_out, sem, *, n_tok):
    del xb_in
    i = pl.program_id(0)
    tm = h2_ref.shape[0]
    tail = n_tok % tm

    def push(rows):
        def body(r, c):
            a = (i * tm + r) * TOP_K_INNER
            for k in range(TOP_K_INNER):
                pltpu.make_async_copy(h2_ref.at[pl.ds(r, 1)], xb_out.at[pl.ds(slot_ref[a + k], 1)], sem).start()
            return c
        lax.fori_loop(0, rows, body, 0, unroll=DMA_UNROLL)
        for k in range(TOP_K_INNER):
            pltpu.make_async_copy(h2_ref.at[pl.ds(0, rows)], xb_out.at[pl.ds(0, rows)], sem).wait()

    last = pl.num_programs(0) - 1
    if tail == 0:
        push(tm)
    else:
        @pl.when(i < last)
        def _full():
            push(tm)

        @pl.when(i == last)
        def _tail():
            push(tail)


def _dispatch(slot_flat, h2, n_tok, n_slots):
    tm = ROUTE_TM
    grid_spec = pltpu.PrefetchScalarGridSpec(
        num_scalar_prefetch=1,
        grid=(h2.shape[0] // tm,),
        in_specs=[pl.BlockSpec((tm, D_MODEL), lambda i, s: (i, 0)),
                  pl.BlockSpec(memory_space=pl.ANY)],
        out_specs=pl.BlockSpec(memory_space=pl.ANY),
        scratch_shapes=[pltpu.SemaphoreType.DMA(())],
    )
    return pl.pallas_call(
        functools.partial(_dispatch_kernel, n_tok=n_tok),
        grid_spec=grid_spec,
        out_shape=jax.ShapeDtypeStruct((n_slots, D_MODEL), F32),
        input_output_aliases={2: 0},
        compiler_params=_cparams(("arbitrary",)),
        name="dispatch",
    )(slot_flat, h2, jnp.zeros((n_slots, D_MODEL), F32))


def _expert_kernel(be_ref, nu_ref, x_ref, wg_ref, wu_ref, wd_ref, y_ref, wgb, wub, wdb):
    i = pl.program_id(0)
    e = be_ref[i]
    e_prev = be_ref[jnp.maximum(i - 1, 0)]

    @pl.when(jnp.logical_or(i == 0, e != e_prev))
    def _cast():
        wgb[...] = wg_ref[...].astype(BF16)
        wub[...] = wu_ref[...].astype(BF16)
        wdb[...] = wd_ref[...].astype(BF16)

    @pl.when(i < nu_ref[0])
    def _run():
        x = x_ref[...].astype(BF16)
        a = jnp.dot(x, wgb[...], preferred_element_type=F32)
        u = jnp.dot(x, wub[...], preferred_element_type=F32)
        mid = (a * _sigmoid(a) * u).astype(BF16)
        y_ref[...] = jnp.dot(mid, wdb[...], preferred_element_type=F32)

    @pl.when(i >= nu_ref[0])
    def _skip():
        y_ref[...] = jnp.zeros_like(y_ref)


def _experts(xb, blk_expert, n_used, w_g, w_u, w_d, blk):
    nblk = blk_expert.shape[0]
    wmap = lambda i, be, nu: (be[i], 0, 0)
    grid_spec = pltpu.PrefetchScalarGridSpec(
        num_scalar_prefetch=2,
        grid=(nblk,),
        in_specs=[pl.BlockSpec((blk, D_MODEL), lambda i, be, nu: (jnp.minimum(i, nu[0] - 1), 0)),
                  pl.BlockSpec((None, D_MODEL, D_FF_EXPERT), wmap),
                  pl.BlockSpec((None, D_MODEL, D_FF_EXPERT), wmap),
                  pl.BlockSpec((None, D_FF_EXPERT, D_MODEL), wmap)],
        out_specs=pl.BlockSpec((blk, D_MODEL), lambda i, be, nu: (i, 0)),
        scratch_shapes=[pltpu.VMEM((D_MODEL, D_FF_EXPERT), BF16),
                        pltpu.VMEM((D_MODEL, D_FF_EXPERT), BF16),
                        pltpu.VMEM((D_FF_EXPERT, D_MODEL), BF16)],
    )
    return pl.pallas_call(
        _expert_kernel,
        grid_spec=grid_spec,
        out_shape=jax.ShapeDtypeStruct((nblk * blk, D_MODEL), F32),
        compiler_params=_cparams(("arbitrary",)),
        name="experts",
    )(blk_expert, n_used, xb, w_g, w_u, w_d)


def _final_kernel(slot_ref, x_ref, route_ref, g_ref, yb_hbm, o_ref, ybuf, sem, *, tok0):
    i = pl.program_id(0)
    tm = x_ref.shape[0]

    def gather(j, buf):
        def body(r, c):
            a = (tok0 + j * tm + r) * TOP_K_INNER
            for k in range(TOP_K_INNER):
                pltpu.make_async_copy(yb_hbm.at[pl.ds(slot_ref[a + k], 1)], ybuf.at[buf, k, pl.ds(r, 1)],
                                      sem.at[buf]).start()
            return c
        lax.fori_loop(0, tm, body, 0, unroll=DMA_UNROLL)

    @pl.when(i == 0)
    def _first():
        gather(0, 0)

    @pl.when(i + 1 < pl.num_programs(0))
    def _next():
        gather(i + 1, (i + 1) % 2)

    buf = i % 2
    for k in range(TOP_K_INNER):
        pltpu.make_async_copy(yb_hbm.at[pl.ds(0, tm)], ybuf.at[buf, k], sem.at[buf]).wait()
    route = route_ref[...]
    x = x_ref[...] + (ybuf[buf, 0] * route[:, 2:3] + ybuf[buf, 1] * route[:, 3:4])
    ms = jnp.mean(x * x, axis=-1, keepdims=True)
    o_ref[...] = x * lax.rsqrt(ms + RMS_EPS) * g_ref[...]


def _final(slot_flat, xmid, route, g_final, yb, tok0, n_out, tm):
    blk0 = tok0 // tm
    grid_spec = pltpu.PrefetchScalarGridSpec(
        num_scalar_prefetch=1,
        grid=(n_out // tm,),
        in_specs=[pl.BlockSpec((tm, D_MODEL), lambda i, s: (i + blk0, 0)),
                  pl.BlockSpec((tm, LANES), lambda i, s: (i + blk0, 0)),
                  pl.BlockSpec((1, D_MODEL), lambda i, s: (0, 0)),
                  pl.BlockSpec(memory_space=pl.ANY)],
        out_specs=pl.BlockSpec((tm, D_MODEL), lambda i, s: (i, 0)),
        scratch_shapes=[pltpu.VMEM((2, TOP_K_INNER, tm, D_MODEL), F32),
                        pltpu.SemaphoreType.DMA((2,))],
    )
    return pl.pallas_call(
        functools.partial(_final_kernel, tok0=tok0),
        grid_spec=grid_spec,
        out_shape=jax.ShapeDtypeStruct((n_out, D_MODEL), F32),
        compiler_params=_cparams(("arbitrary",)),
        name="final",
    )(slot_flat, xmid, route, g_final.reshape(1, D_MODEL), yb)


def kernel(x_prompt, x_sample, cache_attn_k, cache_attn_v, state_hgrn, w_in, w_out, hg_lb_logits,
           hg_norm_g, norm_mix_g, norm_ffn_g, norm_final_g, w_route_group, w_route_expert,
           w_expert_gate, w_expert_up, w_expert_down):
    bp, tp, _ = x_prompt.shape
    bs = x_sample.shape[0]
    l = 0
    w_in_bf = w_in[l].astype(BF16)
    w_out_bf = w_out[l].astype(BF16)
    w_router = jnp.concatenate(
        [w_route_group[l],
         jnp.transpose(w_route_expert[l], (1, 0, 2)).reshape(D_MODEL, N_EXPERTS),
         jnp.zeros((D_MODEL, LANES - N_GROUPS - N_EXPERTS), F32)], axis=-1)

    n_p = bp * tp
    xp = x_prompt.reshape(n_p, D_MODEL)
    pos_p = jnp.tile(jnp.arange(tp, dtype=jnp.int32), bp)
    hq, hk, hv, lf, zg, aq, ak, av = _inproj(xp, norm_mix_g[l], w_in_bf, hg_lb_logits, pos_p, 256)
    seq3 = lambda a: a.reshape(bp, tp, HG_WIDTH)
    hn_p, s_fin = _hgrn_prompt(seq3(hq), seq3(hk), seq3(hv), seq3(lf), seq3(zg), hg_norm_g[l])
    att_p = _attn_prompt(seq3(aq), seq3(ak), seq3(av))
    keep = min(MAX_WINDOW, tp)
    heads = lambda a: a.reshape(1, bp, keep, ATT_HEADS, ATT_HEAD_DIM)
    new_k_p = heads(seq3(ak)[:, tp - keep:])
    new_v_p = heads(seq3(av)[:, tp - keep:])

    xs = x_sample.reshape(bs, D_MODEL)
    pos_s = jnp.full((bs,), PAST_LEN, jnp.int32)
    hq, hk, hv, lf, zg, aq, ak, av = _inproj(xs, norm_mix_g[l], w_in_bf, hg_lb_logits, pos_s, bs)
    hn_s, s_new = _hgrn_step(hq, hk, hv, lf, zg, hg_norm_g[l], state_hgrn[l])
    feat = lambda a: jnp.transpose(a, (0, 2, 3, 1))
    att_s, new_k_s, new_v_s = _attn_step(aq, ak, av, feat(cache_attn_k[l]), feat(cache_attn_v[l]))
    cache5 = lambda a: jnp.transpose(a, (0, 3, 1, 2))[None]

    assert n_p % ROUTE_TM == 0 and bs <= ROUTE_TM
    n_tok = n_p + bs
    xmid, h2, route = _outproj(hn_p.reshape(n_p, HG_WIDTH), att_p.reshape(n_p, ATT_WIDTH), xp,
                               hn_s, att_s, xs, w_out_bf, norm_ffn_g[l], w_router)
    blk = MOE_BLOCK
    nblk = (n_tok * TOP_K_INNER + N_EXPERTS * (blk - 1)) // blk + 1
    slot_flat, blk_expert, n_used = _rank(route, n_tok, blk, nblk)
    xb = _dispatch(slot_flat, h2, n_tok, nblk * blk)
    yb = _experts(xb, blk_expert, n_used, w_expert_gate[l], w_expert_up[l], w_expert_down[l], blk)
    y_prompt = _final(slot_flat, xmid, route, norm_final_g, yb, 0, n_p, 256)
    y_sample = _final(slot_flat, xmid, route, norm_final_g, yb, n_p, bs, bs)

    return (y_prompt.reshape(bp, tp, D_MODEL), y_sample.reshape(bs, 1, D_MODEL),
            new_k_p, new_v_p, s_fin[None], cache5(new_k_s), cache5(new_v_s), s_new[None])
```

```python
import functools

import jax
import jax.numpy as jnp
from jax import lax
from jax.experimental import pallas as pl
from jax.experimental.pallas import tpu as pltpu

F32 = jnp.float32
BF16 = jnp.bfloat16

D_MODEL = 1024
HG_WIDTH = 512
HG_HEAD_DIM = 128
HG_HEADS = 4
ATT_WIDTH = 512
ATT_HEAD_DIM = 64
ATT_HEADS = 8
ROPE_DIM = 16
ROPE_THETA = 500000.0
DILATED_PATTERNS = ((128, 1), (512, 4), (2048, 16))
MAX_WINDOW = 2048
PAST_LEN = 16384
N_GROUPS = 8
EXPERTS_PER_GROUP = 8
N_EXPERTS = 64
TOP_K_INNER = 2
D_FF_EXPERT = 512
MOE_BLOCK = 128
IN_COLS = 4 * HG_WIDTH + 3 * ATT_WIDTH
RMS_EPS = 1e-6

LANES = 128
VMEM_LIMIT = 56 * 1024 * 1024
NEG = -1e30
HIGHEST = lax.Precision.HIGHEST
NT_DIMS = (((1,), (1,)), ((), ()))


def _sigmoid(z):
    return 1.0 / (1.0 + jnp.exp(-z))


def _cparams(sem):
    return pltpu.CompilerParams(dimension_semantics=sem, vmem_limit_bytes=VMEM_LIMIT)


def _inproj_kernel(x_ref, g_ref, w_ref, lbl_ref, cos_ref, sa_ref, sb_ref,
                   hq_ref, hk_ref, hv_ref, lf_ref, zg_ref, aq_ref, ak_ref, av_ref):
    x = x_ref[...]
    ms = jnp.mean(x * x, axis=-1, keepdims=True)
    h = (x * lax.rsqrt(ms + RMS_EPS) * g_ref[...]).astype(BF16)

    def mm(c0):
        return jnp.dot(h, w_ref[:, c0:c0 + HG_WIDTH], preferred_element_type=F32)

    lbl = lbl_ref[...]
    le = jnp.exp(lbl - jnp.max(lbl, axis=0, keepdims=True))
    lb = le[0:1, :] / jnp.sum(le, axis=0, keepdims=True)

    zq = mm(0)
    hq_ref[...] = zq * _sigmoid(zq)
    zf = mm(HG_WIDTH)
    f = lb + (1.0 - lb) * _sigmoid(zf)
    hk_ref[...] = 1.0 - f
    lf_ref[...] = jnp.log(f)
    hv_ref[...] = mm(2 * HG_WIDTH)
    zg_ref[...] = mm(3 * HG_WIDTH)

    cos, sa, sb = cos_ref[...], sa_ref[...], sb_ref[...]

    def rope(a, out_ref):
        for j in range(ATT_WIDTH // LANES):
            xj = a[:, j * LANES:(j + 1) * LANES]
            up = pltpu.roll(xj, LANES - ROPE_DIM // 2, 1)
            dn = pltpu.roll(xj, ROPE_DIM // 2, 1)
            out_ref[:, j * LANES:(j + 1) * LANES] = xj * cos + up * sa + dn * sb

    rope(mm(4 * HG_WIDTH), aq_ref)
    rope(mm(4 * HG_WIDTH + ATT_WIDTH), ak_ref)
    av_ref[...] = mm(4 * HG_WIDTH + 2 * ATT_WIDTH)


def _rope_tables(pos):
    half = ROPE_DIM // 2
    inv_freq = ROPE_THETA ** (-jnp.arange(half, dtype=F32) / half)
    ang = pos.astype(F32)[:, None] * inv_freq[None, :]
    cos, sin = jnp.cos(ang), jnp.sin(ang)
    m = pos.shape[0]
    rest = ATT_HEAD_DIM - ROPE_DIM
    one, zero, z8 = jnp.ones((m, rest), F32), jnp.zeros((m, rest), F32), jnp.zeros((m, half), F32)
    reps = LANES // ATT_HEAD_DIM
    c = jnp.tile(jnp.concatenate([cos, cos, one], axis=-1), (1, reps))
    sa = jnp.tile(jnp.concatenate([-sin, z8, zero], axis=-1), (1, reps))
    sb = jnp.tile(jnp.concatenate([z8, sin, zero], axis=-1), (1, reps))
    return c, sa, sb


def _inproj(x2d, g, w_bf, lb_logits, pos, tm):
    m = x2d.shape[0]
    cos, sa, sb = _rope_tables(pos)
    row = lambda i: (i, 0)
    const = lambda i: (0, 0)
    outs = [jax.ShapeDtypeStruct((m, HG_WIDTH), F32)] * 8
    return pl.pallas_call(
        _inproj_kernel,
        grid=(m // tm,),
        in_specs=[pl.BlockSpec((tm, D_MODEL), row),
                  pl.BlockSpec((1, D_MODEL), const),
                  pl.BlockSpec((D_MODEL, IN_COLS), const),
                  pl.BlockSpec(lb_logits.shape, const),
                  pl.BlockSpec((tm, LANES), row),
                  pl.BlockSpec((tm, LANES), row),
                  pl.BlockSpec((tm, LANES), row)],
        out_specs=[pl.BlockSpec((tm, HG_WIDTH), row)] * 8,
        out_shape=outs,
        compiler_params=_cparams(("parallel",)),
        name="inproj",
    )(x2d, g.reshape(1, D_MODEL), w_bf, lb_logits, cos, sa, sb)


HG_C = 128
HG_SB = 16


def _hgrn_kernel(q_ref, k_ref, v_ref, lf_ref, zg_ref, g_ref, hn_ref, sfin_ref, st_ref, *, n_chunks):
    t = pl.program_id(2)

    @pl.when(t == 0)
    def _init():
        st_ref[...] = jnp.zeros_like(st_ref)

    ri = lax.broadcasted_iota(jnp.int32, (HG_C, HG_C), 0)
    ci = lax.broadcasted_iota(jnp.int32, (HG_C, HG_C), 1)
    ltri = (ri >= ci).astype(F32)
    ones_b = jnp.ones((LANES, LANES), BF16)
    n_sb = HG_C // HG_SB
    row_sb = lax.broadcasted_iota(jnp.int32, (n_sb, HG_SB, LANES), 1)
    col_sb = lax.broadcasted_iota(jnp.int32, (n_sb, HG_SB, HG_C), 2)
    lo_sb = lax.broadcasted_iota(jnp.int32, (n_sb, HG_SB, HG_C), 0) * HG_SB
    g = g_ref[...]

    def chunk(c, carry):
        r0 = pl.multiple_of(c * HG_C, HG_C)
        q = q_ref[pl.ds(r0, HG_C), :]
        k = k_ref[pl.ds(r0, HG_C), :]
        v = v_ref[pl.ds(r0, HG_C), :]
        lf = lf_ref[pl.ds(r0, HG_C), :]
        b = jnp.dot(ltri, lf, precision=HIGHEST, preferred_element_type=F32)
        st = st_ref[...]
        vb = v.astype(BF16)
        qb = (q * jnp.exp(b)).astype(BF16)
        o_inter = lax.dot_general(qb, st.astype(BF16), NT_DIMS, preferred_element_type=F32)
        b3, q3, k3, v3 = (a.reshape(n_sb, HG_SB, LANES) for a in (b, q, k, v))
        ps = []
        for s in range(HG_SB):
            d = jnp.where(row_sb >= s, b3 - b3[:, s:s + 1, :], NEG)
            ps.append(q3 * jnp.exp(d) * k3[:, s:s + 1, :])
        p_all = jnp.concatenate(ps, axis=1).reshape(n_sb * HG_SB * HG_SB, LANES).astype(BF16)
        r_all = jnp.dot(p_all, ones_b, preferred_element_type=F32)
        r_all = r_all.reshape(n_sb, HG_SB * HG_SB, LANES)
        o3 = o_inter.reshape(n_sb, HG_SB, LANES)
        for s in range(HG_SB):
            o3 = o3 + r_all[:, s * HG_SB:(s + 1) * HG_SB, :] * v3[:, s:s + 1, :]
        b_ref = jnp.concatenate([b3[0:1, 0:1], b3[:n_sb - 1, HG_SB - 1:HG_SB]], axis=0)
        qs = (q3 * jnp.exp(jnp.minimum(b3 - b_ref, 0.0))).astype(BF16)
        ks = (k[None] * jnp.exp(jnp.minimum(b_ref - b[None], 0.0))).astype(BF16)
        a = lax.dot_general(qs, ks, (((2,), (2,)), ((0,), (0,))), preferred_element_type=F32)
        a = jnp.where(col_sb < lo_sb, a, 0.0).astype(BF16).reshape(HG_C, HG_C)
        o = o3.reshape(HG_C, LANES) + jnp.dot(a, vb, preferred_element_type=F32)
        b_last = b[HG_C - 1:HG_C, :]
        kdec = (k * jnp.exp(b_last - b)).astype(BF16)
        st_ref[...] = st * jnp.exp(b_last) + jnp.dot(v.T.astype(BF16), kdec, preferred_element_type=F32)
        ms = jnp.mean(o * o, axis=-1, keepdims=True)
        zg = zg_ref[pl.ds(r0, HG_C), :]
        hn_ref[pl.ds(r0, HG_C), :] = o * lax.rsqrt(ms + RMS_EPS) * g * (zg * _sigmoid(zg))
        return carry

    lax.fori_loop(0, n_chunks, chunk, 0, unroll=2)

    @pl.when(t == pl.num_programs(2) - 1)
    def _fin():
        sfin_ref[...] = st_ref[...].T


def _hgrn_prompt(hq, hk, hv, lf, zg, g_hg, tb=1024):
    bsz, t, _ = hq.shape
    seq = pl.BlockSpec((None, tb, HG_HEAD_DIM), lambda b, h, i: (b, i, h))
    return pl.pallas_call(
        functools.partial(_hgrn_kernel, n_chunks=tb // HG_C),
        grid=(bsz, HG_HEADS, t // tb),
        in_specs=[seq, seq, seq, seq, seq,
                  pl.BlockSpec((1, HG_HEAD_DIM), lambda b, h, i: (0, h))],
        out_specs=[seq,
                   pl.BlockSpec((None, None, HG_HEAD_DIM, HG_HEAD_DIM), lambda b, h, i: (b, h, 0, 0))],
        out_shape=[jax.ShapeDtypeStruct((bsz, t, HG_WIDTH), F32),
                   jax.ShapeDtypeStruct((bsz, HG_HEADS, HG_HEAD_DIM, HG_HEAD_DIM), F32)],
        scratch_shapes=[pltpu.VMEM((HG_HEAD_DIM, HG_HEAD_DIM), F32)],
        compiler_params=_cparams(("parallel", "parallel", "arbitrary")),
        name="hgrn_prompt",
    )(hq, hk, hv, lf, zg, g_hg.reshape(1, HG_WIDTH))


def _hgrn_step_kernel(q_ref, k_ref, v_ref, lf_ref, zg_ref, g_ref, s_ref, hn_ref, snew_ref):
    row = slice(None)
    zeros = jnp.zeros((HG_HEAD_DIM - 3, HG_HEAD_DIM), F32)
    for h in range(HG_HEADS):
        cs = slice(h * HG_HEAD_DIM, (h + 1) * HG_HEAD_DIM)
        q, k, v = q_ref[row, cs], k_ref[row, cs], v_ref[row, cs]
        f = jnp.exp(lf_ref[row, cs])
        cols = jnp.concatenate([f, k, q, zeros], axis=0).T
        s_new = cols[:, 0:1] * s_ref[h] + cols[:, 1:2] * v
        snew_ref[h] = s_new
        o = jnp.sum(cols[:, 2:3] * s_new, axis=0, keepdims=True)
        ms = jnp.mean(o * o, axis=-1, keepdims=True)
        zg = zg_ref[row, cs]
        hn_ref[row, cs] = o * lax.rsqrt(ms + RMS_EPS) * g_ref[:, cs] * (zg * _sigmoid(zg))


def _hgrn_step(hq, hk, hv, lf, zg, g_hg, state):
    bsz = hq.shape[0]
    one = pl.BlockSpec((None, 1, HG_WIDTH), lambda b: (b, 0, 0))
    st = pl.BlockSpec((None, HG_HEADS, HG_HEAD_DIM, HG_HEAD_DIM), lambda b: (b, 0, 0, 0))
    r3 = lambda a: a.reshape(bsz, 1, HG_WIDTH)
    hn, s_new = pl.pallas_call(
        _hgrn_step_kernel,
        grid=(bsz,),
        in_specs=[one, one, one, one, one, pl.BlockSpec((1, HG_WIDTH), lambda b: (0, 0)), st],
        out_specs=[one, st],
        out_shape=[jax.ShapeDtypeStruct((bsz, 1, HG_WIDTH), F32),
                   jax.ShapeDtypeStruct(state.shape, F32)],
        compiler_params=_cparams(("parallel",)),
        name="hgrn_step",
    )(r3(hq), r3(hk), r3(hv), r3(lf), r3(zg), g_hg.reshape(1, HG_WIDTH), state)
    return hn.reshape(bsz, HG_WIDTH), s_new


ATT_N = 128
ATT_SUPER = 2048
ATT_G = 4


def _attn_prompt_kernel(q_ref, k_ref, v_ref, o_ref, osc, lsc, *, seq_len):
    lane = lax.broadcasted_iota(jnp.int32, (ATT_N, LANES), 1)
    rowi = lax.broadcasted_iota(jnp.int32, (ATT_N, LANES), 0)
    head0 = lane < ATT_HEAD_DIM
    kidx = lax.broadcasted_iota(jnp.int32, (ATT_N, 2 * ATT_N), 1)
    qidx = lax.broadcasted_iota(jnp.int32, (ATT_N, 2 * ATT_N), 0)
    band = jnp.logical_and(kidx >= qidx, kidx <= qidx + ATT_N)
    in_prev = kidx < ATT_N
    scale = ATT_HEAD_DIM ** -0.5

    bidx = lax.broadcasted_iota(jnp.int32, (ATT_G, ATT_N, 2 * ATT_N), 0)
    bqk = (((2,), (2,)), ((0,), (0,)))
    bkd = (((2,), (1,)), ((0,), (0,)))
    rows_g = ATT_G * ATT_N

    def do_group(p, d, base, g):
        span = ATT_N * d
        if d == 1:
            off = g * rows_g
            start = base + off

            def cur(ref):
                return ref[pl.ds(start, rows_g), :].reshape(ATT_G, ATT_N, LANES)

            def prv(ref, c):
                before = ref[pl.ds(jnp.maximum(start - ATT_N, 0), ATT_N), :].astype(BF16)
                return jnp.concatenate([before[None], c[:ATT_G - 1]], axis=0)

            pen = jnp.where(jnp.logical_and(jnp.logical_and(bidx == 0, in_prev), start == 0), NEG, 0.0)
        else:
            per_blk = d // ATT_G
            off = (g // per_blk) * span + (g % per_blk) * ATT_G
            start = base + off
            prev = jnp.maximum(start - span, 0)

            def cur(ref):
                return jnp.stack([ref[pl.ds(start + r, ATT_N, stride=d), :] for r in range(ATT_G)])

            def prv(ref, c):
                return jnp.stack([ref[pl.ds(prev + r, ATT_N, stride=d), :] for r in range(ATT_G)]).astype(BF16)

            pen = jnp.where(jnp.logical_and(in_prev, start < span), NEG, 0.0)
        q = cur(q_ref) * scale
        kc = cur(k_ref).astype(BF16)
        vc = cur(v_ref).astype(BF16)
        kk = jnp.concatenate([prv(k_ref, kc), kc], axis=1)
        vv = jnp.concatenate([prv(v_ref, vc), vc], axis=1)
        res = []
        for hm in (head0, jnp.logical_not(head0)):
            qh = jnp.where(hm, q, 0.0).astype(BF16)
            s = lax.dot_general(qh, kk, bqk, preferred_element_type=F32)
            s = jnp.where(band, s, NEG) + pen
            m = jnp.max(s, axis=-1, keepdims=True)
            pr = jnp.exp(s - m)
            den = jnp.sum(pr, axis=-1, keepdims=True)
            o = lax.dot_general(pr.astype(BF16), vv, bkd, preferred_element_type=F32)
            res.append((o / den, m + jnp.log(den)))
        o = jnp.where(head0, res[0][0], res[1][0])
        lse = jnp.where(head0, res[0][1], res[1][1])
        if d == 1:
            osc[p, pl.ds(off, rows_g), :] = o.reshape(rows_g, LANES)
            lsc[p, pl.ds(off, rows_g), :] = lse.reshape(rows_g, LANES)
        else:
            for r in range(ATT_G):
                osc[p, pl.ds(off + r, ATT_N, stride=d), :] = o[r]
                lsc[p, pl.ds(off + r, ATT_N, stride=d), :] = lse[r]

    def superblock(sb, carry):
        base = sb * ATT_SUPER

        def groups(g, c2):
            for p, (w, d) in enumerate(DILATED_PATTERNS):
                do_group(p, d, base, g)
            return c2

        lax.fori_loop(0, ATT_SUPER // rows_g, groups, 0)

        piece = 256

        def merge(j, c2):
            r = pl.ds(pl.multiple_of(j * piece, piece), piece)
            ls = [lsc[p, r, :] for p in range(len(DILATED_PATTERNS))]
            mx = jnp.maximum(jnp.maximum(ls[0], ls[1]), ls[2])
            ws = [jnp.exp(l - mx) for l in ls]
            num = ws[0] * osc[0, r, :] + ws[1] * osc[1, r, :] + ws[2] * osc[2, r, :]
            o_ref[pl.ds(pl.multiple_of(base + j * piece, piece), piece), :] = num / (ws[0] + ws[1] + ws[2])
            return c2

        lax.fori_loop(0, ATT_SUPER // piece, merge, 0)
        return carry

    lax.fori_loop(0, seq_len // ATT_SUPER, superblock, 0)


def _attn_prompt(aq, ak, av):
    bsz, t, _ = aq.shape
    spec = pl.BlockSpec((None, t, LANES), lambda b, p: (b, 0, p))
    n_pat = len(DILATED_PATTERNS)
    return pl.pallas_call(
        functools.partial(_attn_prompt_kernel, seq_len=t),
        grid=(bsz, ATT_WIDTH // LANES),
        in_specs=[spec, spec, spec],
        out_specs=spec,
        out_shape=jax.ShapeDtypeStruct((bsz, t, ATT_WIDTH), F32),
        scratch_shapes=[pltpu.VMEM((n_pat, ATT_SUPER, LANES), F32),
                        pltpu.VMEM((n_pat, ATT_SUPER, LANES), F32)],
        compiler_params=_cparams(("parallel", "parallel")),
        name="attn_prompt",
    )(aq, ak, av)


def _attn_step_kernel(q_ref, kn_ref, vn_ref, ck_ref, cv_ref, o_ref, nk_ref, nv_ref):
    win = ck_ref.shape[-1]
    kt, vt = ck_ref[...], cv_ref[...]
    q = q_ref[...] * (ATT_HEAD_DIM ** -0.5)
    kn, vn = kn_ref[...], vn_ref[...]
    s_all = jnp.sum(kt * q, axis=1, keepdims=True)
    s_new = jnp.sum(kn * q, axis=1, keepdims=True)
    dist = win - lax.broadcasted_iota(jnp.int32, (1, 1, win), 2)

    ps, pnews, lses = [], [], []
    for w, d in DILATED_PATTERNS:
        on_stride = (dist & (d - 1)) == 0 if d & (d - 1) == 0 else dist % d == 0
        valid = jnp.logical_and(dist <= w, on_stride)
        sm = jnp.where(valid, s_all, NEG)
        m = jnp.maximum(jnp.max(sm, axis=-1, keepdims=True), s_new)
        p = jnp.exp(sm - m)
        pn = jnp.exp(s_new - m)
        den = jnp.sum(p, axis=-1, keepdims=True) + pn
        ps.append(p / den)
        pnews.append(pn / den)
        lses.append(m + jnp.log(den))
    mx = jnp.maximum(jnp.maximum(lses[0], lses[1]), lses[2])
    ws = [jnp.exp(l - mx) for l in lses]
    wsum = ws[0] + ws[1] + ws[2]
    p_tot = (ws[0] * ps[0] + ws[1] * ps[1] + ws[2] * ps[2]) / wsum
    pn_tot = (ws[0] * pnews[0] + ws[1] * pnews[1] + ws[2] * pnews[2]) / wsum
    o_ref[...] = jnp.sum(vt * p_tot, axis=-1, keepdims=True) + pn_tot * vn

    last = lax.broadcasted_iota(jnp.int32, (1, 1, win), 2) == win - 1
    nk_ref[...] = jnp.where(last, kn, pltpu.roll(kt, win - 1, 2))
    nv_ref[...] = jnp.where(last, vn, pltpu.roll(vt, win - 1, 2))


def _attn_step(aq, ak, av, cache_k, cache_v):
    bsz, _, _, win = cache_k.shape
    one = pl.BlockSpec((None, ATT_HEADS, ATT_HEAD_DIM, 1), lambda b: (b, 0, 0, 0))
    cache = pl.BlockSpec((None, ATT_HEADS, ATT_HEAD_DIM, win), lambda b: (b, 0, 0, 0))
    col = lambda a: a.reshape(bsz, ATT_HEADS, ATT_HEAD_DIM, 1)
    att, new_k, new_v = pl.pallas_call(
        _attn_step_kernel,
        grid=(bsz,),
        in_specs=[one, one, one, cache, cache],
        out_specs=[one, cache, cache],
        out_shape=[jax.ShapeDtypeStruct((bsz, ATT_HEADS, ATT_HEAD_DIM, 1), F32),
                   jax.ShapeDtypeStruct(cache_k.shape, F32),
                   jax.ShapeDtypeStruct(cache_v.shape, F32)],
        compiler_params=_cparams(("parallel",)),
        name="attn_step",
    )(col(aq), col(ak), col(av), cache_k, cache_v)
    return att.reshape(bsz, ATT_WIDTH), new_k, new_v


ROUTE_TM = 256


def _outproj_kernel(hn_ref, att_ref, x_ref, hn_s_ref, att_s_ref, x_s_ref, w_ref, g_ref, wr_ref,
                    xmid_ref, h2_ref, route_ref):
    is_prompt = pl.program_id(0) < pl.num_programs(0) - 1
    hn = jnp.where(is_prompt, hn_ref[...], hn_s_ref[...])
    att = jnp.where(is_prompt, att_ref[...], att_s_ref[...])
    y = (jnp.dot(hn.astype(BF16), w_ref[0:HG_WIDTH, :], preferred_element_type=F32)
         + jnp.dot(att.astype(BF16), w_ref[HG_WIDTH:, :], preferred_element_type=F32))
    xm = jnp.where(is_prompt, x_ref[...], x_s_ref[...]) + y
    xmid_ref[...] = xm
    ms = jnp.mean(xm * xm, axis=-1, keepdims=True)
    h2 = xm * lax.rsqrt(ms + RMS_EPS) * g_ref[...]
    h2_ref[...] = h2
    lg = jnp.dot(h2, wr_ref[...], precision=HIGHEST, preferred_element_type=F32)
    lane = lax.broadcasted_iota(jnp.int32, lg.shape, 1).astype(F32)
    big = float(LANES)
    gmask = lane < N_GROUPS
    lgg = jnp.where(gmask, lg, NEG)
    mg = jnp.max(lgg, axis=-1, keepdims=True)
    gi = jnp.min(jnp.where(lgg == mg, lane, big), axis=-1, keepdims=True)
    p_grp = 1.0 / jnp.sum(jnp.exp(lgg - mg), axis=-1, keepdims=True)
    lo = N_GROUPS + gi * EXPERTS_PER_GROUP
    emask = jnp.logical_and(lane >= lo, lane < lo + EXPERTS_PER_GROUP)
    le1 = jnp.where(emask, lg, NEG)
    m1 = jnp.max(le1, axis=-1, keepdims=True)
    i1 = jnp.min(jnp.where(le1 == m1, lane, big), axis=-1, keepdims=True)
    le2 = jnp.where(lane == i1, NEG, le1)
    m2 = jnp.max(le2, axis=-1, keepdims=True)
    i2 = jnp.min(jnp.where(le2 == m2, lane, big), axis=-1, keepdims=True)
    r = jnp.exp(m2 - m1)
    g1 = p_grp / (1.0 + r)
    g2 = p_grp * r / (1.0 + r)
    route_ref[...] = jnp.where(lane == 0, i1 - N_GROUPS,
                               jnp.where(lane == 1, i2 - N_GROUPS,
                                         jnp.where(lane == 2, g1, jnp.where(lane == 3, g2, 0.0))))


def _outproj(hn_p, att_p, x_p, hn_s, att_s, x_s, w_out_bf, g_ffn, w_router):
    tm = ROUTE_TM
    n_p = x_p.shape[0]
    n_tiles = n_p // tm + 1
    pad = lambda a: jnp.pad(a, ((0, tm - a.shape[0]), (0, 0)))
    row_p = lambda i: (jnp.minimum(i, n_tiles - 2), 0)
    row = lambda i: (i, 0)
    const = lambda i: (0, 0)
    n_rows = n_tiles * tm
    return pl.pallas_call(
        _outproj_kernel,
        grid=(n_tiles,),
        in_specs=[pl.BlockSpec((tm, HG_WIDTH), row_p),
                  pl.BlockSpec((tm, ATT_WIDTH), row_p),
                  pl.BlockSpec((tm, D_MODEL), row_p),
                  pl.BlockSpec((tm, HG_WIDTH), const),
                  pl.BlockSpec((tm, ATT_WIDTH), const),
                  pl.BlockSpec((tm, D_MODEL), const),
                  pl.BlockSpec((D_MODEL, D_MODEL), const),
                  pl.BlockSpec((1, D_MODEL), const),
                  pl.BlockSpec((D_MODEL, LANES), const)],
        out_specs=[pl.BlockSpec((tm, D_MODEL), row),
                   pl.BlockSpec((tm, D_MODEL), row),
                   pl.BlockSpec((tm, LANES), row)],
        out_shape=[jax.ShapeDtypeStruct((n_rows, D_MODEL), F32),
                   jax.ShapeDtypeStruct((n_rows, D_MODEL), F32),
                   jax.ShapeDtypeStruct((n_rows, LANES), F32)],
        compiler_params=_cparams(("arbitrary",)),
        name="outproj",
    )(hn_p, att_p, x_p, pad(hn_s), pad(att_s), pad(x_s), w_out_bf, g_ffn.reshape(1, D_MODEL), w_router)


def _rank_kernel(route_ref, slot_ref, meta_ref, cnt_ref, base_ref, *, n_tok, blk):
    ph, i = pl.program_id(0), pl.program_id(1)
    tm = route_ref.shape[0]
    lane = lax.broadcasted_iota(jnp.int32, (tm, LANES), 1).astype(F32)
    rowg = i * tm + lax.broadcasted_iota(jnp.int32, (tm, LANES), 0)
    valid = rowg < n_tok
    r = route_ref[...]
    oh0 = jnp.where(jnp.logical_and(valid, lane == r[:, 0:1]), 1.0, 0.0)
    oh1 = jnp.where(jnp.logical_and(valid, lane == r[:, 1:2]), 1.0, 0.0)
    oh = oh0 + oh1
    colsum = jnp.sum(oh, axis=0, keepdims=True)

    @pl.when(jnp.logical_and(ph == 0, i == 0))
    def _zero():
        cnt_ref[...] = jnp.zeros_like(cnt_ref)

    @pl.when(ph == 0)
    def _count():
        cnt_ref[...] += colsum

    @pl.when(jnp.logical_and(ph == 1, i == 0))
    def _starts():
        cnt = cnt_ref[...].astype(jnp.int32)
        shift = blk.bit_length() - 1
        padded = (((cnt + (blk - 1)) >> shift) << shift).astype(F32)
        up = (lax.broadcasted_iota(jnp.int32, (LANES, LANES), 0)
              < lax.broadcasted_iota(jnp.int32, (LANES, LANES), 1)).astype(F32)
        start = jnp.dot(jnp.broadcast_to(padded, (8, LANES)), up, precision=HIGHEST,
                        preferred_element_type=F32)[0:1]
        base_ref[...] = start
        nb = meta_ref.shape[0]
        lane_b = lax.broadcasted_iota(jnp.int32, (nb, LANES), 1)
        blk_start = (lax.broadcasted_iota(jnp.int32, (nb, LANES), 0) * blk).astype(F32)
        ended = jnp.logical_and(start + padded <= blk_start, lane_b < N_EXPERTS)
        be = jnp.minimum(jnp.sum(jnp.where(ended, 1.0, 0.0), axis=-1, keepdims=True), N_EXPERTS - 1.0)
        n_used = jnp.sum(padded, axis=-1, keepdims=True) * (1.0 / blk)
        meta_ref[...] = jnp.where(lane_b == 0, be, jnp.where(lane_b == 1, n_used, 0.0)).astype(jnp.int32)

    @pl.when(ph == 1)
    def _slots():
        before = (lax.broadcasted_iota(jnp.int32, (tm, tm), 1)
                  < lax.broadcasted_iota(jnp.int32, (tm, tm), 0)).astype(BF16)
        pre = jnp.dot(before, oh.astype(BF16), preferred_element_type=F32) + base_ref[...]
        s0 = jnp.sum(oh0 * pre, axis=-1, keepdims=True)
        s1 = jnp.sum(oh1 * pre, axis=-1, keepdims=True)
        slot_ref[...] = jnp.where(lane == 0, s0, jnp.where(lane == 1, s1, 0.0)).astype(jnp.int32)
        base_ref[...] += colsum


def _rank(route, n_tok, blk, nblk):
    n_rows = route.shape[0]
    nb = (nblk + 7) // 8 * 8
    slots, meta = pl.pallas_call(
        functools.partial(_rank_kernel, n_tok=n_tok, blk=blk),
        grid=(2, n_rows // ROUTE_TM),
        in_specs=[pl.BlockSpec((ROUTE_TM, LANES), lambda ph, i: (i, 0))],
        out_specs=[pl.BlockSpec((ROUTE_TM, LANES), lambda ph, i: (i * ph, 0)),
                   pl.BlockSpec((nb, LANES), lambda ph, i: (0, 0))],
        out_shape=[jax.ShapeDtypeStruct((n_rows, LANES), jnp.int32),
                   jax.ShapeDtypeStruct((nb, LANES), jnp.int32)],
        scratch_shapes=[pltpu.VMEM((1, LANES), F32), pltpu.VMEM((1, LANES), F32)],
        compiler_params=_cparams(("arbitrary", "arbitrary")),
        name="rank",
    )(route)
    return slots[:n_tok, :TOP_K_INNER].reshape(-1), meta[:nblk, 0], meta[0:1, 1]


DMA_UNROLL = 8


def _dispatch_kernel(slot_ref, h2_ref, xb_in, xb_out, sem, *, n_tok):
    del xb_in
    i = pl.program_id(0)
    tm = h2_ref.shape[0]
    tail = n_tok % tm

    def push(rows):
        def body(r, c):
            a = (i * tm + r) * TOP_K_INNER
            for k in range(TOP_K_INNER):
                pltpu.make_async_copy(h2_ref.at[pl.ds(r, 1)], xb_out.at[pl.ds(slot_ref[a + k], 1)], sem).start()
            return c
        lax.fori_loop(0, rows, body, 0, unroll=DMA_UNROLL)
        for k in range(TOP_K_INNER):
            pltpu.make_async_copy(h2_ref.at[pl.ds(0, rows)], xb_out.at[pl.ds(0, rows)], sem).wait()

    last = pl.num_programs(0) - 1
    if tail == 0:
        push(tm)
    else:
        @pl.when(i < last)
        def _full():
            push(tm)

        @pl.when(i == last)
        def _tail():
            push(tail)


def _dispatch(slot_flat, h2, n_tok, n_slots):
    tm = ROUTE_TM
    grid_spec = pltpu.PrefetchScalarGridSpec(
        num_scalar_prefetch=1,
        grid=(h2.shape[0] // tm,),
        in_specs=[pl.BlockSpec((tm, D_MODEL), lambda i, s: (i, 0)),
                  pl.BlockSpec(memory_space=pl.ANY)],
        out_specs=pl.BlockSpec(memory_space=pl.ANY),
        scratch_shapes=[pltpu.SemaphoreType.DMA(())],
    )
    return pl.pallas_call(
        functools.partial(_dispatch_kernel, n_tok=n_tok),
        grid_spec=grid_spec,
        out_shape=jax.ShapeDtypeStruct((n_slots, D_MODEL), F32),
        input_output_aliases={2: 0},
        compiler_params=_cparams(("arbitrary",)),
        name="dispatch",
    )(slot_flat, h2, jnp.zeros((n_slots, D_MODEL), F32))


def _expert_kernel(be_ref, nu_ref, x_ref, wg_hbm, wu_hbm, wd_hbm, y_ref,
                   wg_f, wu_f, wd_f, sem, wgb, wub, wdb, cur_ref):
    i = pl.program_id(0)
    n_used = nu_ref[0]
    e = be_ref[i]
    e_prev = be_ref[jnp.maximum(i - 1, 0)]

    def fetch(ex, s):
        return [pltpu.make_async_copy(hbm.at[ex], buf.at[s], sem.at[s])
                for hbm, buf in ((wg_hbm, wg_f), (wu_hbm, wu_f), (wd_hbm, wd_f))]

    @pl.when(i == 0)
    def _first():
        cur_ref[0] = 0
        for c in fetch(e, 0):
            c.start()

    @pl.when(jnp.logical_and(jnp.logical_or(i == 0, e != e_prev), i < n_used))
    def _new_expert():
        s = cur_ref[0]
        j = lax.while_loop(lambda j: jnp.logical_and(j < n_used, be_ref[jnp.minimum(j, n_used - 1)] == e),
                           lambda j: j + 1, i + 1)

        @pl.when(j < n_used)
        def _prefetch():
            for c in fetch(be_ref[j], 1 - s):
                c.start()

        for c in fetch(e, s):
            c.wait()
        wgb[...] = wg_f[s].astype(BF16)
        wub[...] = wu_f[s].astype(BF16)
        wdb[...] = wd_f[s].astype(BF16)
        cur_ref[0] = 1 - s

    @pl.when(i < n_used)
    def _run():
        x = x_ref[...].astype(BF16)
        a = jnp.dot(x, wgb[...], preferred_element_type=F32)
        u = jnp.dot(x, wub[...], preferred_element_type=F32)
        mid = (a * _sigmoid(a) * u).astype(BF16)
        y_ref[...] = jnp.dot(mid, wdb[...], preferred_element_type=F32)

    @pl.when(i >= nu_ref[0])
    def _skip():
        y_ref[...] = jnp.zeros_like(y_ref)


def _experts(xb, blk_expert, n_used, w_g, w_u, w_d, blk):
    nblk = blk_expert.shape[0]
    hbm = pl.BlockSpec(memory_space=pl.ANY)
    grid_spec = pltpu.PrefetchScalarGridSpec(
        num_scalar_prefetch=2,
        grid=(nblk,),
        in_specs=[pl.BlockSpec((blk, D_MODEL), lambda i, be, nu: (jnp.minimum(i, nu[0] - 1), 0)),
                  hbm, hbm, hbm],
        out_specs=pl.BlockSpec((blk, D_MODEL), lambda i, be, nu: (i, 0)),
        scratch_shapes=[pltpu.VMEM((2, D_MODEL, D_FF_EXPERT), F32),
                        pltpu.VMEM((2, D_MODEL, D_FF_EXPERT), F32),
                        pltpu.VMEM((2, D_FF_EXPERT, D_MODEL), F32),
                        pltpu.SemaphoreType.DMA((2,)),
                        pltpu.VMEM((D_MODEL, D_FF_EXPERT), BF16),
                        pltpu.VMEM((D_MODEL, D_FF_EXPERT), BF16),
                        pltpu.VMEM((D_FF_EXPERT, D_MODEL), BF16),
                        pltpu.SMEM((1,), jnp.int32)],
    )
    return pl.pallas_call(
        _expert_kernel,
        grid_spec=grid_spec,
        out_shape=jax.ShapeDtypeStruct((nblk * blk, D_MODEL), F32),
        compiler_params=_cparams(("arbitrary",)),
        name="experts",
    )(blk_expert, n_used, xb, w_g, w_u, w_d)


def _final_kernel(slot_ref, x_ref, route_ref, g_ref, yb_hbm, o_ref, ybuf, sem, *, tok0):
    i = pl.program_id(0)
    tm = x_ref.shape[0]

    def gather(j, buf):
        def body(r, c):
            a = (tok0 + j * tm + r) * TOP_K_INNER
            for k in range(TOP_K_INNER):
                pltpu.make_async_copy(yb_hbm.at[pl.ds(slot_ref[a + k], 1)], ybuf.at[buf, k, pl.ds(r, 1)],
                                      sem.at[buf]).start()
            return c
        lax.fori_loop(0, tm, body, 0, unroll=DMA_UNROLL)

    @pl.when(i == 0)
    def _first():
        gather(0, 0)

    @pl.when(i + 1 < pl.num_programs(0))
    def _next():
        gather(i + 1, (i + 1) % 2)

    buf = i % 2
    for k in range(TOP_K_INNER):
        pltpu.make_async_copy(yb_hbm.at[pl.ds(0, tm)], ybuf.at[buf, k], sem.at[buf]).wait()
    route = route_ref[...]
    x = x_ref[...] + (ybuf[buf, 0] * route[:, 2:3] + ybuf[buf, 1] * route[:, 3:4])
    ms = jnp.mean(x * x, axis=-1, keepdims=True)
    o_ref[...] = x * lax.rsqrt(ms + RMS_EPS) * g_ref[...]


def _final(slot_flat, xmid, route, g_final, yb, tok0, n_out, tm):
    blk0 = tok0 // tm
    grid_spec = pltpu.PrefetchScalarGridSpec(
        num_scalar_prefetch=1,
        grid=(n_out // tm,),
        in_specs=[pl.BlockSpec((tm, D_MODEL), lambda i, s: (i + blk0, 0)),
                  pl.BlockSpec((tm, LANES), lambda i, s: (i + blk0, 0)),
                  pl.BlockSpec((1, D_MODEL), lambda i, s: (0, 0)),
                  pl.BlockSpec(memory_space=pl.ANY)],
        out_specs=pl.BlockSpec((tm, D_MODEL), lambda i, s: (i, 0)),
        scratch_shapes=[pltpu.VMEM((2, TOP_K_INNER, tm, D_MODEL), F32),
                        pltpu.SemaphoreType.DMA((2,))],
    )
    return pl.pallas_call(
        functools.partial(_final_kernel, tok0=tok0),
        grid_spec=grid_spec,
        out_shape=jax.ShapeDtypeStruct((n_out, D_MODEL), F32),
        compiler_params=_cparams(("arbitrary",)),
        name="final",
    )(slot_flat, xmid, route, g_final.reshape(1, D_MODEL), yb)


def kernel(x_prompt, x_sample, cache_attn_k, cache_attn_v, state_hgrn, w_in, w_out, hg_lb_logits,
           hg_norm_g, norm_mix_g, norm_ffn_g, norm_final_g, w_route_group, w_route_expert,
           w_expert_gate, w_expert_up, w_expert_down):
    bp, tp, _ = x_prompt.shape
    bs = x_sample.shape[0]
    l = 0
    w_in_bf = w_in[l].astype(BF16)
    w_out_bf = w_out[l].astype(BF16)
    w_router = jnp.concatenate(
        [w_route_group[l],
         jnp.transpose(w_route_expert[l], (1, 0, 2)).reshape(D_MODEL, N_EXPERTS),
         jnp.zeros((D_MODEL, LANES - N_GROUPS - N_EXPERTS), F32)], axis=-1)

    n_p = bp * tp
    xp = x_prompt.reshape(n_p, D_MODEL)
    pos_p = jnp.tile(jnp.arange(tp, dtype=jnp.int32), bp)
    hq, hk, hv, lf, zg, aq, ak, av = _inproj(xp, norm_mix_g[l], w_in_bf, hg_lb_logits, pos_p, 256)
    seq3 = lambda a: a.reshape(bp, tp, HG_WIDTH)
    hn_p, s_fin = _hgrn_prompt(seq3(hq), seq3(hk), seq3(hv), seq3(lf), seq3(zg), hg_norm_g[l])
    att_p = _attn_prompt(seq3(aq), seq3(ak), seq3(av))
    keep = min(MAX_WINDOW, tp)
    heads = lambda a: a.reshape(1, bp, keep, ATT_HEADS, ATT_HEAD_DIM)
    new_k_p = heads(seq3(ak)[:, tp - keep:])
    new_v_p = heads(seq3(av)[:, tp - keep:])

    xs = x_sample.reshape(bs, D_MODEL)
    pos_s = jnp.full((bs,), PAST_LEN, jnp.int32)
    hq, hk, hv, lf, zg, aq, ak, av = _inproj(xs, norm_mix_g[l], w_in_bf, hg_lb_logits, pos_s, bs)
    hn_s, s_new = _hgrn_step(hq, hk, hv, lf, zg, hg_norm_g[l], state_hgrn[l])
    feat = lambda a: jnp.transpose(a, (0, 2, 3, 1))
    att_s, new_k_s, new_v_s = _attn_step(aq, ak, av, feat(cache_attn_k[l]), feat(cache_attn_v[l]))
    cache5 = lambda a: jnp.transpose(a, (0, 3, 1, 2))[None]

    assert n_p % ROUTE_TM == 0 and bs <= ROUTE_TM
    n_tok = n_p + bs
    xmid, h2, route = _outproj(hn_p.reshape(n_p, HG_WIDTH), att_p.reshape(n_p, ATT_WIDTH), xp,
                               hn_s, att_s, xs, w_out_bf, norm_ffn_g[l], w_router)
    blk = MOE_BLOCK
    nblk = (n_tok * TOP_K_INNER + N_EXPERTS * (blk - 1)) // blk + 1
    slot_flat, blk_expert, n_used = _rank(route, n_tok, blk, nblk)
    xb = _dispatch(slot_flat, h2, n_tok, nblk * blk)
    yb = _experts(xb, blk_expert, n_used, w_expert_gate[l], w_expert_up[l], w_expert_down[l], blk)
    y_prompt = _final(slot_flat, xmid, route, norm_final_g, yb, 0, n_p, 256)
    y_sample = _final(slot_flat, xmid, route, norm_final_g, yb, n_p, bs, bs)

    return (y_prompt.reshape(bp, tp, D_MODEL), y_sample.reshape(bs, 1, D_MODEL),
            new_k_p, new_v_p, s_fin[None], cache5(new_k_s), cache5(new_v_s), s_new[None])
```

```python
import functools

import jax
import jax.numpy as jnp
from jax import lax
from jax.experimental import pallas as pl
from jax.experimental.pallas import tpu as pltpu

F32 = jnp.float32
BF16 = jnp.bfloat16

D_MODEL = 1024
HG_WIDTH = 512
HG_HEAD_DIM = 128
HG_HEADS = 4
ATT_WIDTH = 512
ATT_HEAD_DIM = 64
ATT_HEADS = 8
ROPE_DIM = 16
ROPE_THETA = 500000.0
DILATED_PATTERNS = ((128, 1), (512, 4), (2048, 16))
MAX_WINDOW = 2048
PAST_LEN = 16384
N_GROUPS = 8
EXPERTS_PER_GROUP = 8
N_EXPERTS = 64
TOP_K_INNER = 2
D_FF_EXPERT = 512
MOE_BLOCK = 128
IN_COLS = 4 * HG_WIDTH + 3 * ATT_WIDTH
RMS_EPS = 1e-6

LANES = 128
VMEM_LIMIT = 56 * 1024 * 1024
NEG = -1e30
HIGHEST = lax.Precision.HIGHEST
NT_DIMS = (((1,), (1,)), ((), ()))


def _sigmoid(z):
    return 1.0 / (1.0 + jnp.exp(-z))


def _cparams(sem):
    return pltpu.CompilerParams(dimension_semantics=sem, vmem_limit_bytes=VMEM_LIMIT)


def _inproj_kernel(x_ref, g_ref, w_ref, lbl_ref, cos_ref, sa_ref, sb_ref,
                   hq_ref, hk_ref, hv_ref, lf_ref, zg_ref, aq_ref, ak_ref, av_ref):
    x = x_ref[...]
    ms = jnp.mean(x * x, axis=-1, keepdims=True)
    h = (x * lax.rsqrt(ms + RMS_EPS) * g_ref[...]).astype(BF16)

    def mm(c0):
        return jnp.dot(h, w_ref[:, c0:c0 + HG_WIDTH], preferred_element_type=F32)

    lbl = lbl_ref[...]
    le = jnp.exp(lbl - jnp.max(lbl, axis=0, keepdims=True))
    lb = le[0:1, :] / jnp.sum(le, axis=0, keepdims=True)

    zq = mm(0)
    hq_ref[...] = zq * _sigmoid(zq)
    zf = mm(HG_WIDTH)
    f = lb + (1.0 - lb) * _sigmoid(zf)
    hk_ref[...] = 1.0 - f
    lf_ref[...] = jnp.log(f)
    hv_ref[...] = mm(2 * HG_WIDTH)
    zg_ref[...] = mm(3 * HG_WIDTH)

    cos, sa, sb = cos_ref[...], sa_ref[...], sb_ref[...]

    def rope(a, out_ref):
        for j in range(ATT_WIDTH // LANES):
            xj = a[:, j * LANES:(j + 1) * LANES]
            up = pltpu.roll(xj, LANES - ROPE_DIM // 2, 1)
            dn = pltpu.roll(xj, ROPE_DIM // 2, 1)
            out_ref[:, j * LANES:(j + 1) * LANES] = xj * cos + up * sa + dn * sb

    rope(mm(4 * HG_WIDTH), aq_ref)
    rope(mm(4 * HG_WIDTH + ATT_WIDTH), ak_ref)
    av_ref[...] = mm(4 * HG_WIDTH + 2 * ATT_WIDTH)


def _rope_tables(pos):
    half = ROPE_DIM // 2
    inv_freq = ROPE_THETA ** (-jnp.arange(half, dtype=F32) / half)
    ang = pos.astype(F32)[:, None] * inv_freq[None, :]
    cos, sin = jnp.cos(ang), jnp.sin(ang)
    m = pos.shape[0]
    rest = ATT_HEAD_DIM - ROPE_DIM
    one, zero, z8 = jnp.ones((m, rest), F32), jnp.zeros((m, rest), F32), jnp.zeros((m, half), F32)
    reps = LANES // ATT_HEAD_DIM
    c = jnp.tile(jnp.concatenate([cos, cos, one], axis=-1), (1, reps))
    sa = jnp.tile(jnp.concatenate([-sin, z8, zero], axis=-1), (1, reps))
    sb = jnp.tile(jnp.concatenate([z8, sin, zero], axis=-1), (1, reps))
    return c, sa, sb


def _inproj(x2d, g, w_bf, lb_logits, pos, tm):
    m = x2d.shape[0]
    cos, sa, sb = _rope_tables(pos)
    row = lambda i: (i, 0)
    const = lambda i: (0, 0)
    outs = [jax.ShapeDtypeStruct((m, HG_WIDTH), F32)] * 8
    return pl.pallas_call(
        _inproj_kernel,
        grid=(m // tm,),
        in_specs=[pl.BlockSpec((tm, D_MODEL), row),
                  pl.BlockSpec((1, D_MODEL), const),
                  pl.BlockSpec((D_MODEL, IN_COLS), const),
                  pl.BlockSpec(lb_logits.shape, const),
                  pl.BlockSpec((tm, LANES), row),
                  pl.BlockSpec((tm, LANES), row),
                  pl.BlockSpec((tm, LANES), row)],
        out_specs=[pl.BlockSpec((tm, HG_WIDTH), row)] * 8,
        out_shape=outs,
        compiler_params=_cparams(("parallel",)),
        name="inproj",
    )(x2d, g.reshape(1, D_MODEL), w_bf, lb_logits, cos, sa, sb)


HG_C = 128
HG_SB = 16


def _hgrn_kernel(q_ref, k_ref, v_ref, lf_ref, zg_ref, g_ref, hn_ref, sfin_ref, st_ref, *, n_chunks):
    t = pl.program_id(2)

    @pl.when(t == 0)
    def _init():
        st_ref[...] = jnp.zeros_like(st_ref)

    ri = lax.broadcasted_iota(jnp.int32, (HG_C, HG_C), 0)
    ci = lax.broadcasted_iota(jnp.int32, (HG_C, HG_C), 1)
    ltri = (ri >= ci).astype(F32)
    ones_b = jnp.ones((LANES, LANES), BF16)
    n_sb = HG_C // HG_SB
    row_sb = lax.broadcasted_iota(jnp.int32, (n_sb, HG_SB, LANES), 1)
    col_sb = lax.broadcasted_iota(jnp.int32, (n_sb, HG_SB, HG_C), 2)
    lo_sb = lax.broadcasted_iota(jnp.int32, (n_sb, HG_SB, HG_C), 0) * HG_SB
    g = g_ref[...]

    def chunk(c, carry):
        r0 = pl.multiple_of(c * HG_C, HG_C)
        q = q_ref[pl.ds(r0, HG_C), :]
        k = k_ref[pl.ds(r0, HG_C), :]
        v = v_ref[pl.ds(r0, HG_C), :]
        lf = lf_ref[pl.ds(r0, HG_C), :]
        b = jnp.dot(ltri, lf, precision=HIGHEST, preferred_element_type=F32)
        st = st_ref[...]
        vb = v.astype(BF16)
        qb = (q * jnp.exp(b)).astype(BF16)
        o_inter = lax.dot_general(qb, st.astype(BF16), NT_DIMS, preferred_element_type=F32)
        b3, q3, k3, v3 = (a.reshape(n_sb, HG_SB, LANES) for a in (b, q, k, v))
        ps = []
        for s in range(HG_SB):
            d = jnp.where(row_sb >= s, b3 - b3[:, s:s + 1, :], NEG)
            ps.append(q3 * jnp.exp(d) * k3[:, s:s + 1, :])
        p_all = jnp.concatenate(ps, axis=1).reshape(n_sb * HG_SB * HG_SB, LANES).astype(BF16)
        r_all = jnp.dot(p_all, ones_b, preferred_element_type=F32)
        r_all = r_all.reshape(n_sb, HG_SB * HG_SB, LANES)
        o3 = o_inter.reshape(n_sb, HG_SB, LANES)
        for s in range(HG_SB):
            o3 = o3 + r_all[:, s * HG_SB:(s + 1) * HG_SB, :] * v3[:, s:s + 1, :]
        b_ref = jnp.concatenate([b3[0:1, 0:1], b3[:n_sb - 1, HG_SB - 1:HG_SB]], axis=0)
        qs = (q3 * jnp.exp(jnp.minimum(b3 - b_ref, 0.0))).astype(BF16)
        ks = (k[None] * jnp.exp(jnp.minimum(b_ref - b[None], 0.0))).astype(BF16)
        a = lax.dot_general(qs, ks, (((2,), (2,)), ((0,), (0,))), preferred_element_type=F32)
        a = jnp.where(col_sb < lo_sb, a, 0.0).astype(BF16).reshape(HG_C, HG_C)
        o = o3.reshape(HG_C, LANES) + jnp.dot(a, vb, preferred_element_type=F32)
        b_last = b[HG_C - 1:HG_C, :]
        kdec = (k * jnp.exp(b_last - b)).astype(BF16)
        st_ref[...] = st * jnp.exp(b_last) + jnp.dot(v.T.astype(BF16), kdec, preferred_element_type=F32)
        ms = jnp.mean(o * o, axis=-1, keepdims=True)
        zg = zg_ref[pl.ds(r0, HG_C), :]
        hn_ref[pl.ds(r0, HG_C), :] = o * lax.rsqrt(ms + RMS_EPS) * g * (zg * _sigmoid(zg))
        return carry

    lax.fori_loop(0, n_chunks, chunk, 0, unroll=2)

    @pl.when(t == pl.num_programs(2) - 1)
    def _fin():
        sfin_ref[...] = st_ref[...].T


def _hgrn_prompt(hq, hk, hv, lf, zg, g_hg, tb=1024):
    bsz, t, _ = hq.shape
    seq = pl.BlockSpec((None, tb, HG_HEAD_DIM), lambda b, h, i: (b, i, h))
    return pl.pallas_call(
        functools.partial(_hgrn_kernel, n_chunks=tb // HG_C),
        grid=(bsz, HG_HEADS, t // tb),
        in_specs=[seq, seq, seq, seq, seq,
                  pl.BlockSpec((1, HG_HEAD_DIM), lambda b, h, i: (0, h))],
        out_specs=[seq,
                   pl.BlockSpec((None, None, HG_HEAD_DIM, HG_HEAD_DIM), lambda b, h, i: (b, h, 0, 0))],
        out_shape=[jax.ShapeDtypeStruct((bsz, t, HG_WIDTH), F32),
                   jax.ShapeDtypeStruct((bsz, HG_HEADS, HG_HEAD_DIM, HG_HEAD_DIM), F32)],
        scratch_shapes=[pltpu.VMEM((HG_HEAD_DIM, HG_HEAD_DIM), F32)],
        compiler_params=_cparams(("parallel", "parallel", "arbitrary")),
        name="hgrn_prompt",
    )(hq, hk, hv, lf, zg, g_hg.reshape(1, HG_WIDTH))


def _hgrn_step_kernel(q_ref, k_ref, v_ref, lf_ref, zg_ref, g_ref, s_ref, hn_ref, snew_ref):
    row = slice(None)
    zeros = jnp.zeros((HG_HEAD_DIM - 3, HG_HEAD_DIM), F32)
    for h in range(HG_HEADS):
        cs = slice(h * HG_HEAD_DIM, (h + 1) * HG_HEAD_DIM)
        q, k, v = q_ref[row, cs], k_ref[row, cs], v_ref[row, cs]
        f = jnp.exp(lf_ref[row, cs])
        cols = jnp.concatenate([f, k, q, zeros], axis=0).T
        s_new = cols[:, 0:1] * s_ref[h] + cols[:, 1:2] * v
        snew_ref[h] = s_new
        o = jnp.sum(cols[:, 2:3] * s_new, axis=0, keepdims=True)
        ms = jnp.mean(o * o, axis=-1, keepdims=True)
        zg = zg_ref[row, cs]
        hn_ref[row, cs] = o * lax.rsqrt(ms + RMS_EPS) * g_ref[:, cs] * (zg * _sigmoid(zg))


def _hgrn_step(hq, hk, hv, lf, zg, g_hg, state):
    bsz = hq.shape[0]
    one = pl.BlockSpec((None, 1, HG_WIDTH), lambda b: (b, 0, 0))
    st = pl.BlockSpec((None, HG_HEADS, HG_HEAD_DIM, HG_HEAD_DIM), lambda b: (b, 0, 0, 0))
    r3 = lambda a: a.reshape(bsz, 1, HG_WIDTH)
    hn, s_new = pl.pallas_call(
        _hgrn_step_kernel,
        grid=(bsz,),
        in_specs=[one, one, one, one, one, pl.BlockSpec((1, HG_WIDTH), lambda b: (0, 0)), st],
        out_specs=[one, st],
        out_shape=[jax.ShapeDtypeStruct((bsz, 1, HG_WIDTH), F32),
                   jax.ShapeDtypeStruct(state.shape, F32)],
        compiler_params=_cparams(("parallel",)),
        name="hgrn_step",
    )(r3(hq), r3(hk), r3(hv), r3(lf), r3(zg), g_hg.reshape(1, HG_WIDTH), state)
    return hn.reshape(bsz, HG_WIDTH), s_new


ATT_N = 128
ATT_SUPER = 2048
ATT_G = 4


def _attn_prompt_kernel(q_ref, k_ref, v_ref, o_ref, osc, lsc, *, seq_len):
    lane = lax.broadcasted_iota(jnp.int32, (ATT_N, LANES), 1)
    rowi = lax.broadcasted_iota(jnp.int32, (ATT_N, LANES), 0)
    head0 = lane < ATT_HEAD_DIM
    kidx = lax.broadcasted_iota(jnp.int32, (ATT_N, 2 * ATT_N), 1)
    qidx = lax.broadcasted_iota(jnp.int32, (ATT_N, 2 * ATT_N), 0)
    band = jnp.logical_and(kidx >= qidx, kidx <= qidx + ATT_N)
    in_prev = kidx < ATT_N
    scale = ATT_HEAD_DIM ** -0.5

    bidx = lax.broadcasted_iota(jnp.int32, (ATT_G, ATT_N, 2 * ATT_N), 0)
    bqk = (((2,), (2,)), ((0,), (0,)))
    bkd = (((2,), (1,)), ((0,), (0,)))
    rows_g = ATT_G * ATT_N

    def do_group(p, d, base, g):
        span = ATT_N * d
        if d == 1:
            off = g * rows_g
            start = base + off

            def cur(ref):
                return ref[pl.ds(start, rows_g), :].reshape(ATT_G, ATT_N, LANES)

            def prv(ref, c):
                before = ref[pl.ds(jnp.maximum(start - ATT_N, 0), ATT_N), :].astype(BF16)
                return jnp.concatenate([before[None], c[:ATT_G - 1]], axis=0)

            pen = jnp.where(jnp.logical_and(jnp.logical_and(bidx == 0, in_prev), start == 0), NEG, 0.0)
        else:
            per_blk = d // ATT_G
            off = (g // per_blk) * span + (g % per_blk) * ATT_G
            start = base + off
            prev = jnp.maximum(start - span, 0)

            def cur(ref):
                return jnp.stack([ref[pl.ds(start + r, ATT_N, stride=d), :] for r in range(ATT_G)])

            def prv(ref, c):
                return jnp.stack([ref[pl.ds(prev + r, ATT_N, stride=d), :] for r in range(ATT_G)]).astype(BF16)

            pen = jnp.where(jnp.logical_and(in_prev, start < span), NEG, 0.0)
        q = cur(q_ref) * scale
        kc = cur(k_ref).astype(BF16)
        vc = cur(v_ref).astype(BF16)
        kk = jnp.concatenate([prv(k_ref, kc), kc], axis=1)
        vv = jnp.concatenate([prv(v_ref, vc), vc], axis=1)
        res = []
        for hm in (head0, jnp.logical_not(head0)):
            qh = jnp.where(hm, q, 0.0).astype(BF16)
            s = lax.dot_general(qh, kk, bqk, preferred_element_type=F32)
            s = jnp.where(band, s, NEG) + pen
            m = jnp.max(s, axis=-1, keepdims=True)
            pr = jnp.exp(s - m)
            den = jnp.sum(pr, axis=-1, keepdims=True)
            o = lax.dot_general(pr.astype(BF16), vv, bkd, preferred_element_type=F32)
            res.append((o / den, m + jnp.log(den)))
        o = jnp.where(head0, res[0][0], res[1][0])
        lse = jnp.where(head0, res[0][1], res[1][1])
        if d == 1:
            osc[p, pl.ds(off, rows_g), :] = o.reshape(rows_g, LANES)
            lsc[p, pl.ds(off, rows_g), :] = lse.reshape(rows_g, LANES)
        else:
            for r in range(ATT_G):
                osc[p, pl.ds(off + r, ATT_N, stride=d), :] = o[r]
                lsc[p, pl.ds(off + r, ATT_N, stride=d), :] = lse[r]

    def superblock(sb, carry):
        base = sb * ATT_SUPER

        def groups(g, c2):
            for p, (w, d) in enumerate(DILATED_PATTERNS):
                do_group(p, d, base, g)
            return c2

        lax.fori_loop(0, ATT_SUPER // rows_g, groups, 0)

        piece = 256

        def merge(j, c2):
            r = pl.ds(pl.multiple_of(j * piece, piece), piece)
            ls = [lsc[p, r, :] for p in range(len(DILATED_PATTERNS))]
            mx = jnp.maximum(jnp.maximum(ls[0], ls[1]), ls[2])
            ws = [jnp.exp(l - mx) for l in ls]
            num = ws[0] * osc[0, r, :] + ws[1] * osc[1, r, :] + ws[2] * osc[2, r, :]
            o_ref[pl.ds(pl.multiple_of(base + j * piece, piece), piece), :] = num / (ws[0] + ws[1] + ws[2])
            return c2

        lax.fori_loop(0, ATT_SUPER // piece, merge, 0)
        return carry

    lax.fori_loop(0, seq_len // ATT_SUPER, superblock, 0)


def _attn_prompt(aq, ak, av):
    bsz, t, _ = aq.shape
    spec = pl.BlockSpec((None, t, LANES), lambda b, p: (b, 0, p))
    n_pat = len(DILATED_PATTERNS)
    return pl.pallas_call(
        functools.partial(_attn_prompt_kernel, seq_len=t),
        grid=(bsz, ATT_WIDTH // LANES),
        in_specs=[spec, spec, spec],
        out_specs=spec,
        out_shape=jax.ShapeDtypeStruct((bsz, t, ATT_WIDTH), F32),
        scratch_shapes=[pltpu.VMEM((n_pat, ATT_SUPER, LANES), F32),
                        pltpu.VMEM((n_pat, ATT_SUPER, LANES), F32)],
        compiler_params=_cparams(("parallel", "parallel")),
        name="attn_prompt",
    )(aq, ak, av)


def _attn_step_kernel(q_ref, kn_ref, vn_ref, ck_ref, cv_ref, o_ref, nk_ref, nv_ref):
    win = ck_ref.shape[-1]
    kt, vt = ck_ref[...], cv_ref[...]
    q = q_ref[...] * (ATT_HEAD_DIM ** -0.5)
    kn, vn = kn_ref[...], vn_ref[...]
    s_all = jnp.sum(kt * q, axis=1, keepdims=True)
    s_new = jnp.sum(kn * q, axis=1, keepdims=True)
    dist = win - lax.broadcasted_iota(jnp.int32, (1, 1, win), 2)

    ps, pnews, lses = [], [], []
    for w, d in DILATED_PATTERNS:
        on_stride = (dist & (d - 1)) == 0 if d & (d - 1) == 0 else dist % d == 0
        valid = jnp.logical_and(dist <= w, on_stride)
        sm = jnp.where(valid, s_all, NEG)
        m = jnp.maximum(jnp.max(sm, axis=-1, keepdims=True), s_new)
        p = jnp.exp(sm - m)
        pn = jnp.exp(s_new - m)
        den = jnp.sum(p, axis=-1, keepdims=True) + pn
        ps.append(p / den)
        pnews.append(pn / den)
        lses.append(m + jnp.log(den))
    mx = jnp.maximum(jnp.maximum(lses[0], lses[1]), lses[2])
    ws = [jnp.exp(l - mx) for l in lses]
    wsum = ws[0] + ws[1] + ws[2]
    p_tot = (ws[0] * ps[0] + ws[1] * ps[1] + ws[2] * ps[2]) / wsum
    pn_tot = (ws[0] * pnews[0] + ws[1] * pnews[1] + ws[2] * pnews[2]) / wsum
    o_ref[...] = jnp.sum(vt * p_tot, axis=-1, keepdims=True) + pn_tot * vn

    last = lax.broadcasted_iota(jnp.int32, (1, 1, win), 2) == win - 1
    nk_ref[...] = jnp.where(last, kn, pltpu.roll(kt, win - 1, 2))
    nv_ref[...] = jnp.where(last, vn, pltpu.roll(vt, win - 1, 2))


def _attn_step(aq, ak, av, cache_k, cache_v):
    bsz, _, _, win = cache_k.shape
    one = pl.BlockSpec((None, ATT_HEADS, ATT_HEAD_DIM, 1), lambda b: (b, 0, 0, 0))
    cache = pl.BlockSpec((None, ATT_HEADS, ATT_HEAD_DIM, win), lambda b: (b, 0, 0, 0))
    col = lambda a: a.reshape(bsz, ATT_HEADS, ATT_HEAD_DIM, 1)
    att, new_k, new_v = pl.pallas_call(
        _attn_step_kernel,
        grid=(bsz,),
        in_specs=[one, one, one, cache, cache],
        out_specs=[one, cache, cache],
        out_shape=[jax.ShapeDtypeStruct((bsz, ATT_HEADS, ATT_HEAD_DIM, 1), F32),
                   jax.ShapeDtypeStruct(cache_k.shape, F32),
                   jax.ShapeDtypeStruct(cache_v.shape, F32)],
        compiler_params=_cparams(("parallel",)),
        name="attn_step",
    )(col(aq), col(ak), col(av), cache_k, cache_v)
    return att.reshape(bsz, ATT_WIDTH), new_k, new_v


ROUTE_TM = 256
SUBLANES = 8
ROW_TILE = D_MODEL // LANES
assert ROW_TILE == SUBLANES


def _store_row_tiles(ref, val, lead=()):
    n = val.shape[0]
    for j in range(ROW_TILE):
        ref[lead + (pl.ds(j, n, stride=ROW_TILE), slice(None))] = val[:, j * LANES:(j + 1) * LANES]


def _load_row_tiles(ref, n, lead=()):
    return jnp.concatenate([ref[lead + (pl.ds(j, n, stride=ROW_TILE), slice(None))] for j in range(ROW_TILE)],
                           axis=1)


def _row_tile(ref, r, n=1, lead=()):
    start = r * ROW_TILE if isinstance(r, int) else pl.multiple_of(r * ROW_TILE, ROW_TILE)
    return ref.at[lead + (pl.ds(start, n * ROW_TILE),)]


def _outproj_kernel(hn_ref, att_ref, x_ref, hn_s_ref, att_s_ref, x_s_ref, w_ref, g_ref, wr_ref,
                    xmid_ref, h2_ref, route_ref):
    is_prompt = pl.program_id(0) < pl.num_programs(0) - 1
    hn = jnp.where(is_prompt, hn_ref[...], hn_s_ref[...])
    att = jnp.where(is_prompt, att_ref[...], att_s_ref[...])
    y = (jnp.dot(hn.astype(BF16), w_ref[0:HG_WIDTH, :], preferred_element_type=F32)
         + jnp.dot(att.astype(BF16), w_ref[HG_WIDTH:, :], preferred_element_type=F32))
    xm = jnp.where(is_prompt, x_ref[...], x_s_ref[...]) + y
    xmid_ref[...] = xm
    ms = jnp.mean(xm * xm, axis=-1, keepdims=True)
    h2 = xm * lax.rsqrt(ms + RMS_EPS) * g_ref[...]
    _store_row_tiles(h2_ref, h2)
    lg =jnp.dot(h2, wr_ref[...], precision=HIGHEST, preferred_element_type=F32)
    lane = lax.broadcasted_iota(jnp.int32, lg.shape, 1).astype(F32)
    big = float(LANES)
    gmask = lane < N_GROUPS
    lgg = jnp.where(gmask, lg, NEG)
    mg = jnp.max(lgg, axis=-1, keepdims=True)
    gi = jnp.min(jnp.where(lgg == mg, lane, big), axis=-1, keepdims=True)
    p_grp = 1.0 / jnp.sum(jnp.exp(lgg - mg), axis=-1, keepdims=True)
    lo = N_GROUPS + gi * EXPERTS_PER_GROUP
    emask = jnp.logical_and(lane >= lo, lane < lo + EXPERTS_PER_GROUP)
    le1 = jnp.where(emask, lg, NEG)
    m1 = jnp.max(le1, axis=-1, keepdims=True)
    i1 = jnp.min(jnp.where(le1 == m1, lane, big), axis=-1, keepdims=True)
    le2 = jnp.where(lane == i1, NEG, le1)
    m2 = jnp.max(le2, axis=-1, keepdims=True)
    i2 = jnp.min(jnp.where(le2 == m2, lane, big), axis=-1, keepdims=True)
    r = jnp.exp(m2 - m1)
    g1 = p_grp / (1.0 + r)
    g2 = p_grp * r / (1.0 + r)
    route_ref[...] = jnp.where(lane == 0, i1 - N_GROUPS,
                               jnp.where(lane == 1, i2 - N_GROUPS,
                                         jnp.where(lane == 2, g1, jnp.where(lane == 3, g2, 0.0))))


def _outproj(hn_p, att_p, x_p, hn_s, att_s, x_s, w_out_bf, g_ffn, w_router):
    tm = ROUTE_TM
    n_p = x_p.shape[0]
    n_tiles = n_p // tm + 1
    pad = lambda a: jnp.pad(a, ((0, tm - a.shape[0]), (0, 0)))
    row_p = lambda i: (jnp.minimum(i, n_tiles - 2), 0)
    row = lambda i: (i, 0)
    const = lambda i: (0, 0)
    n_rows = n_tiles * tm
    return pl.pallas_call(
        _outproj_kernel,
        grid=(n_tiles,),
        in_specs=[pl.BlockSpec((tm, HG_WIDTH), row_p),
                  pl.BlockSpec((tm, ATT_WIDTH), row_p),
                  pl.BlockSpec((tm, D_MODEL), row_p),
                  pl.BlockSpec((tm, HG_WIDTH), const),
                  pl.BlockSpec((tm, ATT_WIDTH), const),
                  pl.BlockSpec((tm, D_MODEL), const),
                  pl.BlockSpec((D_MODEL, D_MODEL), const),
                  pl.BlockSpec((1, D_MODEL), const),
                  pl.BlockSpec((D_MODEL, LANES), const)],
        out_specs=[pl.BlockSpec((tm, D_MODEL), row),
                   pl.BlockSpec((tm * ROW_TILE, LANES), row),
                   pl.BlockSpec((tm, LANES), row)],
        out_shape=[jax.ShapeDtypeStruct((n_rows, D_MODEL), F32),
                   jax.ShapeDtypeStruct((n_rows * ROW_TILE, LANES), F32),
                   jax.ShapeDtypeStruct((n_rows, LANES), F32)],
        compiler_params=_cparams(("arbitrary",)),
        name="outproj",
    )(hn_p, att_p, x_p, pad(hn_s), pad(att_s), pad(x_s), w_out_bf, g_ffn.reshape(1, D_MODEL), w_router)


def _rank_kernel(route_ref, slot_ref, meta_ref, cnt_ref, base_ref, *, n_tok, blk):
    ph, i = pl.program_id(0), pl.program_id(1)
    tm = route_ref.shape[0]
    lane = lax.broadcasted_iota(jnp.int32, (tm, LANES), 1).astype(F32)
    rowg = i * tm + lax.broadcasted_iota(jnp.int32, (tm, LANES), 0)
    valid = rowg < n_tok
    r = route_ref[...]
    oh0 = jnp.where(jnp.logical_and(valid, lane == r[:, 0:1]), 1.0, 0.0)
    oh1 = jnp.where(jnp.logical_and(valid, lane == r[:, 1:2]), 1.0, 0.0)
    oh = oh0 + oh1
    colsum = jnp.sum(oh, axis=0, keepdims=True)

    @pl.when(jnp.logical_and(ph == 0, i == 0))
    def _zero():
        cnt_ref[...] = jnp.zeros_like(cnt_ref)

    @pl.when(ph == 0)
    def _count():
        cnt_ref[...] += colsum

    @pl.when(jnp.logical_and(ph == 1, i == 0))
    def _starts():
        cnt = cnt_ref[...].astype(jnp.int32)
        shift = blk.bit_length() - 1
        padded = (((cnt + (blk - 1)) >> shift) << shift).astype(F32)
        up = (lax.broadcasted_iota(jnp.int32, (LANES, LANES), 0)
              < lax.broadcasted_iota(jnp.int32, (LANES, LANES), 1)).astype(F32)
        start = jnp.dot(jnp.broadcast_to(padded, (8, LANES)), up, precision=HIGHEST,
                        preferred_element_type=F32)[0:1]
        base_ref[...] = start
        nb = meta_ref.shape[0]
        lane_b = lax.broadcasted_iota(jnp.int32, (nb, LANES), 1)
        blk_start = (lax.broadcasted_iota(jnp.int32, (nb, LANES), 0) * blk).astype(F32)
        ended = jnp.logical_and(start + padded <= blk_start, lane_b < N_EXPERTS)
        be = jnp.minimum(jnp.sum(jnp.where(ended, 1.0, 0.0), axis=-1, keepdims=True), N_EXPERTS - 1.0)
        n_used = jnp.sum(padded, axis=-1, keepdims=True) * (1.0 / blk)
        meta_ref[...] = jnp.where(lane_b == 0, be, jnp.where(lane_b == 1, n_used, 0.0)).astype(jnp.int32)

    @pl.when(ph == 1)
    def _slots():
        before = (lax.broadcasted_iota(jnp.int32, (tm, tm), 1)
                  < lax.broadcasted_iota(jnp.int32, (tm, tm), 0)).astype(BF16)
        pre = jnp.dot(before, oh.astype(BF16), preferred_element_type=F32) + base_ref[...]
        s0 = jnp.sum(oh0 * pre, axis=-1, keepdims=True)
        s1 = jnp.sum(oh1 * pre, axis=-1, keepdims=True)
        slot_ref[...] = jnp.where(lane == 0, s0, jnp.where(lane == 1, s1, 0.0)).astype(jnp.int32)
        base_ref[...] += colsum


def _rank(route, n_tok, blk, nblk):
    n_rows = route.shape[0]
    nb = (nblk + 7) // 8 * 8
    slots, meta = pl.pallas_call(
        functools.partial(_rank_kernel, n_tok=n_tok, blk=blk),
        grid=(2, n_rows // ROUTE_TM),
        in_specs=[pl.BlockSpec((ROUTE_TM, LANES), lambda ph, i: (i, 0))],
        out_specs=[pl.BlockSpec((ROUTE_TM, LANES), lambda ph, i: (i * ph, 0)),
                   pl.BlockSpec((nb, LANES), lambda ph, i: (0, 0))],
        out_shape=[jax.ShapeDtypeStruct((n_rows, LANES), jnp.int32),
                   jax.ShapeDtypeStruct((nb, LANES), jnp.int32)],
        scratch_shapes=[pltpu.VMEM((1, LANES), F32), pltpu.VMEM((1, LANES), F32)],
        compiler_params=_cparams(("arbitrary", "arbitrary")),
        name="rank",
    )(route)
    return slots[:n_tok, :TOP_K_INNER].reshape(-1), meta[:nblk, 0], meta[0:1, 1]


DMA_UNROLL = 8


def _dispatch_kernel(slot_ref, h2_ref, xb_in, xb_out, sem, *, n_tok):
    del xb_in
    i = pl.program_id(0)
    tm = h2_ref.shape[0] // ROW_TILE
    tail = n_tok % tm

    def push(rows):
        def body(r, c):
            a = (i * tm + r) * TOP_K_INNER
            for k in range(TOP_K_INNER):
                pltpu.make_async_copy(_row_tile(h2_ref, r), _row_tile(xb_out, slot_ref[a + k]), sem).start()
            return c
        lax.fori_loop(0, rows, body, 0, unroll=DMA_UNROLL)
        for k in range(TOP_K_INNER):
            pltpu.make_async_copy(_row_tile(h2_ref, 0, rows), _row_tile(xb_out, 0, rows), sem).wait()

    last = pl.num_programs(0) - 1
    if tail == 0:
        push(tm)
    else:
        @pl.when(i < last)
        def _full():
            push(tm)

        @pl.when(i == last)
        def _tail():
            push(tail)


def _dispatch(slot_flat, h2, n_tok, n_slots):
    tm = ROUTE_TM
    grid_spec = pltpu.PrefetchScalarGridSpec(
        num_scalar_prefetch=1,
        grid=(h2.shape[0] // (tm * ROW_TILE),),
        in_specs=[pl.BlockSpec((tm * ROW_TILE, LANES), lambda i, s: (i, 0)),
                  pl.BlockSpec(memory_space=pl.ANY)],
        out_specs=pl.BlockSpec(memory_space=pl.ANY),
        scratch_shapes=[pltpu.SemaphoreType.DMA(())],
    )
    return pl.pallas_call(
        functools.partial(_dispatch_kernel, n_tok=n_tok),
        grid_spec=grid_spec,
        out_shape=jax.ShapeDtypeStruct((n_slots * ROW_TILE, LANES), F32),
        input_output_aliases={2: 0},
        compiler_params=_cparams(("arbitrary",)),
        name="dispatch",
    )(slot_flat, h2, jnp.zeros((n_slots * ROW_TILE, LANES), F32))


def _expert_kernel(be_ref, nu_ref, x_ref, wg_hbm, wu_hbm, wd_hbm, y_ref,
                   wg_f, wu_f, wd_f, sem, wgb, wub, wdb, cur_ref):
    i = pl.program_id(0)
    n_used = nu_ref[0]
    e = be_ref[i]
    e_prev = be_ref[jnp.maximum(i - 1, 0)]

    def fetch(ex, s):
        return [pltpu.make_async_copy(hbm.at[ex], buf.at[s], sem.at[s])
                for hbm, buf in ((wg_hbm, wg_f), (wu_hbm, wu_f), (wd_hbm, wd_f))]

    @pl.when(i == 0)
    def _first():
        cur_ref[0] = 0
        for c in fetch(e, 0):
            c.start()

    @pl.when(jnp.logical_and(jnp.logical_or(i == 0, e != e_prev), i < n_used))
    def _new_expert():
        s = cur_ref[0]
        j = lax.while_loop(lambda j: jnp.logical_and(j < n_used, be_ref[jnp.minimum(j, n_used - 1)] == e),
                           lambda j: j + 1, i + 1)

        @pl.when(j < n_used)
        def _prefetch():
            for c in fetch(be_ref[j], 1 - s):
                c.start()

        for c in fetch(e, s):
            c.wait()
        wgb[...] = wg_f[s].astype(BF16)
        wub[...] = wu_f[s].astype(BF16)
        wdb[...] = wd_f[s].astype(BF16)
        cur_ref[0] = 1 - s

    @pl.when(i < n_used)
    def _run():
        x = _load_row_tiles(x_ref, x_ref.shape[0] // ROW_TILE).astype(BF16)
        a = jnp.dot(x, wgb[...], preferred_element_type=F32)
        u = jnp.dot(x, wub[...], preferred_element_type=F32)
        mid = (a * _sigmoid(a) * u).astype(BF16)
        _store_row_tiles(y_ref, jnp.dot(mid, wdb[...], preferred_element_type=F32))

    @pl.when(i >= nu_ref[0])
    def _skip():
        y_ref[...] = jnp.zeros_like(y_ref)


def _experts(xb, blk_expert, n_used, w_g, w_u, w_d, blk):
    nblk = blk_expert.shape[0]
    hbm = pl.BlockSpec(memory_space=pl.ANY)
    grid_spec = pltpu.PrefetchScalarGridSpec(
        num_scalar_prefetch=2,
        grid=(nblk,),
        in_specs=[pl.BlockSpec((blk * ROW_TILE, LANES), lambda i, be, nu: (jnp.minimum(i, nu[0] - 1), 0)),
                  hbm, hbm, hbm],
        out_specs=pl.BlockSpec((blk * ROW_TILE, LANES), lambda i, be, nu: (i, 0)),
        scratch_shapes=[pltpu.VMEM((2, D_MODEL, D_FF_EXPERT), F32),
                        pltpu.VMEM((2, D_MODEL, D_FF_EXPERT), F32),
                        pltpu.VMEM((2, D_FF_EXPERT, D_MODEL), F32),
                        pltpu.SemaphoreType.DMA((2,)),
                        pltpu.VMEM((D_MODEL, D_FF_EXPERT), BF16),
                        pltpu.VMEM((D_MODEL, D_FF_EXPERT), BF16),
                        pltpu.VMEM((D_FF_EXPERT, D_MODEL), BF16),
                        pltpu.SMEM((1,), jnp.int32)],
    )
    return pl.pallas_call(
        _expert_kernel,
        grid_spec=grid_spec,
        out_shape=jax.ShapeDtypeStruct((nblk * blk * ROW_TILE, LANES), F32),
        compiler_params=_cparams(("arbitrary",)),
        name="experts",
    )(blk_expert, n_used, xb, w_g, w_u, w_d)


def _final_kernel(slot_ref, x_ref, route_ref, g_ref, yb_hbm, o_ref, ybuf, sem, *, tok0):
    i = pl.program_id(0)
    tm = x_ref.shape[0]

    def gather(j, buf):
        def body(r, c):
            a = (tok0 + j * tm + r) * TOP_K_INNER
            for k in range(TOP_K_INNER):
                pltpu.make_async_copy(_row_tile(yb_hbm, slot_ref[a + k]), _row_tile(ybuf, r, lead=(buf, k)),
                                      sem.at[buf]).start()
            return c
        lax.fori_loop(0, tm, body, 0, unroll=DMA_UNROLL)

    @pl.when(i == 0)
    def _first():
        gather(0, 0)

    @pl.when(i + 1 < pl.num_programs(0))
    def _next():
        gather(i + 1, (i + 1) % 2)

    buf = i % 2
    for k in range(TOP_K_INNER):
        pltpu.make_async_copy(_row_tile(yb_hbm, 0, tm), ybuf.at[buf, k], sem.at[buf]).wait()
    route = route_ref[...]
    y0 = _load_row_tiles(ybuf, tm, lead=(buf, 0))
    y1 = _load_row_tiles(ybuf, tm, lead=(buf, 1))
    x = x_ref[...] + (y0 * route[:, 2:3] + y1 * route[:, 3:4])
    ms = jnp.mean(x * x, axis=-1, keepdims=True)
    o_ref[...] = x * lax.rsqrt(ms + RMS_EPS) * g_ref[...]


def _final(slot_flat, xmid, route, g_final, yb, tok0, n_out, tm):
    blk0 = tok0 // tm
    grid_spec = pltpu.PrefetchScalarGridSpec(
        num_scalar_prefetch=1,
        grid=(n_out // tm,),
        in_specs=[pl.BlockSpec((tm, D_MODEL), lambda i, s: (i + blk0, 0)),
                  pl.BlockSpec((tm, LANES), lambda i, s: (i + blk0, 0)),
                  pl.BlockSpec((1, D_MODEL), lambda i, s: (0, 0)),
                  pl.BlockSpec(memory_space=pl.ANY)],
        out_specs=pl.BlockSpec((tm, D_MODEL), lambda i, s: (i, 0)),
        scratch_shapes=[pltpu.VMEM((2, TOP_K_INNER, tm * ROW_TILE, LANES), F32),
                        pltpu.SemaphoreType.DMA((2,))],
    )
    return pl.pallas_call(
        functools.partial(_final_kernel, tok0=tok0),
        grid_spec=grid_spec,
        out_shape=jax.ShapeDtypeStruct((n_out, D_MODEL), F32),
        compiler_params=_cparams(("arbitrary",)),
        name="final",
    )(slot_flat, xmid, route, g_final.reshape(1, D_MODEL), yb)


def kernel(x_prompt, x_sample, cache_attn_k, cache_attn_v, state_hgrn, w_in, w_out, hg_lb_logits,
           hg_norm_g, norm_mix_g, norm_ffn_g, norm_final_g, w_route_group, w_route_expert,
           w_expert_gate, w_expert_up, w_expert_down):
    bp, tp, _ = x_prompt.shape
    bs = x_sample.shape[0]
    l = 0
    w_in_bf = w_in[l].astype(BF16)
    w_out_bf = w_out[l].astype(BF16)
    w_router = jnp.concatenate(
        [w_route_group[l],
         jnp.transpose(w_route_expert[l], (1, 0, 2)).reshape(D_MODEL, N_EXPERTS),
         jnp.zeros((D_MODEL, LANES - N_GROUPS - N_EXPERTS), F32)], axis=-1)

    n_p = bp * tp
    xp = x_prompt.reshape(n_p, D_MODEL)
    pos_p = jnp.tile(jnp.arange(tp, dtype=jnp.int32), bp)
    hq, hk, hv, lf, zg, aq, ak, av = _inproj(xp, norm_mix_g[l], w_in_bf, hg_lb_logits, pos_p, 256)
    seq3 = lambda a: a.reshape(bp, tp, HG_WIDTH)
    hn_p, s_fin = _hgrn_prompt(seq3(hq), seq3(hk), seq3(hv), seq3(lf), seq3(zg), hg_norm_g[l])
    att_p = _attn_prompt(seq3(aq), seq3(ak), seq3(av))
    keep = min(MAX_WINDOW, tp)
    heads = lambda a: a.reshape(1, bp, keep, ATT_HEADS, ATT_HEAD_DIM)
    new_k_p = heads(seq3(ak)[:, tp - keep:])
    new_v_p = heads(seq3(av)[:, tp - keep:])

    xs = x_sample.reshape(bs, D_MODEL)
    pos_s = jnp.full((bs,), PAST_LEN, jnp.int32)
    hq, hk, hv, lf, zg, aq, ak, av = _inproj(xs, norm_mix_g[l], w_in_bf, hg_lb_logits, pos_s, bs)
    hn_s, s_new = _hgrn_step(hq, hk, hv, lf, zg, hg_norm_g[l], state_hgrn[l])
    feat = lambda a: jnp.transpose(a, (0, 2, 3, 1))
    att_s, new_k_s, new_v_s = _attn_step(aq, ak, av, feat(cache_attn_k[l]), feat(cache_attn_v[l]))
    cache5 = lambda a: jnp.transpose(a, (0, 3, 1, 2))[None]

    assert n_p % ROUTE_TM == 0 and bs <= ROUTE_TM
    n_tok = n_p + bs
    xmid, h2, route = _outproj(hn_p.reshape(n_p, HG_WIDTH), att_p.reshape(n_p, ATT_WIDTH), xp,
                               hn_s, att_s, xs, w_out_bf, norm_ffn_g[l], w_router)
    blk = MOE_BLOCK
    nblk = (n_tok * TOP_K_INNER + N_EXPERTS * (blk - 1)) // blk + 1
    slot_flat, blk_expert, n_used = _rank(route, n_tok, blk, nblk)
    xb = _dispatch(slot_flat, h2, n_tok, nblk * blk)
    yb = _experts(xb, blk_expert, n_used, w_expert_gate[l], w_expert_up[l], w_expert_down[l], blk)
    y_prompt = _final(slot_flat, xmid, route, norm_final_g, yb, 0, n_p, 256)
    y_sample = _final(slot_flat, xmid, route, norm_final_g, yb, n_p, bs, bs)

    return (y_prompt.reshape(bp, tp, D_MODEL), y_sample.reshape(bs, 1, D_MODEL),
            new_k_p, new_v_p, s_fin[None], cache5(new_k_s), cache5(new_v_s), s_new[None])
```

```python
import functools

import jax
import jax.numpy as jnp
from jax import lax
from jax.experimental import pallas as pl
from jax.experimental.pallas import tpu as pltpu

F32 = jnp.float32
BF16 = jnp.bfloat16

D_MODEL = 1024
HG_WIDTH = 512
HG_HEAD_DIM = 128
HG_HEADS = 4
ATT_WIDTH = 512
ATT_HEAD_DIM = 64
ATT_HEADS = 8
ROPE_DIM = 16
ROPE_THETA = 500000.0
DILATED_PATTERNS = ((128, 1), (512, 4), (2048, 16))
MAX_WINDOW = 2048
PAST_LEN = 16384
N_GROUPS = 8
EXPERTS_PER_GROUP = 8
N_EXPERTS = 64
TOP_K_INNER = 2
D_FF_EXPERT = 512
MOE_BLOCK = 128
IN_COLS = 4 * HG_WIDTH + 3 * ATT_WIDTH
RMS_EPS = 1e-6

LANES = 128
VMEM_LIMIT = 56 * 1024 * 1024
NEG = -1e30
HIGHEST = lax.Precision.HIGHEST
NT_DIMS = (((1,), (1,)), ((), ()))


def _sigmoid(z):
    return 1.0 / (1.0 + jnp.exp(-z))


def _cparams(sem):
    return pltpu.CompilerParams(dimension_semantics=sem, vmem_limit_bytes=VMEM_LIMIT)


def _inproj_kernel(x_ref, g_ref, w_ref, lbl_ref, cos_ref, sa_ref, sb_ref,
                   hq_ref, hk_ref, hv_ref, lf_ref, zg_ref, aq_ref, ak_ref, av_ref):
    x = x_ref[...]
    ms = jnp.mean(x * x, axis=-1, keepdims=True)
    h = (x * lax.rsqrt(ms + RMS_EPS) * g_ref[...]).astype(BF16)

    def mm(c0):
        return jnp.dot(h, w_ref[:, c0:c0 + HG_WIDTH], preferred_element_type=F32)

    lbl = lbl_ref[...]
    le = jnp.exp(lbl - jnp.max(lbl, axis=0, keepdims=True))
    lb = le[0:1, :] / jnp.sum(le, axis=0, keepdims=True)

    zq = mm(0)
    hq_ref[...] = zq * _sigmoid(zq)
    zf = mm(HG_WIDTH)
    f = lb + (1.0 - lb) * _sigmoid(zf)
    hk_ref[...] = 1.0 - f
    lf_ref[...] = jnp.log(f)
    hv_ref[...] = mm(2 * HG_WIDTH)
    zg_ref[...] = mm(3 * HG_WIDTH)

    cos, sa, sb = cos_ref[...], sa_ref[...], sb_ref[...]

    def rope(a, out_ref):
        for j in range(ATT_WIDTH // LANES):
            xj = a[:, j * LANES:(j + 1) * LANES]
            up = pltpu.roll(xj, LANES - ROPE_DIM // 2, 1)
            dn = pltpu.roll(xj, ROPE_DIM // 2, 1)
            out_ref[:, j * LANES:(j + 1) * LANES] = xj * cos + up * sa + dn * sb

    rope(mm(4 * HG_WIDTH), aq_ref)
    rope(mm(4 * HG_WIDTH + ATT_WIDTH), ak_ref)
    av_ref[...] = mm(4 * HG_WIDTH + 2 * ATT_WIDTH)


def _rope_tables(pos):
    half = ROPE_DIM // 2
    c = jnp.arange(LANES) % ATT_HEAD_DIM
    inv_freq = ROPE_THETA ** (-(c % half).astype(F32) / half)
    ang = pos.astype(F32)[:, None] * inv_freq[None, :]
    cos, sin = jnp.cos(ang), jnp.sin(ang)
    return (jnp.where(c < ROPE_DIM, cos, 1.0),
            jnp.where(c < half, -sin, 0.0),
            jnp.where(jnp.logical_and(c >= half, c < ROPE_DIM), sin, 0.0))


def _inproj(x2d, g, w_bf, lb_logits, pos, tm):
    m = x2d.shape[0]
    cos, sa, sb = _rope_tables(pos)
    row = lambda i: (i, 0)
    seq_tiles = pos.shape[0] // tm
    row_pos = lambda i: (i % seq_tiles, 0)
    const = lambda i: (0, 0)
    outs = [jax.ShapeDtypeStruct((m, HG_WIDTH), F32)] * 8
    return pl.pallas_call(
        _inproj_kernel,
        grid=(m // tm,),
        in_specs=[pl.BlockSpec((tm, D_MODEL), row),
                  pl.BlockSpec((1, D_MODEL), const),
                  pl.BlockSpec((D_MODEL, IN_COLS), const),
                  pl.BlockSpec(lb_logits.shape, const),
                  pl.BlockSpec((tm, LANES), row_pos),
                  pl.BlockSpec((tm, LANES), row_pos),
                  pl.BlockSpec((tm, LANES), row_pos)],
        out_specs=[pl.BlockSpec((tm, HG_WIDTH), row)] * 8,
        out_shape=outs,
        compiler_params=_cparams(("parallel",)),
        name="inproj",
    )(x2d, g.reshape(1, D_MODEL), w_bf, lb_logits, cos, sa, sb)


HG_C = 128
HG_SB = 16


def _hgrn_kernel(q_ref, k_ref, v_ref, lf_ref, zg_ref, g_ref, hn_ref, sfin_ref, st_ref, *, n_chunks):
    t = pl.program_id(2)

    @pl.when(t == 0)
    def _init():
        st_ref[...] = jnp.zeros_like(st_ref)

    ri = lax.broadcasted_iota(jnp.int32, (HG_C, HG_C), 0)
    ci = lax.broadcasted_iota(jnp.int32, (HG_C, HG_C), 1)
    ltri = (ri >= ci).astype(BF16)
    ones_b = jnp.ones((LANES, LANES), BF16)
    n_sb = HG_C // HG_SB
    row_sb = lax.broadcasted_iota(jnp.int32, (n_sb, HG_SB, LANES), 1)
    col_sb = lax.broadcasted_iota(jnp.int32, (n_sb, HG_SB, HG_C), 2)
    lo_sb = lax.broadcasted_iota(jnp.int32, (n_sb, HG_SB, HG_C), 0) * HG_SB
    g = g_ref[...]

    def chunk(c, carry):
        r0 = pl.multiple_of(c * HG_C, HG_C)
        q = q_ref[pl.ds(r0, HG_C), :]
        k = k_ref[pl.ds(r0, HG_C), :]
        v = v_ref[pl.ds(r0, HG_C), :]
        lf = lf_ref[pl.ds(r0, HG_C), :]
        lf_hi = lf.astype(BF16)
        lf_r = lf - lf_hi.astype(F32)
        lf_mid = lf_r.astype(BF16)
        lf_lo = (lf_r - lf_mid.astype(F32)).astype(BF16)
        b = (jnp.dot(ltri, lf_hi, preferred_element_type=F32)
             + (jnp.dot(ltri, lf_mid, preferred_element_type=F32)
                + jnp.dot(ltri, lf_lo, preferred_element_type=F32)))
        st = st_ref[...]
        vb = v.astype(BF16)
        qb = (q * jnp.exp(b)).astype(BF16)
        o_inter = lax.dot_general(qb, st.astype(BF16), NT_DIMS, preferred_element_type=F32)
        b3, q3, k3, v3 = (a.reshape(n_sb, HG_SB, LANES) for a in (b, q, k, v))
        ps = []
        for s in range(HG_SB):
            d = jnp.where(row_sb >= s, b3 - b3[:, s:s + 1, :], NEG)
            ps.append(q3 * jnp.exp(d) * k3[:, s:s + 1, :])
        p_all = jnp.concatenate(ps, axis=1).reshape(n_sb * HG_SB * HG_SB, LANES).astype(BF16)
        r_all = jnp.dot(p_all, ones_b, preferred_element_type=F32)
        r_all = r_all.reshape(n_sb, HG_SB * HG_SB, LANES)
        o3 = o_inter.reshape(n_sb, HG_SB, LANES)
        for s in range(HG_SB):
            o3 = o3 + r_all[:, s * HG_SB:(s + 1) * HG_SB, :] * v3[:, s:s + 1, :]
        b_ref = jnp.concatenate([b3[0:1, 0:1], b3[:n_sb - 1, HG_SB - 1:HG_SB]], axis=0)
        qs = (q3 * jnp.exp(jnp.minimum(b3 - b_ref, 0.0))).astype(BF16)
        ks = (k[None] * jnp.exp(jnp.minimum(b_ref - b[None], 0.0))).astype(BF16)
        a = lax.dot_general(qs, ks, (((2,), (2,)), ((0,), (0,))), preferred_element_type=F32)
        a = jnp.where(col_sb < lo_sb, a, 0.0).astype(BF16).reshape(HG_C, HG_C)
        o = o3.reshape(HG_C, LANES) + jnp.dot(a, vb, preferred_element_type=F32)
        b_last = b[HG_C - 1:HG_C, :]
        kdec = (k * jnp.exp(b_last - b)).astype(BF16)
        st_ref[...] = st * jnp.exp(b_last) + jnp.dot(v.T.astype(BF16), kdec, preferred_element_type=F32)
        ms = jnp.mean(o * o, axis=-1, keepdims=True)
        zg = zg_ref[pl.ds(r0, HG_C), :]
        hn_ref[pl.ds(r0, HG_C), :] = o * lax.rsqrt(ms + RMS_EPS) * g * (zg * _sigmoid(zg))
        return carry

    lax.fori_loop(0, n_chunks, chunk, 0, unroll=2)

    @pl.when(t == pl.num_programs(2) - 1)
    def _fin():
        sfin_ref[...] = st_ref[...].T


def _hgrn_prompt(hq, hk, hv, lf, zg, g_hg, tb=1024):
    bsz, t, _ = hq.shape
    seq = pl.BlockSpec((None, tb, HG_HEAD_DIM), lambda b, h, i: (b, i, h))
    return pl.pallas_call(
        functools.partial(_hgrn_kernel, n_chunks=tb // HG_C),
        grid=(bsz, HG_HEADS, t // tb),
        in_specs=[seq, seq, seq, seq, seq,
                  pl.BlockSpec((1, HG_HEAD_DIM), lambda b, h, i: (0, h))],
        out_specs=[seq,
                   pl.BlockSpec((None, None, HG_HEAD_DIM, HG_HEAD_DIM), lambda b, h, i: (b, h, 0, 0))],
        out_shape=[jax.ShapeDtypeStruct((bsz, t, HG_WIDTH), F32),
                   jax.ShapeDtypeStruct((bsz, HG_HEADS, HG_HEAD_DIM, HG_HEAD_DIM), F32)],
        scratch_shapes=[pltpu.VMEM((HG_HEAD_DIM, HG_HEAD_DIM), F32)],
        compiler_params=_cparams(("parallel", "parallel", "arbitrary")),
        name="hgrn_prompt",
    )(hq, hk, hv, lf, zg, g_hg.reshape(1, HG_WIDTH))


def _hgrn_step_kernel(q_ref, k_ref, v_ref, lf_ref, zg_ref, g_ref, s_ref, hn_ref, snew_ref):
    row = slice(None)
    zeros = jnp.zeros((HG_HEAD_DIM - 3, HG_HEAD_DIM), F32)
    for h in range(HG_HEADS):
        cs = slice(h * HG_HEAD_DIM, (h + 1) * HG_HEAD_DIM)
        q, k, v = q_ref[row, cs], k_ref[row, cs], v_ref[row, cs]
        f = jnp.exp(lf_ref[row, cs])
        cols = jnp.concatenate([f, k, q, zeros], axis=0).T
        s_new = cols[:, 0:1] * s_ref[h] + cols[:, 1:2] * v
        snew_ref[h] = s_new
        o = jnp.sum(cols[:, 2:3] * s_new, axis=0, keepdims=True)
        ms = jnp.mean(o * o, axis=-1, keepdims=True)
        zg = zg_ref[row, cs]
        hn_ref[row, cs] = o * lax.rsqrt(ms + RMS_EPS) * g_ref[:, cs] * (zg * _sigmoid(zg))


def _hgrn_step(hq, hk, hv, lf, zg, g_hg, state):
    bsz = hq.shape[0]
    one = pl.BlockSpec((None, 1, HG_WIDTH), lambda b: (b, 0, 0))
    st = pl.BlockSpec((None, HG_HEADS, HG_HEAD_DIM, HG_HEAD_DIM), lambda b: (b, 0, 0, 0))
    r3 = lambda a: a.reshape(bsz, 1, HG_WIDTH)
    hn, s_new = pl.pallas_call(
        _hgrn_step_kernel,
        grid=(bsz,),
        in_specs=[one, one, one, one, one, pl.BlockSpec((1, HG_WIDTH), lambda b: (0, 0)), st],
        out_specs=[one, st],
        out_shape=[jax.ShapeDtypeStruct((bsz, 1, HG_WIDTH), F32),
                   jax.ShapeDtypeStruct(state.shape, F32)],
        compiler_params=_cparams(("parallel",)),
        name="hgrn_step",
    )(r3(hq), r3(hk), r3(hv), r3(lf), r3(zg), g_hg.reshape(1, HG_WIDTH), state)
    return hn.reshape(bsz, HG_WIDTH), s_new


ATT_N = 128
ATT_SUPER = 2048
ATT_G = 4


def _attn_prompt_kernel(q_ref, k_ref, v_ref, o_ref, osc, lsc, *, seq_len):
    lane = lax.broadcasted_iota(jnp.int32, (ATT_N, LANES), 1)
    rowi = lax.broadcasted_iota(jnp.int32, (ATT_N, LANES), 0)
    head0 = lane < ATT_HEAD_DIM
    kidx = lax.broadcasted_iota(jnp.int32, (ATT_N, 2 * ATT_N), 1)
    qidx = lax.broadcasted_iota(jnp.int32, (ATT_N, 2 * ATT_N), 0)
    band = jnp.logical_and(kidx >= qidx, kidx <= qidx + ATT_N)
    in_prev = kidx < ATT_N
    scale = ATT_HEAD_DIM ** -0.5

    bidx = lax.broadcasted_iota(jnp.int32, (ATT_G, ATT_N, 2 * ATT_N), 0)
    bqk = (((2,), (2,)), ((0,), (0,)))
    bkd = (((2,), (1,)), ((0,), (0,)))
    rows_g = ATT_G * ATT_N

    def do_group(p, d, base, g):
        span = ATT_N * d
        if d == 1:
            off = g * rows_g
            start = base + off

            def cur(ref):
                return ref[pl.ds(start, rows_g), :].reshape(ATT_G, ATT_N, LANES)

            def prv(ref, c):
                before = ref[pl.ds(jnp.maximum(start - ATT_N, 0), ATT_N), :].astype(BF16)
                return jnp.concatenate([before[None], c[:ATT_G - 1]], axis=0)

            pen = jnp.where(jnp.logical_and(jnp.logical_and(bidx == 0, in_prev), start == 0), NEG, 0.0)
        else:
            per_blk = d // ATT_G
            off = (g // per_blk) * span + (g % per_blk) * ATT_G
            start = base + off
            prev = jnp.maximum(start - span, 0)

            def cur(ref):
                return jnp.stack([ref[pl.ds(start + r, ATT_N, stride=d), :] for r in range(ATT_G)])

            def prv(ref, c):
                return jnp.stack([ref[pl.ds(prev + r, ATT_N, stride=d), :] for r in range(ATT_G)]).astype(BF16)

            pen = jnp.where(jnp.logical_and(in_prev, start < span), NEG, 0.0)
        q = cur(q_ref) * scale
        kc = cur(k_ref).astype(BF16)
        vc = cur(v_ref).astype(BF16)
        kk = jnp.concatenate([prv(k_ref, kc), kc], axis=1)
        vv = jnp.concatenate([prv(v_ref, vc), vc], axis=1)
        res = []
        for hm in (head0, jnp.logical_not(head0)):
            qh = jnp.where(hm, q, 0.0).astype(BF16)
            s = lax.dot_general(qh, kk, bqk, preferred_element_type=F32)
            s = jnp.where(band, s, NEG) + pen
            m = jnp.max(s, axis=-1, keepdims=True)
            pr = jnp.exp(s - m)
            den = jnp.sum(pr, axis=-1, keepdims=True)
            o = lax.dot_general(pr.astype(BF16), vv, bkd, preferred_element_type=F32)
            res.append((o / den, m + jnp.log(den)))
        o = jnp.where(head0, res[0][0], res[1][0])
        lse = jnp.where(head0, res[0][1], res[1][1])
        if d == 1:
            osc[p, pl.ds(off, rows_g), :] = o.reshape(rows_g, LANES)
            lsc[p, pl.ds(off, rows_g), :] = lse.reshape(rows_g, LANES)
        else:
            for r in range(ATT_G):
                osc[p, pl.ds(off + r, ATT_N, stride=d), :] = o[r]
                lsc[p, pl.ds(off + r, ATT_N, stride=d), :] = lse[r]

    def superblock(sb, carry):
        base = sb * ATT_SUPER

        def groups(g, c2):
            for p, (w, d) in enumerate(DILATED_PATTERNS):
                do_group(p, d, base, g)
            return c2

        lax.fori_loop(0, ATT_SUPER // rows_g, groups, 0)

        piece = 256

        def merge(j, c2):
            r = pl.ds(pl.multiple_of(j * piece, piece), piece)
            ls = [lsc[p, r, :] for p in range(len(DILATED_PATTERNS))]
            mx = jnp.maximum(jnp.maximum(ls[0], ls[1]), ls[2])
            ws = [jnp.exp(l - mx) for l in ls]
            num = ws[0] * osc[0, r, :] + ws[1] * osc[1, r, :] + ws[2] * osc[2, r, :]
            o_ref[pl.ds(pl.multiple_of(base + j * piece, piece), piece), :] = num / (ws[0] + ws[1] + ws[2])
            return c2

        lax.fori_loop(0, ATT_SUPER // piece, merge, 0)
        return carry

    lax.fori_loop(0, seq_len // ATT_SUPER, superblock, 0)


def _attn_prompt(aq, ak, av):
    bsz, t, _ = aq.shape
    spec = pl.BlockSpec((None, t, LANES), lambda b, p: (b, 0, p))
    n_pat = len(DILATED_PATTERNS)
    return pl.pallas_call(
        functools.partial(_attn_prompt_kernel, seq_len=t),
        grid=(bsz, ATT_WIDTH // LANES),
        in_specs=[spec, spec, spec],
        out_specs=spec,
        out_shape=jax.ShapeDtypeStruct((bsz, t, ATT_WIDTH), F32),
        scratch_shapes=[pltpu.VMEM((n_pat, ATT_SUPER, LANES), F32),
                        pltpu.VMEM((n_pat, ATT_SUPER, LANES), F32)],
        compiler_params=_cparams(("parallel", "parallel")),
        name="attn_prompt",
    )(aq, ak, av)


def _attn_step_kernel(q_ref, kn_ref, vn_ref, ck_ref, cv_ref, o_ref, nk_ref, nv_ref):
    win = ck_ref.shape[-1]
    kt, vt = ck_ref[...], cv_ref[...]
    q = q_ref[...] * (ATT_HEAD_DIM ** -0.5)
    kn, vn = kn_ref[...], vn_ref[...]
    s_all = jnp.sum(kt * q, axis=1, keepdims=True)
    s_new = jnp.sum(kn * q, axis=1, keepdims=True)
    dist = win - lax.broadcasted_iota(jnp.int32, (1, 1, win), 2)

    ps, pnews, lses = [], [], []
    for w, d in DILATED_PATTERNS:
        on_stride = (dist & (d - 1)) == 0 if d & (d - 1) == 0 else dist % d == 0
        valid = jnp.logical_and(dist <= w, on_stride)
        sm = jnp.where(valid, s_all, NEG)
        m = jnp.maximum(jnp.max(sm, axis=-1, keepdims=True), s_new)
        p = jnp.exp(sm - m)
        pn = jnp.exp(s_new - m)
        den = jnp.sum(p, axis=-1, keepdims=True) + pn
        ps.append(p / den)
        pnews.append(pn / den)
        lses.append(m + jnp.log(den))
    mx = jnp.maximum(jnp.maximum(lses[0], lses[1]), lses[2])
    ws = [jnp.exp(l - mx) for l in lses]
    wsum = ws[0] + ws[1] + ws[2]
    p_tot = (ws[0] * ps[0] + ws[1] * ps[1] + ws[2] * ps[2]) / wsum
    pn_tot = (ws[0] * pnews[0] + ws[1] * pnews[1] + ws[2] * pnews[2]) / wsum
    o_ref[...] = jnp.sum(vt * p_tot, axis=-1, keepdims=True) + pn_tot * vn

    last = lax.broadcasted_iota(jnp.int32, (1, 1, win), 2) == win - 1
    nk_ref[...] = jnp.where(last, kn, pltpu.roll(kt, win - 1, 2))
    nv_ref[...] = jnp.where(last, vn, pltpu.roll(vt, win - 1, 2))


def _attn_step(aq, ak, av, cache_k, cache_v):
    bsz, _, _, win = cache_k.shape
    one = pl.BlockSpec((None, ATT_HEADS, ATT_HEAD_DIM, 1), lambda b: (b, 0, 0, 0))
    cache = pl.BlockSpec((None, ATT_HEADS, ATT_HEAD_DIM, win), lambda b: (b, 0, 0, 0))
    col = lambda a: a.reshape(bsz, ATT_HEADS, ATT_HEAD_DIM, 1)
    att, new_k, new_v = pl.pallas_call(
        _attn_step_kernel,
        grid=(bsz,),
        in_specs=[one, one, one, cache, cache],
        out_specs=[one, cache, cache],
        out_shape=[jax.ShapeDtypeStruct((bsz, ATT_HEADS, ATT_HEAD_DIM, 1), F32),
                   jax.ShapeDtypeStruct(cache_k.shape, F32),
                   jax.ShapeDtypeStruct(cache_v.shape, F32)],
        compiler_params=_cparams(("parallel",)),
        name="attn_step",
    )(col(aq), col(ak), col(av), cache_k, cache_v)
    return att.reshape(bsz, ATT_WIDTH), new_k, new_v


ROUTE_TM = 256
SUBLANES = 8
ROW_TILE = D_MODEL // LANES
assert ROW_TILE == SUBLANES


def _store_row_tiles(ref, val, lead=()):
    n = val.shape[0]
    for j in range(ROW_TILE):
        ref[lead + (pl.ds(j, n, stride=ROW_TILE), slice(None))] = val[:, j * LANES:(j + 1) * LANES]


def _load_row_tiles(ref, n, lead=()):
    return jnp.concatenate([ref[lead + (pl.ds(j, n, stride=ROW_TILE), slice(None))] for j in range(ROW_TILE)],
                           axis=1)


def _row_tile(ref, r, n=1, lead=()):
    start = r * ROW_TILE if isinstance(r, int) else pl.multiple_of(r * ROW_TILE, ROW_TILE)
    return ref.at[lead + (pl.ds(start, n * ROW_TILE),)]


def _outproj_kernel(hn_ref, att_ref, x_ref, hn_s_ref, att_s_ref, x_s_ref, w_ref, g_ref, wr_ref,
                    xmid_ref, h2_ref, route_ref, cnt_ref, *, n_sample):
    is_prompt = pl.program_id(0) < pl.num_programs(0) - 1
    hn = jnp.where(is_prompt, hn_ref[...], hn_s_ref[...])
    att = jnp.where(is_prompt, att_ref[...], att_s_ref[...])
    y = (jnp.dot(hn.astype(BF16), w_ref[0:HG_WIDTH, :], preferred_element_type=F32)
         + jnp.dot(att.astype(BF16), w_ref[HG_WIDTH:, :], preferred_element_type=F32))
    xm = jnp.where(is_prompt, x_ref[...], x_s_ref[...]) + y
    xmid_ref[...] = xm
    ms = jnp.mean(xm * xm, axis=-1, keepdims=True)
    h2 = xm * lax.rsqrt(ms + RMS_EPS) * g_ref[...]
    _store_row_tiles(h2_ref, h2)
    h2_hi = h2.astype(BF16)
    h2_lo = (h2 - h2_hi.astype(F32)).astype(BF16)
    lg = (jnp.dot(h2_hi, wr_ref[0], preferred_element_type=F32)
          + (jnp.dot(h2_lo, wr_ref[0], preferred_element_type=F32)
             + jnp.dot(h2_hi, wr_ref[1], preferred_element_type=F32)))
    lane = lax.broadcasted_iota(jnp.int32, lg.shape, 1).astype(F32)
    big = float(LANES)
    gmask = lane < N_GROUPS
    lgg = jnp.where(gmask, lg, NEG)
    mg = jnp.max(lgg, axis=-1, keepdims=True)
    gi = jnp.min(jnp.where(lgg == mg, lane, big), axis=-1, keepdims=True)
    p_grp = 1.0 / jnp.sum(jnp.exp(lgg - mg), axis=-1, keepdims=True)
    lo = N_GROUPS + gi * EXPERTS_PER_GROUP
    emask = jnp.logical_and(lane >= lo, lane < lo + EXPERTS_PER_GROUP)
    le1 = jnp.where(emask, lg, NEG)
    m1 = jnp.max(le1, axis=-1, keepdims=True)
    i1 = jnp.min(jnp.where(le1 == m1, lane, big), axis=-1, keepdims=True)
    le2 = jnp.where(lane == i1, NEG, le1)
    m2 = jnp.max(le2, axis=-1, keepdims=True)
    i2 = jnp.min(jnp.where(le2 == m2, lane, big), axis=-1, keepdims=True)
    r = jnp.exp(m2 - m1)
    g1 = p_grp / (1.0 + r)
    g2 = p_grp * r / (1.0 + r)
    e1, e2 = i1 - N_GROUPS, i2 - N_GROUPS
    route_ref[...] = jnp.where(lane == 0, e1,
                               jnp.where(lane == 1, e2,
                                         jnp.where(lane == 2, g1, jnp.where(lane == 3, g2, 0.0))))

    @pl.when(pl.program_id(0) == 0)
    def _zero():
        cnt_ref[...] = jnp.zeros_like(cnt_ref)

    rows = lax.broadcasted_iota(jnp.int32, lg.shape, 0)
    real = jnp.logical_or(is_prompt, rows < n_sample)
    hit = jnp.logical_and(real, jnp.logical_or(lane == e1, lane == e2))
    cnt_ref[...] += jnp.sum(jnp.where(hit, 1.0, 0.0), axis=0, keepdims=True)


def _outproj(hn_p, att_p, x_p, hn_s, att_s, x_s, w_out_bf, g_ffn, w_router):
    tm = ROUTE_TM
    n_p = x_p.shape[0]
    n_tiles = n_p // tm + 1
    pad = lambda a: jnp.pad(a, ((0, tm - a.shape[0]), (0, 0)))
    row_p = lambda i: (jnp.minimum(i, n_tiles - 2), 0)
    row = lambda i: (i, 0)
    const = lambda i: (0, 0)
    n_rows = n_tiles * tm
    return pl.pallas_call(
        functools.partial(_outproj_kernel, n_sample=x_s.shape[0]),
        grid=(n_tiles,),
        in_specs=[pl.BlockSpec((tm, HG_WIDTH), row_p),
                  pl.BlockSpec((tm, ATT_WIDTH), row_p),
                  pl.BlockSpec((tm, D_MODEL), row_p),
                  pl.BlockSpec((tm, HG_WIDTH), const),
                  pl.BlockSpec((tm, ATT_WIDTH), const),
                  pl.BlockSpec((tm, D_MODEL), const),
                  pl.BlockSpec((D_MODEL, D_MODEL), const),
                  pl.BlockSpec((1, D_MODEL), const),
                  pl.BlockSpec((2, D_MODEL, LANES), lambda i: (0, 0, 0))],
        out_specs=[pl.BlockSpec((tm, D_MODEL), row),
                   pl.BlockSpec((tm * ROW_TILE, LANES), row),
                   pl.BlockSpec((tm, LANES), row),
                   pl.BlockSpec((SUBLANES, LANES), const)],
        out_shape=[jax.ShapeDtypeStruct((n_rows, D_MODEL), F32),
                   jax.ShapeDtypeStruct((n_rows * ROW_TILE, LANES), F32),
                   jax.ShapeDtypeStruct((n_rows, LANES), F32),
                   jax.ShapeDtypeStruct((SUBLANES, LANES), F32)],
        compiler_params=_cparams(("arbitrary",)),
        name="outproj",
    )(hn_p, att_p, x_p, pad(hn_s), pad(att_s), pad(x_s), w_out_bf, g_ffn.reshape(1, D_MODEL), w_router)


def _rank_kernel(route_ref, cnt_ref, slot_ref, meta_ref, base_ref, *, n_tok, blk):
    i = pl.program_id(0)
    tm = route_ref.shape[0]
    lane = lax.broadcasted_iota(jnp.int32, (tm, LANES), 1).astype(F32)
    rowg = i * tm + lax.broadcasted_iota(jnp.int32, (tm, LANES), 0)
    valid = rowg < n_tok
    r = route_ref[...]
    oh0 = jnp.where(jnp.logical_and(valid, lane == r[:, 0:1]), 1.0, 0.0)
    oh1 = jnp.where(jnp.logical_and(valid, lane == r[:, 1:2]), 1.0, 0.0)
    oh = oh0 + oh1

    @pl.when(i == 0)
    def _starts():
        cnt = cnt_ref[0:1, :].astype(jnp.int32)
        shift = blk.bit_length() - 1
        padded = (((cnt + (blk - 1)) >> shift) << shift).astype(F32)
        up = (lax.broadcasted_iota(jnp.int32, (LANES, LANES), 0)
              < lax.broadcasted_iota(jnp.int32, (LANES, LANES), 1)).astype(F32)
        start = jnp.dot(jnp.broadcast_to(padded, (8, LANES)), up, precision=HIGHEST,
                        preferred_element_type=F32)[0:1]
        base_ref[...] = start
        nb = meta_ref.shape[0]
        lane_b = lax.broadcasted_iota(jnp.int32, (nb, LANES), 1)
        blk_start = (lax.broadcasted_iota(jnp.int32, (nb, LANES), 0) * blk).astype(F32)
        ended = jnp.logical_and(start + padded <= blk_start, lane_b < N_EXPERTS)
        be = jnp.minimum(jnp.sum(jnp.where(ended, 1.0, 0.0), axis=-1, keepdims=True), N_EXPERTS - 1.0)
        n_used = jnp.sum(padded, axis=-1, keepdims=True) * (1.0 / blk)
        meta_ref[...] = jnp.where(lane_b == 0, be, jnp.where(lane_b == 1, n_used, 0.0)).astype(jnp.int32)

    before = (lax.broadcasted_iota(jnp.int32, (tm, tm), 1)
              < lax.broadcasted_iota(jnp.int32, (tm, tm), 0)).astype(BF16)
    pre = jnp.dot(before, oh.astype(BF16), preferred_element_type=F32) + base_ref[...]
    s0 = jnp.sum(oh0 * pre, axis=-1, keepdims=True)
    s1 = jnp.sum(oh1 * pre, axis=-1, keepdims=True)
    slot_ref[...] = jnp.where(lane == 0, s0, jnp.where(lane == 1, s1, 0.0)).astype(jnp.int32)
    base_ref[...] += jnp.sum(oh, axis=0, keepdims=True)


def _rank(route, counts, n_tok, blk, nblk):
    n_rows = route.shape[0]
    nb = (nblk + 7) // 8 * 8
    slots, meta = pl.pallas_call(
        functools.partial(_rank_kernel, n_tok=n_tok, blk=blk),
        grid=(n_rows // ROUTE_TM,),
        in_specs=[pl.BlockSpec((ROUTE_TM, LANES), lambda i: (i, 0)),
                  pl.BlockSpec(counts.shape, lambda i: (0, 0))],
        out_specs=[pl.BlockSpec((ROUTE_TM, LANES), lambda i: (i, 0)),
                   pl.BlockSpec((nb, LANES), lambda i: (0, 0))],
        out_shape=[jax.ShapeDtypeStruct((n_rows, LANES), jnp.int32),
                   jax.ShapeDtypeStruct((nb, LANES), jnp.int32)],
        scratch_shapes=[pltpu.VMEM((1, LANES), F32)],
        compiler_params=_cparams(("arbitrary",)),
        name="rank",
    )(route, counts)
    return slots[:n_tok, :TOP_K_INNER].reshape(-1), meta[:nblk, 0], meta[0:1, 1]


DMA_UNROLL = 8


def _dispatch_kernel(slot_ref, be_ref, nu_ref, h2_ref, xb_out, zbuf, sem, zsem, *, n_tok):
    i = pl.program_id(0)
    tm = h2_ref.shape[0] // ROW_TILE
    tail = n_tok % tm

    @pl.when(i == 0)
    def _zero_padding():
        blk = zbuf.shape[0] // ROW_TILE
        n_blocks = be_ref.shape[0]
        zbuf[...] = jnp.zeros_like(zbuf)

        def ends_expert(j):
            nxt = be_ref[jnp.minimum(j + 1, n_blocks - 1)]
            return jnp.logical_or(j >= nu_ref[0] - 1, be_ref[j] != nxt)

        def issue(j, c):
            @pl.when(ends_expert(j))
            def _():
                pltpu.make_async_copy(zbuf, _row_tile(xb_out, j * blk, blk), zsem).start()
            return c

        def drain(j, c):
            @pl.when(ends_expert(j))
            def _():
                pltpu.make_async_copy(zbuf, _row_tile(xb_out, 0, blk), zsem).wait()
            return c

        lax.fori_loop(0, n_blocks, issue, 0)
        lax.fori_loop(0, n_blocks, drain, 0)

    def push(rows):
        def body(r, c):
            a = (i * tm + r) * TOP_K_INNER
            for k in range(TOP_K_INNER):
                pltpu.make_async_copy(_row_tile(h2_ref, r), _row_tile(xb_out, slot_ref[a + k]), sem).start()
            return c
        lax.fori_loop(0, rows, body, 0, unroll=DMA_UNROLL)
        for k in range(TOP_K_INNER):
            pltpu.make_async_copy(_row_tile(h2_ref, 0, rows), _row_tile(xb_out, 0, rows), sem).wait()

    last = pl.num_programs(0) - 1
    if tail == 0:
        push(tm)
    else:
        @pl.when(i < last)
        def _full():
            push(tm)

        @pl.when(i == last)
        def _tail():
            push(tail)


def _dispatch(slot_flat, blk_expert, n_used, h2, n_tok, blk):
    tm = ROUTE_TM
    n_slots = blk_expert.shape[0] * blk
    grid_spec = pltpu.PrefetchScalarGridSpec(
        num_scalar_prefetch=3,
        grid=(h2.shape[0] // (tm * ROW_TILE),),
        in_specs=[pl.BlockSpec((tm * ROW_TILE, LANES), lambda i, s, be, nu: (i, 0))],
        out_specs=pl.BlockSpec(memory_space=pl.ANY),
        scratch_shapes=[pltpu.VMEM((blk * ROW_TILE, LANES), F32),
                        pltpu.SemaphoreType.DMA(()),
                        pltpu.SemaphoreType.DMA(())],
    )
    return pl.pallas_call(
        functools.partial(_dispatch_kernel, n_tok=n_tok),
        grid_spec=grid_spec,
        out_shape=jax.ShapeDtypeStruct((n_slots * ROW_TILE, LANES), F32),
        compiler_params=_cparams(("arbitrary",)),
        name="dispatch",
    )(slot_flat, blk_expert, n_used, h2)


def _expert_kernel(be_ref, nu_ref, x_ref, wg_hbm, wu_hbm, wd_hbm, y_ref,
                   wg_f, wu_f, wd_f, sem, wgb, wub, wdb, cur_ref):
    i = pl.program_id(0)
    n_used = nu_ref[0]
    e = be_ref[i]
    e_prev = be_ref[jnp.maximum(i - 1, 0)]

    def fetch(ex, s):
        return [pltpu.make_async_copy(hbm.at[ex], buf.at[s], sem.at[s])
                for hbm, buf in ((wg_hbm, wg_f), (wu_hbm, wu_f), (wd_hbm, wd_f))]

    @pl.when(i == 0)
    def _first():
        cur_ref[0] = 0
        for c in fetch(e, 0):
            c.start()

    @pl.when(jnp.logical_and(jnp.logical_or(i == 0, e != e_prev), i < n_used))
    def _new_expert():
        s = cur_ref[0]
        j = lax.while_loop(lambda j: jnp.logical_and(j < n_used, be_ref[jnp.minimum(j, n_used - 1)] == e),
                           lambda j: j + 1, i + 1)

        @pl.when(j < n_used)
        def _prefetch():
            for c in fetch(be_ref[j], 1 - s):
                c.start()

        for c in fetch(e, s):
            c.wait()
        wgb[...] = wg_f[s].astype(BF16)
        wub[...] = wu_f[s].astype(BF16)
        wdb[...] = wd_f[s].astype(BF16)
        cur_ref[0] = 1 - s

    @pl.when(i < n_used)
    def _run():
        x = _load_row_tiles(x_ref, x_ref.shape[0] // ROW_TILE).astype(BF16)
        a = jnp.dot(x, wgb[...], preferred_element_type=F32)
        u = jnp.dot(x, wub[...], preferred_element_type=F32)
        mid = (a * _sigmoid(a) * u).astype(BF16)
        _store_row_tiles(y_ref, jnp.dot(mid, wdb[...], preferred_element_type=F32))

    @pl.when(i >= nu_ref[0])
    def _skip():
        y_ref[...] = jnp.zeros_like(y_ref)


def _experts(xb, blk_expert, n_used, w_g, w_u, w_d, blk):
    nblk = blk_expert.shape[0]
    hbm = pl.BlockSpec(memory_space=pl.ANY)
    grid_spec = pltpu.PrefetchScalarGridSpec(
        num_scalar_prefetch=2,
        grid=(nblk,),
        in_specs=[pl.BlockSpec((blk * ROW_TILE, LANES), lambda i, be, nu: (jnp.minimum(i, nu[0] - 1), 0)),
                  hbm, hbm, hbm],
        out_specs=pl.BlockSpec((blk * ROW_TILE, LANES), lambda i, be, nu: (i, 0)),
        scratch_shapes=[pltpu.VMEM((2, D_MODEL, D_FF_EXPERT), F32),
                        pltpu.VMEM((2, D_MODEL, D_FF_EXPERT), F32),
                        pltpu.VMEM((2, D_FF_EXPERT, D_MODEL), F32),
                        pltpu.SemaphoreType.DMA((2,)),
                        pltpu.VMEM((D_MODEL, D_FF_EXPERT), BF16),
                        pltpu.VMEM((D_MODEL, D_FF_EXPERT), BF16),
                        pltpu.VMEM((D_FF_EXPERT, D_MODEL), BF16),
                        pltpu.SMEM((1,), jnp.int32)],
    )
    return pl.pallas_call(
        _expert_kernel,
        grid_spec=grid_spec,
        out_shape=jax.ShapeDtypeStruct((nblk * blk * ROW_TILE, LANES), F32),
        compiler_params=_cparams(("arbitrary",)),
        name="experts",
    )(blk_expert, n_used, xb, w_g, w_u, w_d)


def _final_kernel(slot_ref, x_ref, route_ref, g_ref, yb_hbm, o_ref, ybuf, sem, *, tok0):
    i = pl.program_id(0)
    tm = x_ref.shape[0]

    def gather(j, buf):
        def body(r, c):
            a = (tok0 + j * tm + r) * TOP_K_INNER
            for k in range(TOP_K_INNER):
                pltpu.make_async_copy(_row_tile(yb_hbm, slot_ref[a + k]), _row_tile(ybuf, r, lead=(buf, k)),
                                      sem.at[buf]).start()
            return c
        lax.fori_loop(0, tm, body, 0, unroll=DMA_UNROLL)

    @pl.when(i == 0)
    def _first():
        gather(0, 0)

    @pl.when(i + 1 < pl.num_programs(0))
    def _next():
        gather(i + 1, (i + 1) % 2)

    buf = i % 2
    for k in range(TOP_K_INNER):
        pltpu.make_async_copy(_row_tile(yb_hbm, 0, tm), ybuf.at[buf, k], sem.at[buf]).wait()
    route = route_ref[...]
    y0 = _load_row_tiles(ybuf, tm, lead=(buf, 0))
    y1 = _load_row_tiles(ybuf, tm, lead=(buf, 1))
    x = x_ref[...] + (y0 * route[:, 2:3] + y1 * route[:, 3:4])
    ms = jnp.mean(x * x, axis=-1, keepdims=True)
    o_ref[...] = x * lax.rsqrt(ms + RMS_EPS) * g_ref[...]


def _final(slot_flat, xmid, route, g_final, yb, tok0, n_out, tm):
    blk0 = tok0 // tm
    grid_spec = pltpu.PrefetchScalarGridSpec(
        num_scalar_prefetch=1,
        grid=(n_out // tm,),
        in_specs=[pl.BlockSpec((tm, D_MODEL), lambda i, s: (i + blk0, 0)),
                  pl.BlockSpec((tm, LANES), lambda i, s: (i + blk0, 0)),
                  pl.BlockSpec((1, D_MODEL), lambda i, s: (0, 0)),
                  pl.BlockSpec(memory_space=pl.ANY)],
        out_specs=pl.BlockSpec((tm, D_MODEL), lambda i, s: (i, 0)),
        scratch_shapes=[pltpu.VMEM((2, TOP_K_INNER, tm * ROW_TILE, LANES), F32),
                        pltpu.SemaphoreType.DMA((2,))],
    )
    return pl.pallas_call(
        functools.partial(_final_kernel, tok0=tok0),
        grid_spec=grid_spec,
        out_shape=jax.ShapeDtypeStruct((n_out, D_MODEL), F32),
        compiler_params=_cparams(("arbitrary",)),
        name="final",
    )(slot_flat, xmid, route, g_final.reshape(1, D_MODEL), yb)


def kernel(x_prompt, x_sample, cache_attn_k, cache_attn_v, state_hgrn, w_in, w_out, hg_lb_logits,
           hg_norm_g, norm_mix_g, norm_ffn_g, norm_final_g, w_route_group, w_route_expert,
           w_expert_gate, w_expert_up, w_expert_down):
    bp, tp, _ = x_prompt.shape
    bs = x_sample.shape[0]
    l = 0
    w_in_bf = w_in[l].astype(BF16)
    w_out_bf = w_out[l].astype(BF16)
    w_router = jnp.concatenate(
        [w_route_group[l],
         jnp.transpose(w_route_expert[l], (1, 0, 2)).reshape(D_MODEL, N_EXPERTS),
         jnp.zeros((D_MODEL, LANES - N_GROUPS - N_EXPERTS), F32)], axis=-1)
    w_router_hi = w_router.astype(BF16)
    w_router = jnp.stack([w_router_hi, (w_router - w_router_hi.astype(F32)).astype(BF16)])

    n_p = bp * tp
    xp = x_prompt.reshape(n_p, D_MODEL)
    pos_p = jnp.arange(tp, dtype=jnp.int32)
    hq, hk, hv, lf, zg, aq, ak, av = _inproj(xp, norm_mix_g[l], w_in_bf, hg_lb_logits, pos_p, 256)
    seq3 = lambda a: a.reshape(bp, tp, HG_WIDTH)
    hn_p, s_fin = _hgrn_prompt(seq3(hq), seq3(hk), seq3(hv), seq3(lf), seq3(zg), hg_norm_g[l])
    att_p = _attn_prompt(seq3(aq), seq3(ak), seq3(av))
    keep = min(MAX_WINDOW, tp)
    heads = lambda a: a.reshape(1, bp, keep, ATT_HEADS, ATT_HEAD_DIM)
    new_k_p = heads(seq3(ak)[:, tp - keep:])
    new_v_p = heads(seq3(av)[:, tp - keep:])

    xs = x_sample.reshape(bs, D_MODEL)
    pos_s = jnp.full((bs,), PAST_LEN, jnp.int32)
    hq, hk, hv, lf, zg, aq, ak, av = _inproj(xs, norm_mix_g[l], w_in_bf, hg_lb_logits, pos_s, bs)
    hn_s, s_new = _hgrn_step(hq, hk, hv, lf, zg, hg_norm_g[l], state_hgrn[l])
    feat = lambda a: jnp.transpose(a, (0, 2, 3, 1))
    att_s, new_k_s, new_v_s = _attn_step(aq, ak, av, feat(cache_attn_k[l]), feat(cache_attn_v[l]))
    cache5 = lambda a: jnp.transpose(a, (0, 3, 1, 2))[None]

    assert n_p % ROUTE_TM == 0 and bs <= ROUTE_TM
    n_tok = n_p + bs
    xmid, h2, route, counts = _outproj(hn_p.reshape(n_p, HG_WIDTH), att_p.reshape(n_p, ATT_WIDTH), xp,
                                       hn_s, att_s, xs, w_out_bf, norm_ffn_g[l], w_router)
    blk = MOE_BLOCK
    nblk = (n_tok * TOP_K_INNER + N_EXPERTS * (blk - 1)) // blk + 1
    slot_flat, blk_expert, n_used = _rank(route, counts, n_tok, blk, nblk)
    xb = _dispatch(slot_flat, blk_expert, n_used, h2, n_tok, blk)
    yb = _experts(xb, blk_expert, n_used, w_expert_gate[l], w_expert_up[l], w_expert_down[l], blk)
    y_prompt = _final(slot_flat, xmid, route, norm_final_g, yb, 0, n_p, 256)
    y_sample = _final(slot_flat, xmid, route, norm_final_g, yb, n_p, bs, bs)

    return (y_prompt.reshape(bp, tp, D_MODEL), y_sample.reshape(bs, 1, D_MODEL),
            new_k_p, new_v_p, s_fin[None], cache5(new_k_s), cache5(new_v_s), s_new[None])
```

```python
import functools

import jax
import jax.numpy as jnp
from jax import lax
from jax.experimental import pallas as pl
from jax.experimental.pallas import tpu as pltpu

F32 = jnp.float32
BF16 = jnp.bfloat16

D_MODEL = 1024
HG_WIDTH = 512
HG_HEAD_DIM = 128
HG_HEADS = 4
ATT_WIDTH = 512
ATT_HEAD_DIM = 64
ATT_HEADS = 8
ROPE_DIM = 16
ROPE_THETA = 500000.0
DILATED_PATTERNS = ((128, 1), (512, 4), (2048, 16))
MAX_WINDOW = 2048
PAST_LEN = 16384
N_GROUPS = 8
EXPERTS_PER_GROUP = 8
N_EXPERTS = 64
TOP_K_INNER = 2
D_FF_EXPERT = 512
MOE_BLOCK = 256
IN_COLS = 4 * HG_WIDTH + 3 * ATT_WIDTH
RMS_EPS = 1e-6

LANES = 128
VMEM_LIMIT = 56 * 1024 * 1024
NEG = -1e30
HIGHEST = lax.Precision.HIGHEST
NT_DIMS = (((1,), (1,)), ((), ()))


def _sigmoid(z):
    return 1.0 / (1.0 + jnp.exp(-z))


def _cparams(sem):
    return pltpu.CompilerParams(dimension_semantics=sem, vmem_limit_bytes=VMEM_LIMIT)


def _inproj_kernel(x_ref, g_ref, w_ref, lbl_ref, cos_ref, sa_ref, sb_ref,
                   hq_ref, hk_ref, hv_ref, lf_ref, zg_ref, aq_ref, ak_ref, av_ref):
    x = x_ref[...]
    ms = jnp.mean(x * x, axis=-1, keepdims=True)
    h = (x * lax.rsqrt(ms + RMS_EPS) * g_ref[...]).astype(BF16)

    def mm(c0):
        return jnp.dot(h, w_ref[:, c0:c0 + HG_WIDTH], preferred_element_type=F32)

    lbl = lbl_ref[...]
    le = jnp.exp(lbl - jnp.max(lbl, axis=0, keepdims=True))
    lb = le[0:1, :] / jnp.sum(le, axis=0, keepdims=True)

    zq = mm(0)
    hq_ref[...] = zq * _sigmoid(zq)
    zf = mm(HG_WIDTH)
    f = lb + (1.0 - lb) * _sigmoid(zf)
    hk_ref[...] = 1.0 - f
    lf_ref[...] = jnp.log(f)
    hv_ref[...] = mm(2 * HG_WIDTH)
    zg_ref[...] = mm(3 * HG_WIDTH)

    cos, sa, sb = cos_ref[...], sa_ref[...], sb_ref[...]

    def rope(a, out_ref):
        for j in range(ATT_WIDTH // LANES):
            xj = a[:, j * LANES:(j + 1) * LANES]
            up = pltpu.roll(xj, LANES - ROPE_DIM // 2, 1)
            dn = pltpu.roll(xj, ROPE_DIM // 2, 1)
            out_ref[:, j * LANES:(j + 1) * LANES] = xj * cos + up * sa + dn * sb

    rope(mm(4 * HG_WIDTH), aq_ref)
    rope(mm(4 * HG_WIDTH + ATT_WIDTH), ak_ref)
    av_ref[...] = mm(4 * HG_WIDTH + 2 * ATT_WIDTH)


def _rope_tables(pos):
    half = ROPE_DIM // 2
    c = jnp.arange(LANES) % ATT_HEAD_DIM
    inv_freq = ROPE_THETA ** (-(c % half).astype(F32) / half)
    ang = pos.astype(F32)[:, None] * inv_freq[None, :]
    cos, sin = jnp.cos(ang), jnp.sin(ang)
    return (jnp.where(c < ROPE_DIM, cos, 1.0),
            jnp.where(c < half, -sin, 0.0),
            jnp.where(jnp.logical_and(c >= half, c < ROPE_DIM), sin, 0.0))


def _inproj(x2d, g, w_bf, lb_logits, pos, tm):
    m = x2d.shape[0]
    cos, sa, sb = _rope_tables(pos)
    row = lambda i: (i, 0)
    seq_tiles = pos.shape[0] // tm
    row_pos = lambda i: (i % seq_tiles, 0)
    const = lambda i: (0, 0)
    outs = [jax.ShapeDtypeStruct((m, HG_WIDTH), F32)] * 8
    return pl.pallas_call(
        _inproj_kernel,
        grid=(m // tm,),
        in_specs=[pl.BlockSpec((tm, D_MODEL), row),
                  pl.BlockSpec((1, D_MODEL), const),
                  pl.BlockSpec((D_MODEL, IN_COLS), const),
                  pl.BlockSpec(lb_logits.shape, const),
                  pl.BlockSpec((tm, LANES), row_pos),
                  pl.BlockSpec((tm, LANES), row_pos),
                  pl.BlockSpec((tm, LANES), row_pos)],
        out_specs=[pl.BlockSpec((tm, HG_WIDTH), row)] * 8,
        out_shape=outs,
        compiler_params=_cparams(("parallel",)),
        name="inproj",
    )(x2d, g.reshape(1, D_MODEL), w_bf, lb_logits, cos, sa, sb)


HG_C = 128
HG_SB = 16


def _hgrn_kernel(q_ref, k_ref, v_ref, lf_ref, zg_ref, g_ref, hn_ref, sfin_ref, st_ref, *, n_chunks):
    t = pl.program_id(2)

    @pl.when(t == 0)
    def _init():
        st_ref[...] = jnp.zeros_like(st_ref)

    ri = lax.broadcasted_iota(jnp.int32, (HG_C, HG_C), 0)
    ci = lax.broadcasted_iota(jnp.int32, (HG_C, HG_C), 1)
    ltri = (ri >= ci).astype(BF16)
    ones_b = jnp.ones((LANES, LANES), BF16)
    n_sb = HG_C // HG_SB
    row_sb = lax.broadcasted_iota(jnp.int32, (n_sb, HG_SB, LANES), 1)
    col_sb = lax.broadcasted_iota(jnp.int32, (n_sb, HG_SB, HG_C), 2)
    lo_sb = lax.broadcasted_iota(jnp.int32, (n_sb, HG_SB, HG_C), 0) * HG_SB
    g = g_ref[...]

    def chunk(c, carry):
        r0 = pl.multiple_of(c * HG_C, HG_C)
        q = q_ref[pl.ds(r0, HG_C), :]
        k = k_ref[pl.ds(r0, HG_C), :]
        v = v_ref[pl.ds(r0, HG_C), :]
        lf = lf_ref[pl.ds(r0, HG_C), :]
        lf_hi = lf.astype(BF16)
        lf_r = lf - lf_hi.astype(F32)
        lf_mid = lf_r.astype(BF16)
        lf_lo = (lf_r - lf_mid.astype(F32)).astype(BF16)
        b = (jnp.dot(ltri, lf_hi, preferred_element_type=F32)
             + (jnp.dot(ltri, lf_mid, preferred_element_type=F32)
                + jnp.dot(ltri, lf_lo, preferred_element_type=F32)))
        st = st_ref[...]
        vb = v.astype(BF16)
        qb = (q * jnp.exp(b)).astype(BF16)
        o_inter = lax.dot_general(qb, st.astype(BF16), NT_DIMS, preferred_element_type=F32)
        b3, q3, k3, v3 = (a.reshape(n_sb, HG_SB, LANES) for a in (b, q, k, v))
        ps = []
        for s in range(HG_SB):
            d = jnp.where(row_sb >= s, b3 - b3[:, s:s + 1, :], NEG)
            ps.append(q3 * jnp.exp(d) * k3[:, s:s + 1, :])
        p_all = jnp.concatenate(ps, axis=1).reshape(n_sb * HG_SB * HG_SB, LANES).astype(BF16)
        r_all = jnp.dot(p_all, ones_b, preferred_element_type=F32)
        r_all = r_all.reshape(n_sb, HG_SB * HG_SB, LANES)
        o3 = o_inter.reshape(n_sb, HG_SB, LANES)
        for s in range(HG_SB):
            o3 = o3 + r_all[:, s * HG_SB:(s + 1) * HG_SB, :] * v3[:, s:s + 1, :]
        b_ref = jnp.concatenate([b3[0:1, 0:1], b3[:n_sb - 1, HG_SB - 1:HG_SB]], axis=0)
        qs = (q3 * jnp.exp(jnp.minimum(b3 - b_ref, 0.0))).astype(BF16)
        ks = (k[None] * jnp.exp(jnp.minimum(b_ref - b[None], 0.0))).astype(BF16)
        a = lax.dot_general(qs, ks, (((2,), (2,)), ((0,), (0,))), preferred_element_type=F32)
        a = jnp.where(col_sb < lo_sb, a, 0.0).astype(BF16).reshape(HG_C, HG_C)
        o = o3.reshape(HG_C, LANES) + jnp.dot(a, vb, preferred_element_type=F32)
        b_last = b[HG_C - 1:HG_C, :]
        kdec = (k * jnp.exp(b_last - b)).astype(BF16)
        st_ref[...] = st * jnp.exp(b_last) + jnp.dot(v.T.astype(BF16), kdec, preferred_element_type=F32)
        ms = jnp.mean(o * o, axis=-1, keepdims=True)
        zg = zg_ref[pl.ds(r0, HG_C), :]
        hn_ref[pl.ds(r0, HG_C), :] = o * lax.rsqrt(ms + RMS_EPS) * g * (zg * _sigmoid(zg))
        return carry

    lax.fori_loop(0, n_chunks, chunk, 0, unroll=2)

    @pl.when(t == pl.num_programs(2) - 1)
    def _fin():
        sfin_ref[...] = st_ref[...].T


def _hgrn_prompt(hq, hk, hv, lf, zg, g_hg, tb=1024):
    bsz, t, _ = hq.shape
    seq = pl.BlockSpec((None, tb, HG_HEAD_DIM), lambda b, h, i: (b, i, h))
    return pl.pallas_call(
        functools.partial(_hgrn_kernel, n_chunks=tb // HG_C),
        grid=(bsz, HG_HEADS, t // tb),
        in_specs=[seq, seq, seq, seq, seq,
                  pl.BlockSpec((1, HG_HEAD_DIM), lambda b, h, i: (0, h))],
        out_specs=[seq,
                   pl.BlockSpec((None, None, HG_HEAD_DIM, HG_HEAD_DIM), lambda b, h, i: (b, h, 0, 0))],
        out_shape=[jax.ShapeDtypeStruct((bsz, t, HG_WIDTH), F32),
                   jax.ShapeDtypeStruct((bsz, HG_HEADS, HG_HEAD_DIM, HG_HEAD_DIM), F32)],
        scratch_shapes=[pltpu.VMEM((HG_HEAD_DIM, HG_HEAD_DIM), F32)],
        compiler_params=_cparams(("parallel", "parallel", "arbitrary")),
        name="hgrn_prompt",
    )(hq, hk, hv, lf, zg, g_hg.reshape(1, HG_WIDTH))


def _hgrn_step_kernel(q_ref, k_ref, v_ref, lf_ref, zg_ref, g_ref, s_ref, hn_ref, snew_ref):
    row = slice(None)
    zeros = jnp.zeros((HG_HEAD_DIM - 3, HG_HEAD_DIM), F32)
    for h in range(HG_HEADS):
        cs = slice(h * HG_HEAD_DIM, (h + 1) * HG_HEAD_DIM)
        q, k, v = q_ref[row, cs], k_ref[row, cs], v_ref[row, cs]
        f = jnp.exp(lf_ref[row, cs])
        cols = jnp.concatenate([f, k, q, zeros], axis=0).T
        s_new = cols[:, 0:1] * s_ref[h] + cols[:, 1:2] * v
        snew_ref[h] = s_new
        o = jnp.sum(cols[:, 2:3] * s_new, axis=0, keepdims=True)
        ms = jnp.mean(o * o, axis=-1, keepdims=True)
        zg = zg_ref[row, cs]
        hn_ref[row, cs] = o * lax.rsqrt(ms + RMS_EPS) * g_ref[:, cs] * (zg * _sigmoid(zg))


def _hgrn_step(hq, hk, hv, lf, zg, g_hg, state):
    bsz = hq.shape[0]
    one = pl.BlockSpec((None, 1, HG_WIDTH), lambda b: (b, 0, 0))
    st = pl.BlockSpec((None, HG_HEADS, HG_HEAD_DIM, HG_HEAD_DIM), lambda b: (b, 0, 0, 0))
    r3 = lambda a: a.reshape(bsz, 1, HG_WIDTH)
    hn, s_new = pl.pallas_call(
        _hgrn_step_kernel,
        grid=(bsz,),
        in_specs=[one, one, one, one, one, pl.BlockSpec((1, HG_WIDTH), lambda b: (0, 0)), st],
        out_specs=[one, st],
        out_shape=[jax.ShapeDtypeStruct((bsz, 1, HG_WIDTH), F32),
                   jax.ShapeDtypeStruct(state.shape, F32)],
        compiler_params=_cparams(("parallel",)),
        name="hgrn_step",
    )(r3(hq), r3(hk), r3(hv), r3(lf), r3(zg), g_hg.reshape(1, HG_WIDTH), state)
    return hn.reshape(bsz, HG_WIDTH), s_new


ATT_N = 128
ATT_SUPER = 2048
ATT_G = 4


def _attn_prompt_kernel(q_ref, k_ref, v_ref, o_ref, osc, lsc, *, seq_len):
    lane = lax.broadcasted_iota(jnp.int32, (ATT_N, LANES), 1)
    rowi = lax.broadcasted_iota(jnp.int32, (ATT_N, LANES), 0)
    head0 = lane < ATT_HEAD_DIM
    kidx = lax.broadcasted_iota(jnp.int32, (ATT_N, 2 * ATT_N), 1)
    qidx = lax.broadcasted_iota(jnp.int32, (ATT_N, 2 * ATT_N), 0)
    band = jnp.logical_and(kidx >= qidx, kidx <= qidx + ATT_N)
    in_prev = kidx < ATT_N
    scale = ATT_HEAD_DIM ** -0.5

    bidx = lax.broadcasted_iota(jnp.int32, (ATT_G, ATT_N, 2 * ATT_N), 0)
    bqk = (((2,), (2,)), ((0,), (0,)))
    bkd = (((2,), (1,)), ((0,), (0,)))
    rows_g = ATT_G * ATT_N

    def do_group(p, d, base, g):
        span = ATT_N * d
        if d == 1:
            off = g * rows_g
            start = base + off

            def cur(ref):
                return ref[pl.ds(start, rows_g), :].reshape(ATT_G, ATT_N, LANES)

            def prv(ref, c):
                before = ref[pl.ds(jnp.maximum(start - ATT_N, 0), ATT_N), :].astype(BF16)
                return jnp.concatenate([before[None], c[:ATT_G - 1]], axis=0)

            pen = jnp.where(jnp.logical_and(jnp.logical_and(bidx == 0, in_prev), start == 0), NEG, 0.0)
        else:
            per_blk = d // ATT_G
            off = (g // per_blk) * span + (g % per_blk) * ATT_G
            start = base + off
            prev = jnp.maximum(start - span, 0)

            def cur(ref):
                return jnp.stack([ref[pl.ds(start + r, ATT_N, stride=d), :] for r in range(ATT_G)])

            def prv(ref, c):
                return jnp.stack([ref[pl.ds(prev + r, ATT_N, stride=d), :] for r in range(ATT_G)]).astype(BF16)

            pen = jnp.where(jnp.logical_and(in_prev, start < span), NEG, 0.0)
        q = cur(q_ref) * scale
        kc = cur(k_ref).astype(BF16)
        vc = cur(v_ref).astype(BF16)
        kk = jnp.concatenate([prv(k_ref, kc), kc], axis=1)
        vv = jnp.concatenate([prv(v_ref, vc), vc], axis=1)
        res = []
        for hm in (head0, jnp.logical_not(head0)):
            qh = jnp.where(hm, q, 0.0).astype(BF16)
            s = lax.dot_general(qh, kk, bqk, preferred_element_type=F32)
            s = jnp.where(band, s, NEG) + pen
            m = jnp.max(s, axis=-1, keepdims=True)
            pr = jnp.exp(s - m)
            den = jnp.sum(pr, axis=-1, keepdims=True)
            o = lax.dot_general(pr.astype(BF16), vv, bkd, preferred_element_type=F32)
            res.append((o / den, m + jnp.log(den)))
        o = jnp.where(head0, res[0][0], res[1][0])
        lse = jnp.where(head0, res[0][1], res[1][1])
        if d == 1:
            osc[p, pl.ds(off, rows_g), :] = o.reshape(rows_g, LANES)
            lsc[p, pl.ds(off, rows_g), :] = lse.reshape(rows_g, LANES)
        else:
            for r in range(ATT_G):
                osc[p, pl.ds(off + r, ATT_N, stride=d), :] = o[r]
                lsc[p, pl.ds(off + r, ATT_N, stride=d), :] = lse[r]

    def superblock(sb, carry):
        base = sb * ATT_SUPER

        def groups(g, c2):
            for p, (w, d) in enumerate(DILATED_PATTERNS):
                do_group(p, d, base, g)
            return c2

        lax.fori_loop(0, ATT_SUPER // rows_g, groups, 0)

        piece = 256

        def merge(j, c2):
            r = pl.ds(pl.multiple_of(j * piece, piece), piece)
            ls = [lsc[p, r, :] for p in range(len(DILATED_PATTERNS))]
            mx = jnp.maximum(jnp.maximum(ls[0], ls[1]), ls[2])
            ws = [jnp.exp(l - mx) for l in ls]
            num = ws[0] * osc[0, r, :] + ws[1] * osc[1, r, :] + ws[2] * osc[2, r, :]
            o_ref[pl.ds(pl.multiple_of(base + j * piece, piece), piece), :] = num / (ws[0] + ws[1] + ws[2])
            return c2

        lax.fori_loop(0, ATT_SUPER // piece, merge, 0)
        return carry

    lax.fori_loop(0, seq_len // ATT_SUPER, superblock, 0)


def _attn_prompt(aq, ak, av):
    bsz, t, _ = aq.shape
    spec = pl.BlockSpec((None, t, LANES), lambda b, p: (b, 0, p))
    n_pat = len(DILATED_PATTERNS)
    return pl.pallas_call(
        functools.partial(_attn_prompt_kernel, seq_len=t),
        grid=(bsz, ATT_WIDTH // LANES),
        in_specs=[spec, spec, spec],
        out_specs=spec,
        out_shape=jax.ShapeDtypeStruct((bsz, t, ATT_WIDTH), F32),
        scratch_shapes=[pltpu.VMEM((n_pat, ATT_SUPER, LANES), F32),
                        pltpu.VMEM((n_pat, ATT_SUPER, LANES), F32)],
        compiler_params=_cparams(("parallel", "parallel")),
        name="attn_prompt",
    )(aq, ak, av)


def _attn_step_kernel(q_ref, kn_ref, vn_ref, ck_ref, cv_ref, o_ref, nk_ref, nv_ref):
    win = ck_ref.shape[-1]
    kt, vt = ck_ref[...], cv_ref[...]
    q = q_ref[...] * (ATT_HEAD_DIM ** -0.5)
    kn, vn = kn_ref[...], vn_ref[...]
    s_all = jnp.sum(kt * q, axis=1, keepdims=True)
    s_new = jnp.sum(kn * q, axis=1, keepdims=True)
    dist = win - lax.broadcasted_iota(jnp.int32, (1, 1, win), 2)

    ps, pnews, lses = [], [], []
    for w, d in DILATED_PATTERNS:
        on_stride = (dist & (d - 1)) == 0 if d & (d - 1) == 0 else dist % d == 0
        valid = jnp.logical_and(dist <= w, on_stride)
        sm = jnp.where(valid, s_all, NEG)
        m = jnp.maximum(jnp.max(sm, axis=-1, keepdims=True), s_new)
        p = jnp.exp(sm - m)
        pn = jnp.exp(s_new - m)
        den = jnp.sum(p, axis=-1, keepdims=True) + pn
        ps.append(p / den)
        pnews.append(pn / den)
        lses.append(m + jnp.log(den))
    mx = jnp.maximum(jnp.maximum(lses[0], lses[1]), lses[2])
    ws = [jnp.exp(l - mx) for l in lses]
    wsum = ws[0] + ws[1] + ws[2]
    p_tot = (ws[0] * ps[0] + ws[1] * ps[1] + ws[2] * ps[2]) / wsum
    pn_tot = (ws[0] * pnews[0] + ws[1] * pnews[1] + ws[2] * pnews[2]) / wsum
    o_ref[...] = jnp.sum(vt * p_tot, axis=-1, keepdims=True) + pn_tot * vn

    last = lax.broadcasted_iota(jnp.int32, (1, 1, win), 2) == win - 1
    nk_ref[...] = jnp.where(last, kn, pltpu.roll(kt, win - 1, 2))
    nv_ref[...] = jnp.where(last, vn, pltpu.roll(vt, win - 1, 2))


def _attn_step(aq, ak, av, cache_k, cache_v):
    bsz, _, _, win = cache_k.shape
    one = pl.BlockSpec((None, ATT_HEADS, ATT_HEAD_DIM, 1), lambda b: (b, 0, 0, 0))
    cache = pl.BlockSpec((None, ATT_HEADS, ATT_HEAD_DIM, win), lambda b: (b, 0, 0, 0))
    col = lambda a: a.reshape(bsz, ATT_HEADS, ATT_HEAD_DIM, 1)
    att, new_k, new_v = pl.pallas_call(
        _attn_step_kernel,
        grid=(bsz,),
        in_specs=[one, one, one, cache, cache],
        out_specs=[one, cache, cache],
        out_shape=[jax.ShapeDtypeStruct((bsz, ATT_HEADS, ATT_HEAD_DIM, 1), F32),
                   jax.ShapeDtypeStruct(cache_k.shape, F32),
                   jax.ShapeDtypeStruct(cache_v.shape, F32)],
        compiler_params=_cparams(("parallel",)),
        name="attn_step",
    )(col(aq), col(ak), col(av), cache_k, cache_v)
    return att.reshape(bsz, ATT_WIDTH), new_k, new_v


ROUTE_TM = 256
SUBLANES = 8
ROW_TILE = D_MODEL // LANES
assert ROW_TILE == SUBLANES


def _store_row_tiles(ref, val, lead=()):
    n = val.shape[0]
    for j in range(ROW_TILE):
        ref[lead + (pl.ds(j, n, stride=ROW_TILE), slice(None))] = val[:, j * LANES:(j + 1) * LANES]


def _load_row_tiles(ref, n, lead=()):
    return jnp.concatenate([ref[lead + (pl.ds(j, n, stride=ROW_TILE), slice(None))] for j in range(ROW_TILE)],
                           axis=1)


def _row_tile(ref, r, n=1, lead=()):
    start = r * ROW_TILE if isinstance(r, int) else pl.multiple_of(r * ROW_TILE, ROW_TILE)
    return ref.at[lead + (pl.ds(start, n * ROW_TILE),)]


def _outproj_kernel(hn_ref, att_ref, x_ref, hn_s_ref, att_s_ref, x_s_ref, w_ref, g_ref, wr_ref,
                    xmid_ref, h2_ref, route_ref, cnt_ref, *, n_sample):
    is_prompt = pl.program_id(0) < pl.num_programs(0) - 1
    hn = jnp.where(is_prompt, hn_ref[...], hn_s_ref[...])
    att = jnp.where(is_prompt, att_ref[...], att_s_ref[...])
    y = (jnp.dot(hn.astype(BF16), w_ref[0:HG_WIDTH, :], preferred_element_type=F32)
         + jnp.dot(att.astype(BF16), w_ref[HG_WIDTH:, :], preferred_element_type=F32))
    xm = jnp.where(is_prompt, x_ref[...], x_s_ref[...]) + y
    xmid_ref[...] = xm
    ms = jnp.mean(xm * xm, axis=-1, keepdims=True)
    h2 = xm * lax.rsqrt(ms + RMS_EPS) * g_ref[...]
    _store_row_tiles(h2_ref, h2)
    h2_hi = h2.astype(BF16)
    h2_lo = (h2 - h2_hi.astype(F32)).astype(BF16)
    lg = (jnp.dot(h2_hi, wr_ref[0], preferred_element_type=F32)
          + (jnp.dot(h2_lo, wr_ref[0], preferred_element_type=F32)
             + jnp.dot(h2_hi, wr_ref[1], preferred_element_type=F32)))
    lane = lax.broadcasted_iota(jnp.int32, lg.shape, 1).astype(F32)
    big = float(LANES)
    gmask = lane < N_GROUPS
    lgg = jnp.where(gmask, lg, NEG)
    mg = jnp.max(lgg, axis=-1, keepdims=True)
    gi = jnp.min(jnp.where(lgg == mg, lane, big), axis=-1, keepdims=True)
    p_grp = 1.0 / jnp.sum(jnp.exp(lgg - mg), axis=-1, keepdims=True)
    lo = N_GROUPS + gi * EXPERTS_PER_GROUP
    emask = jnp.logical_and(lane >= lo, lane < lo + EXPERTS_PER_GROUP)
    le1 = jnp.where(emask, lg, NEG)
    m1 = jnp.max(le1, axis=-1, keepdims=True)
    i1 = jnp.min(jnp.where(le1 == m1, lane, big), axis=-1, keepdims=True)
    le2 = jnp.where(lane == i1, NEG, le1)
    m2 = jnp.max(le2, axis=-1, keepdims=True)
    i2 = jnp.min(jnp.where(le2 == m2, lane, big), axis=-1, keepdims=True)
    r = jnp.exp(m2 - m1)
    g1 = p_grp / (1.0 + r)
    g2 = p_grp * r / (1.0 + r)
    e1, e2 = i1 - N_GROUPS, i2 - N_GROUPS
    route_ref[...] = jnp.where(lane == 0, e1,
                               jnp.where(lane == 1, e2,
                                         jnp.where(lane == 2, g1, jnp.where(lane == 3, g2, 0.0))))

    @pl.when(pl.program_id(0) == 0)
    def _zero():
        cnt_ref[...] = jnp.zeros_like(cnt_ref)

    rows = lax.broadcasted_iota(jnp.int32, lg.shape, 0)
    real = jnp.logical_or(is_prompt, rows < n_sample)
    hit = jnp.logical_and(real, jnp.logical_or(lane == e1, lane == e2))
    cnt_ref[...] += jnp.sum(jnp.where(hit, 1.0, 0.0), axis=0, keepdims=True)


def _outproj(hn_p, att_p, x_p, hn_s, att_s, x_s, w_out_bf, g_ffn, w_router):
    tm = ROUTE_TM
    n_p = x_p.shape[0]
    n_tiles = n_p // tm + 1
    pad = lambda a: jnp.pad(a, ((0, tm - a.shape[0]), (0, 0)))
    row_p = lambda i: (jnp.minimum(i, n_tiles - 2), 0)
    row = lambda i: (i, 0)
    const = lambda i: (0, 0)
    n_rows = n_tiles * tm
    return pl.pallas_call(
        functools.partial(_outproj_kernel, n_sample=x_s.shape[0]),
        grid=(n_tiles,),
        in_specs=[pl.BlockSpec((tm, HG_WIDTH), row_p),
                  pl.BlockSpec((tm, ATT_WIDTH), row_p),
                  pl.BlockSpec((tm, D_MODEL), row_p),
                  pl.BlockSpec((tm, HG_WIDTH), const),
                  pl.BlockSpec((tm, ATT_WIDTH), const),
                  pl.BlockSpec((tm, D_MODEL), const),
                  pl.BlockSpec((D_MODEL, D_MODEL), const),
                  pl.BlockSpec((1, D_MODEL), const),
                  pl.BlockSpec((2, D_MODEL, LANES), lambda i: (0, 0, 0))],
        out_specs=[pl.BlockSpec((tm, D_MODEL), row),
                   pl.BlockSpec((tm * ROW_TILE, LANES), row),
                   pl.BlockSpec((tm, LANES), row),
                   pl.BlockSpec((SUBLANES, LANES), const)],
        out_shape=[jax.ShapeDtypeStruct((n_rows, D_MODEL), F32),
                   jax.ShapeDtypeStruct((n_rows * ROW_TILE, LANES), F32),
                   jax.ShapeDtypeStruct((n_rows, LANES), F32),
                   jax.ShapeDtypeStruct((SUBLANES, LANES), F32)],
        compiler_params=_cparams(("arbitrary",)),
        name="outproj",
    )(hn_p, att_p, x_p, pad(hn_s), pad(att_s), pad(x_s), w_out_bf, g_ffn.reshape(1, D_MODEL), w_router)


def _rank_kernel(route_ref, cnt_ref, slot_ref, meta_ref, base_ref, *, n_tok, blk):
    i = pl.program_id(0)
    tm = route_ref.shape[0]
    lane = lax.broadcasted_iota(jnp.int32, (tm, LANES), 1).astype(F32)
    rowg = i * tm + lax.broadcasted_iota(jnp.int32, (tm, LANES), 0)
    valid = rowg < n_tok
    r = route_ref[...]
    oh0 = jnp.where(jnp.logical_and(valid, lane == r[:, 0:1]), 1.0, 0.0)
    oh1 = jnp.where(jnp.logical_and(valid, lane == r[:, 1:2]), 1.0, 0.0)
    oh = oh0 + oh1

    @pl.when(i == 0)
    def _starts():
        cnt = cnt_ref[0:1, :].astype(jnp.int32)
        shift = blk.bit_length() - 1
        padded = (((cnt + (blk - 1)) >> shift) << shift).astype(F32)
        up = (lax.broadcasted_iota(jnp.int32, (LANES, LANES), 0)
              < lax.broadcasted_iota(jnp.int32, (LANES, LANES), 1)).astype(F32)
        start = jnp.dot(jnp.broadcast_to(padded, (8, LANES)), up, precision=HIGHEST,
                        preferred_element_type=F32)[0:1]
        base_ref[...] = start
        nb = meta_ref.shape[0]
        lane_b = lax.broadcasted_iota(jnp.int32, (nb, LANES), 1)
        blk_start = (lax.broadcasted_iota(jnp.int32, (nb, LANES), 0) * blk).astype(F32)
        ended = jnp.logical_and(start + padded <= blk_start, lane_b < N_EXPERTS)
        be = jnp.minimum(jnp.sum(jnp.where(ended, 1.0, 0.0), axis=-1, keepdims=True), N_EXPERTS - 1.0)
        n_used = jnp.sum(padded, axis=-1, keepdims=True) * (1.0 / blk)
        meta_ref[...] = jnp.where(lane_b == 0, be, jnp.where(lane_b == 1, n_used, 0.0)).astype(jnp.int32)

    before = (lax.broadcasted_iota(jnp.int32, (tm, tm), 1)
              < lax.broadcasted_iota(jnp.int32, (tm, tm), 0)).astype(BF16)
    pre = jnp.dot(before, oh.astype(BF16), preferred_element_type=F32) + base_ref[...]
    s0 = jnp.sum(oh0 * pre, axis=-1, keepdims=True)
    s1 = jnp.sum(oh1 * pre, axis=-1, keepdims=True)
    slot_ref[...] = jnp.where(lane == 0, s0, jnp.where(lane == 1, s1, 0.0)).astype(jnp.int32)
    base_ref[...] += jnp.sum(oh, axis=0, keepdims=True)


def _rank(route, counts, n_tok, blk, nblk):
    n_rows = route.shape[0]
    nb = (nblk + 7) // 8 * 8
    slots, meta = pl.pallas_call(
        functools.partial(_rank_kernel, n_tok=n_tok, blk=blk),
        grid=(n_rows // ROUTE_TM,),
        in_specs=[pl.BlockSpec((ROUTE_TM, LANES), lambda i: (i, 0)),
                  pl.BlockSpec(counts.shape, lambda i: (0, 0))],
        out_specs=[pl.BlockSpec((ROUTE_TM, LANES), lambda i: (i, 0)),
                   pl.BlockSpec((nb, LANES), lambda i: (0, 0))],
        out_shape=[jax.ShapeDtypeStruct((n_rows, LANES), jnp.int32),
                   jax.ShapeDtypeStruct((nb, LANES), jnp.int32)],
        scratch_shapes=[pltpu.VMEM((1, LANES), F32)],
        compiler_params=_cparams(("arbitrary",)),
        name="rank",
    )(route, counts)
    return slots[:n_tok, :TOP_K_INNER].reshape(-1), meta[:nblk, 0], meta[0:1, 1]


DMA_UNROLL = 8


def _dispatch_kernel(slot_ref, be_ref, nu_ref, h2_ref, xb_out, zbuf, sem, zsem, *, n_tok):
    i = pl.program_id(0)
    tm = h2_ref.shape[0] // ROW_TILE
    tail = n_tok % tm

    @pl.when(i == 0)
    def _zero_padding():
        blk = zbuf.shape[0] // ROW_TILE
        n_blocks = be_ref.shape[0]
        zbuf[...] = jnp.zeros_like(zbuf)

        def ends_expert(j):
            nxt = be_ref[jnp.minimum(j + 1, n_blocks - 1)]
            return jnp.logical_or(j >= nu_ref[0] - 1, be_ref[j] != nxt)

        def issue(j, c):
            @pl.when(ends_expert(j))
            def _():
                pltpu.make_async_copy(zbuf, _row_tile(xb_out, j * blk, blk), zsem).start()
            return c

        def drain(j, c):
            @pl.when(ends_expert(j))
            def _():
                pltpu.make_async_copy(zbuf, _row_tile(xb_out, 0, blk), zsem).wait()
            return c

        lax.fori_loop(0, n_blocks, issue, 0)
        lax.fori_loop(0, n_blocks, drain, 0)

    def push(rows):
        def body(r, c):
            a = (i * tm + r) * TOP_K_INNER
            for k in range(TOP_K_INNER):
                pltpu.make_async_copy(_row_tile(h2_ref, r), _row_tile(xb_out, slot_ref[a + k]), sem).start()
            return c
        lax.fori_loop(0, rows, body, 0, unroll=DMA_UNROLL)
        for k in range(TOP_K_INNER):
            pltpu.make_async_copy(_row_tile(h2_ref, 0, rows), _row_tile(xb_out, 0, rows), sem).wait()

    last = pl.num_programs(0) - 1
    if tail == 0:
        push(tm)
    else:
        @pl.when(i < last)
        def _full():
            push(tm)

        @pl.when(i == last)
        def _tail():
            push(tail)


def _dispatch(slot_flat, blk_expert, n_used, h2, n_tok, blk):
    tm = ROUTE_TM
    n_slots = blk_expert.shape[0] * blk
    grid_spec = pltpu.PrefetchScalarGridSpec(
        num_scalar_prefetch=3,
        grid=(h2.shape[0] // (tm * ROW_TILE),),
        in_specs=[pl.BlockSpec((tm * ROW_TILE, LANES), lambda i, s, be, nu: (i, 0))],
        out_specs=pl.BlockSpec(memory_space=pl.ANY),
        scratch_shapes=[pltpu.VMEM((blk * ROW_TILE, LANES), F32),
                        pltpu.SemaphoreType.DMA(()),
                        pltpu.SemaphoreType.DMA(())],
    )
    return pl.pallas_call(
        functools.partial(_dispatch_kernel, n_tok=n_tok),
        grid_spec=grid_spec,
        out_shape=jax.ShapeDtypeStruct((n_slots * ROW_TILE, LANES), F32),
        compiler_params=_cparams(("arbitrary",)),
        name="dispatch",
    )(slot_flat, blk_expert, n_used, h2)


def _expert_kernel(be_ref, nu_ref, x_ref, wg_hbm, wu_hbm, wd_hbm, y_ref,
                   wg_f, wu_f, wd_f, sem, wgb, wub, wdb, cur_ref):
    i = pl.program_id(0)
    n_used = nu_ref[0]
    e = be_ref[i]
    e_prev = be_ref[jnp.maximum(i - 1, 0)]

    def fetch(ex, s):
        return [pltpu.make_async_copy(hbm.at[ex], buf.at[s], sem.at[s])
                for hbm, buf in ((wg_hbm, wg_f), (wu_hbm, wu_f), (wd_hbm, wd_f))]

    @pl.when(i == 0)
    def _first():
        cur_ref[0] = 0
        for c in fetch(e, 0):
            c.start()

    @pl.when(jnp.logical_and(jnp.logical_or(i == 0, e != e_prev), i < n_used))
    def _new_expert():
        s = cur_ref[0]
        j = lax.while_loop(lambda j: jnp.logical_and(j < n_used, be_ref[jnp.minimum(j, n_used - 1)] == e),
                           lambda j: j + 1, i + 1)

        @pl.when(j < n_used)
        def _prefetch():
            for c in fetch(be_ref[j], 1 - s):
                c.start()

        for c in fetch(e, s):
            c.wait()
        wgb[...] = wg_f[s].astype(BF16)
        wub[...] = wu_f[s].astype(BF16)
        wdb[...] = wd_f[s].astype(BF16)
        cur_ref[0] = 1 - s

    @pl.when(i < n_used)
    def _run():
        x = _load_row_tiles(x_ref, x_ref.shape[0] // ROW_TILE).astype(BF16)
        a = jnp.dot(x, wgb[...], preferred_element_type=F32)
        u = jnp.dot(x, wub[...], preferred_element_type=F32)
        mid = (a * _sigmoid(a) * u).astype(BF16)
        _store_row_tiles(y_ref, jnp.dot(mid, wdb[...], preferred_element_type=F32))

    @pl.when(i >= nu_ref[0])
    def _skip():
        y_ref[...] = jnp.zeros_like(y_ref)


def _experts(xb, blk_expert, n_used, w_g, w_u, w_d, blk):
    nblk = blk_expert.shape[0]
    hbm = pl.BlockSpec(memory_space=pl.ANY)
    grid_spec = pltpu.PrefetchScalarGridSpec(
        num_scalar_prefetch=2,
        grid=(nblk,),
        in_specs=[pl.BlockSpec((blk * ROW_TILE, LANES), lambda i, be, nu: (jnp.minimum(i, nu[0] - 1), 0)),
                  hbm, hbm, hbm],
        out_specs=pl.BlockSpec((blk * ROW_TILE, LANES), lambda i, be, nu: (i, 0)),
        scratch_shapes=[pltpu.VMEM((2, D_MODEL, D_FF_EXPERT), F32),
                        pltpu.VMEM((2, D_MODEL, D_FF_EXPERT), F32),
                        pltpu.VMEM((2, D_FF_EXPERT, D_MODEL), F32),
                        pltpu.SemaphoreType.DMA((2,)),
                        pltpu.VMEM((D_MODEL, D_FF_EXPERT), BF16),
                        pltpu.VMEM((D_MODEL, D_FF_EXPERT), BF16),
                        pltpu.VMEM((D_FF_EXPERT, D_MODEL), BF16),
                        pltpu.SMEM((1,), jnp.int32)],
    )
    return pl.pallas_call(
        _expert_kernel,
        grid_spec=grid_spec,
        out_shape=jax.ShapeDtypeStruct((nblk * blk * ROW_TILE, LANES), F32),
        compiler_params=_cparams(("arbitrary",)),
        name="experts",
    )(blk_expert, n_used, xb, w_g, w_u, w_d)


def _final_kernel(slot_ref, x_ref, route_ref, g_ref, yb_hbm, o_ref, ybuf, sem, *, tok0):
    i = pl.program_id(0)
    tm = x_ref.shape[0]

    def gather(j, buf):
        def body(r, c):
            a = (tok0 + j * tm + r) * TOP_K_INNER
            for k in range(TOP_K_INNER):
                pltpu.make_async_copy(_row_tile(yb_hbm, slot_ref[a + k]), _row_tile(ybuf, r, lead=(buf, k)),
                                      sem.at[buf]).start()
            return c
        lax.fori_loop(0, tm, body, 0, unroll=DMA_UNROLL)

    @pl.when(i == 0)
    def _first():
        gather(0, 0)

    @pl.when(i + 1 < pl.num_programs(0))
    def _next():
        gather(i + 1, (i + 1) % 2)

    buf = i % 2
    for k in range(TOP_K_INNER):
        pltpu.make_async_copy(_row_tile(yb_hbm, 0, tm), ybuf.at[buf, k], sem.at[buf]).wait()
    route = route_ref[...]
    y0 = _load_row_tiles(ybuf, tm, lead=(buf, 0))
    y1 = _load_row_tiles(ybuf, tm, lead=(buf, 1))
    x = x_ref[...] + (y0 * route[:, 2:3] + y1 * route[:, 3:4])
    ms = jnp.mean(x * x, axis=-1, keepdims=True)
    o_ref[...] = x * lax.rsqrt(ms + RMS_EPS) * g_ref[...]


def _final(slot_flat, xmid, route, g_final, yb, tok0, n_out, tm):
    blk0 = tok0 // tm
    grid_spec = pltpu.PrefetchScalarGridSpec(
        num_scalar_prefetch=1,
        grid=(n_out // tm,),
        in_specs=[pl.BlockSpec((tm, D_MODEL), lambda i, s: (i + blk0, 0)),
                  pl.BlockSpec((tm, LANES), lambda i, s: (i + blk0, 0)),
                  pl.BlockSpec((1, D_MODEL), lambda i, s: (0, 0)),
                  pl.BlockSpec(memory_space=pl.ANY)],
        out_specs=pl.BlockSpec((tm, D_MODEL), lambda i, s: (i, 0)),
        scratch_shapes=[pltpu.VMEM((2, TOP_K_INNER, tm * ROW_TILE, LANES), F32),
                        pltpu.SemaphoreType.DMA((2,))],
    )
    return pl.pallas_call(
        functools.partial(_final_kernel, tok0=tok0),
        grid_spec=grid_spec,
        out_shape=jax.ShapeDtypeStruct((n_out, D_MODEL), F32),
        compiler_params=_cparams(("arbitrary",)),
        name="final",
    )(slot_flat, xmid, route, g_final.reshape(1, D_MODEL), yb)


def kernel(x_prompt, x_sample, cache_attn_k, cache_attn_v, state_hgrn, w_in, w_out, hg_lb_logits,
           hg_norm_g, norm_mix_g, norm_ffn_g, norm_final_g, w_route_group, w_route_expert,
           w_expert_gate, w_expert_up, w_expert_down):
    bp, tp, _ = x_prompt.shape
    bs = x_sample.shape[0]
    l = 0
    w_in_bf = w_in[l].astype(BF16)
    w_out_bf = w_out[l].astype(BF16)
    w_router = jnp.concatenate(
        [w_route_group[l],
         jnp.transpose(w_route_expert[l], (1, 0, 2)).reshape(D_MODEL, N_EXPERTS),
         jnp.zeros((D_MODEL, LANES - N_GROUPS - N_EXPERTS), F32)], axis=-1)
    w_router_hi = w_router.astype(BF16)
    w_router = jnp.stack([w_router_hi, (w_router - w_router_hi.astype(F32)).astype(BF16)])

    n_p = bp * tp
    xp = x_prompt.reshape(n_p, D_MODEL)
    pos_p = jnp.arange(tp, dtype=jnp.int32)
    hq, hk, hv, lf, zg, aq, ak, av = _inproj(xp, norm_mix_g[l], w_in_bf, hg_lb_logits, pos_p, 256)
    seq3 = lambda a: a.reshape(bp, tp, HG_WIDTH)
    hn_p, s_fin = _hgrn_prompt(seq3(hq), seq3(hk), seq3(hv), seq3(lf), seq3(zg), hg_norm_g[l])
    att_p = _attn_prompt(seq3(aq), seq3(ak), seq3(av))
    keep = min(MAX_WINDOW, tp)
    heads = lambda a: a.reshape(1, bp, keep, ATT_HEADS, ATT_HEAD_DIM)
    new_k_p = heads(seq3(ak)[:, tp - keep:])
    new_v_p = heads(seq3(av)[:, tp - keep:])

    xs = x_sample.reshape(bs, D_MODEL)
    pos_s = jnp.full((bs,), PAST_LEN, jnp.int32)
    hq, hk, hv, lf, zg, aq, ak, av = _inproj(xs, norm_mix_g[l], w_in_bf, hg_lb_logits, pos_s, bs)
    hn_s, s_new = _hgrn_step(hq, hk, hv, lf, zg, hg_norm_g[l], state_hgrn[l])
    feat = lambda a: jnp.transpose(a, (0, 2, 3, 1))
    att_s, new_k_s, new_v_s = _attn_step(aq, ak, av, feat(cache_attn_k[l]), feat(cache_attn_v[l]))
    cache5 = lambda a: jnp.transpose(a, (0, 3, 1, 2))[None]

    assert n_p % ROUTE_TM == 0 and bs <= ROUTE_TM
    n_tok = n_p + bs
    xmid, h2, route, counts = _outproj(hn_p.reshape(n_p, HG_WIDTH), att_p.reshape(n_p, ATT_WIDTH), xp,
                                       hn_s, att_s, xs, w_out_bf, norm_ffn_g[l], w_router)
    blk = MOE_BLOCK
    nblk = (n_tok * TOP_K_INNER + N_EXPERTS * (blk - 1)) // blk + 1
    slot_flat, blk_expert, n_used = _rank(route, counts, n_tok, blk, nblk)
    xb = _dispatch(slot_flat, blk_expert, n_used, h2, n_tok, blk)
    yb = _experts(xb, blk_expert, n_used, w_expert_gate[l], w_expert_up[l], w_expert_down[l], blk)
    y_prompt = _final(slot_flat, xmid, route, norm_final_g, yb, 0, n_p, 256)
    y_sample = _final(slot_flat, xmid, route, norm_final_g, yb, n_p, bs, bs)

    return (y_prompt.reshape(bp, tp, D_MODEL), y_sample.reshape(bs, 1, D_MODEL),
            new_k_p, new_v_p, s_fin[None], cache5(new_k_s), cache5(new_v_s), s_new[None])
```

```python
import functools

import jax
import jax.numpy as jnp
from jax import lax
from jax.experimental import pallas as pl
from jax.experimental.pallas import tpu as pltpu

F32 = jnp.float32
BF16 = jnp.bfloat16

D_MODEL = 1024
HG_WIDTH = 512
HG_HEAD_DIM = 128
HG_HEADS = 4
ATT_WIDTH = 512
ATT_HEAD_DIM = 64
ATT_HEADS = 8
ROPE_DIM = 16
ROPE_THETA = 500000.0
DILATED_PATTERNS = ((128, 1), (512, 4), (2048, 16))
MAX_WINDOW = 2048
PAST_LEN = 16384
N_GROUPS = 8
EXPERTS_PER_GROUP = 8
N_EXPERTS = 64
TOP_K_INNER = 2
D_FF_EXPERT = 512
MOE_BLOCK = 256
IN_COLS = 4 * HG_WIDTH + 3 * ATT_WIDTH
RMS_EPS = 1e-6

LANES = 128
VMEM_LIMIT = 56 * 1024 * 1024
NEG = -1e30
HIGHEST = lax.Precision.HIGHEST
NT_DIMS = (((1,), (1,)), ((), ()))


def _sigmoid(z):
    return 1.0 / (1.0 + jnp.exp(-z))


def _cparams(sem):
    return pltpu.CompilerParams(dimension_semantics=sem, vmem_limit_bytes=VMEM_LIMIT)


def _inproj_kernel(x_ref, g_ref, w_ref, lbl_ref, cos_ref, sa_ref, sb_ref,
                   hq_ref, hk_ref, hv_ref, lf_ref, zg_ref, aq_ref, ak_ref, av_ref):
    x = x_ref[...]
    ms = jnp.mean(x * x, axis=-1, keepdims=True)
    h = (x * lax.rsqrt(ms + RMS_EPS) * g_ref[...]).astype(BF16)

    def mm(c0):
        return jnp.dot(h, w_ref[:, c0:c0 + HG_WIDTH], preferred_element_type=F32)

    lbl = lbl_ref[...]
    le = jnp.exp(lbl - jnp.max(lbl, axis=0, keepdims=True))
    lb = le[0:1, :] / jnp.sum(le, axis=0, keepdims=True)

    zq = mm(0)
    hq_ref[...] = zq * _sigmoid(zq)
    zf = mm(HG_WIDTH)
    f = lb + (1.0 - lb) * _sigmoid(zf)
    hk_ref[...] = 1.0 - f
    lf_ref[...] = jnp.log(f)
    hv_ref[...] = mm(2 * HG_WIDTH)
    zg_ref[...] = mm(3 * HG_WIDTH)

    cos, sa, sb = cos_ref[...], sa_ref[...], sb_ref[...]

    def rope(a, out_ref):
        for j in range(ATT_WIDTH // LANES):
            xj = a[:, j * LANES:(j + 1) * LANES]
            up = pltpu.roll(xj, LANES - ROPE_DIM // 2, 1)
            dn = pltpu.roll(xj, ROPE_DIM // 2, 1)
            out_ref[:, j * LANES:(j + 1) * LANES] = xj * cos + up * sa + dn * sb

    rope(mm(4 * HG_WIDTH), aq_ref)
    rope(mm(4 * HG_WIDTH + ATT_WIDTH), ak_ref)
    av_ref[...] = mm(4 * HG_WIDTH + 2 * ATT_WIDTH)


def _rope_tables(pos):
    half = ROPE_DIM // 2
    c = jnp.arange(LANES) % ATT_HEAD_DIM
    inv_freq = ROPE_THETA ** (-(c % half).astype(F32) / half)
    ang = pos.astype(F32)[:, None] * inv_freq[None, :]
    cos, sin = jnp.cos(ang), jnp.sin(ang)
    return (jnp.where(c < ROPE_DIM, cos, 1.0),
            jnp.where(c < half, -sin, 0.0),
            jnp.where(jnp.logical_and(c >= half, c < ROPE_DIM), sin, 0.0))


def _inproj(x2d, g, w_bf, lb_logits, pos, tm):
    m = x2d.shape[0]
    cos, sa, sb = _rope_tables(pos)
    row = lambda i: (i, 0)
    seq_tiles = pos.shape[0] // tm
    row_pos = lambda i: (i % seq_tiles, 0)
    const = lambda i: (0, 0)
    outs = [jax.ShapeDtypeStruct((m, HG_WIDTH), F32)] * 8
    return pl.pallas_call(
        _inproj_kernel,
        grid=(m // tm,),
        in_specs=[pl.BlockSpec((tm, D_MODEL), row),
                  pl.BlockSpec((1, D_MODEL), const),
                  pl.BlockSpec((D_MODEL, IN_COLS), const),
                  pl.BlockSpec(lb_logits.shape, const),
                  pl.BlockSpec((tm, LANES), row_pos),
                  pl.BlockSpec((tm, LANES), row_pos),
                  pl.BlockSpec((tm, LANES), row_pos)],
        out_specs=[pl.BlockSpec((tm, HG_WIDTH), row)] * 8,
        out_shape=outs,
        compiler_params=_cparams(("parallel",)),
        name="inproj",
    )(x2d, g.reshape(1, D_MODEL), w_bf, lb_logits, cos, sa, sb)


HG_C = 128
HG_SB = 16


def _hgrn_kernel(q_ref, k_ref, v_ref, lf_ref, zg_ref, g_ref, hn_ref, sfin_ref, st_ref, *, n_chunks):
    t = pl.program_id(2)

    @pl.when(t == 0)
    def _init():
        st_ref[...] = jnp.zeros_like(st_ref)

    ri = lax.broadcasted_iota(jnp.int32, (HG_C, HG_C), 0)
    ci = lax.broadcasted_iota(jnp.int32, (HG_C, HG_C), 1)
    ltri = (ri >= ci).astype(BF16)
    ones_b = jnp.ones((LANES, LANES), BF16)
    n_sb = HG_C // HG_SB
    row_sb = lax.broadcasted_iota(jnp.int32, (n_sb, HG_SB, LANES), 1)
    col_sb = lax.broadcasted_iota(jnp.int32, (n_sb, HG_SB, HG_C), 2)
    lo_sb = lax.broadcasted_iota(jnp.int32, (n_sb, HG_SB, HG_C), 0) * HG_SB
    g = g_ref[...]

    def chunk(c, carry):
        r0 = pl.multiple_of(c * HG_C, HG_C)
        q = q_ref[pl.ds(r0, HG_C), :]
        k = k_ref[pl.ds(r0, HG_C), :]
        v = v_ref[pl.ds(r0, HG_C), :]
        lf = lf_ref[pl.ds(r0, HG_C), :]
        lf_hi = lf.astype(BF16)
        lf_r = lf - lf_hi.astype(F32)
        lf_mid = lf_r.astype(BF16)
        lf_lo = (lf_r - lf_mid.astype(F32)).astype(BF16)
        b = (jnp.dot(ltri, lf_hi, preferred_element_type=F32)
             + (jnp.dot(ltri, lf_mid, preferred_element_type=F32)
                + jnp.dot(ltri, lf_lo, preferred_element_type=F32)))
        st = st_ref[...]
        vb = v.astype(BF16)
        qb = (q * jnp.exp(b)).astype(BF16)
        o_inter = lax.dot_general(qb, st.astype(BF16), NT_DIMS, preferred_element_type=F32)
        b3, q3, v3 = (a.reshape(n_sb, HG_SB, LANES) for a in (b, q, v))
        bk = b - jnp.log(k)
        bk3 = bk.reshape(n_sb, HG_SB, LANES)
        ps, t_lo, offs = [], [], [0]
        for s in range(HG_SB):
            lo = (s // SUBLANES) * SUBLANES
            d = jnp.where(row_sb[:, lo:] >= s, b3[:, lo:] - bk3[:, s:s + 1, :], NEG)
            ps.append(q3[:, lo:] * jnp.exp(d))
            t_lo.append(lo)
            offs.append(offs[-1] + HG_SB - lo)
        n_rows = offs[-1]
        p_all = jnp.concatenate(ps, axis=1).reshape(n_sb * n_rows, LANES).astype(BF16)
        r_all = jnp.dot(p_all, ones_b, preferred_element_type=F32)
        r_all = r_all.reshape(n_sb, n_rows, LANES)
        o3 = o_inter.reshape(n_sb, HG_SB, LANES)
        tiles = [o3[:, j * SUBLANES:(j + 1) * SUBLANES] for j in range(HG_SB // SUBLANES)]
        for s in range(HG_SB):
            for j in range(t_lo[s] // SUBLANES, HG_SB // SUBLANES):
                r0_ = offs[s] + j * SUBLANES - t_lo[s]
                tiles[j] = tiles[j] + r_all[:, r0_:r0_ + SUBLANES, :] * v3[:, s:s + 1, :]
        o3 = jnp.concatenate(tiles, axis=1)
        b_ref = jnp.concatenate([b3[0:1, 0:1], b3[:n_sb - 1, HG_SB - 1:HG_SB]], axis=0)
        qs = (q3 * jnp.exp(jnp.minimum(b3 - b_ref, 0.0))).astype(BF16)
        ks = jnp.exp(jnp.minimum(b_ref - bk[None], 0.0)).astype(BF16)
        a = lax.dot_general(qs, ks, (((2,), (2,)), ((0,), (0,))), preferred_element_type=F32)
        a = jnp.where(col_sb < lo_sb, a, 0.0).astype(BF16).reshape(HG_C, HG_C)
        o = o3.reshape(HG_C, LANES) + jnp.dot(a, vb, preferred_element_type=F32)
        b_last = b[HG_C - 1:HG_C, :]
        kdec = jnp.exp(b_last - bk).astype(BF16)
        st_ref[...] = st * jnp.exp(b_last) + jnp.dot(v.T.astype(BF16), kdec, preferred_element_type=F32)
        ms = jnp.mean(o * o, axis=-1, keepdims=True)
        zg = zg_ref[pl.ds(r0, HG_C), :]
        hn_ref[pl.ds(r0, HG_C), :] = o * lax.rsqrt(ms + RMS_EPS) * g * (zg * _sigmoid(zg))
        return carry

    lax.fori_loop(0, n_chunks, chunk, 0, unroll=8)

    @pl.when(t == pl.num_programs(2) - 1)
    def _fin():
        sfin_ref[...] = st_ref[...].T


def _hgrn_prompt(hq, hk, hv, lf, zg, g_hg, tb=1024):
    bsz, t, _ = hq.shape
    seq = pl.BlockSpec((None, tb, HG_HEAD_DIM), lambda b, h, i: (b, i, h))
    return pl.pallas_call(
        functools.partial(_hgrn_kernel, n_chunks=tb // HG_C),
        grid=(bsz, HG_HEADS, t // tb),
        in_specs=[seq, seq, seq, seq, seq,
                  pl.BlockSpec((1, HG_HEAD_DIM), lambda b, h, i: (0, h))],
        out_specs=[seq,
                   pl.BlockSpec((None, None, HG_HEAD_DIM, HG_HEAD_DIM), lambda b, h, i: (b, h, 0, 0))],
        out_shape=[jax.ShapeDtypeStruct((bsz, t, HG_WIDTH), F32),
                   jax.ShapeDtypeStruct((bsz, HG_HEADS, HG_HEAD_DIM, HG_HEAD_DIM), F32)],
        scratch_shapes=[pltpu.VMEM((HG_HEAD_DIM, HG_HEAD_DIM), F32)],
        compiler_params=_cparams(("parallel", "parallel", "arbitrary")),
        name="hgrn_prompt",
    )(hq, hk, hv, lf, zg, g_hg.reshape(1, HG_WIDTH))


def _hgrn_step_kernel(q_ref, k_ref, v_ref, lf_ref, zg_ref, g_ref, s_ref, hn_ref, snew_ref):
    row = slice(None)
    zeros = jnp.zeros((HG_HEAD_DIM - 3, HG_HEAD_DIM), F32)
    for h in range(HG_HEADS):
        cs = slice(h * HG_HEAD_DIM, (h + 1) * HG_HEAD_DIM)
        q, k, v = q_ref[row, cs], k_ref[row, cs], v_ref[row, cs]
        f = jnp.exp(lf_ref[row, cs])
        cols = jnp.concatenate([f, k, q, zeros], axis=0).T
        s_new = cols[:, 0:1] * s_ref[h] + cols[:, 1:2] * v
        snew_ref[h] = s_new
        o = jnp.sum(cols[:, 2:3] * s_new, axis=0, keepdims=True)
        ms = jnp.mean(o * o, axis=-1, keepdims=True)
        zg = zg_ref[row, cs]
        hn_ref[row, cs] = o * lax.rsqrt(ms + RMS_EPS) * g_ref[:, cs] * (zg * _sigmoid(zg))


def _hgrn_step(hq, hk, hv, lf, zg, g_hg, state):
    bsz = hq.shape[0]
    one = pl.BlockSpec((None, 1, HG_WIDTH), lambda b: (b, 0, 0))
    st = pl.BlockSpec((None, HG_HEADS, HG_HEAD_DIM, HG_HEAD_DIM), lambda b: (b, 0, 0, 0))
    r3 = lambda a: a.reshape(bsz, 1, HG_WIDTH)
    hn, s_new = pl.pallas_call(
        _hgrn_step_kernel,
        grid=(bsz,),
        in_specs=[one, one, one, one, one, pl.BlockSpec((1, HG_WIDTH), lambda b: (0, 0)), st],
        out_specs=[one, st],
        out_shape=[jax.ShapeDtypeStruct((bsz, 1, HG_WIDTH), F32),
                   jax.ShapeDtypeStruct(state.shape, F32)],
        compiler_params=_cparams(("parallel",)),
        name="hgrn_step",
    )(r3(hq), r3(hk), r3(hv), r3(lf), r3(zg), g_hg.reshape(1, HG_WIDTH), state)
    return hn.reshape(bsz, HG_WIDTH), s_new


ATT_N = 128
ATT_SUPER = 2048
ATT_G = 4


def _attn_prompt_kernel(q_ref, k_ref, v_ref, o_ref, osc, lsc, *, seq_len):
    lane = lax.broadcasted_iota(jnp.int32, (ATT_N, LANES), 1)
    rowi = lax.broadcasted_iota(jnp.int32, (ATT_N, LANES), 0)
    head0 = lane < ATT_HEAD_DIM
    kidx = lax.broadcasted_iota(jnp.int32, (ATT_N, 2 * ATT_N), 1)
    qidx = lax.broadcasted_iota(jnp.int32, (ATT_N, 2 * ATT_N), 0)
    band = jnp.logical_and(kidx >= qidx, kidx <= qidx + ATT_N)
    in_prev = kidx < ATT_N
    scale = ATT_HEAD_DIM ** -0.5

    bidx = lax.broadcasted_iota(jnp.int32, (ATT_G, ATT_N, 2 * ATT_N), 0)
    bqk = (((2,), (2,)), ((0,), (0,)))
    bkd = (((2,), (1,)), ((0,), (0,)))
    rows_g = ATT_G * ATT_N

    def do_group(p, d, base, g):
        span = ATT_N * d
        if d == 1:
            off = g * rows_g
            start = base + off

            def cur(ref):
                return ref[pl.ds(start, rows_g), :].reshape(ATT_G, ATT_N, LANES)

            def prv(ref, c):
                before = ref[pl.ds(jnp.maximum(start - ATT_N, 0), ATT_N), :].astype(BF16)
                return jnp.concatenate([before[None], c[:ATT_G - 1]], axis=0)

            pen = jnp.where(jnp.logical_and(jnp.logical_and(bidx == 0, in_prev), start == 0), NEG, 0.0)
        else:
            per_blk = d // ATT_G
            off = (g // per_blk) * span + (g % per_blk) * ATT_G
            start = base + off
            prev = jnp.maximum(start - span, 0)

            def cur(ref):
                return jnp.stack([ref[pl.ds(start + r, ATT_N, stride=d), :] for r in range(ATT_G)])

            def prv(ref, c):
                return jnp.stack([ref[pl.ds(prev + r, ATT_N, stride=d), :] for r in range(ATT_G)]).astype(BF16)

            pen = jnp.where(jnp.logical_and(in_prev, start < span), NEG, 0.0)
        q = cur(q_ref) * scale
        kc = cur(k_ref).astype(BF16)
        vc = cur(v_ref).astype(BF16)
        kk = jnp.concatenate([prv(k_ref, kc), kc], axis=1)
        vv = jnp.concatenate([prv(v_ref, vc), vc], axis=1)
        res = []
        for hm in (head0, jnp.logical_not(head0)):
            qh = jnp.where(hm, q, 0.0).astype(BF16)
            s = lax.dot_general(qh, kk, bqk, preferred_element_type=F32)
            s = jnp.where(band, s, NEG) + pen
            m = jnp.max(s, axis=-1, keepdims=True)
            pr = jnp.exp(s - m)
            den = jnp.sum(pr, axis=-1, keepdims=True)
            o = lax.dot_general(pr.astype(BF16), vv, bkd, preferred_element_type=F32)
            res.append((o / den, m + jnp.log(den)))
        o = jnp.where(head0, res[0][0], res[1][0])
        lse = jnp.where(head0, res[0][1], res[1][1])
        if d == 1:
            osc[p, pl.ds(off, rows_g), :] = o.reshape(rows_g, LANES)
            lsc[p, pl.ds(off, rows_g), :] = lse.reshape(rows_g, LANES)
        else:
            for r in range(ATT_G):
                osc[p, pl.ds(off + r, ATT_N, stride=d), :] = o[r]
                lsc[p, pl.ds(off + r, ATT_N, stride=d), :] = lse[r]

    def superblock(sb, carry):
        base = sb * ATT_SUPER

        def groups(g, c2):
            for p, (w, d) in enumerate(DILATED_PATTERNS):
                do_group(p, d, base, g)
            return c2

        lax.fori_loop(0, ATT_SUPER // rows_g, groups, 0, unroll=2)

        piece = 256

        def merge(j, c2):
            r = pl.ds(pl.multiple_of(j * piece, piece), piece)
            ls = [lsc[p, r, :] for p in range(len(DILATED_PATTERNS))]
            mx = jnp.maximum(jnp.maximum(ls[0], ls[1]), ls[2])
            ws = [jnp.exp(l - mx) for l in ls]
            num = ws[0] * osc[0, r, :] + ws[1] * osc[1, r, :] + ws[2] * osc[2, r, :]
            o_ref[pl.ds(pl.multiple_of(base + j * piece, piece), piece), :] = num / (ws[0] + ws[1] + ws[2])
            return c2

        lax.fori_loop(0, ATT_SUPER // piece, merge, 0)
        return carry

    lax.fori_loop(0, seq_len // ATT_SUPER, superblock, 0)


def _attn_prompt(aq, ak, av):
    bsz, t, _ = aq.shape
    spec = pl.BlockSpec((None, t, LANES), lambda b, p: (b, 0, p))
    n_pat = len(DILATED_PATTERNS)
    return pl.pallas_call(
        functools.partial(_attn_prompt_kernel, seq_len=t),
        grid=(bsz, ATT_WIDTH // LANES),
        in_specs=[spec, spec, spec],
        out_specs=spec,
        out_shape=jax.ShapeDtypeStruct((bsz, t, ATT_WIDTH), F32),
        scratch_shapes=[pltpu.VMEM((n_pat, ATT_SUPER, LANES), F32),
                        pltpu.VMEM((n_pat, ATT_SUPER, LANES), F32)],
        compiler_params=_cparams(("parallel", "parallel")),
        name="attn_prompt",
    )(aq, ak, av)


def _attn_step_kernel(q_ref, kn_ref, vn_ref, ck_ref, cv_ref, o_ref, nk_ref, nv_ref):
    win = ck_ref.shape[-1]
    kt, vt = ck_ref[...], cv_ref[...]
    q = q_ref[...] * (ATT_HEAD_DIM ** -0.5)
    kn, vn = kn_ref[...], vn_ref[...]
    s_all = jnp.sum(kt * q, axis=1, keepdims=True)
    s_new = jnp.sum(kn * q, axis=1, keepdims=True)
    dist = win - lax.broadcasted_iota(jnp.int32, (1, 1, win), 2)

    ps, pnews, lses = [], [], []
    for w, d in DILATED_PATTERNS:
        on_stride = (dist & (d - 1)) == 0 if d & (d - 1) == 0 else dist % d == 0
        valid = jnp.logical_and(dist <= w, on_stride)
        sm = jnp.where(valid, s_all, NEG)
        m = jnp.maximum(jnp.max(sm, axis=-1, keepdims=True), s_new)
        p = jnp.exp(sm - m)
        pn = jnp.exp(s_new - m)
        den = jnp.sum(p, axis=-1, keepdims=True) + pn
        ps.append(p / den)
        pnews.append(pn / den)
        lses.append(m + jnp.log(den))
    mx = jnp.maximum(jnp.maximum(lses[0], lses[1]), lses[2])
    ws = [jnp.exp(l - mx) for l in lses]
    wsum = ws[0] + ws[1] + ws[2]
    p_tot = (ws[0] * ps[0] + ws[1] * ps[1] + ws[2] * ps[2]) / wsum
    pn_tot = (ws[0] * pnews[0] + ws[1] * pnews[1] + ws[2] * pnews[2]) / wsum
    o_ref[...] = jnp.sum(vt * p_tot, axis=-1, keepdims=True) + pn_tot * vn

    last = lax.broadcasted_iota(jnp.int32, (1, 1, win), 2) == win - 1
    nk_ref[...] = jnp.where(last, kn, pltpu.roll(kt, win - 1, 2))
    nv_ref[...] = jnp.where(last, vn, pltpu.roll(vt, win - 1, 2))


def _attn_step(aq, ak, av, cache_k, cache_v):
    bsz, _, _, win = cache_k.shape
    one = pl.BlockSpec((None, ATT_HEADS, ATT_HEAD_DIM, 1), lambda b: (b, 0, 0, 0))
    cache = pl.BlockSpec((None, ATT_HEADS, ATT_HEAD_DIM, win), lambda b: (b, 0, 0, 0))
    col = lambda a: a.reshape(bsz, ATT_HEADS, ATT_HEAD_DIM, 1)
    att, new_k, new_v = pl.pallas_call(
        _attn_step_kernel,
        grid=(bsz,),
        in_specs=[one, one, one, cache, cache],
        out_specs=[one, cache, cache],
        out_shape=[jax.ShapeDtypeStruct((bsz, ATT_HEADS, ATT_HEAD_DIM, 1), F32),
                   jax.ShapeDtypeStruct(cache_k.shape, F32),
                   jax.ShapeDtypeStruct(cache_v.shape, F32)],
        compiler_params=_cparams(("parallel",)),
        name="attn_step",
    )(col(aq), col(ak), col(av), cache_k, cache_v)
    return att.reshape(bsz, ATT_WIDTH), new_k, new_v


ROUTE_TM = 256
SUBLANES = 8
ROW_TILE = D_MODEL // LANES
assert ROW_TILE == SUBLANES


def _store_row_tiles(ref, val, lead=()):
    n = val.shape[0]
    for j in range(ROW_TILE):
        ref[lead + (pl.ds(j, n, stride=ROW_TILE), slice(None))] = val[:, j * LANES:(j + 1) * LANES]


def _load_row_tiles(ref, n, lead=()):
    return jnp.concatenate([ref[lead + (pl.ds(j, n, stride=ROW_TILE), slice(None))] for j in range(ROW_TILE)],
                           axis=1)


def _row_tile(ref, r, n=1, lead=()):
    start = r * ROW_TILE if isinstance(r, int) else pl.multiple_of(r * ROW_TILE, ROW_TILE)
    return ref.at[lead + (pl.ds(start, n * ROW_TILE),)]


def _outproj_kernel(hn_ref, att_ref, x_ref, hn_s_ref, att_s_ref, x_s_ref, w_ref, g_ref, wr_ref,
                    xmid_ref, h2_ref, route_ref, cnt_ref, *, n_sample):
    is_prompt = pl.program_id(0) < pl.num_programs(0) - 1
    hn = jnp.where(is_prompt, hn_ref[...], hn_s_ref[...])
    att = jnp.where(is_prompt, att_ref[...], att_s_ref[...])
    y = (jnp.dot(hn.astype(BF16), w_ref[0:HG_WIDTH, :], preferred_element_type=F32)
         + jnp.dot(att.astype(BF16), w_ref[HG_WIDTH:, :], preferred_element_type=F32))
    xm = jnp.where(is_prompt, x_ref[...], x_s_ref[...]) + y
    xmid_ref[...] = xm
    ms = jnp.mean(xm * xm, axis=-1, keepdims=True)
    h2 = xm * lax.rsqrt(ms + RMS_EPS) * g_ref[...]
    _store_row_tiles(h2_ref, h2)
    h2_hi = h2.astype(BF16)
    h2_lo = (h2 - h2_hi.astype(F32)).astype(BF16)
    lg = (jnp.dot(h2_hi, wr_ref[0], preferred_element_type=F32)
          + (jnp.dot(h2_lo, wr_ref[0], preferred_element_type=F32)
             + jnp.dot(h2_hi, wr_ref[1], preferred_element_type=F32)))
    lane = lax.broadcasted_iota(jnp.int32, lg.shape, 1).astype(F32)
    big = float(LANES)
    gmask = lane < N_GROUPS
    lgg = jnp.where(gmask, lg, NEG)
    mg = jnp.max(lgg, axis=-1, keepdims=True)
    gi = jnp.min(jnp.where(lgg == mg, lane, big), axis=-1, keepdims=True)
    p_grp = 1.0 / jnp.sum(jnp.exp(lgg - mg), axis=-1, keepdims=True)
    lo = N_GROUPS + gi * EXPERTS_PER_GROUP
    emask = jnp.logical_and(lane >= lo, lane < lo + EXPERTS_PER_GROUP)
    le1 = jnp.where(emask, lg, NEG)
    m1 = jnp.max(le1, axis=-1, keepdims=True)
    i1 = jnp.min(jnp.where(le1 == m1, lane, big), axis=-1, keepdims=True)
    le2 = jnp.where(lane == i1, NEG, le1)
    m2 = jnp.max(le2, axis=-1, keepdims=True)
    i2 = jnp.min(jnp.where(le2 == m2, lane, big), axis=-1, keepdims=True)
    r = jnp.exp(m2 - m1)
    g1 = p_grp / (1.0 + r)
    g2 = p_grp * r / (1.0 + r)
    e1, e2 = i1 - N_GROUPS, i2 - N_GROUPS
    route_ref[...] = jnp.where(lane == 0, e1,
                               jnp.where(lane == 1, e2,
                                         jnp.where(lane == 2, g1, jnp.where(lane == 3, g2, 0.0))))

    @pl.when(pl.program_id(0) == 0)
    def _zero():
        cnt_ref[...] = jnp.zeros_like(cnt_ref)

    rows = lax.broadcasted_iota(jnp.int32, lg.shape, 0)
    real = jnp.logical_or(is_prompt, rows < n_sample)
    hit = jnp.logical_and(real, jnp.logical_or(lane == e1, lane == e2))
    cnt_ref[...] += jnp.sum(jnp.where(hit, 1.0, 0.0), axis=0, keepdims=True)


def _outproj(hn_p, att_p, x_p, hn_s, att_s, x_s, w_out_bf, g_ffn, w_router):
    tm = ROUTE_TM
    n_p = x_p.shape[0]
    n_tiles = n_p // tm + 1
    pad = lambda a: jnp.pad(a, ((0, tm - a.shape[0]), (0, 0)))
    row_p = lambda i: (jnp.minimum(i, n_tiles - 2), 0)
    row = lambda i: (i, 0)
    const = lambda i: (0, 0)
    n_rows = n_tiles * tm
    return pl.pallas_call(
        functools.partial(_outproj_kernel, n_sample=x_s.shape[0]),
        grid=(n_tiles,),
        in_specs=[pl.BlockSpec((tm, HG_WIDTH), row_p),
                  pl.BlockSpec((tm, ATT_WIDTH), row_p),
                  pl.BlockSpec((tm, D_MODEL), row_p),
                  pl.BlockSpec((tm, HG_WIDTH), const),
                  pl.BlockSpec((tm, ATT_WIDTH), const),
                  pl.BlockSpec((tm, D_MODEL), const),
                  pl.BlockSpec((D_MODEL, D_MODEL), const),
                  pl.BlockSpec((1, D_MODEL), const),
                  pl.BlockSpec((2, D_MODEL, LANES), lambda i: (0, 0, 0))],
        out_specs=[pl.BlockSpec((tm, D_MODEL), row),
                   pl.BlockSpec((tm * ROW_TILE, LANES), row),
                   pl.BlockSpec((tm, LANES), row),
                   pl.BlockSpec((SUBLANES, LANES), const)],
        out_shape=[jax.ShapeDtypeStruct((n_rows, D_MODEL), F32),
                   jax.ShapeDtypeStruct((n_rows * ROW_TILE, LANES), F32),
                   jax.ShapeDtypeStruct((n_rows, LANES), F32),
                   jax.ShapeDtypeStruct((SUBLANES, LANES), F32)],
        compiler_params=_cparams(("arbitrary",)),
        name="outproj",
    )(hn_p, att_p, x_p, pad(hn_s), pad(att_s), pad(x_s), w_out_bf, g_ffn.reshape(1, D_MODEL), w_router)


def _rank_kernel(route_ref, cnt_ref, slot_ref, meta_ref, base_ref, *, n_tok, blk):
    i = pl.program_id(0)
    tm = route_ref.shape[0]
    lane = lax.broadcasted_iota(jnp.int32, (tm, LANES), 1).astype(F32)
    rowg = i * tm + lax.broadcasted_iota(jnp.int32, (tm, LANES), 0)
    valid = rowg < n_tok
    r = route_ref[...]
    oh0 = jnp.where(jnp.logical_and(valid, lane == r[:, 0:1]), 1.0, 0.0)
    oh1 = jnp.where(jnp.logical_and(valid, lane == r[:, 1:2]), 1.0, 0.0)
    oh = oh0 + oh1

    @pl.when(i == 0)
    def _starts():
        cnt = cnt_ref[0:1, :].astype(jnp.int32)
        shift = blk.bit_length() - 1
        padded = (((cnt + (blk - 1)) >> shift) << shift).astype(F32)
        up = (lax.broadcasted_iota(jnp.int32, (LANES, LANES), 0)
              < lax.broadcasted_iota(jnp.int32, (LANES, LANES), 1)).astype(F32)
        start = jnp.dot(jnp.broadcast_to(padded, (8, LANES)), up, precision=HIGHEST,
                        preferred_element_type=F32)[0:1]
        base_ref[...] = start
        nb = meta_ref.shape[0]
        lane_b = lax.broadcasted_iota(jnp.int32, (nb, LANES), 1)
        blk_start = (lax.broadcasted_iota(jnp.int32, (nb, LANES), 0) * blk).astype(F32)
        ended = jnp.logical_and(start + padded <= blk_start, lane_b < N_EXPERTS)
        be = jnp.minimum(jnp.sum(jnp.where(ended, 1.0, 0.0), axis=-1, keepdims=True), N_EXPERTS - 1.0)
        n_used = jnp.sum(padded, axis=-1, keepdims=True) * (1.0 / blk)
        meta_ref[...] = jnp.where(lane_b == 0, be, jnp.where(lane_b == 1, n_used, 0.0)).astype(jnp.int32)

    before = (lax.broadcasted_iota(jnp.int32, (tm, tm), 1)
              < lax.broadcasted_iota(jnp.int32, (tm, tm), 0)).astype(BF16)
    pre = jnp.dot(before, oh.astype(BF16), preferred_element_type=F32) + base_ref[...]
    s0 = jnp.sum(oh0 * pre, axis=-1, keepdims=True)
    s1 = jnp.sum(oh1 * pre, axis=-1, keepdims=True)
    slot_ref[...] = jnp.where(lane == 0, s0, jnp.where(lane == 1, s1, 0.0)).astype(jnp.int32)
    base_ref[...] += jnp.sum(oh, axis=0, keepdims=True)


def _rank(route, counts, n_tok, blk, nblk):
    n_rows = route.shape[0]
    nb = (nblk + 7) // 8 * 8
    slots, meta = pl.pallas_call(
        functools.partial(_rank_kernel, n_tok=n_tok, blk=blk),
        grid=(n_rows // ROUTE_TM,),
        in_specs=[pl.BlockSpec((ROUTE_TM, LANES), lambda i: (i, 0)),
                  pl.BlockSpec(counts.shape, lambda i: (0, 0))],
        out_specs=[pl.BlockSpec((ROUTE_TM, LANES), lambda i: (i, 0)),
                   pl.BlockSpec((nb, LANES), lambda i: (0, 0))],
        out_shape=[jax.ShapeDtypeStruct((n_rows, LANES), jnp.int32),
                   jax.ShapeDtypeStruct((nb, LANES), jnp.int32)],
        scratch_shapes=[pltpu.VMEM((1, LANES), F32)],
        compiler_params=_cparams(("arbitrary",)),
        name="rank",
    )(route, counts)
    return slots[:n_tok, :TOP_K_INNER].reshape(-1), meta[:nblk, 0], meta[0:1, 1]


DMA_UNROLL = 8


def _dispatch_kernel(slot_ref, be_ref, nu_ref, h2_ref, xb_out, zbuf, sem, zsem, *, n_tok):
    i = pl.program_id(0)
    tm = h2_ref.shape[0] // ROW_TILE
    tail = n_tok % tm

    @pl.when(i == 0)
    def _zero_padding():
        blk = zbuf.shape[0] // ROW_TILE
        n_blocks = be_ref.shape[0]
        zbuf[...] = jnp.zeros_like(zbuf)

        def ends_expert(j):
            nxt = be_ref[jnp.minimum(j + 1, n_blocks - 1)]
            return jnp.logical_or(j >= nu_ref[0] - 1, be_ref[j] != nxt)

        def issue(j, c):
            @pl.when(ends_expert(j))
            def _():
                pltpu.make_async_copy(zbuf, _row_tile(xb_out, j * blk, blk), zsem).start()
            return c

        def drain(j, c):
            @pl.when(ends_expert(j))
            def _():
                pltpu.make_async_copy(zbuf, _row_tile(xb_out, 0, blk), zsem).wait()
            return c

        lax.fori_loop(0, n_blocks, issue, 0)
        lax.fori_loop(0, n_blocks, drain, 0)

    def push(rows):
        def body(r, c):
            a = (i * tm + r) * TOP_K_INNER
            for k in range(TOP_K_INNER):
                pltpu.make_async_copy(_row_tile(h2_ref, r), _row_tile(xb_out, slot_ref[a + k]), sem).start()
            return c
        lax.fori_loop(0, rows, body, 0, unroll=DMA_UNROLL)
        for k in range(TOP_K_INNER):
            pltpu.make_async_copy(_row_tile(h2_ref, 0, rows), _row_tile(xb_out, 0, rows), sem).wait()

    last = pl.num_programs(0) - 1
    if tail == 0:
        push(tm)
    else:
        @pl.when(i < last)
        def _full():
            push(tm)

        @pl.when(i == last)
        def _tail():
            push(tail)


def _dispatch(slot_flat, blk_expert, n_used, h2, n_tok, blk):
    tm = ROUTE_TM
    n_slots = blk_expert.shape[0] * blk
    grid_spec = pltpu.PrefetchScalarGridSpec(
        num_scalar_prefetch=3,
        grid=(h2.shape[0] // (tm * ROW_TILE),),
        in_specs=[pl.BlockSpec((tm * ROW_TILE, LANES), lambda i, s, be, nu: (i, 0))],
        out_specs=pl.BlockSpec(memory_space=pl.ANY),
        scratch_shapes=[pltpu.VMEM((blk * ROW_TILE, LANES), F32),
                        pltpu.SemaphoreType.DMA(()),
                        pltpu.SemaphoreType.DMA(())],
    )
    return pl.pallas_call(
        functools.partial(_dispatch_kernel, n_tok=n_tok),
        grid_spec=grid_spec,
        out_shape=jax.ShapeDtypeStruct((n_slots * ROW_TILE, LANES), F32),
        compiler_params=_cparams(("arbitrary",)),
        name="dispatch",
    )(slot_flat, blk_expert, n_used, h2)


def _expert_kernel(be_ref, nu_ref, x_ref, wg_hbm, wu_hbm, wd_hbm, y_ref,
                   wg_f, wu_f, wd_f, sem, wgb, wub, wdb, cur_ref):
    i = pl.program_id(0)
    n_used = nu_ref[0]
    e = be_ref[i]
    e_prev = be_ref[jnp.maximum(i - 1, 0)]

    def fetch(ex, s):
        return [pltpu.make_async_copy(hbm.at[ex], buf.at[s], sem.at[s])
                for hbm, buf in ((wg_hbm, wg_f), (wu_hbm, wu_f), (wd_hbm, wd_f))]

    @pl.when(i == 0)
    def _first():
        cur_ref[0] = 0
        for c in fetch(e, 0):
            c.start()

    @pl.when(jnp.logical_and(jnp.logical_or(i == 0, e != e_prev), i < n_used))
    def _new_expert():
        s = cur_ref[0]
        j = lax.while_loop(lambda j: jnp.logical_and(j < n_used, be_ref[jnp.minimum(j, n_used - 1)] == e),
                           lambda j: j + 1, i + 1)

        @pl.when(j < n_used)
        def _prefetch():
            for c in fetch(be_ref[j], 1 - s):
                c.start()

        for c in fetch(e, s):
            c.wait()
        wgb[...] = wg_f[s].astype(BF16)
        wub[...] = wu_f[s].astype(BF16)
        wdb[...] = wd_f[s].astype(BF16)
        cur_ref[0] = 1 - s

    @pl.when(i < n_used)
    def _run():
        x = _load_row_tiles(x_ref, x_ref.shape[0] // ROW_TILE).astype(BF16)
        a = jnp.dot(x, wgb[...], preferred_element_type=F32)
        u = jnp.dot(x, wub[...], preferred_element_type=F32)
        mid = (a * _sigmoid(a) * u).astype(BF16)
        _store_row_tiles(y_ref, jnp.dot(mid, wdb[...], preferred_element_type=F32))

    @pl.when(i >= nu_ref[0])
    def _skip():
        y_ref[...] = jnp.zeros_like(y_ref)


def _experts(xb, blk_expert, n_used, w_g, w_u, w_d, blk):
    nblk = blk_expert.shape[0]
    hbm = pl.BlockSpec(memory_space=pl.ANY)
    grid_spec = pltpu.PrefetchScalarGridSpec(
        num_scalar_prefetch=2,
        grid=(nblk,),
        in_specs=[pl.BlockSpec((blk * ROW_TILE, LANES), lambda i, be, nu: (jnp.minimum(i, nu[0] - 1), 0)),
                  hbm, hbm, hbm],
        out_specs=pl.BlockSpec((blk * ROW_TILE, LANES), lambda i, be, nu: (i, 0)),
        scratch_shapes=[pltpu.VMEM((2, D_MODEL, D_FF_EXPERT), F32),
                        pltpu.VMEM((2, D_MODEL, D_FF_EXPERT), F32),
                        pltpu.VMEM((2, D_FF_EXPERT, D_MODEL), F32),
                        pltpu.SemaphoreType.DMA((2,)),
                        pltpu.VMEM((D_MODEL, D_FF_EXPERT), BF16),
                        pltpu.VMEM((D_MODEL, D_FF_EXPERT), BF16),
                        pltpu.VMEM((D_FF_EXPERT, D_MODEL), BF16),
                        pltpu.SMEM((1,), jnp.int32)],
    )
    return pl.pallas_call(
        _expert_kernel,
        grid_spec=grid_spec,
        out_shape=jax.ShapeDtypeStruct((nblk * blk * ROW_TILE, LANES), F32),
        compiler_params=_cparams(("arbitrary",)),
        name="experts",
    )(blk_expert, n_used, xb, w_g, w_u, w_d)


def _final_kernel(slot_ref, x_ref, route_ref, g_ref, yb_hbm, o_ref, ybuf, sem, *, tok0):
    i = pl.program_id(0)
    tm = x_ref.shape[0]

    def gather(j, buf):
        def body(r, c):
            a = (tok0 + j * tm + r) * TOP_K_INNER
            for k in range(TOP_K_INNER):
                pltpu.make_async_copy(_row_tile(yb_hbm, slot_ref[a + k]), _row_tile(ybuf, r, lead=(buf, k)),
                                      sem.at[buf]).start()
            return c
        lax.fori_loop(0, tm, body, 0, unroll=DMA_UNROLL)

    @pl.when(i == 0)
    def _first():
        gather(0, 0)

    @pl.when(i + 1 < pl.num_programs(0))
    def _next():
        gather(i + 1, (i + 1) % 2)

    buf = i % 2
    for k in range(TOP_K_INNER):
        pltpu.make_async_copy(_row_tile(yb_hbm, 0, tm), ybuf.at[buf, k], sem.at[buf]).wait()
    route = route_ref[...]
    y0 = _load_row_tiles(ybuf, tm, lead=(buf, 0))
    y1 = _load_row_tiles(ybuf, tm, lead=(buf, 1))
    x = x_ref[...] + (y0 * route[:, 2:3] + y1 * route[:, 3:4])
    ms = jnp.mean(x * x, axis=-1, keepdims=True)
    o_ref[...] = x * lax.rsqrt(ms + RMS_EPS) * g_ref[...]


def _final(slot_flat, xmid, route, g_final, yb, tok0, n_out, tm):
    blk0 = tok0 // tm
    grid_spec = pltpu.PrefetchScalarGridSpec(
        num_scalar_prefetch=1,
        grid=(n_out // tm,),
        in_specs=[pl.BlockSpec((tm, D_MODEL), lambda i, s: (i + blk0, 0)),
                  pl.BlockSpec((tm, LANES), lambda i, s: (i + blk0, 0)),
                  pl.BlockSpec((1, D_MODEL), lambda i, s: (0, 0)),
                  pl.BlockSpec(memory_space=pl.ANY)],
        out_specs=pl.BlockSpec((tm, D_MODEL), lambda i, s: (i, 0)),
        scratch_shapes=[pltpu.VMEM((2, TOP_K_INNER, tm * ROW_TILE, LANES), F32),
                        pltpu.SemaphoreType.DMA((2,))],
    )
    return pl.pallas_call(
        functools.partial(_final_kernel, tok0=tok0),
        grid_spec=grid_spec,
        out_shape=jax.ShapeDtypeStruct((n_out, D_MODEL), F32),
        compiler_params=_cparams(("arbitrary",)),
        name="final",
    )(slot_flat, xmid, route, g_final.reshape(1, D_MODEL), yb)


def kernel(x_prompt, x_sample, cache_attn_k, cache_attn_v, state_hgrn, w_in, w_out, hg_lb_logits,
           hg_norm_g, norm_mix_g, norm_ffn_g, norm_final_g, w_route_group, w_route_expert,
           w_expert_gate, w_expert_up, w_expert_down):
    bp, tp, _ = x_prompt.shape
    bs = x_sample.shape[0]
    l = 0
    w_in_bf = w_in[l].astype(BF16)
    w_out_bf = w_out[l].astype(BF16)
    w_router = jnp.concatenate(
        [w_route_group[l],
         jnp.transpose(w_route_expert[l], (1, 0, 2)).reshape(D_MODEL, N_EXPERTS),
         jnp.zeros((D_MODEL, LANES - N_GROUPS - N_EXPERTS), F32)], axis=-1)
    w_router_hi = w_router.astype(BF16)
    w_router = jnp.stack([w_router_hi, (w_router - w_router_hi.astype(F32)).astype(BF16)])

    n_p = bp * tp
    xp = x_prompt.reshape(n_p, D_MODEL)
    pos_p = jnp.arange(tp, dtype=jnp.int32)
    hq, hk, hv, lf, zg, aq, ak, av = _inproj(xp, norm_mix_g[l], w_in_bf, hg_lb_logits, pos_p, 256)
    seq3 = lambda a: a.reshape(bp, tp, HG_WIDTH)
    hn_p, s_fin = _hgrn_prompt(seq3(hq), seq3(hk), seq3(hv), seq3(lf), seq3(zg), hg_norm_g[l])
    att_p = _attn_prompt(seq3(aq), seq3(ak), seq3(av))
    keep = min(MAX_WINDOW, tp)
    heads = lambda a: a.reshape(1, bp, keep, ATT_HEADS, ATT_HEAD_DIM)
    new_k_p = heads(seq3(ak)[:, tp - keep:])
    new_v_p = heads(seq3(av)[:, tp - keep:])

    xs = x_sample.reshape(bs, D_MODEL)
    pos_s = jnp.full((bs,), PAST_LEN, jnp.int32)
    hq, hk, hv, lf, zg, aq, ak, av = _inproj(xs, norm_mix_g[l], w_in_bf, hg_lb_logits, pos_s, bs)
    hn_s, s_new = _hgrn_step(hq, hk, hv, lf, zg, hg_norm_g[l], state_hgrn[l])
    feat = lambda a: jnp.transpose(a, (0, 2, 3, 1))
    att_s, new_k_s, new_v_s = _attn_step(aq, ak, av, feat(cache_attn_k[l]), feat(cache_attn_v[l]))
    cache5 = lambda a: jnp.transpose(a, (0, 3, 1, 2))[None]

    assert n_p % ROUTE_TM == 0 and bs <= ROUTE_TM
    n_tok = n_p + bs
    xmid, h2, route, counts = _outproj(hn_p.reshape(n_p, HG_WIDTH), att_p.reshape(n_p, ATT_WIDTH), xp,
                                       hn_s, att_s, xs, w_out_bf, norm_ffn_g[l], w_router)
    blk = MOE_BLOCK
    nblk = (n_tok * TOP_K_INNER + N_EXPERTS * (blk - 1)) // blk + 1
    slot_flat, blk_expert, n_used = _rank(route, counts, n_tok, blk, nblk)
    xb = _dispatch(slot_flat, blk_expert, n_used, h2, n_tok, blk)
    yb = _experts(xb, blk_expert, n_used, w_expert_gate[l], w_expert_up[l], w_expert_down[l], blk)
    y_prompt = _final(slot_flat, xmid, route, norm_final_g, yb, 0, n_p, 256)
    y_sample = _final(slot_flat, xmid, route, norm_final_g, yb, n_p, bs, bs)

    return (y_prompt.reshape(bp, tp, D_MODEL), y_sample.reshape(bs, 1, D_MODEL),
            new_k_p, new_v_p, s_fin[None], cache5(new_k_s), cache5(new_v_s), s_new[None])
```

```python
import functools

import jax
import jax.numpy as jnp
from jax import lax
from jax.experimental import pallas as pl
from jax.experimental.pallas import tpu as pltpu

F32 = jnp.float32
BF16 = jnp.bfloat16

D_MODEL = 1024
HG_WIDTH = 512
HG_HEAD_DIM = 128
HG_HEADS = 4
ATT_WIDTH = 512
ATT_HEAD_DIM = 64
ATT_HEADS = 8
ROPE_DIM = 16
ROPE_THETA = 500000.0
DILATED_PATTERNS = ((128, 1), (512, 4), (2048, 16))
MAX_WINDOW = 2048
PAST_LEN = 16384
N_GROUPS = 8
EXPERTS_PER_GROUP = 8
N_EXPERTS = 64
TOP_K_INNER = 2
D_FF_EXPERT = 512
MOE_BLOCK = 256
IN_COLS = 4 * HG_WIDTH + 3 * ATT_WIDTH
RMS_EPS = 1e-6

LANES = 128
VMEM_LIMIT = 56 * 1024 * 1024
NEG = -1e30
HIGHEST = lax.Precision.HIGHEST
NT_DIMS = (((1,), (1,)), ((), ()))


def _sigmoid(z):
    return 1.0 / (1.0 + jnp.exp(-z))


def _cparams(sem):
    return pltpu.CompilerParams(dimension_semantics=sem, vmem_limit_bytes=VMEM_LIMIT)


def _inproj_kernel(x_ref, g_ref, w_ref, lbl_ref, cos_ref, sa_ref, sb_ref,
                   hq_ref, hk_ref, hv_ref, lf_ref, zg_ref, aq_ref, ak_ref, av_ref):
    x = x_ref[...]
    ms = jnp.mean(x * x, axis=-1, keepdims=True)
    h = (x * lax.rsqrt(ms + RMS_EPS) * g_ref[...]).astype(BF16)

    def mm(c0):
        return jnp.dot(h, w_ref[:, c0:c0 + HG_WIDTH], preferred_element_type=F32)

    lbl = lbl_ref[...]
    le = jnp.exp(lbl - jnp.max(lbl, axis=0, keepdims=True))
    lb = le[0:1, :] / jnp.sum(le, axis=0, keepdims=True)

    zq = mm(0)
    hq_ref[...] = zq * _sigmoid(zq)
    zf = mm(HG_WIDTH)
    f = lb + (1.0 - lb) * _sigmoid(zf)
    hk_ref[...] = 1.0 - f
    lf_ref[...] = jnp.log(f)
    hv_ref[...] = mm(2 * HG_WIDTH)
    zg_ref[...] = mm(3 * HG_WIDTH)

    cos, sa, sb = cos_ref[...], sa_ref[...], sb_ref[...]

    def rope(a, out_ref):
        for j in range(ATT_WIDTH // LANES):
            xj = a[:, j * LANES:(j + 1) * LANES]
            up = pltpu.roll(xj, LANES - ROPE_DIM // 2, 1)
            dn = pltpu.roll(xj, ROPE_DIM // 2, 1)
            out_ref[:, j * LANES:(j + 1) * LANES] = xj * cos + up * sa + dn * sb

    rope(mm(4 * HG_WIDTH), aq_ref)
    rope(mm(4 * HG_WIDTH + ATT_WIDTH), ak_ref)
    av_ref[...] = mm(4 * HG_WIDTH + 2 * ATT_WIDTH)


def _rope_tables(pos):
    half = ROPE_DIM // 2
    c = jnp.arange(LANES) % ATT_HEAD_DIM
    inv_freq = ROPE_THETA ** (-(c % half).astype(F32) / half)
    ang = pos.astype(F32)[:, None] * inv_freq[None, :]
    cos, sin = jnp.cos(ang), jnp.sin(ang)
    return (jnp.where(c < ROPE_DIM, cos, 1.0),
            jnp.where(c < half, -sin, 0.0),
            jnp.where(jnp.logical_and(c >= half, c < ROPE_DIM), sin, 0.0))


def _inproj(x2d, g, w_bf, lb_logits, pos, tm):
    m = x2d.shape[0]
    cos, sa, sb = _rope_tables(pos)
    row = lambda i: (i, 0)
    seq_tiles = pos.shape[0] // tm
    row_pos = lambda i: (i % seq_tiles, 0)
    const = lambda i: (0, 0)
    outs = [jax.ShapeDtypeStruct((m, HG_WIDTH), F32)] * 8
    return pl.pallas_call(
        _inproj_kernel,
        grid=(m // tm,),
        in_specs=[pl.BlockSpec((tm, D_MODEL), row),
                  pl.BlockSpec((1, D_MODEL), const),
                  pl.BlockSpec((D_MODEL, IN_COLS), const),
                  pl.BlockSpec(lb_logits.shape, const),
                  pl.BlockSpec((tm, LANES), row_pos),
                  pl.BlockSpec((tm, LANES), row_pos),
                  pl.BlockSpec((tm, LANES), row_pos)],
        out_specs=[pl.BlockSpec((tm, HG_WIDTH), row)] * 8,
        out_shape=outs,
        compiler_params=_cparams(("parallel",)),
        name="inproj",
    )(x2d, g.reshape(1, D_MODEL), w_bf, lb_logits, cos, sa, sb)


HG_C = 128
HG_SB = 16


def _hgrn_kernel(q_ref, k_ref, v_ref, lf_ref, zg_ref, g_ref, hn_ref, sfin_ref, st_ref, *, n_chunks):
    t = pl.program_id(2)

    @pl.when(t == 0)
    def _init():
        st_ref[...] = jnp.zeros_like(st_ref)

    ri = lax.broadcasted_iota(jnp.int32, (HG_C, HG_C), 0)
    ci = lax.broadcasted_iota(jnp.int32, (HG_C, HG_C), 1)
    ltri = (ri >= ci).astype(BF16)
    ones_b = jnp.ones((LANES, LANES), BF16)
    n_sb = HG_C // HG_SB
    row_sb = lax.broadcasted_iota(jnp.int32, (n_sb, HG_SB, LANES), 1)
    col_sb = lax.broadcasted_iota(jnp.int32, (n_sb, HG_SB, HG_C), 2)
    lo_sb = lax.broadcasted_iota(jnp.int32, (n_sb, HG_SB, HG_C), 0) * HG_SB
    g = g_ref[...]

    def chunk(c, carry):
        r0 = pl.multiple_of(c * HG_C, HG_C)
        q = q_ref[pl.ds(r0, HG_C), :]
        k = k_ref[pl.ds(r0, HG_C), :]
        v = v_ref[pl.ds(r0, HG_C), :]
        lf = lf_ref[pl.ds(r0, HG_C), :]
        lf_hi = lf.astype(BF16)
        lf_r = lf - lf_hi.astype(F32)
        lf_mid = lf_r.astype(BF16)
        lf_lo = (lf_r - lf_mid.astype(F32)).astype(BF16)
        b = (jnp.dot(ltri, lf_hi, preferred_element_type=F32)
             + (jnp.dot(ltri, lf_mid, preferred_element_type=F32)
                + jnp.dot(ltri, lf_lo, preferred_element_type=F32)))
        st = st_ref[...]
        vb = v.astype(BF16)
        qb = (q * jnp.exp(b)).astype(BF16)
        o_inter = lax.dot_general(qb, st.astype(BF16), NT_DIMS, preferred_element_type=F32)
        b3, q3, v3 = (a.reshape(n_sb, HG_SB, LANES) for a in (b, q, v))
        bk = b - jnp.log(k)
        bk3 = bk.reshape(n_sb, HG_SB, LANES)
        ps, t_lo, offs = [], [], [0]
        for s in range(HG_SB):
            lo = (s // SUBLANES) * SUBLANES
            d = jnp.where(row_sb[:, lo:] >= s, b3[:, lo:] - bk3[:, s:s + 1, :], NEG)
            ps.append(q3[:, lo:] * jnp.exp(d))
            t_lo.append(lo)
            offs.append(offs[-1] + HG_SB - lo)
        n_rows = offs[-1]
        p_all = jnp.concatenate(ps, axis=1).reshape(n_sb * n_rows, LANES).astype(BF16)
        r_all = jnp.dot(p_all, ones_b, preferred_element_type=F32)
        r_all = r_all.reshape(n_sb, n_rows, LANES)
        o3 = o_inter.reshape(n_sb, HG_SB, LANES)
        tiles = [o3[:, j * SUBLANES:(j + 1) * SUBLANES] for j in range(HG_SB // SUBLANES)]
        for s in range(HG_SB):
            for j in range(t_lo[s] // SUBLANES, HG_SB // SUBLANES):
                r0_ = offs[s] + j * SUBLANES - t_lo[s]
                tiles[j] = tiles[j] + r_all[:, r0_:r0_ + SUBLANES, :] * v3[:, s:s + 1, :]
        o3 = jnp.concatenate(tiles, axis=1)
        b_ref = jnp.concatenate([b3[0:1, 0:1], b3[:n_sb - 1, HG_SB - 1:HG_SB]], axis=0)
        qs = (q3 * jnp.exp(jnp.minimum(b3 - b_ref, 0.0))).astype(BF16)
        ks = jnp.exp(jnp.minimum(b_ref - bk[None], 0.0)).astype(BF16)
        a = lax.dot_general(qs, ks, (((2,), (2,)), ((0,), (0,))), preferred_element_type=F32)
        a = jnp.where(col_sb < lo_sb, a, 0.0).astype(BF16).reshape(HG_C, HG_C)
        o = o3.reshape(HG_C, LANES) + jnp.dot(a, vb, preferred_element_type=F32)
        b_last = b[HG_C - 1:HG_C, :]
        kdec = jnp.exp(b_last - bk).astype(BF16)
        st_ref[...] = st * jnp.exp(b_last) + jnp.dot(v.T.astype(BF16), kdec, preferred_element_type=F32)
        ms = jnp.mean(o * o, axis=-1, keepdims=True)
        zg = zg_ref[pl.ds(r0, HG_C), :]
        hn_ref[pl.ds(r0, HG_C), :] = o * lax.rsqrt(ms + RMS_EPS) * g * (zg * _sigmoid(zg))
        return carry

    lax.fori_loop(0, n_chunks, chunk, 0, unroll=8)

    @pl.when(t == pl.num_programs(2) - 1)
    def _fin():
        sfin_ref[...] = st_ref[...].T


def _hgrn_prompt(hq, hk, hv, lf, zg, g_hg, tb=1024):
    bsz, t, _ = hq.shape
    seq = pl.BlockSpec((None, tb, HG_HEAD_DIM), lambda b, h, i: (b, i, h))
    return pl.pallas_call(
        functools.partial(_hgrn_kernel, n_chunks=tb // HG_C),
        grid=(bsz, HG_HEADS, t // tb),
        in_specs=[seq, seq, seq, seq, seq,
                  pl.BlockSpec((1, HG_HEAD_DIM), lambda b, h, i: (0, h))],
        out_specs=[seq,
                   pl.BlockSpec((None, None, HG_HEAD_DIM, HG_HEAD_DIM), lambda b, h, i: (b, h, 0, 0))],
        out_shape=[jax.ShapeDtypeStruct((bsz, t, HG_WIDTH), F32),
                   jax.ShapeDtypeStruct((bsz, HG_HEADS, HG_HEAD_DIM, HG_HEAD_DIM), F32)],
        scratch_shapes=[pltpu.VMEM((HG_HEAD_DIM, HG_HEAD_DIM), F32)],
        compiler_params=_cparams(("parallel", "parallel", "arbitrary")),
        name="hgrn_prompt",
    )(hq, hk, hv, lf, zg, g_hg.reshape(1, HG_WIDTH))


def _hgrn_step_kernel(q_ref, k_ref, v_ref, lf_ref, zg_ref, g_ref, s_ref, hn_ref, snew_ref):
    row = slice(None)
    zeros = jnp.zeros((HG_HEAD_DIM - 3, HG_HEAD_DIM), F32)
    for h in range(HG_HEADS):
        cs = slice(h * HG_HEAD_DIM, (h + 1) * HG_HEAD_DIM)
        q, k, v = q_ref[row, cs], k_ref[row, cs], v_ref[row, cs]
        f = jnp.exp(lf_ref[row, cs])
        cols = jnp.concatenate([f, k, q, zeros], axis=0).T
        s_new = cols[:, 0:1] * s_ref[h] + cols[:, 1:2] * v
        snew_ref[h] = s_new
        o = jnp.sum(cols[:, 2:3] * s_new, axis=0, keepdims=True)
        ms = jnp.mean(o * o, axis=-1, keepdims=True)
        zg = zg_ref[row, cs]
        hn_ref[row, cs] = o * lax.rsqrt(ms + RMS_EPS) * g_ref[:, cs] * (zg * _sigmoid(zg))


def _hgrn_step(hq, hk, hv, lf, zg, g_hg, state):
    bsz = hq.shape[0]
    one = pl.BlockSpec((None, 1, HG_WIDTH), lambda b: (b, 0, 0))
    st = pl.BlockSpec((None, HG_HEADS, HG_HEAD_DIM, HG_HEAD_DIM), lambda b: (b, 0, 0, 0))
    r3 = lambda a: a.reshape(bsz, 1, HG_WIDTH)
    hn, s_new = pl.pallas_call(
        _hgrn_step_kernel,
        grid=(bsz,),
        in_specs=[one, one, one, one, one, pl.BlockSpec((1, HG_WIDTH), lambda b: (0, 0)), st],
        out_specs=[one, st],
        out_shape=[jax.ShapeDtypeStruct((bsz, 1, HG_WIDTH), F32),
                   jax.ShapeDtypeStruct(state.shape, F32)],
        compiler_params=_cparams(("parallel",)),
        name="hgrn_step",
    )(r3(hq), r3(hk), r3(hv), r3(lf), r3(zg), g_hg.reshape(1, HG_WIDTH), state)
    return hn.reshape(bsz, HG_WIDTH), s_new


ATT_N = 128
ATT_SUPER = 2048
ATT_G = 4


def _attn_prompt_kernel(q_ref, k_ref, v_ref, o_ref, osc, lsc, *, seq_len):
    lane = lax.broadcasted_iota(jnp.int32, (ATT_N, LANES), 1)
    rowi = lax.broadcasted_iota(jnp.int32, (ATT_N, LANES), 0)
    head0 = lane < ATT_HEAD_DIM
    kidx = lax.broadcasted_iota(jnp.int32, (ATT_N, 2 * ATT_N), 1)
    qidx = lax.broadcasted_iota(jnp.int32, (ATT_N, 2 * ATT_N), 0)
    band = jnp.logical_and(kidx >= qidx, kidx <= qidx + ATT_N)
    in_prev = kidx < ATT_N
    scale = ATT_HEAD_DIM ** -0.5

    bidx = lax.broadcasted_iota(jnp.int32, (ATT_G, ATT_N, 2 * ATT_N), 0)
    bqk = (((2,), (2,)), ((0,), (0,)))
    bkd = (((2,), (1,)), ((0,), (0,)))
    rows_g = ATT_G * ATT_N

    def do_group(p, d, base, g):
        span = ATT_N * d
        if d == 1:
            off = g * rows_g
            start = base + off

            def cur(ref):
                return ref[pl.ds(start, rows_g), :].reshape(ATT_G, ATT_N, LANES)

            def prv(ref, c):
                before = ref[pl.ds(jnp.maximum(start - ATT_N, 0), ATT_N), :].astype(BF16)
                return jnp.concatenate([before[None], c[:ATT_G - 1]], axis=0)

            pen = jnp.where(jnp.logical_and(jnp.logical_and(bidx == 0, in_prev), start == 0), NEG, 0.0)
        else:
            per_blk = d // ATT_G
            off = (g // per_blk) * span + (g % per_blk) * ATT_G
            start = base + off
            prev = jnp.maximum(start - span, 0)

            def cur(ref):
                return jnp.stack([ref[pl.ds(start + r, ATT_N, stride=d), :] for r in range(ATT_G)])

            def prv(ref, c):
                return jnp.stack([ref[pl.ds(prev + r, ATT_N, stride=d), :] for r in range(ATT_G)]).astype(BF16)

            pen = jnp.where(jnp.logical_and(in_prev, start < span), NEG, 0.0)
        q = cur(q_ref) * scale
        kc = cur(k_ref).astype(BF16)
        vc = cur(v_ref).astype(BF16)
        kk = jnp.concatenate([prv(k_ref, kc), kc], axis=1)
        vv = jnp.concatenate([prv(v_ref, vc), vc], axis=1)
        res = []
        for hm in (head0, jnp.logical_not(head0)):
            qh = jnp.where(hm, q, 0.0).astype(BF16)
            s = lax.dot_general(qh, kk, bqk, preferred_element_type=F32)
            s = jnp.where(band, s, NEG) + pen
            m = jnp.max(s, axis=-1, keepdims=True)
            pr = jnp.exp(s - m)
            den = jnp.sum(pr, axis=-1, keepdims=True)
            o = lax.dot_general(pr.astype(BF16), vv, bkd, preferred_element_type=F32)
            res.append((o / den, m + jnp.log(den)))
        o = jnp.where(head0, res[0][0], res[1][0])
        lse = jnp.where(head0, res[0][1], res[1][1])
        if d == 1:
            osc[p, pl.ds(off, rows_g), :] = o.reshape(rows_g, LANES)
            lsc[p, pl.ds(off, rows_g), :] = lse.reshape(rows_g, LANES)
        else:
            for r in range(ATT_G):
                osc[p, pl.ds(off + r, ATT_N, stride=d), :] = o[r]
                lsc[p, pl.ds(off + r, ATT_N, stride=d), :] = lse[r]

    def superblock(sb, carry):
        base = sb * ATT_SUPER

        def groups(g, c2):
            for p, (w, d) in enumerate(DILATED_PATTERNS):
                do_group(p, d, base, g)
            return c2

        lax.fori_loop(0, ATT_SUPER // rows_g, groups, 0, unroll=2)

        piece = 256

        def merge(j, c2):
            r = pl.ds(pl.multiple_of(j * piece, piece), piece)
            ls = [lsc[p, r, :] for p in range(len(DILATED_PATTERNS))]
            mx = jnp.maximum(jnp.maximum(ls[0], ls[1]), ls[2])
            ws = [jnp.exp(l - mx) for l in ls]
            num = ws[0] * osc[0, r, :] + ws[1] * osc[1, r, :] + ws[2] * osc[2, r, :]
            o_ref[pl.ds(pl.multiple_of(base + j * piece, piece), piece), :] = num / (ws[0] + ws[1] + ws[2])
            return c2

        lax.fori_loop(0, ATT_SUPER // piece, merge, 0)
        return carry

    lax.fori_loop(0, seq_len // ATT_SUPER, superblock, 0)


def _attn_prompt(aq, ak, av):
    bsz, t, _ = aq.shape
    spec = pl.BlockSpec((None, t, LANES), lambda b, p: (b, 0, p))
    n_pat = len(DILATED_PATTERNS)
    return pl.pallas_call(
        functools.partial(_attn_prompt_kernel, seq_len=t),
        grid=(bsz, ATT_WIDTH // LANES),
        in_specs=[spec, spec, spec],
        out_specs=spec,
        out_shape=jax.ShapeDtypeStruct((bsz, t, ATT_WIDTH), F32),
        scratch_shapes=[pltpu.VMEM((n_pat, ATT_SUPER, LANES), F32),
                        pltpu.VMEM((n_pat, ATT_SUPER, LANES), F32)],
        compiler_params=_cparams(("parallel", "parallel")),
        name="attn_prompt",
    )(aq, ak, av)


def _attn_step_kernel(q_ref, kn_ref, vn_ref, ck_ref, cv_ref, o_ref, nk_ref, nv_ref):
    win = ck_ref.shape[-1]
    kt, vt = ck_ref[...], cv_ref[...]
    q = q_ref[...] * (ATT_HEAD_DIM ** -0.5)
    kn, vn = kn_ref[...], vn_ref[...]
    s_all = jnp.sum(kt * q, axis=1, keepdims=True)
    s_new = jnp.sum(kn * q, axis=1, keepdims=True)
    dist = win - lax.broadcasted_iota(jnp.int32, (1, 1, win), 2)

    ps, pnews, lses = [], [], []
    for w, d in DILATED_PATTERNS:
        on_stride = (dist & (d - 1)) == 0 if d & (d - 1) == 0 else dist % d == 0
        valid = jnp.logical_and(dist <= w, on_stride)
        sm = jnp.where(valid, s_all, NEG)
        m = jnp.maximum(jnp.max(sm, axis=-1, keepdims=True), s_new)
        p = jnp.exp(sm - m)
        pn = jnp.exp(s_new - m)
        den = jnp.sum(p, axis=-1, keepdims=True) + pn
        ps.append(p / den)
        pnews.append(pn / den)
        lses.append(m + jnp.log(den))
    mx = jnp.maximum(jnp.maximum(lses[0], lses[1]), lses[2])
    ws = [jnp.exp(l - mx) for l in lses]
    wsum = ws[0] + ws[1] + ws[2]
    p_tot = (ws[0] * ps[0] + ws[1] * ps[1] + ws[2] * ps[2]) / wsum
    pn_tot = (ws[0] * pnews[0] + ws[1] * pnews[1] + ws[2] * pnews[2]) / wsum
    o_ref[...] = jnp.sum(vt * p_tot, axis=-1, keepdims=True) + pn_tot * vn

    last = lax.broadcasted_iota(jnp.int32, (1, 1, win), 2) == win - 1
    nk_ref[...] = jnp.where(last, kn, pltpu.roll(kt, win - 1, 2))
    nv_ref[...] = jnp.where(last, vn, pltpu.roll(vt, win - 1, 2))


def _attn_step(aq, ak, av, cache_k, cache_v):
    bsz, _, _, win = cache_k.shape
    one = pl.BlockSpec((None, ATT_HEADS, ATT_HEAD_DIM, 1), lambda b: (b, 0, 0, 0))
    cache = pl.BlockSpec((None, ATT_HEADS, ATT_HEAD_DIM, win), lambda b: (b, 0, 0, 0))
    col = lambda a: a.reshape(bsz, ATT_HEADS, ATT_HEAD_DIM, 1)
    att, new_k, new_v = pl.pallas_call(
        _attn_step_kernel,
        grid=(bsz,),
        in_specs=[one, one, one, cache, cache],
        out_specs=[one, cache, cache],
        out_shape=[jax.ShapeDtypeStruct((bsz, ATT_HEADS, ATT_HEAD_DIM, 1), F32),
                   jax.ShapeDtypeStruct(cache_k.shape, F32),
                   jax.ShapeDtypeStruct(cache_v.shape, F32)],
        compiler_params=_cparams(("parallel",)),
        name="attn_step",
    )(col(aq), col(ak), col(av), cache_k, cache_v)
    return att.reshape(bsz, ATT_WIDTH), new_k, new_v


ROUTE_TM = 256
SUBLANES = 8
ROW_TILE = D_MODEL // LANES
assert ROW_TILE == SUBLANES


def _store_row_tiles(ref, val, lead=()):
    n = val.shape[0]
    for j in range(ROW_TILE):
        ref[lead + (pl.ds(j, n, stride=ROW_TILE), slice(None))] = val[:, j * LANES:(j + 1) * LANES]


def _load_row_tiles(ref, n, lead=()):
    return jnp.concatenate([ref[lead + (pl.ds(j, n, stride=ROW_TILE), slice(None))] for j in range(ROW_TILE)],
                           axis=1)


def _row_tile(ref, r, n=1, lead=()):
    start = r * ROW_TILE if isinstance(r, int) else pl.multiple_of(r * ROW_TILE, ROW_TILE)
    return ref.at[lead + (pl.ds(start, n * ROW_TILE),)]


def _outproj_kernel(hn_ref, att_ref, x_ref, hn_s_ref, att_s_ref, x_s_ref, w_ref, g_ref, wr_ref,
                    xmid_ref, h2_ref, route_ref, cnt_ref, *, n_sample):
    is_prompt = pl.program_id(0) < pl.num_programs(0) - 1
    hn = jnp.where(is_prompt, hn_ref[...], hn_s_ref[...])
    att = jnp.where(is_prompt, att_ref[...], att_s_ref[...])
    y = (jnp.dot(hn.astype(BF16), w_ref[0:HG_WIDTH, :], preferred_element_type=F32)
         + jnp.dot(att.astype(BF16), w_ref[HG_WIDTH:, :], preferred_element_type=F32))
    xm = jnp.where(is_prompt, x_ref[...], x_s_ref[...]) + y
    xmid_ref[...] = xm
    ms = jnp.mean(xm * xm, axis=-1, keepdims=True)
    h2 = xm * lax.rsqrt(ms + RMS_EPS) * g_ref[...]
    _store_row_tiles(h2_ref, h2)
    h2_hi = h2.astype(BF16)
    h2_lo = (h2 - h2_hi.astype(F32)).astype(BF16)
    lg = (jnp.dot(h2_hi, wr_ref[0], preferred_element_type=F32)
          + (jnp.dot(h2_lo, wr_ref[0], preferred_element_type=F32)
             + jnp.dot(h2_hi, wr_ref[1], preferred_element_type=F32)))
    lane = lax.broadcasted_iota(jnp.int32, lg.shape, 1).astype(F32)
    big = float(LANES)
    gmask = lane < N_GROUPS
    lgg = jnp.where(gmask, lg, NEG)
    mg = jnp.max(lgg, axis=-1, keepdims=True)
    gi = jnp.min(jnp.where(lgg == mg, lane, big), axis=-1, keepdims=True)
    p_grp = 1.0 / jnp.sum(jnp.exp(lgg - mg), axis=-1, keepdims=True)
    lo = N_GROUPS + gi * EXPERTS_PER_GROUP
    emask = jnp.logical_and(lane >= lo, lane < lo + EXPERTS_PER_GROUP)
    le1 = jnp.where(emask, lg, NEG)
    m1 = jnp.max(le1, axis=-1, keepdims=True)
    i1 = jnp.min(jnp.where(le1 == m1, lane, big), axis=-1, keepdims=True)
    le2 = jnp.where(lane == i1, NEG, le1)
    m2 = jnp.max(le2, axis=-1, keepdims=True)
    i2 = jnp.min(jnp.where(le2 == m2, lane, big), axis=-1, keepdims=True)
    r = jnp.exp(m2 - m1)
    g1 = p_grp / (1.0 + r)
    g2 = p_grp * r / (1.0 + r)
    e1, e2 = i1 - N_GROUPS, i2 - N_GROUPS
    route_ref[...] = jnp.where(lane == 0, e1,
                               jnp.where(lane == 1, e2,
                                         jnp.where(lane == 2, g1, jnp.where(lane == 3, g2, 0.0))))

    @pl.when(pl.program_id(0) == 0)
    def _zero():
        cnt_ref[...] = jnp.zeros_like(cnt_ref)

    rows = lax.broadcasted_iota(jnp.int32, lg.shape, 0)
    real = jnp.logical_or(is_prompt, rows < n_sample)
    hit = jnp.logical_and(real, jnp.logical_or(lane == e1, lane == e2))
    cnt_ref[...] += jnp.sum(jnp.where(hit, 1.0, 0.0), axis=0, keepdims=True)


def _outproj(hn_p, att_p, x_p, hn_s, att_s, x_s, w_out_bf, g_ffn, w_router):
    tm = ROUTE_TM
    n_p = x_p.shape[0]
    n_tiles = n_p // tm + 1
    pad = lambda a: jnp.pad(a, ((0, tm - a.shape[0]), (0, 0)))
    row_p = lambda i: (jnp.minimum(i, n_tiles - 2), 0)
    row = lambda i: (i, 0)
    const = lambda i: (0, 0)
    n_rows = n_tiles * tm
    return pl.pallas_call(
        functools.partial(_outproj_kernel, n_sample=x_s.shape[0]),
        grid=(n_tiles,),
        in_specs=[pl.BlockSpec((tm, HG_WIDTH), row_p),
                  pl.BlockSpec((tm, ATT_WIDTH), row_p),
                  pl.BlockSpec((tm, D_MODEL), row_p),
                  pl.BlockSpec((tm, HG_WIDTH), const),
                  pl.BlockSpec((tm, ATT_WIDTH), const),
                  pl.BlockSpec((tm, D_MODEL), const),
                  pl.BlockSpec((D_MODEL, D_MODEL), const),
                  pl.BlockSpec((1, D_MODEL), const),
                  pl.BlockSpec((2, D_MODEL, LANES), lambda i: (0, 0, 0))],
        out_specs=[pl.BlockSpec((tm, D_MODEL), row),
                   pl.BlockSpec((tm * ROW_TILE, LANES), row),
                   pl.BlockSpec((tm, LANES), row),
                   pl.BlockSpec((SUBLANES, LANES), const)],
        out_shape=[jax.ShapeDtypeStruct((n_rows, D_MODEL), F32),
                   jax.ShapeDtypeStruct((n_rows * ROW_TILE, LANES), F32),
                   jax.ShapeDtypeStruct((n_rows, LANES), F32),
                   jax.ShapeDtypeStruct((SUBLANES, LANES), F32)],
        compiler_params=_cparams(("arbitrary",)),
        name="outproj",
    )(hn_p, att_p, x_p, pad(hn_s), pad(att_s), pad(x_s), w_out_bf, g_ffn.reshape(1, D_MODEL), w_router)


def _rank_kernel(route_ref, cnt_ref, slot_ref, meta_ref, base_ref, *, n_tok, blk):
    i = pl.program_id(0)
    tm = route_ref.shape[0]
    lane = lax.broadcasted_iota(jnp.int32, (tm, LANES), 1).astype(F32)
    rowg = i * tm + lax.broadcasted_iota(jnp.int32, (tm, LANES), 0)
    valid = rowg < n_tok
    r = route_ref[...]
    oh0 = jnp.where(jnp.logical_and(valid, lane == r[:, 0:1]), 1.0, 0.0)
    oh1 = jnp.where(jnp.logical_and(valid, lane == r[:, 1:2]), 1.0, 0.0)
    oh = oh0 + oh1

    @pl.when(i == 0)
    def _starts():
        cnt = cnt_ref[0:1, :].astype(jnp.int32)
        shift = blk.bit_length() - 1
        padded = (((cnt + (blk - 1)) >> shift) << shift).astype(F32)
        up = (lax.broadcasted_iota(jnp.int32, (LANES, LANES), 0)
              < lax.broadcasted_iota(jnp.int32, (LANES, LANES), 1)).astype(F32)
        start = jnp.dot(jnp.broadcast_to(padded, (8, LANES)), up, precision=HIGHEST,
                        preferred_element_type=F32)[0:1]
        base_ref[...] = start
        nb = meta_ref.shape[0]
        lane_b = lax.broadcasted_iota(jnp.int32, (nb, LANES), 1)
        blk_start = (lax.broadcasted_iota(jnp.int32, (nb, LANES), 0) * blk).astype(F32)
        ended = jnp.logical_and(start + padded <= blk_start, lane_b < N_EXPERTS)
        be = jnp.minimum(jnp.sum(jnp.where(ended, 1.0, 0.0), axis=-1, keepdims=True), N_EXPERTS - 1.0)
        n_used = jnp.sum(padded, axis=-1, keepdims=True) * (1.0 / blk)
        meta_ref[...] = jnp.where(lane_b == 0, be, jnp.where(lane_b == 1, n_used, 0.0)).astype(jnp.int32)

    before = (lax.broadcasted_iota(jnp.int32, (tm, tm), 1)
              < lax.broadcasted_iota(jnp.int32, (tm, tm), 0)).astype(BF16)
    pre = jnp.dot(before, oh.astype(BF16), preferred_element_type=F32) + base_ref[...]
    s0 = jnp.sum(oh0 * pre, axis=-1, keepdims=True)
    s1 = jnp.sum(oh1 * pre, axis=-1, keepdims=True)
    slot_ref[...] = jnp.where(lane == 0, s0, jnp.where(lane == 1, s1, 0.0)).astype(jnp.int32)
    base_ref[...] += jnp.sum(oh, axis=0, keepdims=True)


def _rank(route, counts, n_tok, blk, nblk):
    n_rows = route.shape[0]
    nb = (nblk + 7) // 8 * 8
    slots, meta = pl.pallas_call(
        functools.partial(_rank_kernel, n_tok=n_tok, blk=blk),
        grid=(n_rows // ROUTE_TM,),
        in_specs=[pl.BlockSpec((ROUTE_TM, LANES), lambda i: (i, 0)),
                  pl.BlockSpec(counts.shape, lambda i: (0, 0))],
        out_specs=[pl.BlockSpec((ROUTE_TM, LANES), lambda i: (i, 0)),
                   pl.BlockSpec((nb, LANES), lambda i: (0, 0))],
        out_shape=[jax.ShapeDtypeStruct((n_rows, LANES), jnp.int32),
                   jax.ShapeDtypeStruct((nb, LANES), jnp.int32)],
        scratch_shapes=[pltpu.VMEM((1, LANES), F32)],
        compiler_params=_cparams(("arbitrary",)),
        name="rank",
    )(route, counts)
    return slots[:n_tok, :TOP_K_INNER].reshape(-1), meta[:nblk, 0], meta[0:1, 1]


DMA_UNROLL = 8


def _dispatch_kernel(slot_ref, be_ref, nu_ref, h2_ref, xb_out, zbuf, sem, zsem, *, n_tok):
    i = pl.program_id(0)
    tm = h2_ref.shape[0] // ROW_TILE
    tail = n_tok % tm

    @pl.when(i == 0)
    def _zero_padding():
        blk = zbuf.shape[0] // ROW_TILE
        n_blocks = be_ref.shape[0]
        zbuf[...] = jnp.zeros_like(zbuf)

        def ends_expert(j):
            nxt = be_ref[jnp.minimum(j + 1, n_blocks - 1)]
            return jnp.logical_or(j >= nu_ref[0] - 1, be_ref[j] != nxt)

        def issue(j, c):
            @pl.when(ends_expert(j))
            def _():
                pltpu.make_async_copy(zbuf, _row_tile(xb_out, j * blk, blk), zsem).start()
            return c

        def drain(j, c):
            @pl.when(ends_expert(j))
            def _():
                pltpu.make_async_copy(zbuf, _row_tile(xb_out, 0, blk), zsem).wait()
            return c

        lax.fori_loop(0, n_blocks, issue, 0)
        lax.fori_loop(0, n_blocks, drain, 0)

    def push(rows):
        def body(r, c):
            a = (i * tm + r) * TOP_K_INNER
            for k in range(TOP_K_INNER):
                pltpu.make_async_copy(_row_tile(h2_ref, r), _row_tile(xb_out, slot_ref[a + k]), sem).start(
                    priority=k % 2)
            return c
        lax.fori_loop(0, rows, body, 0, unroll=DMA_UNROLL)
        for k in range(TOP_K_INNER):
            pltpu.make_async_copy(_row_tile(h2_ref, 0, rows), _row_tile(xb_out, 0, rows), sem).wait()

    last = pl.num_programs(0) - 1
    if tail == 0:
        push(tm)
    else:
        @pl.when(i < last)
        def _full():
            push(tm)

        @pl.when(i == last)
        def _tail():
            push(tail)


def _dispatch(slot_flat, blk_expert, n_used, h2, n_tok, blk):
    tm = ROUTE_TM
    n_slots = blk_expert.shape[0] * blk
    grid_spec = pltpu.PrefetchScalarGridSpec(
        num_scalar_prefetch=3,
        grid=(h2.shape[0] // (tm * ROW_TILE),),
        in_specs=[pl.BlockSpec((tm * ROW_TILE, LANES), lambda i, s, be, nu: (i, 0))],
        out_specs=pl.BlockSpec(memory_space=pl.ANY),
        scratch_shapes=[pltpu.VMEM((blk * ROW_TILE, LANES), F32),
                        pltpu.SemaphoreType.DMA(()),
                        pltpu.SemaphoreType.DMA(())],
    )
    return pl.pallas_call(
        functools.partial(_dispatch_kernel, n_tok=n_tok),
        grid_spec=grid_spec,
        out_shape=jax.ShapeDtypeStruct((n_slots * ROW_TILE, LANES), F32),
        compiler_params=_cparams(("arbitrary",)),
        name="dispatch",
    )(slot_flat, blk_expert, n_used, h2)


def _expert_kernel(be_ref, nu_ref, x_ref, wg_hbm, wu_hbm, wd_hbm, y_ref,
                   wg_f, wu_f, wd_f, sem, wgb, wub, wdb, cur_ref):
    i = pl.program_id(0)
    n_used = nu_ref[0]
    e = be_ref[i]
    e_prev = be_ref[jnp.maximum(i - 1, 0)]

    def fetch(ex, s):
        return [pltpu.make_async_copy(hbm.at[ex], buf.at[s], sem.at[s])
                for hbm, buf in ((wg_hbm, wg_f), (wu_hbm, wu_f), (wd_hbm, wd_f))]

    @pl.when(i == 0)
    def _first():
        cur_ref[0] = 0
        for c in fetch(e, 0):
            c.start()

    @pl.when(jnp.logical_and(jnp.logical_or(i == 0, e != e_prev), i < n_used))
    def _new_expert():
        s = cur_ref[0]
        j = lax.while_loop(lambda j: jnp.logical_and(j < n_used, be_ref[jnp.minimum(j, n_used - 1)] == e),
                           lambda j: j + 1, i + 1)

        @pl.when(j < n_used)
        def _prefetch():
            for c in fetch(be_ref[j], 1 - s):
                c.start()

        for c in fetch(e, s):
            c.wait()
        wgb[...] = wg_f[s].astype(BF16)
        wub[...] = wu_f[s].astype(BF16)
        wdb[...] = wd_f[s].astype(BF16)
        cur_ref[0] = 1 - s

    @pl.when(i < n_used)
    def _run():
        x = _load_row_tiles(x_ref, x_ref.shape[0] // ROW_TILE).astype(BF16)
        a = jnp.dot(x, wgb[...], preferred_element_type=F32)
        u = jnp.dot(x, wub[...], preferred_element_type=F32)
        mid = (a * _sigmoid(a) * u).astype(BF16)
        _store_row_tiles(y_ref, jnp.dot(mid, wdb[...], preferred_element_type=F32))

    @pl.when(i >= nu_ref[0])
    def _skip():
        y_ref[...] = jnp.zeros_like(y_ref)


def _experts(xb, blk_expert, n_used, w_g, w_u, w_d, blk):
    nblk = blk_expert.shape[0]
    hbm = pl.BlockSpec(memory_space=pl.ANY)
    grid_spec = pltpu.PrefetchScalarGridSpec(
        num_scalar_prefetch=2,
        grid=(nblk,),
        in_specs=[pl.BlockSpec((blk * ROW_TILE, LANES), lambda i, be, nu: (jnp.minimum(i, nu[0] - 1), 0)),
                  hbm, hbm, hbm],
        out_specs=pl.BlockSpec((blk * ROW_TILE, LANES), lambda i, be, nu: (i, 0)),
        scratch_shapes=[pltpu.VMEM((2, D_MODEL, D_FF_EXPERT), F32),
                        pltpu.VMEM((2, D_MODEL, D_FF_EXPERT), F32),
                        pltpu.VMEM((2, D_FF_EXPERT, D_MODEL), F32),
                        pltpu.SemaphoreType.DMA((2,)),
                        pltpu.VMEM((D_MODEL, D_FF_EXPERT), BF16),
                        pltpu.VMEM((D_MODEL, D_FF_EXPERT), BF16),
                        pltpu.VMEM((D_FF_EXPERT, D_MODEL), BF16),
                        pltpu.SMEM((1,), jnp.int32)],
    )
    return pl.pallas_call(
        _expert_kernel,
        grid_spec=grid_spec,
        out_shape=jax.ShapeDtypeStruct((nblk * blk * ROW_TILE, LANES), F32),
        compiler_params=_cparams(("arbitrary",)),
        name="experts",
    )(blk_expert, n_used, xb, w_g, w_u, w_d)


def _final_kernel(slot_ref, x_ref, route_ref, g_ref, yb_hbm, o_ref, ybuf, sem, *, tok0):
    i = pl.program_id(0)
    tm = x_ref.shape[0]

    def gather(j, buf):
        def body(r, c):
            a = (tok0 + j * tm + r) * TOP_K_INNER
            for k in range(TOP_K_INNER):
                pltpu.make_async_copy(_row_tile(yb_hbm, slot_ref[a + k]), _row_tile(ybuf, r, lead=(buf, k)),
                                      sem.at[buf]).start(priority=k % 2)
            return c
        lax.fori_loop(0, tm, body, 0, unroll=DMA_UNROLL)

    @pl.when(i == 0)
    def _first():
        gather(0, 0)

    @pl.when(i + 1 < pl.num_programs(0))
    def _next():
        gather(i + 1, (i + 1) % 2)

    buf = i % 2
    for k in range(TOP_K_INNER):
        pltpu.make_async_copy(_row_tile(yb_hbm, 0, tm), ybuf.at[buf, k], sem.at[buf]).wait()
    route = route_ref[...]
    y0 = _load_row_tiles(ybuf, tm, lead=(buf, 0))
    y1 = _load_row_tiles(ybuf, tm, lead=(buf, 1))
    x = x_ref[...] + (y0 * route[:, 2:3] + y1 * route[:, 3:4])
    ms = jnp.mean(x * x, axis=-1, keepdims=True)
    o_ref[...] = x * lax.rsqrt(ms + RMS_EPS) * g_ref[...]


def _final(slot_flat, xmid, route, g_final, yb, tok0, n_out, tm):
    blk0 = tok0 // tm
    grid_spec = pltpu.PrefetchScalarGridSpec(
        num_scalar_prefetch=1,
        grid=(n_out // tm,),
        in_specs=[pl.BlockSpec((tm, D_MODEL), lambda i, s: (i + blk0, 0)),
                  pl.BlockSpec((tm, LANES), lambda i, s: (i + blk0, 0)),
                  pl.BlockSpec((1, D_MODEL), lambda i, s: (0, 0)),
                  pl.BlockSpec(memory_space=pl.ANY)],
        out_specs=pl.BlockSpec((tm, D_MODEL), lambda i, s: (i, 0)),
        scratch_shapes=[pltpu.VMEM((2, TOP_K_INNER, tm * ROW_TILE, LANES), F32),
                        pltpu.SemaphoreType.DMA((2,))],
    )
    return pl.pallas_call(
        functools.partial(_final_kernel, tok0=tok0),
        grid_spec=grid_spec,
        out_shape=jax.ShapeDtypeStruct((n_out, D_MODEL), F32),
        compiler_params=_cparams(("arbitrary",)),
        name="final",
    )(slot_flat, xmid, route, g_final.reshape(1, D_MODEL), yb)


def kernel(x_prompt, x_sample, cache_attn_k, cache_attn_v, state_hgrn, w_in, w_out, hg_lb_logits,
           hg_norm_g, norm_mix_g, norm_ffn_g, norm_final_g, w_route_group, w_route_expert,
           w_expert_gate, w_expert_up, w_expert_down):
    bp, tp, _ = x_prompt.shape
    bs = x_sample.shape[0]
    l = 0
    w_in_bf = w_in[l].astype(BF16)
    w_out_bf = w_out[l].astype(BF16)
    w_router = jnp.concatenate(
        [w_route_group[l],
         jnp.transpose(w_route_expert[l], (1, 0, 2)).reshape(D_MODEL, N_EXPERTS),
         jnp.zeros((D_MODEL, LANES - N_GROUPS - N_EXPERTS), F32)], axis=-1)
    w_router_hi = w_router.astype(BF16)
    w_router = jnp.stack([w_router_hi, (w_router - w_router_hi.astype(F32)).astype(BF16)])

    n_p = bp * tp
    xp = x_prompt.reshape(n_p, D_MODEL)
    pos_p = jnp.arange(tp, dtype=jnp.int32)
    hq, hk, hv, lf, zg, aq, ak, av = _inproj(xp, norm_mix_g[l], w_in_bf, hg_lb_logits, pos_p, 256)
    seq3 = lambda a: a.reshape(bp, tp, HG_WIDTH)
    hn_p, s_fin = _hgrn_prompt(seq3(hq), seq3(hk), seq3(hv), seq3(lf), seq3(zg), hg_norm_g[l])
    att_p = _attn_prompt(seq3(aq), seq3(ak), seq3(av))
    keep = min(MAX_WINDOW, tp)
    heads = lambda a: a.reshape(1, bp, keep, ATT_HEADS, ATT_HEAD_DIM)
    new_k_p = heads(seq3(ak)[:, tp - keep:])
    new_v_p = heads(seq3(av)[:, tp - keep:])

    xs = x_sample.reshape(bs, D_MODEL)
    pos_s = jnp.full((bs,), PAST_LEN, jnp.int32)
    hq, hk, hv, lf, zg, aq, ak, av = _inproj(xs, norm_mix_g[l], w_in_bf, hg_lb_logits, pos_s, bs)
    hn_s, s_new = _hgrn_step(hq, hk, hv, lf, zg, hg_norm_g[l], state_hgrn[l])
    feat = lambda a: jnp.transpose(a, (0, 2, 3, 1))
    att_s, new_k_s, new_v_s = _attn_step(aq, ak, av, feat(cache_attn_k[l]), feat(cache_attn_v[l]))
    cache5 = lambda a: jnp.transpose(a, (0, 3, 1, 2))[None]

    assert n_p % ROUTE_TM == 0 and bs <= ROUTE_TM
    n_tok = n_p + bs
    xmid, h2, route, counts = _outproj(hn_p.reshape(n_p, HG_WIDTH), att_p.reshape(n_p, ATT_WIDTH), xp,
                                       hn_s, att_s, xs, w_out_bf, norm_ffn_g[l], w_router)
    blk = MOE_BLOCK
    nblk = (n_tok * TOP_K_INNER + N_EXPERTS * (blk - 1)) // blk + 1
    slot_flat, blk_expert, n_used = _rank(route, counts, n_tok, blk, nblk)
    xb = _dispatch(slot_flat, blk_expert, n_used, h2, n_tok, blk)
    yb = _experts(xb, blk_expert, n_used, w_expert_gate[l], w_expert_up[l], w_expert_down[l], blk)
    y_prompt = _final(slot_flat, xmid, route, norm_final_g, yb, 0, n_p, 256)
    y_sample = _final(slot_flat, xmid, route, norm_final_g, yb, n_p, bs, bs)

    return (y_prompt.reshape(bp, tp, D_MODEL), y_sample.reshape(bs, 1, D_MODEL),
            new_k_p, new_v_p, s_fin[None], cache5(new_k_s), cache5(new_v_s), s_new[None])
```

```python
import functools

import jax
import jax.numpy as jnp
from jax import lax
from jax.experimental import pallas as pl
from jax.experimental.pallas import tpu as pltpu

F32 = jnp.float32
BF16 = jnp.bfloat16

D_MODEL = 1024
HG_WIDTH = 512
HG_HEAD_DIM = 128
HG_HEADS = 4
ATT_WIDTH = 512
ATT_HEAD_DIM = 64
ATT_HEADS = 8
ROPE_DIM = 16
ROPE_THETA = 500000.0
DILATED_PATTERNS = ((128, 1), (512, 4), (2048, 16))
MAX_WINDOW = 2048
PAST_LEN = 16384
N_GROUPS = 8
EXPERTS_PER_GROUP = 8
N_EXPERTS = 64
TOP_K_INNER = 2
D_FF_EXPERT = 512
MOE_BLOCK = 256
IN_COLS = 4 * HG_WIDTH + 3 * ATT_WIDTH
RMS_EPS = 1e-6

LANES = 128
VMEM_LIMIT = 56 * 1024 * 1024
NEG = -1e30
HIGHEST = lax.Precision.HIGHEST
NT_DIMS = (((1,), (1,)), ((), ()))


def _sigmoid(z):
    return 1.0 / (1.0 + jnp.exp(-z))


def _cparams(sem):
    return pltpu.CompilerParams(dimension_semantics=sem, vmem_limit_bytes=VMEM_LIMIT)


def _inproj_kernel(x_ref, g_ref, w_ref, lbl_ref, cos_ref, sa_ref, sb_ref,
                   hq_ref, hk_ref, hv_ref, lf_ref, zg_ref, aq_ref, ak_ref, av_ref):
    x = x_ref[...]
    ms = jnp.mean(x * x, axis=-1, keepdims=True)
    h = (x * lax.rsqrt(ms + RMS_EPS) * g_ref[...]).astype(BF16)

    def mm(c0):
        return jnp.dot(h, w_ref[:, c0:c0 + HG_WIDTH], preferred_element_type=F32)

    lbl = lbl_ref[...]
    le = jnp.exp(lbl - jnp.max(lbl, axis=0, keepdims=True))
    lb = le[0:1, :] / jnp.sum(le, axis=0, keepdims=True)

    zq = mm(0)
    hq_ref[...] = zq * _sigmoid(zq)
    zf = mm(HG_WIDTH)
    f = lb + (1.0 - lb) * _sigmoid(zf)
    hk_ref[...] = 1.0 - f
    lf_ref[...] = jnp.log(f)
    hv_ref[...] = mm(2 * HG_WIDTH)
    zg_ref[...] = mm(3 * HG_WIDTH)

    cos, sa, sb = cos_ref[...], sa_ref[...], sb_ref[...]

    def rope(a, out_ref):
        for j in range(ATT_WIDTH // LANES):
            xj = a[:, j * LANES:(j + 1) * LANES]
            up = pltpu.roll(xj, LANES - ROPE_DIM // 2, 1)
            dn = pltpu.roll(xj, ROPE_DIM // 2, 1)
            out_ref[:, j * LANES:(j + 1) * LANES] = xj * cos + up * sa + dn * sb

    rope(mm(4 * HG_WIDTH), aq_ref)
    rope(mm(4 * HG_WIDTH + ATT_WIDTH), ak_ref)
    av_ref[...] = mm(4 * HG_WIDTH + 2 * ATT_WIDTH)


def _rope_tables(pos):
    half = ROPE_DIM // 2
    c = jnp.arange(LANES) % ATT_HEAD_DIM
    inv_freq = ROPE_THETA ** (-(c % half).astype(F32) / half)
    ang = pos.astype(F32)[:, None] * inv_freq[None, :]
    cos, sin = jnp.cos(ang), jnp.sin(ang)
    return (jnp.where(c < ROPE_DIM, cos, 1.0),
            jnp.where(c < half, -sin, 0.0),
            jnp.where(jnp.logical_and(c >= half, c < ROPE_DIM), sin, 0.0))


def _inproj(x2d, g, w_bf, lb_logits, pos, tm):
    m = x2d.shape[0]
    cos, sa, sb = _rope_tables(pos)
    row = lambda i: (i, 0)
    seq_tiles = pos.shape[0] // tm
    row_pos = lambda i: (i % seq_tiles, 0)
    const = lambda i: (0, 0)
    outs = [jax.ShapeDtypeStruct((m, HG_WIDTH), F32)] * 8
    return pl.pallas_call(
        _inproj_kernel,
        grid=(m // tm,),
        in_specs=[pl.BlockSpec((tm, D_MODEL), row),
                  pl.BlockSpec((1, D_MODEL), const),
                  pl.BlockSpec((D_MODEL, IN_COLS), const),
                  pl.BlockSpec(lb_logits.shape, const),
                  pl.BlockSpec((tm, LANES), row_pos),
                  pl.BlockSpec((tm, LANES), row_pos),
                  pl.BlockSpec((tm, LANES), row_pos)],
        out_specs=[pl.BlockSpec((tm, HG_WIDTH), row)] * 8,
        out_shape=outs,
        compiler_params=_cparams(("parallel",)),
        name="inproj",
    )(x2d, g.reshape(1, D_MODEL), w_bf, lb_logits, cos, sa, sb)


HG_C = 128
HG_SB = 16


def _hgrn_kernel(q_ref, k_ref, v_ref, lf_ref, zg_ref, g_ref, hn_ref, sfin_ref, st_ref, *, n_chunks):
    t = pl.program_id(2)

    @pl.when(t == 0)
    def _init():
        st_ref[...] = jnp.zeros_like(st_ref)

    ri = lax.broadcasted_iota(jnp.int32, (HG_C, HG_C), 0)
    ci = lax.broadcasted_iota(jnp.int32, (HG_C, HG_C), 1)
    ltri = (ri >= ci).astype(BF16)
    ones_b = jnp.ones((LANES, LANES), BF16)
    n_sb = HG_C // HG_SB
    row_sb = lax.broadcasted_iota(jnp.int32, (n_sb, HG_SB, LANES), 1)
    col_sb = lax.broadcasted_iota(jnp.int32, (n_sb, HG_SB, HG_C), 2)
    lo_sb = lax.broadcasted_iota(jnp.int32, (n_sb, HG_SB, HG_C), 0) * HG_SB
    g = g_ref[...]

    def chunk(c, carry):
        r0 = pl.multiple_of(c * HG_C, HG_C)
        q = q_ref[pl.ds(r0, HG_C), :]
        k = k_ref[pl.ds(r0, HG_C), :]
        v = v_ref[pl.ds(r0, HG_C), :]
        lf = lf_ref[pl.ds(r0, HG_C), :]
        lf_hi = lf.astype(BF16)
        lf_r = lf - lf_hi.astype(F32)
        lf_mid = lf_r.astype(BF16)
        lf_lo = (lf_r - lf_mid.astype(F32)).astype(BF16)
        b = (jnp.dot(ltri, lf_hi, preferred_element_type=F32)
             + (jnp.dot(ltri, lf_mid, preferred_element_type=F32)
                + jnp.dot(ltri, lf_lo, preferred_element_type=F32)))
        st = st_ref[...]
        vb = v.astype(BF16)
        qb = (q * jnp.exp(b)).astype(BF16)
        o_inter = lax.dot_general(qb, st.astype(BF16), NT_DIMS, preferred_element_type=F32)
        b3, q3, v3 = (a.reshape(n_sb, HG_SB, LANES) for a in (b, q, v))
        bk = b - jnp.log(k)
        bk3 = bk.reshape(n_sb, HG_SB, LANES)
        ps, t_lo, offs = [], [], [0]
        for s in range(HG_SB):
            lo = (s // SUBLANES) * SUBLANES
            d = jnp.where(row_sb[:, lo:] >= s, b3[:, lo:] - bk3[:, s:s + 1, :], NEG)
            ps.append(q3[:, lo:] * jnp.exp(d))
            t_lo.append(lo)
            offs.append(offs[-1] + HG_SB - lo)
        n_rows = offs[-1]
        p_all = jnp.concatenate(ps, axis=1).reshape(n_sb * n_rows, LANES).astype(BF16)
        r_all = jnp.dot(p_all, ones_b, preferred_element_type=F32)
        r_all = r_all.reshape(n_sb, n_rows, LANES)
        o3 = o_inter.reshape(n_sb, HG_SB, LANES)
        tiles = [o3[:, j * SUBLANES:(j + 1) * SUBLANES] for j in range(HG_SB // SUBLANES)]
        for s in range(HG_SB):
            for j in range(t_lo[s] // SUBLANES, HG_SB // SUBLANES):
                r0_ = offs[s] + j * SUBLANES - t_lo[s]
                tiles[j] = tiles[j] + r_all[:, r0_:r0_ + SUBLANES, :] * v3[:, s:s + 1, :]
        o3 = jnp.concatenate(tiles, axis=1)
        b_ref = jnp.concatenate([b3[0:1, 0:1], b3[:n_sb - 1, HG_SB - 1:HG_SB]], axis=0)
        qs = (q3 * jnp.exp(jnp.minimum(b3 - b_ref, 0.0))).astype(BF16)
        ks = jnp.exp(jnp.minimum(b_ref - bk[None], 0.0)).astype(BF16)
        a = lax.dot_general(qs, ks, (((2,), (2,)), ((0,), (0,))), preferred_element_type=F32)
        a = jnp.where(col_sb < lo_sb, a, 0.0).astype(BF16).reshape(HG_C, HG_C)
        o = o3.reshape(HG_C, LANES) + jnp.dot(a, vb, preferred_element_type=F32)
        b_last = b[HG_C - 1:HG_C, :]
        kdec = jnp.exp(b_last - bk).astype(BF16)
        st_ref[...] = st * jnp.exp(b_last) + jnp.dot(v.T.astype(BF16), kdec, preferred_element_type=F32)
        ms = jnp.mean(o * o, axis=-1, keepdims=True)
        zg = zg_ref[pl.ds(r0, HG_C), :]
        hn_ref[pl.ds(r0, HG_C), :] = o * lax.rsqrt(ms + RMS_EPS) * g * (zg * _sigmoid(zg))
        return carry

    lax.fori_loop(0, n_chunks, chunk, 0, unroll=8)

    @pl.when(t == pl.num_programs(2) - 1)
    def _fin():
        sfin_ref[...] = st_ref[...].T


def _hgrn_prompt(hq, hk, hv, lf, zg, g_hg, tb=1024):
    bsz, t, _ = hq.shape
    seq = pl.BlockSpec((None, tb, HG_HEAD_DIM), lambda b, h, i: (b, i, h))
    return pl.pallas_call(
        functools.partial(_hgrn_kernel, n_chunks=tb // HG_C),
        grid=(bsz, HG_HEADS, t // tb),
        in_specs=[seq, seq, seq, seq, seq,
                  pl.BlockSpec((1, HG_HEAD_DIM), lambda b, h, i: (0, h))],
        out_specs=[seq,
                   pl.BlockSpec((None, None, HG_HEAD_DIM, HG_HEAD_DIM), lambda b, h, i: (b, h, 0, 0))],
        out_shape=[jax.ShapeDtypeStruct((bsz, t, HG_WIDTH), F32),
                   jax.ShapeDtypeStruct((bsz, HG_HEADS, HG_HEAD_DIM, HG_HEAD_DIM), F32)],
        scratch_shapes=[pltpu.VMEM((HG_HEAD_DIM, HG_HEAD_DIM), F32)],
        compiler_params=_cparams(("parallel", "parallel", "arbitrary")),
        name="hgrn_prompt",
    )(hq, hk, hv, lf, zg, g_hg.reshape(1, HG_WIDTH))


def _hgrn_step_kernel(q_ref, k_ref, v_ref, lf_ref, zg_ref, g_ref, s_ref, hn_ref, snew_ref):
    row = slice(None)
    zeros = jnp.zeros((HG_HEAD_DIM - 3, HG_HEAD_DIM), F32)
    for h in range(HG_HEADS):
        cs = slice(h * HG_HEAD_DIM, (h + 1) * HG_HEAD_DIM)
        q, k, v = q_ref[row, cs], k_ref[row, cs], v_ref[row, cs]
        f = jnp.exp(lf_ref[row, cs])
        cols = jnp.concatenate([f, k, q, zeros], axis=0).T
        s_new = cols[:, 0:1] * s_ref[h] + cols[:, 1:2] * v
        snew_ref[h] = s_new
        o = jnp.sum(cols[:, 2:3] * s_new, axis=0, keepdims=True)
        ms = jnp.mean(o * o, axis=-1, keepdims=True)
        zg = zg_ref[row, cs]
        hn_ref[row, cs] = o * lax.rsqrt(ms + RMS_EPS) * g_ref[:, cs] * (zg * _sigmoid(zg))


def _hgrn_step(hq, hk, hv, lf, zg, g_hg, state):
    bsz = hq.shape[0]
    one = pl.BlockSpec((None, 1, HG_WIDTH), lambda b: (b, 0, 0))
    st = pl.BlockSpec((None, HG_HEADS, HG_HEAD_DIM, HG_HEAD_DIM), lambda b: (b, 0, 0, 0))
    r3 = lambda a: a.reshape(bsz, 1, HG_WIDTH)
    hn, s_new = pl.pallas_call(
        _hgrn_step_kernel,
        grid=(bsz,),
        in_specs=[one, one, one, one, one, pl.BlockSpec((1, HG_WIDTH), lambda b: (0, 0)), st],
        out_specs=[one, st],
        out_shape=[jax.ShapeDtypeStruct((bsz, 1, HG_WIDTH), F32),
                   jax.ShapeDtypeStruct(state.shape, F32)],
        compiler_params=_cparams(("parallel",)),
        name="hgrn_step",
    )(r3(hq), r3(hk), r3(hv), r3(lf), r3(zg), g_hg.reshape(1, HG_WIDTH), state)
    return hn.reshape(bsz, HG_WIDTH), s_new


ATT_N = 128
ATT_SUPER = 2048
ATT_G = 4


def _attn_prompt_kernel(q_ref, k_ref, v_ref, o_ref, osc, lsc, *, seq_len):
    lane = lax.broadcasted_iota(jnp.int32, (ATT_N, LANES), 1)
    rowi = lax.broadcasted_iota(jnp.int32, (ATT_N, LANES), 0)
    head0 = lane < ATT_HEAD_DIM
    kidx = lax.broadcasted_iota(jnp.int32, (ATT_N, 2 * ATT_N), 1)
    qidx = lax.broadcasted_iota(jnp.int32, (ATT_N, 2 * ATT_N), 0)
    band = jnp.logical_and(kidx >= qidx, kidx <= qidx + ATT_N)
    in_prev = kidx < ATT_N
    scale = ATT_HEAD_DIM ** -0.5

    bidx = lax.broadcasted_iota(jnp.int32, (ATT_G, ATT_N, 2 * ATT_N), 0)
    bqk = (((2,), (2,)), ((0,), (0,)))
    bkd = (((2,), (1,)), ((0,), (0,)))
    rows_g = ATT_G * ATT_N

    def do_group(p, d, base, g):
        span = ATT_N * d
        if d == 1:
            off = g * rows_g
            start = base + off

            def cur(ref):
                return ref[pl.ds(start, rows_g), :].reshape(ATT_G, ATT_N, LANES)

            def prv(ref, c):
                before = ref[pl.ds(jnp.maximum(start - ATT_N, 0), ATT_N), :].astype(BF16)
                return jnp.concatenate([before[None], c[:ATT_G - 1]], axis=0)

            pen = jnp.where(jnp.logical_and(jnp.logical_and(bidx == 0, in_prev), start == 0), NEG, 0.0)
        else:
            per_blk = d // ATT_G
            off = (g // per_blk) * span + (g % per_blk) * ATT_G
            start = base + off
            prev = jnp.maximum(start - span, 0)

            def cur(ref):
                return jnp.stack([ref[pl.ds(start + r, ATT_N, stride=d), :] for r in range(ATT_G)])

            def prv(ref, c):
                return jnp.stack([ref[pl.ds(prev + r, ATT_N, stride=d), :] for r in range(ATT_G)]).astype(BF16)

            pen = jnp.where(jnp.logical_and(in_prev, start < span), NEG, 0.0)
        q = cur(q_ref) * scale
        kc = cur(k_ref).astype(BF16)
        vc = cur(v_ref).astype(BF16)
        kk = jnp.concatenate([prv(k_ref, kc), kc], axis=1)
        vv = jnp.concatenate([prv(v_ref, vc), vc], axis=1)
        res = []
        for hm in (head0, jnp.logical_not(head0)):
            qh = jnp.where(hm, q, 0.0).astype(BF16)
            s = lax.dot_general(qh, kk, bqk, preferred_element_type=F32)
            s = jnp.where(band, s, NEG) + pen
            m = jnp.max(s, axis=-1, keepdims=True)
            pr = jnp.exp(s - m)
            den = jnp.sum(pr, axis=-1, keepdims=True)
            o = lax.dot_general(pr.astype(BF16), vv, bkd, preferred_element_type=F32)
            res.append((o / den, m + jnp.log(den)))
        o = jnp.where(head0, res[0][0], res[1][0])
        lse = jnp.where(head0, res[0][1], res[1][1])
        if d == 1:
            osc[p, pl.ds(off, rows_g), :] = o.reshape(rows_g, LANES)
            lsc[p, pl.ds(off, rows_g), :] = lse.reshape(rows_g, LANES)
        else:
            for r in range(ATT_G):
                osc[p, pl.ds(off + r, ATT_N, stride=d), :] = o[r]
                lsc[p, pl.ds(off + r, ATT_N, stride=d), :] = lse[r]

    def superblock(sb, carry):
        base = sb * ATT_SUPER

        def groups(g, c2):
            for p, (w, d) in enumerate(DILATED_PATTERNS):
                do_group(p, d, base, g)
            return c2

        lax.fori_loop(0, ATT_SUPER // rows_g, groups, 0, unroll=2)

        piece = 256

        def merge(j, c2):
            r = pl.ds(pl.multiple_of(j * piece, piece), piece)
            ls = [lsc[p, r, :] for p in range(len(DILATED_PATTERNS))]
            mx = jnp.maximum(jnp.maximum(ls[0], ls[1]), ls[2])
            ws = [jnp.exp(l - mx) for l in ls]
            num = ws[0] * osc[0, r, :] + ws[1] * osc[1, r, :] + ws[2] * osc[2, r, :]
            o_ref[pl.ds(pl.multiple_of(base + j * piece, piece), piece), :] = num / (ws[0] + ws[1] + ws[2])
            return c2

        lax.fori_loop(0, ATT_SUPER // piece, merge, 0)
        return carry

    lax.fori_loop(0, seq_len // ATT_SUPER, superblock, 0)


def _attn_prompt(aq, ak, av):
    bsz, t, _ = aq.shape
    spec = pl.BlockSpec((None, t, LANES), lambda b, p: (b, 0, p))
    n_pat = len(DILATED_PATTERNS)
    return pl.pallas_call(
        functools.partial(_attn_prompt_kernel, seq_len=t),
        grid=(bsz, ATT_WIDTH // LANES),
        in_specs=[spec, spec, spec],
        out_specs=spec,
        out_shape=jax.ShapeDtypeStruct((bsz, t, ATT_WIDTH), F32),
        scratch_shapes=[pltpu.VMEM((n_pat, ATT_SUPER, LANES), F32),
                        pltpu.VMEM((n_pat, ATT_SUPER, LANES), F32)],
        compiler_params=_cparams(("parallel", "parallel")),
        name="attn_prompt",
    )(aq, ak, av)


def _attn_step_kernel(q_ref, kn_ref, vn_ref, ck_ref, cv_ref, o_ref, nk_ref, nv_ref):
    win = ck_ref.shape[-1]
    kt, vt = ck_ref[...], cv_ref[...]
    q = q_ref[...] * (ATT_HEAD_DIM ** -0.5)
    kn, vn = kn_ref[...], vn_ref[...]
    s_all = jnp.sum(kt * q, axis=1, keepdims=True)
    s_new = jnp.sum(kn * q, axis=1, keepdims=True)
    dist = win - lax.broadcasted_iota(jnp.int32, (1, 1, win), 2)

    ps, pnews, lses = [], [], []
    for w, d in DILATED_PATTERNS:
        on_stride = (dist & (d - 1)) == 0 if d & (d - 1) == 0 else dist % d == 0
        valid = jnp.logical_and(dist <= w, on_stride)
        sm = jnp.where(valid, s_all, NEG)
        m = jnp.maximum(jnp.max(sm, axis=-1, keepdims=True), s_new)
        p = jnp.exp(sm - m)
        pn = jnp.exp(s_new - m)
        den = jnp.sum(p, axis=-1, keepdims=True) + pn
        ps.append(p / den)
        pnews.append(pn / den)
        lses.append(m + jnp.log(den))
    mx = jnp.maximum(jnp.maximum(lses[0], lses[1]), lses[2])
    ws = [jnp.exp(l - mx) for l in lses]
    wsum = ws[0] + ws[1] + ws[2]
    p_tot = (ws[0] * ps[0] + ws[1] * ps[1] + ws[2] * ps[2]) / wsum
    pn_tot = (ws[0] * pnews[0] + ws[1] * pnews[1] + ws[2] * pnews[2]) / wsum
    o_ref[...] = jnp.sum(vt * p_tot, axis=-1, keepdims=True) + pn_tot * vn

    last = lax.broadcasted_iota(jnp.int32, (1, 1, win), 2) == win - 1
    nk_ref[...] = jnp.where(last, kn, pltpu.roll(kt, win - 1, 2))
    nv_ref[...] = jnp.where(last, vn, pltpu.roll(vt, win - 1, 2))


def _attn_step(aq, ak, av, cache_k, cache_v):
    bsz, _, _, win = cache_k.shape
    one = pl.BlockSpec((None, ATT_HEADS, ATT_HEAD_DIM, 1), lambda b: (b, 0, 0, 0))
    cache = pl.BlockSpec((None, ATT_HEADS, ATT_HEAD_DIM, win), lambda b: (b, 0, 0, 0))
    col = lambda a: a.reshape(bsz, ATT_HEADS, ATT_HEAD_DIM, 1)
    att, new_k, new_v = pl.pallas_call(
        _attn_step_kernel,
        grid=(bsz,),
        in_specs=[one, one, one, cache, cache],
        out_specs=[one, cache, cache],
        out_shape=[jax.ShapeDtypeStruct((bsz, ATT_HEADS, ATT_HEAD_DIM, 1), F32),
                   jax.ShapeDtypeStruct(cache_k.shape, F32),
                   jax.ShapeDtypeStruct(cache_v.shape, F32)],
        compiler_params=_cparams(("parallel",)),
        name="attn_step",
    )(col(aq), col(ak), col(av), cache_k, cache_v)
    return att.reshape(bsz, ATT_WIDTH), new_k, new_v


ROUTE_TM = 256
SUBLANES = 8
ROW_TILE = D_MODEL // LANES
assert ROW_TILE == SUBLANES


def _store_row_tiles(ref, val, lead=()):
    n = val.shape[0]
    for j in range(ROW_TILE):
        ref[lead + (pl.ds(j, n, stride=ROW_TILE), slice(None))] = val[:, j * LANES:(j + 1) * LANES]


def _load_row_tiles(ref, n, lead=()):
    return jnp.concatenate([ref[lead + (pl.ds(j, n, stride=ROW_TILE), slice(None))] for j in range(ROW_TILE)],
                           axis=1)


def _row_tile(ref, r, n=1, lead=()):
    start = r * ROW_TILE if isinstance(r, int) else pl.multiple_of(r * ROW_TILE, ROW_TILE)
    return ref.at[lead + (pl.ds(start, n * ROW_TILE),)]


def _outproj_kernel(hn_ref, att_ref, x_ref, hn_s_ref, att_s_ref, x_s_ref, w_ref, g_ref, wr_ref,
                    xmid_ref, h2_ref, route_ref, cnt_ref, *, n_sample):
    is_prompt = pl.program_id(0) < pl.num_programs(0) - 1
    hn = jnp.where(is_prompt, hn_ref[...], hn_s_ref[...])
    att = jnp.where(is_prompt, att_ref[...], att_s_ref[...])
    y = (jnp.dot(hn.astype(BF16), w_ref[0:HG_WIDTH, :], preferred_element_type=F32)
         + jnp.dot(att.astype(BF16), w_ref[HG_WIDTH:, :], preferred_element_type=F32))
    xm = jnp.where(is_prompt, x_ref[...], x_s_ref[...]) + y
    xmid_ref[...] = xm
    ms = jnp.mean(xm * xm, axis=-1, keepdims=True)
    h2 = xm * lax.rsqrt(ms + RMS_EPS) * g_ref[...]
    _store_row_tiles(h2_ref, h2)
    h2_hi = h2.astype(BF16)
    h2_lo = (h2 - h2_hi.astype(F32)).astype(BF16)
    lg = (jnp.dot(h2_hi, wr_ref[0], preferred_element_type=F32)
          + (jnp.dot(h2_lo, wr_ref[0], preferred_element_type=F32)
             + jnp.dot(h2_hi, wr_ref[1], preferred_element_type=F32)))
    lane = lax.broadcasted_iota(jnp.int32, lg.shape, 1).astype(F32)
    big = float(LANES)
    gmask = lane < N_GROUPS
    lgg = jnp.where(gmask, lg, NEG)
    mg = jnp.max(lgg, axis=-1, keepdims=True)
    gi = jnp.min(jnp.where(lgg == mg, lane, big), axis=-1, keepdims=True)
    p_grp = 1.0 / jnp.sum(jnp.exp(lgg - mg), axis=-1, keepdims=True)
    lo = N_GROUPS + gi * EXPERTS_PER_GROUP
    emask = jnp.logical_and(lane >= lo, lane < lo + EXPERTS_PER_GROUP)
    le1 = jnp.where(emask, lg, NEG)
    m1 = jnp.max(le1, axis=-1, keepdims=True)
    i1 = jnp.min(jnp.where(le1 == m1, lane, big), axis=-1, keepdims=True)
    le2 = jnp.where(lane == i1, NEG, le1)
    m2 = jnp.max(le2, axis=-1, keepdims=True)
    i2 = jnp.min(jnp.where(le2 == m2, lane, big), axis=-1, keepdims=True)
    r = jnp.exp(m2 - m1)
    g1 = p_grp / (1.0 + r)
    g2 = p_grp * r / (1.0 + r)
    e1, e2 = i1 - N_GROUPS, i2 - N_GROUPS
    route_ref[...] = jnp.where(lane == 0, e1,
                               jnp.where(lane == 1, e2,
                                         jnp.where(lane == 2, g1, jnp.where(lane == 3, g2, 0.0))))

    @pl.when(pl.program_id(0) == 0)
    def _zero():
        cnt_ref[...] = jnp.zeros_like(cnt_ref)

    rows = lax.broadcasted_iota(jnp.int32, lg.shape, 0)
    real = jnp.logical_or(is_prompt, rows < n_sample)
    hit = jnp.logical_and(real, jnp.logical_or(lane == e1, lane == e2))
    cnt_ref[...] += jnp.sum(jnp.where(hit, 1.0, 0.0), axis=0, keepdims=True)


def _outproj(hn_p, att_p, x_p, hn_s, att_s, x_s, w_out_bf, g_ffn, w_router):
    tm = ROUTE_TM
    n_p = x_p.shape[0]
    n_tiles = n_p // tm + 1
    pad = lambda a: jnp.pad(a, ((0, tm - a.shape[0]), (0, 0)))
    row_p = lambda i: (jnp.minimum(i, n_tiles - 2), 0)
    row = lambda i: (i, 0)
    const = lambda i: (0, 0)
    n_rows = n_tiles * tm
    return pl.pallas_call(
        functools.partial(_outproj_kernel, n_sample=x_s.shape[0]),
        grid=(n_tiles,),
        in_specs=[pl.BlockSpec((tm, HG_WIDTH), row_p),
                  pl.BlockSpec((tm, ATT_WIDTH), row_p),
                  pl.BlockSpec((tm, D_MODEL), row_p),
                  pl.BlockSpec((tm, HG_WIDTH), const),
                  pl.BlockSpec((tm, ATT_WIDTH), const),
                  pl.BlockSpec((tm, D_MODEL), const),
                  pl.BlockSpec((D_MODEL, D_MODEL), const),
                  pl.BlockSpec((1, D_MODEL), const),
                  pl.BlockSpec((2, D_MODEL, LANES), lambda i: (0, 0, 0))],
        out_specs=[pl.BlockSpec((tm, D_MODEL), row),
                   pl.BlockSpec((tm * ROW_TILE, LANES), row),
                   pl.BlockSpec((tm, LANES), row),
                   pl.BlockSpec((SUBLANES, LANES), const)],
        out_shape=[jax.ShapeDtypeStruct((n_rows, D_MODEL), F32),
                   jax.ShapeDtypeStruct((n_rows * ROW_TILE, LANES), F32),
                   jax.ShapeDtypeStruct((n_rows, LANES), F32),
                   jax.ShapeDtypeStruct((SUBLANES, LANES), F32)],
        compiler_params=_cparams(("arbitrary",)),
        name="outproj",
    )(hn_p, att_p, x_p, pad(hn_s), pad(att_s), pad(x_s), w_out_bf, g_ffn.reshape(1, D_MODEL), w_router)


def _rank_kernel(route_ref, cnt_ref, slot_ref, meta_ref, base_ref, *, n_tok, blk):
    i = pl.program_id(0)
    tm = route_ref.shape[0]
    lane = lax.broadcasted_iota(jnp.int32, (tm, LANES), 1).astype(F32)
    rowg = i * tm + lax.broadcasted_iota(jnp.int32, (tm, LANES), 0)
    valid = rowg < n_tok
    r = route_ref[...]
    oh0 = jnp.where(jnp.logical_and(valid, lane == r[:, 0:1]), 1.0, 0.0)
    oh1 = jnp.where(jnp.logical_and(valid, lane == r[:, 1:2]), 1.0, 0.0)
    oh = oh0 + oh1

    @pl.when(i == 0)
    def _starts():
        cnt = cnt_ref[0:1, :].astype(jnp.int32)
        shift = blk.bit_length() - 1
        padded = (((cnt + (blk - 1)) >> shift) << shift).astype(F32)
        up = (lax.broadcasted_iota(jnp.int32, (LANES, LANES), 0)
              < lax.broadcasted_iota(jnp.int32, (LANES, LANES), 1)).astype(F32)
        start = jnp.dot(jnp.broadcast_to(padded, (8, LANES)), up, precision=HIGHEST,
                        preferred_element_type=F32)[0:1]
        base_ref[...] = start
        nb = meta_ref.shape[0]
        lane_b = lax.broadcasted_iota(jnp.int32, (nb, LANES), 1)
        blk_start = (lax.broadcasted_iota(jnp.int32, (nb, LANES), 0) * blk).astype(F32)
        ended = jnp.logical_and(start + padded <= blk_start, lane_b < N_EXPERTS)
        be = jnp.minimum(jnp.sum(jnp.where(ended, 1.0, 0.0), axis=-1, keepdims=True), N_EXPERTS - 1.0)
        n_used = jnp.sum(padded, axis=-1, keepdims=True) * (1.0 / blk)
        meta_ref[...] = jnp.where(lane_b == 0, be, jnp.where(lane_b == 1, n_used, 0.0)).astype(jnp.int32)

    before = (lax.broadcasted_iota(jnp.int32, (tm, tm), 1)
              < lax.broadcasted_iota(jnp.int32, (tm, tm), 0)).astype(BF16)
    pre = jnp.dot(before, oh.astype(BF16), preferred_element_type=F32) + base_ref[...]
    s0 = jnp.sum(oh0 * pre, axis=-1, keepdims=True)
    s1 = jnp.sum(oh1 * pre, axis=-1, keepdims=True)
    slot_ref[...] = jnp.where(lane == 0, s0, jnp.where(lane == 1, s1, 0.0)).astype(jnp.int32)
    base_ref[...] += jnp.sum(oh, axis=0, keepdims=True)


def _rank(route, counts, n_tok, blk, nblk):
    n_rows = route.shape[0]
    nb = (nblk + 7) // 8 * 8
    slots, meta = pl.pallas_call(
        functools.partial(_rank_kernel, n_tok=n_tok, blk=blk),
        grid=(n_rows // ROUTE_TM,),
        in_specs=[pl.BlockSpec((ROUTE_TM, LANES), lambda i: (i, 0)),
                  pl.BlockSpec(counts.shape, lambda i: (0, 0))],
        out_specs=[pl.BlockSpec((ROUTE_TM, LANES), lambda i: (i, 0)),
                   pl.BlockSpec((nb, LANES), lambda i: (0, 0))],
        out_shape=[jax.ShapeDtypeStruct((n_rows, LANES), jnp.int32),
                   jax.ShapeDtypeStruct((nb, LANES), jnp.int32)],
        scratch_shapes=[pltpu.VMEM((1, LANES), F32)],
        compiler_params=_cparams(("arbitrary",)),
        name="rank",
    )(route, counts)
    return slots[:n_tok, :TOP_K_INNER].reshape(-1), meta[:nblk, 0], meta[0:1, 1]


DMA_UNROLL = 8


def _dispatch_kernel(slot_ref, be_ref, nu_ref, h2_ref, xb_out, zbuf, sem, zsem, *, n_tok):
    i = pl.program_id(0)
    tm = h2_ref.shape[0] // ROW_TILE
    tail = n_tok % tm

    @pl.when(i == 0)
    def _zero_padding():
        blk = zbuf.shape[0] // ROW_TILE
        n_blocks = be_ref.shape[0]
        zbuf[...] = jnp.zeros_like(zbuf)

        def ends_expert(j):
            nxt = be_ref[jnp.minimum(j + 1, n_blocks - 1)]
            return jnp.logical_or(j >= nu_ref[0] - 1, be_ref[j] != nxt)

        def issue(j, c):
            @pl.when(ends_expert(j))
            def _():
                pltpu.make_async_copy(zbuf, _row_tile(xb_out, j * blk, blk), zsem).start()
            return c

        def drain(j, c):
            @pl.when(ends_expert(j))
            def _():
                pltpu.make_async_copy(zbuf, _row_tile(xb_out, 0, blk), zsem).wait()
            return c

        lax.fori_loop(0, n_blocks, issue, 0)
        lax.fori_loop(0, n_blocks, drain, 0)

    def push(rows):
        def body(r, c):
            a = (i * tm + r) * TOP_K_INNER
            for k in range(TOP_K_INNER):
                pltpu.make_async_copy(_row_tile(h2_ref, r), _row_tile(xb_out, slot_ref[a + k]), sem).start(
                    priority=k % 2)
            return c
        lax.fori_loop(0, rows, body, 0, unroll=DMA_UNROLL)
        for k in range(TOP_K_INNER):
            pltpu.make_async_copy(_row_tile(h2_ref, 0, rows), _row_tile(xb_out, 0, rows), sem).wait()

    last = pl.num_programs(0) - 1
    if tail == 0:
        push(tm)
    else:
        @pl.when(i < last)
        def _full():
            push(tm)

        @pl.when(i == last)
        def _tail():
            push(tail)


def _dispatch(slot_flat, blk_expert, n_used, h2, n_tok, blk):
    tm = ROUTE_TM
    n_slots = blk_expert.shape[0] * blk
    grid_spec = pltpu.PrefetchScalarGridSpec(
        num_scalar_prefetch=3,
        grid=(h2.shape[0] // (tm * ROW_TILE),),
        in_specs=[pl.BlockSpec((tm * ROW_TILE, LANES), lambda i, s, be, nu: (i, 0))],
        out_specs=pl.BlockSpec(memory_space=pl.ANY),
        scratch_shapes=[pltpu.VMEM((blk * ROW_TILE, LANES), F32),
                        pltpu.SemaphoreType.DMA(()),
                        pltpu.SemaphoreType.DMA(())],
    )
    return pl.pallas_call(
        functools.partial(_dispatch_kernel, n_tok=n_tok),
        grid_spec=grid_spec,
        out_shape=jax.ShapeDtypeStruct((n_slots * ROW_TILE, LANES), F32),
        compiler_params=_cparams(("arbitrary",)),
        name="dispatch",
    )(slot_flat, blk_expert, n_used, h2)


def _expert_kernel(be_ref, nu_ref, x_ref, wg_hbm, wu_hbm, wd_hbm, y_ref,
                   wg_f, wu_f, wd_f, sem, wgb, wub, wdb, cur_ref):
    i = pl.program_id(0)
    n_used = nu_ref[0]
    e = be_ref[i]
    e_prev = be_ref[jnp.maximum(i - 1, 0)]

    def fetch(ex, s):
        return [pltpu.make_async_copy(hbm.at[ex], buf.at[s], sem.at[s])
                for hbm, buf in ((wg_hbm, wg_f), (wu_hbm, wu_f), (wd_hbm, wd_f))]

    @pl.when(i == 0)
    def _first():
        cur_ref[0] = 0
        for c in fetch(e, 0):
            c.start()

    @pl.when(jnp.logical_and(jnp.logical_or(i == 0, e != e_prev), i < n_used))
    def _new_expert():
        s = cur_ref[0]
        j = lax.while_loop(lambda j: jnp.logical_and(j < n_used, be_ref[jnp.minimum(j, n_used - 1)] == e),
                           lambda j: j + 1, i + 1)

        @pl.when(j < n_used)
        def _prefetch():
            for c in fetch(be_ref[j], 1 - s):
                c.start()

        for c in fetch(e, s):
            c.wait()
        wgb[...] = wg_f[s].astype(BF16)
        wub[...] = wu_f[s].astype(BF16)
        wdb[...] = wd_f[s].astype(BF16)
        cur_ref[0] = 1 - s

    @pl.when(i < n_used)
    def _run():
        x = _load_row_tiles(x_ref, x_ref.shape[0] // ROW_TILE).astype(BF16)
        a = jnp.dot(x, wgb[...], preferred_element_type=F32)
        u = jnp.dot(x, wub[...], preferred_element_type=F32)
        mid = (a * _sigmoid(a) * u).astype(BF16)
        _store_row_tiles(y_ref, jnp.dot(mid, wdb[...], preferred_element_type=F32))

    @pl.when(i >= nu_ref[0])
    def _skip():
        y_ref[...] = jnp.zeros_like(y_ref)


def _experts(xb, blk_expert, n_used, w_g, w_u, w_d, blk):
    nblk = blk_expert.shape[0]
    hbm = pl.BlockSpec(memory_space=pl.ANY)
    grid_spec = pltpu.PrefetchScalarGridSpec(
        num_scalar_prefetch=2,
        grid=(nblk,),
        in_specs=[pl.BlockSpec((blk * ROW_TILE, LANES), lambda i, be, nu: (jnp.minimum(i, nu[0] - 1), 0)),
                  hbm, hbm, hbm],
        out_specs=pl.BlockSpec((blk * ROW_TILE, LANES), lambda i, be, nu: (i, 0)),
        scratch_shapes=[pltpu.VMEM((2, D_MODEL, D_FF_EXPERT), F32),
                        pltpu.VMEM((2, D_MODEL, D_FF_EXPERT), F32),
                        pltpu.VMEM((2, D_FF_EXPERT, D_MODEL), F32),
                        pltpu.SemaphoreType.DMA((2,)),
                        pltpu.VMEM((D_MODEL, D_FF_EXPERT), BF16),
                        pltpu.VMEM((D_MODEL, D_FF_EXPERT), BF16),
                        pltpu.VMEM((D_FF_EXPERT, D_MODEL), BF16),
                        pltpu.SMEM((1,), jnp.int32)],
    )
    return pl.pallas_call(
        _expert_kernel,
        grid_spec=grid_spec,
        out_shape=jax.ShapeDtypeStruct((nblk * blk * ROW_TILE, LANES), F32),
        compiler_params=_cparams(("arbitrary",)),
        name="experts",
    )(blk_expert, n_used, xb, w_g, w_u, w_d)


def _invert_kernel(slot_ref, inv_ref):
    def clear(j, c):
        inv_ref[j] = 0
        return c

    lax.fori_loop(0, inv_ref.shape[0], clear, 0, unroll=16)

    def put(a, c):
        inv_ref[slot_ref[a]] = a
        return c

    lax.fori_loop(0, slot_ref.shape[0], put, 0, unroll=16)


def _invert(slot_flat, n_slots):
    return pl.pallas_call(
        _invert_kernel,
        in_specs=[pl.BlockSpec(memory_space=pltpu.SMEM)],
        out_specs=pl.BlockSpec(memory_space=pltpu.SMEM),
        out_shape=jax.ShapeDtypeStruct((n_slots,), jnp.int32),
        name="invert",
    )(slot_flat)


def _expert_gather_kernel(be_ref, nu_ref, inv_ref, h2_hbm, wg_hbm, wu_hbm, wd_hbm, y_ref,
                          xbuf, gsem, wg_f, wu_f, wd_f, sem, wgb, wub, wdb, cur_ref, *, blk):
    i = pl.program_id(0)
    n_used = nu_ref[0]
    e = be_ref[i]
    e_prev = be_ref[jnp.maximum(i - 1, 0)]

    def fetch(ex, s):
        return [pltpu.make_async_copy(hbm.at[ex], buf.at[s], sem.at[s])
                for hbm, buf in ((wg_hbm, wg_f), (wu_hbm, wu_f), (wd_hbm, wd_f))]

    def gather_row(j, r, buf, queue):
        tok = inv_ref[j * blk + r] >> 1
        pltpu.make_async_copy(_row_tile(h2_hbm, tok), _row_tile(xbuf, r, lead=(buf,)), gsem.at[buf]).start(
            priority=queue)

    def gather_wait(buf):
        pltpu.make_async_copy(_row_tile(h2_hbm, 0, blk), xbuf.at[buf], gsem.at[buf]).wait()

    @pl.when(i == 0)
    def _first():
        cur_ref[0] = 0
        for c in fetch(e, 0):
            c.start()

        def body(r8, c):
            for u in range(DMA_UNROLL):
                gather_row(0, r8 * DMA_UNROLL + u, 0, u % 2)
            return c
        lax.fori_loop(0, blk // DMA_UNROLL, body, 0)

    @pl.when(jnp.logical_and(jnp.logical_or(i == 0, e != e_prev), i < n_used))
    def _new_expert():
        s = cur_ref[0]
        j = lax.while_loop(lambda j: jnp.logical_and(j < n_used, be_ref[jnp.minimum(j, n_used - 1)] == e),
                           lambda j: j + 1, i + 1)

        @pl.when(j < n_used)
        def _prefetch():
            for c in fetch(be_ref[j], 1 - s):
                c.start()

        for c in fetch(e, s):
            c.wait()
        wgb[...] = wg_f[s].astype(BF16)
        wub[...] = wu_f[s].astype(BF16)
        wdb[...] = wd_f[s].astype(BF16)
        cur_ref[0] = 1 - s

    @pl.when(i < n_used)
    def _run():
        buf = i % 2
        gather_wait(buf)
        x = _load_row_tiles(xbuf, blk, lead=(buf,)).astype(BF16)
        nxt = jnp.minimum(i + 1, n_used - 1)
        for r in range(blk):
            gather_row(nxt, r, 1 - buf, r % 2)
        a = jnp.dot(x, wgb[...], preferred_element_type=F32)
        u = jnp.dot(x, wub[...], preferred_element_type=F32)
        mid = (a * _sigmoid(a) * u).astype(BF16)
        _store_row_tiles(y_ref, jnp.dot(mid, wdb[...], preferred_element_type=F32))

        @pl.when(i == n_used - 1)
        def _drain():
            gather_wait(1 - buf)

    @pl.when(i >= n_used)
    def _skip():
        y_ref[...] = jnp.zeros_like(y_ref)


def _experts_gather(h2, inv, blk_expert, n_used, w_g, w_u, w_d, blk):
    nblk = blk_expert.shape[0]
    hbm = pl.BlockSpec(memory_space=pl.ANY)
    grid_spec = pltpu.PrefetchScalarGridSpec(
        num_scalar_prefetch=3,
        grid=(nblk,),
        in_specs=[hbm, hbm, hbm, hbm],
        out_specs=pl.BlockSpec((blk * ROW_TILE, LANES), lambda i, be, nu, inv: (i, 0)),
        scratch_shapes=[pltpu.VMEM((2, blk * ROW_TILE, LANES), F32),
                        pltpu.SemaphoreType.DMA((2,)),
                        pltpu.VMEM((2, D_MODEL, D_FF_EXPERT), F32),
                        pltpu.VMEM((2, D_MODEL, D_FF_EXPERT), F32),
                        pltpu.VMEM((2, D_FF_EXPERT, D_MODEL), F32),
                        pltpu.SemaphoreType.DMA((2,)),
                        pltpu.VMEM((D_MODEL, D_FF_EXPERT), BF16),
                        pltpu.VMEM((D_MODEL, D_FF_EXPERT), BF16),
                        pltpu.VMEM((D_FF_EXPERT, D_MODEL), BF16),
                        pltpu.SMEM((1,), jnp.int32)],
    )
    return pl.pallas_call(
        functools.partial(_expert_gather_kernel, blk=blk),
        grid_spec=grid_spec,
        out_shape=jax.ShapeDtypeStruct((nblk * blk * ROW_TILE, LANES), F32),
        compiler_params=_cparams(("arbitrary",)),
        name="experts",
    )(blk_expert, n_used, inv, h2, w_g, w_u, w_d)


def _final_kernel(slot_ref, x_ref, route_ref, g_ref, yb_hbm, o_ref, ybuf, sem, *, tok0):
    i = pl.program_id(0)
    tm = x_ref.shape[0]

    def gather(j, buf):
        def body(r, c):
            a = (tok0 + j * tm + r) * TOP_K_INNER
            for k in range(TOP_K_INNER):
                pltpu.make_async_copy(_row_tile(yb_hbm, slot_ref[a + k]), _row_tile(ybuf, r, lead=(buf, k)),
                                      sem.at[buf]).start(priority=k % 2)
            return c
        lax.fori_loop(0, tm, body, 0, unroll=DMA_UNROLL)

    @pl.when(i == 0)
    def _first():
        gather(0, 0)

    @pl.when(i + 1 < pl.num_programs(0))
    def _next():
        gather(i + 1, (i + 1) % 2)

    buf = i % 2
    for k in range(TOP_K_INNER):
        pltpu.make_async_copy(_row_tile(yb_hbm, 0, tm), ybuf.at[buf, k], sem.at[buf]).wait()
    route = route_ref[...]
    y0 = _load_row_tiles(ybuf, tm, lead=(buf, 0))
    y1 = _load_row_tiles(ybuf, tm, lead=(buf, 1))
    x = x_ref[...] + (y0 * route[:, 2:3] + y1 * route[:, 3:4])
    ms = jnp.mean(x * x, axis=-1, keepdims=True)
    o_ref[...] = x * lax.rsqrt(ms + RMS_EPS) * g_ref[...]


def _final(slot_flat, xmid, route, g_final, yb, tok0, n_out, tm):
    blk0 = tok0 // tm
    grid_spec = pltpu.PrefetchScalarGridSpec(
        num_scalar_prefetch=1,
        grid=(n_out // tm,),
        in_specs=[pl.BlockSpec((tm, D_MODEL), lambda i, s: (i + blk0, 0)),
                  pl.BlockSpec((tm, LANES), lambda i, s: (i + blk0, 0)),
                  pl.BlockSpec((1, D_MODEL), lambda i, s: (0, 0)),
                  pl.BlockSpec(memory_space=pl.ANY)],
        out_specs=pl.BlockSpec((tm, D_MODEL), lambda i, s: (i, 0)),
        scratch_shapes=[pltpu.VMEM((2, TOP_K_INNER, tm * ROW_TILE, LANES), F32),
                        pltpu.SemaphoreType.DMA((2,))],
    )
    return pl.pallas_call(
        functools.partial(_final_kernel, tok0=tok0),
        grid_spec=grid_spec,
        out_shape=jax.ShapeDtypeStruct((n_out, D_MODEL), F32),
        compiler_params=_cparams(("arbitrary",)),
        name="final",
    )(slot_flat, xmid, route, g_final.reshape(1, D_MODEL), yb)


def kernel(x_prompt, x_sample, cache_attn_k, cache_attn_v, state_hgrn, w_in, w_out, hg_lb_logits,
           hg_norm_g, norm_mix_g, norm_ffn_g, norm_final_g, w_route_group, w_route_expert,
           w_expert_gate, w_expert_up, w_expert_down):
    bp, tp, _ = x_prompt.shape
    bs = x_sample.shape[0]
    l = 0
    w_in_bf = w_in[l].astype(BF16)
    w_out_bf = w_out[l].astype(BF16)
    w_router = jnp.concatenate(
        [w_route_group[l],
         jnp.transpose(w_route_expert[l], (1, 0, 2)).reshape(D_MODEL, N_EXPERTS),
         jnp.zeros((D_MODEL, LANES - N_GROUPS - N_EXPERTS), F32)], axis=-1)
    w_router_hi = w_router.astype(BF16)
    w_router = jnp.stack([w_router_hi, (w_router - w_router_hi.astype(F32)).astype(BF16)])

    n_p = bp * tp
    xp = x_prompt.reshape(n_p, D_MODEL)
    pos_p = jnp.arange(tp, dtype=jnp.int32)
    hq, hk, hv, lf, zg, aq, ak, av = _inproj(xp, norm_mix_g[l], w_in_bf, hg_lb_logits, pos_p, 256)
    seq3 = lambda a: a.reshape(bp, tp, HG_WIDTH)
    hn_p, s_fin = _hgrn_prompt(seq3(hq), seq3(hk), seq3(hv), seq3(lf), seq3(zg), hg_norm_g[l])
    att_p = _attn_prompt(seq3(aq), seq3(ak), seq3(av))
    keep = min(MAX_WINDOW, tp)
    heads = lambda a: a.reshape(1, bp, keep, ATT_HEADS, ATT_HEAD_DIM)
    new_k_p = heads(seq3(ak)[:, tp - keep:])
    new_v_p = heads(seq3(av)[:, tp - keep:])

    xs = x_sample.reshape(bs, D_MODEL)
    pos_s = jnp.full((bs,), PAST_LEN, jnp.int32)
    hq, hk, hv, lf, zg, aq, ak, av = _inproj(xs, norm_mix_g[l], w_in_bf, hg_lb_logits, pos_s, bs)
    hn_s, s_new = _hgrn_step(hq, hk, hv, lf, zg, hg_norm_g[l], state_hgrn[l])
    feat = lambda a: jnp.transpose(a, (0, 2, 3, 1))
    att_s, new_k_s, new_v_s = _attn_step(aq, ak, av, feat(cache_attn_k[l]), feat(cache_attn_v[l]))
    cache5 = lambda a: jnp.transpose(a, (0, 3, 1, 2))[None]

    assert n_p % ROUTE_TM == 0 and bs <= ROUTE_TM
    n_tok = n_p + bs
    xmid, h2, route, counts = _outproj(hn_p.reshape(n_p, HG_WIDTH), att_p.reshape(n_p, ATT_WIDTH), xp,
                                       hn_s, att_s, xs, w_out_bf, norm_ffn_g[l], w_router)
    blk = MOE_BLOCK
    nblk = (n_tok * TOP_K_INNER + N_EXPERTS * (blk - 1)) // blk + 1
    slot_flat, blk_expert, n_used = _rank(route, counts, n_tok, blk, nblk)
    inv = _invert(slot_flat, nblk * blk)
    yb = _experts_gather(h2, inv, blk_expert, n_used, w_expert_gate[l], w_expert_up[l], w_expert_down[l], blk)
    y_prompt = _final(slot_flat, xmid, route, norm_final_g, yb, 0, n_p, 256)
    y_sample = _final(slot_flat, xmid, route, norm_final_g, yb, n_p, bs, bs)

    return (y_prompt.reshape(bp, tp, D_MODEL), y_sample.reshape(bs, 1, D_MODEL),
            new_k_p, new_v_p, s_fin[None], cache5(new_k_s), cache5(new_v_s), s_new[None])
```

```python
import functools

import jax
import jax.numpy as jnp
from jax import lax
from jax.experimental import pallas as pl
from jax.experimental.pallas import tpu as pltpu

F32 = jnp.float32
BF16 = jnp.bfloat16

D_MODEL = 1024
HG_WIDTH = 512
HG_HEAD_DIM = 128
HG_HEADS = 4
ATT_WIDTH = 512
ATT_HEAD_DIM = 64
ATT_HEADS = 8
ROPE_DIM = 16
ROPE_THETA = 500000.0
DILATED_PATTERNS = ((128, 1), (512, 4), (2048, 16))
MAX_WINDOW = 2048
PAST_LEN = 16384
N_GROUPS = 8
EXPERTS_PER_GROUP = 8
N_EXPERTS = 64
TOP_K_INNER = 2
D_FF_EXPERT = 512
MOE_BLOCK = 256
IN_COLS = 4 * HG_WIDTH + 3 * ATT_WIDTH
RMS_EPS = 1e-6

LANES = 128
VMEM_LIMIT = 56 * 1024 * 1024
NEG = -1e30
HIGHEST = lax.Precision.HIGHEST
NT_DIMS = (((1,), (1,)), ((), ()))


def _sigmoid(z):
    return 1.0 / (1.0 + jnp.exp(-z))


def _cparams(sem):
    return pltpu.CompilerParams(dimension_semantics=sem, vmem_limit_bytes=VMEM_LIMIT)


def _inproj_kernel(x_ref, g_ref, w_ref, lbl_ref, cos_ref, sa_ref, sb_ref,
                   hq_ref, hk_ref, hv_ref, lf_ref, zg_ref, aq_ref, ak_ref, av_ref):
    x = x_ref[...]
    ms = jnp.mean(x * x, axis=-1, keepdims=True)
    h = (x * lax.rsqrt(ms + RMS_EPS) * g_ref[...]).astype(BF16)

    def mm(c0):
        return jnp.dot(h, w_ref[:, c0:c0 + HG_WIDTH], preferred_element_type=F32)

    lbl = lbl_ref[...]
    le = jnp.exp(lbl - jnp.max(lbl, axis=0, keepdims=True))
    lb = le[0:1, :] / jnp.sum(le, axis=0, keepdims=True)

    zq = mm(0)
    hq_ref[...] = zq * _sigmoid(zq)
    zf = mm(HG_WIDTH)
    f = lb + (1.0 - lb) * _sigmoid(zf)
    hk_ref[...] = 1.0 - f
    lf_ref[...] = jnp.log(f)
    hv_ref[...] = mm(2 * HG_WIDTH)
    zg_ref[...] = mm(3 * HG_WIDTH)

    cos, sa, sb = cos_ref[...], sa_ref[...], sb_ref[...]

    def rope(a, out_ref):
        for j in range(ATT_WIDTH // LANES):
            xj = a[:, j * LANES:(j + 1) * LANES]
            up = pltpu.roll(xj, LANES - ROPE_DIM // 2, 1)
            dn = pltpu.roll(xj, ROPE_DIM // 2, 1)
            out_ref[:, j * LANES:(j + 1) * LANES] = xj * cos + up * sa + dn * sb

    rope(mm(4 * HG_WIDTH), aq_ref)
    rope(mm(4 * HG_WIDTH + ATT_WIDTH), ak_ref)
    av_ref[...] = mm(4 * HG_WIDTH + 2 * ATT_WIDTH)


def _rope_tables(pos):
    half = ROPE_DIM // 2
    c = jnp.arange(LANES) % ATT_HEAD_DIM
    inv_freq = ROPE_THETA ** (-(c % half).astype(F32) / half)
    ang = pos.astype(F32)[:, None] * inv_freq[None, :]
    cos, sin = jnp.cos(ang), jnp.sin(ang)
    return (jnp.where(c < ROPE_DIM, cos, 1.0),
            jnp.where(c < half, -sin, 0.0),
            jnp.where(jnp.logical_and(c >= half, c < ROPE_DIM), sin, 0.0))


def _inproj(x2d, g, w_bf, lb_logits, pos, tm):
    m = x2d.shape[0]
    cos, sa, sb = _rope_tables(pos)
    row = lambda i: (i, 0)
    seq_tiles = pos.shape[0] // tm
    row_pos = lambda i: (i % seq_tiles, 0)
    const = lambda i: (0, 0)
    outs = [jax.ShapeDtypeStruct((m, HG_WIDTH), F32)] * 8
    return pl.pallas_call(
        _inproj_kernel,
        grid=(m // tm,),
        in_specs=[pl.BlockSpec((tm, D_MODEL), row),
                  pl.BlockSpec((1, D_MODEL), const),
                  pl.BlockSpec((D_MODEL, IN_COLS), const),
                  pl.BlockSpec(lb_logits.shape, const),
                  pl.BlockSpec((tm, LANES), row_pos),
                  pl.BlockSpec((tm, LANES), row_pos),
                  pl.BlockSpec((tm, LANES), row_pos)],
        out_specs=[pl.BlockSpec((tm, HG_WIDTH), row)] * 8,
        out_shape=outs,
        compiler_params=_cparams(("parallel",)),
        name="inproj",
    )(x2d, g.reshape(1, D_MODEL), w_bf, lb_logits, cos, sa, sb)


HG_C = 128
HG_SB = 16


def _hgrn_kernel(q_ref, k_ref, v_ref, lf_ref, zg_ref, g_ref, hn_ref, sfin_ref, st_ref, *, n_chunks):
    t = pl.program_id(2)

    @pl.when(t == 0)
    def _init():
        st_ref[...] = jnp.zeros_like(st_ref)

    ri = lax.broadcasted_iota(jnp.int32, (HG_C, HG_C), 0)
    ci = lax.broadcasted_iota(jnp.int32, (HG_C, HG_C), 1)
    ltri = (ri >= ci).astype(BF16)
    ones_b = jnp.ones((LANES, LANES), BF16)
    n_sb = HG_C // HG_SB
    row_sb = lax.broadcasted_iota(jnp.int32, (n_sb, HG_SB, LANES), 1)
    col_sb = lax.broadcasted_iota(jnp.int32, (n_sb, HG_SB, HG_C), 2)
    lo_sb = lax.broadcasted_iota(jnp.int32, (n_sb, HG_SB, HG_C), 0) * HG_SB
    g = g_ref[...]

    def chunk(c, carry):
        r0 = pl.multiple_of(c * HG_C, HG_C)
        q = q_ref[pl.ds(r0, HG_C), :]
        k = k_ref[pl.ds(r0, HG_C), :]
        v = v_ref[pl.ds(r0, HG_C), :]
        lf = lf_ref[pl.ds(r0, HG_C), :]
        lf_hi = lf.astype(BF16)
        lf_r = lf - lf_hi.astype(F32)
        lf_mid = lf_r.astype(BF16)
        lf_lo = (lf_r - lf_mid.astype(F32)).astype(BF16)
        b = (jnp.dot(ltri, lf_hi, preferred_element_type=F32)
             + (jnp.dot(ltri, lf_mid, preferred_element_type=F32)
                + jnp.dot(ltri, lf_lo, preferred_element_type=F32)))
        st = st_ref[...]
        vb = v.astype(BF16)
        qb = (q * jnp.exp(b)).astype(BF16)
        o_inter = lax.dot_general(qb, st.astype(BF16), NT_DIMS, preferred_element_type=F32)
        b3, q3, v3 = (a.reshape(n_sb, HG_SB, LANES) for a in (b, q, v))
        bk = b - jnp.log(k)
        bk3 = bk.reshape(n_sb, HG_SB, LANES)
        ps, t_lo, offs = [], [], [0]
        for s in range(HG_SB):
            lo = (s // SUBLANES) * SUBLANES
            d = jnp.where(row_sb[:, lo:] >= s, b3[:, lo:] - bk3[:, s:s + 1, :], NEG)
            ps.append(q3[:, lo:] * jnp.exp(d))
            t_lo.append(lo)
            offs.append(offs[-1] + HG_SB - lo)
        n_rows = offs[-1]
        p_all = jnp.concatenate(ps, axis=1).reshape(n_sb * n_rows, LANES).astype(BF16)
        r_all = jnp.dot(p_all, ones_b, preferred_element_type=F32)
        r_all = r_all.reshape(n_sb, n_rows, LANES)
        o3 = o_inter.reshape(n_sb, HG_SB, LANES)
        tiles = [o3[:, j * SUBLANES:(j + 1) * SUBLANES] for j in range(HG_SB // SUBLANES)]
        for s in range(HG_SB):
            for j in range(t_lo[s] // SUBLANES, HG_SB // SUBLANES):
                r0_ = offs[s] + j * SUBLANES - t_lo[s]
                tiles[j] = tiles[j] + r_all[:, r0_:r0_ + SUBLANES, :] * v3[:, s:s + 1, :]
        o3 = jnp.concatenate(tiles, axis=1)
        b_ref = jnp.concatenate([b3[0:1, 0:1], b3[:n_sb - 1, HG_SB - 1:HG_SB]], axis=0)
        qs = (q3 * jnp.exp(jnp.minimum(b3 - b_ref, 0.0))).astype(BF16)
        ks = jnp.exp(jnp.minimum(b_ref - bk[None], 0.0)).astype(BF16)
        a = lax.dot_general(qs, ks, (((2,), (2,)), ((0,), (0,))), preferred_element_type=F32)
        a = jnp.where(col_sb < lo_sb, a, 0.0).astype(BF16).reshape(HG_C, HG_C)
        o = o3.reshape(HG_C, LANES) + jnp.dot(a, vb, preferred_element_type=F32)
        b_last = b[HG_C - 1:HG_C, :]
        kdec = jnp.exp(b_last - bk).astype(BF16)
        st_ref[...] = st * jnp.exp(b_last) + jnp.dot(v.T.astype(BF16), kdec, preferred_element_type=F32)
        ms = jnp.mean(o * o, axis=-1, keepdims=True)
        zg = zg_ref[pl.ds(r0, HG_C), :]
        hn_ref[pl.ds(r0, HG_C), :] = o * lax.rsqrt(ms + RMS_EPS) * g * (zg * _sigmoid(zg))
        return carry

    lax.fori_loop(0, n_chunks, chunk, 0, unroll=8)

    @pl.when(t == pl.num_programs(2) - 1)
    def _fin():
        sfin_ref[...] = st_ref[...].T


def _hgrn_prompt(hq, hk, hv, lf, zg, g_hg, tb=1024):
    bsz, t, _ = hq.shape
    seq = pl.BlockSpec((None, tb, HG_HEAD_DIM), lambda b, h, i: (b, i, h))
    return pl.pallas_call(
        functools.partial(_hgrn_kernel, n_chunks=tb // HG_C),
        grid=(bsz, HG_HEADS, t // tb),
        in_specs=[seq, seq, seq, seq, seq,
                  pl.BlockSpec((1, HG_HEAD_DIM), lambda b, h, i: (0, h))],
        out_specs=[seq,
                   pl.BlockSpec((None, None, HG_HEAD_DIM, HG_HEAD_DIM), lambda b, h, i: (b, h, 0, 0))],
        out_shape=[jax.ShapeDtypeStruct((bsz, t, HG_WIDTH), F32),
                   jax.ShapeDtypeStruct((bsz, HG_HEADS, HG_HEAD_DIM, HG_HEAD_DIM), F32)],
        scratch_shapes=[pltpu.VMEM((HG_HEAD_DIM, HG_HEAD_DIM), F32)],
        compiler_params=_cparams(("parallel", "parallel", "arbitrary")),
        name="hgrn_prompt",
    )(hq, hk, hv, lf, zg, g_hg.reshape(1, HG_WIDTH))


def _hgrn_step_kernel(q_ref, k_ref, v_ref, lf_ref, zg_ref, g_ref, s_ref, hn_ref, snew_ref):
    row = slice(None)
    zeros = jnp.zeros((HG_HEAD_DIM - 3, HG_HEAD_DIM), F32)
    for h in range(HG_HEADS):
        cs = slice(h * HG_HEAD_DIM, (h + 1) * HG_HEAD_DIM)
        q, k, v = q_ref[row, cs], k_ref[row, cs], v_ref[row, cs]
        f = jnp.exp(lf_ref[row, cs])
        cols = jnp.concatenate([f, k, q, zeros], axis=0).T
        s_new = cols[:, 0:1] * s_ref[h] + cols[:, 1:2] * v
        snew_ref[h] = s_new
        o = jnp.sum(cols[:, 2:3] * s_new, axis=0, keepdims=True)
        ms = jnp.mean(o * o, axis=-1, keepdims=True)
        zg = zg_ref[row, cs]
        hn_ref[row, cs] = o * lax.rsqrt(ms + RMS_EPS) * g_ref[:, cs] * (zg * _sigmoid(zg))


def _hgrn_step(hq, hk, hv, lf, zg, g_hg, state):
    bsz = hq.shape[0]
    one = pl.BlockSpec((None, 1, HG_WIDTH), lambda b: (b, 0, 0))
    st = pl.BlockSpec((None, HG_HEADS, HG_HEAD_DIM, HG_HEAD_DIM), lambda b: (b, 0, 0, 0))
    r3 = lambda a: a.reshape(bsz, 1, HG_WIDTH)
    hn, s_new = pl.pallas_call(
        _hgrn_step_kernel,
        grid=(bsz,),
        in_specs=[one, one, one, one, one, pl.BlockSpec((1, HG_WIDTH), lambda b: (0, 0)), st],
        out_specs=[one, st],
        out_shape=[jax.ShapeDtypeStruct((bsz, 1, HG_WIDTH), F32),
                   jax.ShapeDtypeStruct(state.shape, F32)],
        compiler_params=_cparams(("parallel",)),
        name="hgrn_step",
    )(r3(hq), r3(hk), r3(hv), r3(lf), r3(zg), g_hg.reshape(1, HG_WIDTH), state)
    return hn.reshape(bsz, HG_WIDTH), s_new


ATT_N = 128
ATT_SUPER = 2048
ATT_G = 8


def _attn_prompt_kernel(q_ref, k_ref, v_ref, o_ref, osc, lsc, *, seq_len):
    lane = lax.broadcasted_iota(jnp.int32, (ATT_N, LANES), 1)
    rowi = lax.broadcasted_iota(jnp.int32, (ATT_N, LANES), 0)
    head0 = lane < ATT_HEAD_DIM
    kidx = lax.broadcasted_iota(jnp.int32, (ATT_N, 2 * ATT_N), 1)
    qidx = lax.broadcasted_iota(jnp.int32, (ATT_N, 2 * ATT_N), 0)
    band = jnp.logical_and(kidx >= qidx, kidx <= qidx + ATT_N)
    in_prev = kidx < ATT_N
    scale = ATT_HEAD_DIM ** -0.5

    bqk = (((2,), (2,)), ((0,), (0,)))
    bkd = (((2,), (1,)), ((0,), (0,)))

    def do_group(p, d, base, g, gsz):
        span = ATT_N * d
        rows_g = gsz * ATT_N
        if d == 1:
            off = g * rows_g
            start = base + off

            def cur(ref):
                return ref[pl.ds(start, rows_g), :].reshape(gsz, ATT_N, LANES)

            def prv(ref, c):
                before = ref[pl.ds(jnp.maximum(start - ATT_N, 0), ATT_N), :].astype(BF16)
                return jnp.concatenate([before[None], c[:gsz - 1]], axis=0)

            bidx = lax.broadcasted_iota(jnp.int32, (gsz, ATT_N, 2 * ATT_N), 0)
            seen = jnp.logical_and(band, jnp.logical_not(
                jnp.logical_and(jnp.logical_and(bidx == 0, in_prev), start == 0)))
        else:
            per_blk = d // gsz
            off = (g // per_blk) * span + (g % per_blk) * gsz
            start = base + off
            prev = jnp.maximum(start - span, 0)

            def cur(ref):
                return jnp.stack([ref[pl.ds(start + r, ATT_N, stride=d), :] for r in range(gsz)])

            def prv(ref, c):
                return jnp.stack([ref[pl.ds(prev + r, ATT_N, stride=d), :] for r in range(gsz)]).astype(BF16)

            seen = jnp.logical_and(band, jnp.logical_not(jnp.logical_and(in_prev, start < span)))
        q = cur(q_ref) * scale
        kc = cur(k_ref).astype(BF16)
        vc = cur(v_ref).astype(BF16)
        kk = jnp.concatenate([prv(k_ref, kc), kc], axis=1)
        vv = jnp.concatenate([prv(v_ref, vc), vc], axis=1)
        res = []
        for hm in (head0, jnp.logical_not(head0)):
            qh = jnp.where(hm, q, 0.0).astype(BF16)
            s = lax.dot_general(qh, kk, bqk, preferred_element_type=F32)
            s = jnp.where(seen, s, NEG)
            m = jnp.max(s, axis=-1, keepdims=True)
            pr = jnp.exp(s - m)
            den = jnp.sum(pr, axis=-1, keepdims=True)
            o = lax.dot_general(pr.astype(BF16), vv, bkd, preferred_element_type=F32)
            res.append((o / den, m + jnp.log(den)))
        o = jnp.where(head0, res[0][0], res[1][0])
        lse = jnp.where(head0, res[0][1], res[1][1])
        if d == 1:
            osc[p, pl.ds(off, rows_g), :] = o.reshape(rows_g, LANES)
            lsc[p, pl.ds(off, rows_g), :] = lse.reshape(rows_g, LANES)
        else:
            for r in range(gsz):
                osc[p, pl.ds(off + r, ATT_N, stride=d), :] = o[r]
                lsc[p, pl.ds(off + r, ATT_N, stride=d), :] = lse[r]

    def superblock(sb, carry):
        base = sb * ATT_SUPER

        def groups(it, c2):
            for p, (w, d) in enumerate(DILATED_PATTERNS):
                gsz = ATT_G if d == 1 else min(ATT_G, d)
                for sub in range(ATT_G // gsz):
                    do_group(p, d, base, it * (ATT_G // gsz) + sub, gsz)
            return c2

        lax.fori_loop(0, ATT_SUPER // (ATT_G * ATT_N), groups, 0)

        piece = 256

        def merge(j, c2):
            r = pl.ds(pl.multiple_of(j * piece, piece), piece)
            ls = [lsc[p, r, :] for p in range(len(DILATED_PATTERNS))]
            mx = jnp.maximum(jnp.maximum(ls[0], ls[1]), ls[2])
            ws = [jnp.exp(l - mx) for l in ls]
            num = ws[0] * osc[0, r, :] + ws[1] * osc[1, r, :] + ws[2] * osc[2, r, :]
            o_ref[pl.ds(pl.multiple_of(base + j * piece, piece), piece), :] = num / (ws[0] + ws[1] + ws[2])
            return c2

        lax.fori_loop(0, ATT_SUPER // piece, merge, 0)
        return carry

    lax.fori_loop(0, seq_len // ATT_SUPER, superblock, 0)


def _attn_prompt(aq, ak, av):
    bsz, t, _ = aq.shape
    spec = pl.BlockSpec((None, t, LANES), lambda b, p: (b, 0, p))
    n_pat = len(DILATED_PATTERNS)
    return pl.pallas_call(
        functools.partial(_attn_prompt_kernel, seq_len=t),
        grid=(bsz, ATT_WIDTH // LANES),
        in_specs=[spec, spec, spec],
        out_specs=spec,
        out_shape=jax.ShapeDtypeStruct((bsz, t, ATT_WIDTH), F32),
        scratch_shapes=[pltpu.VMEM((n_pat, ATT_SUPER, LANES), F32),
                        pltpu.VMEM((n_pat, ATT_SUPER, LANES), F32)],
        compiler_params=_cparams(("parallel", "parallel")),
        name="attn_prompt",
    )(aq, ak, av)


def _attn_step_kernel(q_ref, kn_ref, vn_ref, ck_ref, cv_ref, o_ref, nk_ref, nv_ref):
    win = ck_ref.shape[-1]
    kt, vt = ck_ref[...], cv_ref[...]
    q = q_ref[...] * (ATT_HEAD_DIM ** -0.5)
    kn, vn = kn_ref[...], vn_ref[...]
    s_all = jnp.sum(kt * q, axis=1, keepdims=True)
    s_new = jnp.sum(kn * q, axis=1, keepdims=True)
    dist = win - lax.broadcasted_iota(jnp.int32, (1, 1, win), 2)

    ps, pnews, lses = [], [], []
    for w, d in DILATED_PATTERNS:
        on_stride = (dist & (d - 1)) == 0 if d & (d - 1) == 0 else dist % d == 0
        valid = jnp.logical_and(dist <= w, on_stride)
        sm = jnp.where(valid, s_all, NEG)
        m = jnp.maximum(jnp.max(sm, axis=-1, keepdims=True), s_new)
        p = jnp.exp(sm - m)
        pn = jnp.exp(s_new - m)
        den = jnp.sum(p, axis=-1, keepdims=True) + pn
        ps.append(p / den)
        pnews.append(pn / den)
        lses.append(m + jnp.log(den))
    mx = jnp.maximum(jnp.maximum(lses[0], lses[1]), lses[2])
    ws = [jnp.exp(l - mx) for l in lses]
    wsum = ws[0] + ws[1] + ws[2]
    p_tot = (ws[0] * ps[0] + ws[1] * ps[1] + ws[2] * ps[2]) / wsum
    pn_tot = (ws[0] * pnews[0] + ws[1] * pnews[1] + ws[2] * pnews[2]) / wsum
    o_ref[...] = jnp.sum(vt * p_tot, axis=-1, keepdims=True) + pn_tot * vn

    last = lax.broadcasted_iota(jnp.int32, (1, 1, win), 2) == win - 1
    nk_ref[...] = jnp.where(last, kn, pltpu.roll(kt, win - 1, 2))
    nv_ref[...] = jnp.where(last, vn, pltpu.roll(vt, win - 1, 2))


def _attn_step(aq, ak, av, cache_k, cache_v):
    bsz, _, _, win = cache_k.shape
    one = pl.BlockSpec((None, ATT_HEADS, ATT_HEAD_DIM, 1), lambda b: (b, 0, 0, 0))
    cache = pl.BlockSpec((None, ATT_HEADS, ATT_HEAD_DIM, win), lambda b: (b, 0, 0, 0))
    col = lambda a: a.reshape(bsz, ATT_HEADS, ATT_HEAD_DIM, 1)
    att, new_k, new_v = pl.pallas_call(
        _attn_step_kernel,
        grid=(bsz,),
        in_specs=[one, one, one, cache, cache],
        out_specs=[one, cache, cache],
        out_shape=[jax.ShapeDtypeStruct((bsz, ATT_HEADS, ATT_HEAD_DIM, 1), F32),
                   jax.ShapeDtypeStruct(cache_k.shape, F32),
                   jax.ShapeDtypeStruct(cache_v.shape, F32)],
        compiler_params=_cparams(("parallel",)),
        name="attn_step",
    )(col(aq), col(ak), col(av), cache_k, cache_v)
    return att.reshape(bsz, ATT_WIDTH), new_k, new_v


ROUTE_TM = 256
SUBLANES = 8
ROW_TILE = D_MODEL // LANES
assert ROW_TILE == SUBLANES


def _store_row_tiles(ref, val, lead=()):
    n = val.shape[0]
    for j in range(ROW_TILE):
        ref[lead + (pl.ds(j, n, stride=ROW_TILE), slice(None))] = val[:, j * LANES:(j + 1) * LANES]


def _load_row_tiles(ref, n, lead=()):
    return jnp.concatenate([ref[lead + (pl.ds(j, n, stride=ROW_TILE), slice(None))] for j in range(ROW_TILE)],
                           axis=1)


def _row_tile(ref, r, n=1, lead=()):
    start = r * ROW_TILE if isinstance(r, int) else pl.multiple_of(r * ROW_TILE, ROW_TILE)
    return ref.at[lead + (pl.ds(start, n * ROW_TILE),)]


def _outproj_kernel(hn_ref, att_ref, x_ref, hn_s_ref, att_s_ref, x_s_ref, w_ref, g_ref, wr_ref,
                    xmid_ref, h2_ref, route_ref, cnt_ref, *, n_sample):
    is_prompt = pl.program_id(0) < pl.num_programs(0) - 1
    hn = jnp.where(is_prompt, hn_ref[...], hn_s_ref[...])
    att = jnp.where(is_prompt, att_ref[...], att_s_ref[...])
    y = (jnp.dot(hn.astype(BF16), w_ref[0:HG_WIDTH, :], preferred_element_type=F32)
         + jnp.dot(att.astype(BF16), w_ref[HG_WIDTH:, :], preferred_element_type=F32))
    xm = jnp.where(is_prompt, x_ref[...], x_s_ref[...]) + y
    xmid_ref[...] = xm
    ms = jnp.mean(xm * xm, axis=-1, keepdims=True)
    h2 = xm * lax.rsqrt(ms + RMS_EPS) * g_ref[...]
    _store_row_tiles(h2_ref, h2)
    h2_hi = h2.astype(BF16)
    h2_lo = (h2 - h2_hi.astype(F32)).astype(BF16)
    lg = (jnp.dot(h2_hi, wr_ref[0], preferred_element_type=F32)
          + (jnp.dot(h2_lo, wr_ref[0], preferred_element_type=F32)
             + jnp.dot(h2_hi, wr_ref[1], preferred_element_type=F32)))
    lane = lax.broadcasted_iota(jnp.int32, lg.shape, 1).astype(F32)
    big = float(LANES)
    gmask = lane < N_GROUPS
    lgg = jnp.where(gmask, lg, NEG)
    mg = jnp.max(lgg, axis=-1, keepdims=True)
    gi = jnp.min(jnp.where(lgg == mg, lane, big), axis=-1, keepdims=True)
    p_grp = 1.0 / jnp.sum(jnp.exp(lgg - mg), axis=-1, keepdims=True)
    lo = N_GROUPS + gi * EXPERTS_PER_GROUP
    emask = jnp.logical_and(lane >= lo, lane < lo + EXPERTS_PER_GROUP)
    le1 = jnp.where(emask, lg, NEG)
    m1 = jnp.max(le1, axis=-1, keepdims=True)
    i1 = jnp.min(jnp.where(le1 == m1, lane, big), axis=-1, keepdims=True)
    le2 = jnp.where(lane == i1, NEG, le1)
    m2 = jnp.max(le2, axis=-1, keepdims=True)
    i2 = jnp.min(jnp.where(le2 == m2, lane, big), axis=-1, keepdims=True)
    r = jnp.exp(m2 - m1)
    g1 = p_grp / (1.0 + r)
    g2 = p_grp * r / (1.0 + r)
    e1, e2 = i1 - N_GROUPS, i2 - N_GROUPS
    route_ref[...] = jnp.where(lane == 0, e1,
                               jnp.where(lane == 1, e2,
                                         jnp.where(lane == 2, g1, jnp.where(lane == 3, g2, 0.0))))

    @pl.when(pl.program_id(0) == 0)
    def _zero():
        cnt_ref[...] = jnp.zeros_like(cnt_ref)

    rows = lax.broadcasted_iota(jnp.int32, lg.shape, 0)
    real = jnp.logical_or(is_prompt, rows < n_sample)
    hit = jnp.logical_and(real, jnp.logical_or(lane == e1, lane == e2))
    cnt_ref[...] += jnp.sum(jnp.where(hit, 1.0, 0.0), axis=0, keepdims=True)


def _outproj(hn_p, att_p, x_p, hn_s, att_s, x_s, w_out_bf, g_ffn, w_router):
    tm = ROUTE_TM
    n_p = x_p.shape[0]
    n_tiles = n_p // tm + 1
    pad = lambda a: jnp.pad(a, ((0, tm - a.shape[0]), (0, 0)))
    row_p = lambda i: (jnp.minimum(i, n_tiles - 2), 0)
    row = lambda i: (i, 0)
    const = lambda i: (0, 0)
    n_rows = n_tiles * tm
    return pl.pallas_call(
        functools.partial(_outproj_kernel, n_sample=x_s.shape[0]),
        grid=(n_tiles,),
        in_specs=[pl.BlockSpec((tm, HG_WIDTH), row_p),
                  pl.BlockSpec((tm, ATT_WIDTH), row_p),
                  pl.BlockSpec((tm, D_MODEL), row_p),
                  pl.BlockSpec((tm, HG_WIDTH), const),
                  pl.BlockSpec((tm, ATT_WIDTH), const),
                  pl.BlockSpec((tm, D_MODEL), const),
                  pl.BlockSpec((D_MODEL, D_MODEL), const),
                  pl.BlockSpec((1, D_MODEL), const),
                  pl.BlockSpec((2, D_MODEL, LANES), lambda i: (0, 0, 0))],
        out_specs=[pl.BlockSpec((tm, D_MODEL), row),
                   pl.BlockSpec((tm * ROW_TILE, LANES), row),
                   pl.BlockSpec((tm, LANES), row),
                   pl.BlockSpec((SUBLANES, LANES), const)],
        out_shape=[jax.ShapeDtypeStruct((n_rows, D_MODEL), F32),
                   jax.ShapeDtypeStruct((n_rows * ROW_TILE, LANES), F32),
                   jax.ShapeDtypeStruct((n_rows, LANES), F32),
                   jax.ShapeDtypeStruct((SUBLANES, LANES), F32)],
        compiler_params=_cparams(("arbitrary",)),
        name="outproj",
    )(hn_p, att_p, x_p, pad(hn_s), pad(att_s), pad(x_s), w_out_bf, g_ffn.reshape(1, D_MODEL), w_router)


def _rank_kernel(route_ref, cnt_ref, slot_ref, meta_ref, base_ref, *, n_tok, blk):
    i = pl.program_id(0)
    tm = route_ref.shape[0]
    lane = lax.broadcasted_iota(jnp.int32, (tm, LANES), 1).astype(F32)
    rowg = i * tm + lax.broadcasted_iota(jnp.int32, (tm, LANES), 0)
    valid = rowg < n_tok
    r = route_ref[...]
    oh0 = jnp.where(jnp.logical_and(valid, lane == r[:, 0:1]), 1.0, 0.0)
    oh1 = jnp.where(jnp.logical_and(valid, lane == r[:, 1:2]), 1.0, 0.0)
    oh = oh0 + oh1

    @pl.when(i == 0)
    def _starts():
        cnt = cnt_ref[0:1, :].astype(jnp.int32)
        shift = blk.bit_length() - 1
        padded = (((cnt + (blk - 1)) >> shift) << shift).astype(F32)
        up = (lax.broadcasted_iota(jnp.int32, (LANES, LANES), 0)
              < lax.broadcasted_iota(jnp.int32, (LANES, LANES), 1)).astype(F32)
        start = jnp.dot(jnp.broadcast_to(padded, (8, LANES)), up, precision=HIGHEST,
                        preferred_element_type=F32)[0:1]
        base_ref[...] = start
        nb = meta_ref.shape[0]
        lane_b = lax.broadcasted_iota(jnp.int32, (nb, LANES), 1)
        blk_start = (lax.broadcasted_iota(jnp.int32, (nb, LANES), 0) * blk).astype(F32)
        ended = jnp.logical_and(start + padded <= blk_start, lane_b < N_EXPERTS)
        be = jnp.minimum(jnp.sum(jnp.where(ended, 1.0, 0.0), axis=-1, keepdims=True), N_EXPERTS - 1.0)
        n_used = jnp.sum(padded, axis=-1, keepdims=True) * (1.0 / blk)
        meta_ref[...] = jnp.where(lane_b == 0, be, jnp.where(lane_b == 1, n_used, 0.0)).astype(jnp.int32)

    before = (lax.broadcasted_iota(jnp.int32, (tm, tm), 1)
              < lax.broadcasted_iota(jnp.int32, (tm, tm), 0)).astype(BF16)
    pre = jnp.dot(before, oh.astype(BF16), preferred_element_type=F32) + base_ref[...]
    s0 = jnp.sum(oh0 * pre, axis=-1, keepdims=True)
    s1 = jnp.sum(oh1 * pre, axis=-1, keepdims=True)
    slot_ref[...] = jnp.where(lane == 0, s0, jnp.where(lane == 1, s1, 0.0)).astype(jnp.int32)
    base_ref[...] += jnp.sum(oh, axis=0, keepdims=True)


def _rank(route, counts, n_tok, blk, nblk):
    n_rows = route.shape[0]
    nb = (nblk + 7) // 8 * 8
    slots, meta = pl.pallas_call(
        functools.partial(_rank_kernel, n_tok=n_tok, blk=blk),
        grid=(n_rows // ROUTE_TM,),
        in_specs=[pl.BlockSpec((ROUTE_TM, LANES), lambda i: (i, 0)),
                  pl.BlockSpec(counts.shape, lambda i: (0, 0))],
        out_specs=[pl.BlockSpec((ROUTE_TM, LANES), lambda i: (i, 0)),
                   pl.BlockSpec((nb, LANES), lambda i: (0, 0))],
        out_shape=[jax.ShapeDtypeStruct((n_rows, LANES), jnp.int32),
                   jax.ShapeDtypeStruct((nb, LANES), jnp.int32)],
        scratch_shapes=[pltpu.VMEM((1, LANES), F32)],
        compiler_params=_cparams(("arbitrary",)),
        name="rank",
    )(route, counts)
    return slots[:n_tok, :TOP_K_INNER].reshape(-1), meta[:nblk, 0], meta[0:1, 1]


DMA_UNROLL = 8


def _dispatch_kernel(slot_ref, be_ref, nu_ref, h2_ref, xb_out, zbuf, sem, zsem, *, n_tok):
    i = pl.program_id(0)
    tm = h2_ref.shape[0] // ROW_TILE
    tail = n_tok % tm

    @pl.when(i == 0)
    def _zero_padding():
        blk = zbuf.shape[0] // ROW_TILE
        n_blocks = be_ref.shape[0]
        zbuf[...] = jnp.zeros_like(zbuf)

        def ends_expert(j):
            nxt = be_ref[jnp.minimum(j + 1, n_blocks - 1)]
            return jnp.logical_or(j >= nu_ref[0] - 1, be_ref[j] != nxt)

        def issue(j, c):
            @pl.when(ends_expert(j))
            def _():
                pltpu.make_async_copy(zbuf, _row_tile(xb_out, j * blk, blk), zsem).start()
            return c

        def drain(j, c):
            @pl.when(ends_expert(j))
            def _():
                pltpu.make_async_copy(zbuf, _row_tile(xb_out, 0, blk), zsem).wait()
            return c

        lax.fori_loop(0, n_blocks, issue, 0)
        lax.fori_loop(0, n_blocks, drain, 0)

    def push(rows):
        def body(r, c):
            a = (i * tm + r) * TOP_K_INNER
            for k in range(TOP_K_INNER):
                pltpu.make_async_copy(_row_tile(h2_ref, r), _row_tile(xb_out, slot_ref[a + k]), sem).start(
                    priority=k % 2)
            return c
        lax.fori_loop(0, rows, body, 0, unroll=DMA_UNROLL)
        for k in range(TOP_K_INNER):
            pltpu.make_async_copy(_row_tile(h2_ref, 0, rows), _row_tile(xb_out, 0, rows), sem).wait()

    last = pl.num_programs(0) - 1
    if tail == 0:
        push(tm)
    else:
        @pl.when(i < last)
        def _full():
            push(tm)

        @pl.when(i == last)
        def _tail():
            push(tail)


def _dispatch(slot_flat, blk_expert, n_used, h2, n_tok, blk):
    tm = ROUTE_TM
    n_slots = blk_expert.shape[0] * blk
    grid_spec = pltpu.PrefetchScalarGridSpec(
        num_scalar_prefetch=3,
        grid=(h2.shape[0] // (tm * ROW_TILE),),
        in_specs=[pl.BlockSpec((tm * ROW_TILE, LANES), lambda i, s, be, nu: (i, 0))],
        out_specs=pl.BlockSpec(memory_space=pl.ANY),
        scratch_shapes=[pltpu.VMEM((blk * ROW_TILE, LANES), F32),
                        pltpu.SemaphoreType.DMA(()),
                        pltpu.SemaphoreType.DMA(())],
    )
    return pl.pallas_call(
        functools.partial(_dispatch_kernel, n_tok=n_tok),
        grid_spec=grid_spec,
        out_shape=jax.ShapeDtypeStruct((n_slots * ROW_TILE, LANES), F32),
        compiler_params=_cparams(("arbitrary",)),
        name="dispatch",
    )(slot_flat, blk_expert, n_used, h2)


def _expert_kernel(be_ref, nu_ref, x_ref, wg_hbm, wu_hbm, wd_hbm, y_ref,
                   wg_f, wu_f, wd_f, sem, wgb, wub, wdb, cur_ref):
    i = pl.program_id(0)
    n_used = nu_ref[0]
    e = be_ref[i]
    e_prev = be_ref[jnp.maximum(i - 1, 0)]

    def fetch(ex, s):
        return [pltpu.make_async_copy(hbm.at[ex], buf.at[s], sem.at[s])
                for hbm, buf in ((wg_hbm, wg_f), (wu_hbm, wu_f), (wd_hbm, wd_f))]

    @pl.when(i == 0)
    def _first():
        cur_ref[0] = 0
        for c in fetch(e, 0):
            c.start()

    @pl.when(jnp.logical_and(jnp.logical_or(i == 0, e != e_prev), i < n_used))
    def _new_expert():
        s = cur_ref[0]
        j = lax.while_loop(lambda j: jnp.logical_and(j < n_used, be_ref[jnp.minimum(j, n_used - 1)] == e),
                           lambda j: j + 1, i + 1)

        @pl.when(j < n_used)
        def _prefetch():
            for c in fetch(be_ref[j], 1 - s):
                c.start()

        for c in fetch(e, s):
            c.wait()
        wgb[...] = wg_f[s].astype(BF16)
        wub[...] = wu_f[s].astype(BF16)
        wdb[...] = wd_f[s].astype(BF16)
        cur_ref[0] = 1 - s

    @pl.when(i < n_used)
    def _run():
        x = _load_row_tiles(x_ref, x_ref.shape[0] // ROW_TILE).astype(BF16)
        a = jnp.dot(x, wgb[...], preferred_element_type=F32)
        u = jnp.dot(x, wub[...], preferred_element_type=F32)
        mid = (a * _sigmoid(a) * u).astype(BF16)
        _store_row_tiles(y_ref, jnp.dot(mid, wdb[...], preferred_element_type=F32))

    @pl.when(i >= nu_ref[0])
    def _skip():
        y_ref[...] = jnp.zeros_like(y_ref)


def _experts(xb, blk_expert, n_used, w_g, w_u, w_d, blk):
    nblk = blk_expert.shape[0]
    hbm = pl.BlockSpec(memory_space=pl.ANY)
    grid_spec = pltpu.PrefetchScalarGridSpec(
        num_scalar_prefetch=2,
        grid=(nblk,),
        in_specs=[pl.BlockSpec((blk * ROW_TILE, LANES), lambda i, be, nu: (jnp.minimum(i, nu[0] - 1), 0)),
                  hbm, hbm, hbm],
        out_specs=pl.BlockSpec((blk * ROW_TILE, LANES), lambda i, be, nu: (i, 0)),
        scratch_shapes=[pltpu.VMEM((2, D_MODEL, D_FF_EXPERT), F32),
                        pltpu.VMEM((2, D_MODEL, D_FF_EXPERT), F32),
                        pltpu.VMEM((2, D_FF_EXPERT, D_MODEL), F32),
                        pltpu.SemaphoreType.DMA((2,)),
                        pltpu.VMEM((D_MODEL, D_FF_EXPERT), BF16),
                        pltpu.VMEM((D_MODEL, D_FF_EXPERT), BF16),
                        pltpu.VMEM((D_FF_EXPERT, D_MODEL), BF16),
                        pltpu.SMEM((1,), jnp.int32)],
    )
    return pl.pallas_call(
        _expert_kernel,
        grid_spec=grid_spec,
        out_shape=jax.ShapeDtypeStruct((nblk * blk * ROW_TILE, LANES), F32),
        compiler_params=_cparams(("arbitrary",)),
        name="experts",
    )(blk_expert, n_used, xb, w_g, w_u, w_d)


def _final_kernel(slot_ref, x_ref, route_ref, g_ref, yb_hbm, o_ref, ybuf, sem, *, tok0):
    i = pl.program_id(0)
    tm = x_ref.shape[0]

    def gather(j, buf):
        def body(r, c):
            a = (tok0 + j * tm + r) * TOP_K_INNER
            for k in range(TOP_K_INNER):
                pltpu.make_async_copy(_row_tile(yb_hbm, slot_ref[a + k]), _row_tile(ybuf, r, lead=(buf, k)),
                                      sem.at[buf]).start(priority=k % 2)
            return c
        lax.fori_loop(0, tm, body, 0, unroll=DMA_UNROLL)

    @pl.when(i == 0)
    def _first():
        gather(0, 0)

    @pl.when(i + 1 < pl.num_programs(0))
    def _next():
        gather(i + 1, (i + 1) % 2)

    buf = i % 2
    for k in range(TOP_K_INNER):
        pltpu.make_async_copy(_row_tile(yb_hbm, 0, tm), ybuf.at[buf, k], sem.at[buf]).wait()
    route = route_ref[...]
    y0 = _load_row_tiles(ybuf, tm, lead=(buf, 0))
    y1 = _load_row_tiles(ybuf, tm, lead=(buf, 1))
    x = x_ref[...] + (y0 * route[:, 2:3] + y1 * route[:, 3:4])
    ms = jnp.mean(x * x, axis=-1, keepdims=True)
    o_ref[...] = x * lax.rsqrt(ms + RMS_EPS) * g_ref[...]


def _final(slot_flat, xmid, route, g_final, yb, tok0, n_out, tm):
    blk0 = tok0 // tm
    grid_spec = pltpu.PrefetchScalarGridSpec(
        num_scalar_prefetch=1,
        grid=(n_out // tm,),
        in_specs=[pl.BlockSpec((tm, D_MODEL), lambda i, s: (i + blk0, 0)),
                  pl.BlockSpec((tm, LANES), lambda i, s: (i + blk0, 0)),
                  pl.BlockSpec((1, D_MODEL), lambda i, s: (0, 0)),
                  pl.BlockSpec(memory_space=pl.ANY)],
        out_specs=pl.BlockSpec((tm, D_MODEL), lambda i, s: (i, 0)),
        scratch_shapes=[pltpu.VMEM((2, TOP_K_INNER, tm * ROW_TILE, LANES), F32),
                        pltpu.SemaphoreType.DMA((2,))],
    )
    return pl.pallas_call(
        functools.partial(_final_kernel, tok0=tok0),
        grid_spec=grid_spec,
        out_shape=jax.ShapeDtypeStruct((n_out, D_MODEL), F32),
        compiler_params=_cparams(("arbitrary",)),
        name="final",
    )(slot_flat, xmid, route, g_final.reshape(1, D_MODEL), yb)


def kernel(x_prompt, x_sample, cache_attn_k, cache_attn_v, state_hgrn, w_in, w_out, hg_lb_logits,
           hg_norm_g, norm_mix_g, norm_ffn_g, norm_final_g, w_route_group, w_route_expert,
           w_expert_gate, w_expert_up, w_expert_down):
    bp, tp, _ = x_prompt.shape
    bs = x_sample.shape[0]
    l = 0
    w_in_bf = w_in[l].astype(BF16)
    w_out_bf = w_out[l].astype(BF16)
    w_router = jnp.concatenate(
        [w_route_group[l],
         jnp.transpose(w_route_expert[l], (1, 0, 2)).reshape(D_MODEL, N_EXPERTS),
         jnp.zeros((D_MODEL, LANES - N_GROUPS - N_EXPERTS), F32)], axis=-1)
    w_router_hi = w_router.astype(BF16)
    w_router = jnp.stack([w_router_hi, (w_router - w_router_hi.astype(F32)).astype(BF16)])

    n_p = bp * tp
    xp = x_prompt.reshape(n_p, D_MODEL)
    pos_p = jnp.arange(tp, dtype=jnp.int32)
    hq, hk, hv, lf, zg, aq, ak, av = _inproj(xp, norm_mix_g[l], w_in_bf, hg_lb_logits, pos_p, 256)
    seq3 = lambda a: a.reshape(bp, tp, HG_WIDTH)
    hn_p, s_fin = _hgrn_prompt(seq3(hq), seq3(hk), seq3(hv), seq3(lf), seq3(zg), hg_norm_g[l])
    att_p = _attn_prompt(seq3(aq), seq3(ak), seq3(av))
    keep = min(MAX_WINDOW, tp)
    heads = lambda a: a.reshape(1, bp, keep, ATT_HEADS, ATT_HEAD_DIM)
    new_k_p = heads(seq3(ak)[:, tp - keep:])
    new_v_p = heads(seq3(av)[:, tp - keep:])

    xs = x_sample.reshape(bs, D_MODEL)
    pos_s = jnp.full((bs,), PAST_LEN, jnp.int32)
    hq, hk, hv, lf, zg, aq, ak, av = _inproj(xs, norm_mix_g[l], w_in_bf, hg_lb_logits, pos_s, bs)
    hn_s, s_new = _hgrn_step(hq, hk, hv, lf, zg, hg_norm_g[l], state_hgrn[l])
    feat = lambda a: jnp.transpose(a, (0, 2, 3, 1))
    att_s, new_k_s, new_v_s = _attn_step(aq, ak, av, feat(cache_attn_k[l]), feat(cache_attn_v[l]))
    cache5 = lambda a: jnp.transpose(a, (0, 3, 1, 2))[None]

    assert n_p % ROUTE_TM == 0 and bs <= ROUTE_TM
    n_tok = n_p + bs
    xmid, h2, route, counts = _outproj(hn_p.reshape(n_p, HG_WIDTH), att_p.reshape(n_p, ATT_WIDTH), xp,
                                       hn_s, att_s, xs, w_out_bf, norm_ffn_g[l], w_router)
    blk = MOE_BLOCK
    nblk = (n_tok * TOP_K_INNER + N_EXPERTS * (blk - 1)) // blk + 1
    slot_flat, blk_expert, n_used = _rank(route, counts, n_tok, blk, nblk)
    xb = _dispatch(slot_flat, blk_expert, n_used, h2, n_tok, blk)
    yb = _experts(xb, blk_expert, n_used, w_expert_gate[l], w_expert_up[l], w_expert_down[l], blk)
    y_prompt = _final(slot_flat, xmid, route, norm_final_g, yb, 0, n_p, 256)
    y_sample = _final(slot_flat, xmid, route, norm_final_g, yb, n_p, bs, bs)

    return (y_prompt.reshape(bp, tp, D_MODEL), y_sample.reshape(bs, 1, D_MODEL),
            new_k_p, new_v_p, s_fin[None], cache5(new_k_s), cache5(new_v_s), s_new[None])
```

```python
import functools

import jax
import jax.numpy as jnp
from jax import lax
from jax.experimental import pallas as pl
from jax.experimental.pallas import tpu as pltpu

F32 = jnp.float32
BF16 = jnp.bfloat16

D_MODEL = 1024
HG_WIDTH = 512
HG_HEAD_DIM = 128
HG_HEADS = 4
ATT_WIDTH = 512
ATT_HEAD_DIM = 64
ATT_HEADS = 8
ROPE_DIM = 16
ROPE_THETA = 500000.0
DILATED_PATTERNS = ((128, 1), (512, 4), (2048, 16))
MAX_WINDOW = 2048
PAST_LEN = 16384
N_GROUPS = 8
EXPERTS_PER_GROUP = 8
N_EXPERTS = 64
TOP_K_INNER = 2
D_FF_EXPERT = 512
MOE_BLOCK = 256
IN_COLS = 4 * HG_WIDTH + 3 * ATT_WIDTH
RMS_EPS = 1e-6

LANES = 128
VMEM_LIMIT = 56 * 1024 * 1024
NEG = -1e30
HIGHEST = lax.Precision.HIGHEST
NT_DIMS = (((1,), (1,)), ((), ()))


def _sigmoid(z):
    return 1.0 / (1.0 + jnp.exp(-z))


def _cparams(sem):
    return pltpu.CompilerParams(dimension_semantics=sem, vmem_limit_bytes=VMEM_LIMIT)


def _inproj_kernel(x_ref, g_ref, w_ref, lbl_ref, cos_ref, sa_ref, sb_ref,
                   hq_ref, hk_ref, hv_ref, lf_ref, zg_ref, aq_ref, ak_ref, av_ref):
    x = x_ref[...]
    ms = jnp.mean(x * x, axis=-1, keepdims=True)
    h = (x * lax.rsqrt(ms + RMS_EPS) * g_ref[...]).astype(BF16)

    def mm(c0):
        return jnp.dot(h, w_ref[:, c0:c0 + HG_WIDTH], preferred_element_type=F32)

    lbl = lbl_ref[...]
    le = jnp.exp(lbl - jnp.max(lbl, axis=0, keepdims=True))
    lb = le[0:1, :] / jnp.sum(le, axis=0, keepdims=True)

    zq = mm(0)
    hq_ref[...] = zq * _sigmoid(zq)
    zf = mm(HG_WIDTH)
    f = lb + (1.0 - lb) * _sigmoid(zf)
    hk_ref[...] = 1.0 - f
    lf_ref[...] = jnp.log(f)
    hv_ref[...] = mm(2 * HG_WIDTH)
    zg_ref[...] = mm(3 * HG_WIDTH)

    cos, sa, sb = cos_ref[...], sa_ref[...], sb_ref[...]

    def rope(a, out_ref):
        for j in range(ATT_WIDTH // LANES):
            xj = a[:, j * LANES:(j + 1) * LANES]
            up = pltpu.roll(xj, LANES - ROPE_DIM // 2, 1)
            dn = pltpu.roll(xj, ROPE_DIM // 2, 1)
            out_ref[:, j * LANES:(j + 1) * LANES] = xj * cos + up * sa + dn * sb

    rope(mm(4 * HG_WIDTH), aq_ref)
    rope(mm(4 * HG_WIDTH + ATT_WIDTH), ak_ref)
    av_ref[...] = mm(4 * HG_WIDTH + 2 * ATT_WIDTH)


def _rope_tables(pos):
    half = ROPE_DIM // 2
    c = jnp.arange(LANES) % ATT_HEAD_DIM
    inv_freq = ROPE_THETA ** (-(c % half).astype(F32) / half)
    ang = pos.astype(F32)[:, None] * inv_freq[None, :]
    cos, sin = jnp.cos(ang), jnp.sin(ang)
    return (jnp.where(c < ROPE_DIM, cos, 1.0),
            jnp.where(c < half, -sin, 0.0),
            jnp.where(jnp.logical_and(c >= half, c < ROPE_DIM), sin, 0.0))


def _inproj(x2d, g, w_bf, lb_logits, pos, tm):
    m = x2d.shape[0]
    cos, sa, sb = _rope_tables(pos)
    row = lambda i: (i, 0)
    seq_tiles = pos.shape[0] // tm
    row_pos = lambda i: (i % seq_tiles, 0)
    const = lambda i: (0, 0)
    outs = [jax.ShapeDtypeStruct((m, HG_WIDTH), F32)] * 8
    return pl.pallas_call(
        _inproj_kernel,
        grid=(m // tm,),
        in_specs=[pl.BlockSpec((tm, D_MODEL), row),
                  pl.BlockSpec((1, D_MODEL), const),
                  pl.BlockSpec((D_MODEL, IN_COLS), const),
                  pl.BlockSpec(lb_logits.shape, const),
                  pl.BlockSpec((tm, LANES), row_pos),
                  pl.BlockSpec((tm, LANES), row_pos),
                  pl.BlockSpec((tm, LANES), row_pos)],
        out_specs=[pl.BlockSpec((tm, HG_WIDTH), row)] * 8,
        out_shape=outs,
        compiler_params=_cparams(("parallel",)),
        name="inproj",
    )(x2d, g.reshape(1, D_MODEL), w_bf, lb_logits, cos, sa, sb)


HG_C = 128
HG_SB = 16


def _hgrn_kernel(q_ref, k_ref, v_ref, lf_ref, zg_ref, g_ref, hn_ref, sfin_ref, st_ref, *, n_chunks):
    t = pl.program_id(2)

    @pl.when(t == 0)
    def _init():
        st_ref[...] = jnp.zeros_like(st_ref)

    ri = lax.broadcasted_iota(jnp.int32, (HG_C, HG_C), 0)
    ci = lax.broadcasted_iota(jnp.int32, (HG_C, HG_C), 1)
    ltri = (ri >= ci).astype(BF16)
    ones_b = jnp.ones((LANES, LANES), BF16)
    n_sb = HG_C // HG_SB
    row_sb = lax.broadcasted_iota(jnp.int32, (n_sb, HG_SB, LANES), 1)
    col_sb = lax.broadcasted_iota(jnp.int32, (n_sb, HG_SB, HG_C), 2)
    lo_sb = lax.broadcasted_iota(jnp.int32, (n_sb, HG_SB, HG_C), 0) * HG_SB
    g = g_ref[...]

    def chunk(c, carry):
        r0 = pl.multiple_of(c * HG_C, HG_C)
        q = q_ref[pl.ds(r0, HG_C), :]
        k = k_ref[pl.ds(r0, HG_C), :]
        v = v_ref[pl.ds(r0, HG_C), :]
        lf = lf_ref[pl.ds(r0, HG_C), :]
        lf_hi = lf.astype(BF16)
        lf_r = lf - lf_hi.astype(F32)
        lf_mid = lf_r.astype(BF16)
        lf_lo = (lf_r - lf_mid.astype(F32)).astype(BF16)
        b = (jnp.dot(ltri, lf_hi, preferred_element_type=F32)
             + (jnp.dot(ltri, lf_mid, preferred_element_type=F32)
                + jnp.dot(ltri, lf_lo, preferred_element_type=F32)))
        st = st_ref[...]
        vb = v.astype(BF16)
        qb = (q * jnp.exp(b)).astype(BF16)
        o_inter = lax.dot_general(qb, st.astype(BF16), NT_DIMS, preferred_element_type=F32)
        b3, q3, v3 = (a.reshape(n_sb, HG_SB, LANES) for a in (b, q, v))
        bk = b - jnp.log(k)
        bk3 = bk.reshape(n_sb, HG_SB, LANES)
        ps, t_lo, offs = [], [], [0]
        for s in range(HG_SB):
            lo = (s // SUBLANES) * SUBLANES
            d = jnp.where(row_sb[:, lo:] >= s, b3[:, lo:] - bk3[:, s:s + 1, :], NEG)
            ps.append(q3[:, lo:] * jnp.exp(d))
            t_lo.append(lo)
            offs.append(offs[-1] + HG_SB - lo)
        n_rows = offs[-1]
        p_all = jnp.concatenate(ps, axis=1).reshape(n_sb * n_rows, LANES).astype(BF16)
        r_all = jnp.dot(p_all, ones_b, preferred_element_type=F32)
        r_all = r_all.reshape(n_sb, n_rows, LANES)
        o3 = o_inter.reshape(n_sb, HG_SB, LANES)
        tiles = [o3[:, j * SUBLANES:(j + 1) * SUBLANES] for j in range(HG_SB // SUBLANES)]
        for s in range(HG_SB):
            for j in range(t_lo[s] // SUBLANES, HG_SB // SUBLANES):
                r0_ = offs[s] + j * SUBLANES - t_lo[s]
                tiles[j] = tiles[j] + r_all[:, r0_:r0_ + SUBLANES, :] * v3[:, s:s + 1, :]
        o3 = jnp.concatenate(tiles, axis=1)
        b_ref = jnp.concatenate([b3[0:1, 0:1], b3[:n_sb - 1, HG_SB - 1:HG_SB]], axis=0)
        qs = (q3 * jnp.exp(jnp.minimum(b3 - b_ref, 0.0))).astype(BF16)
        ks = jnp.exp(jnp.minimum(b_ref - bk[None], 0.0)).astype(BF16)
        a = lax.dot_general(qs, ks, (((2,), (2,)), ((0,), (0,))), preferred_element_type=F32)
        a = jnp.where(col_sb < lo_sb, a, 0.0).astype(BF16).reshape(HG_C, HG_C)
        o = o3.reshape(HG_C, LANES) + jnp.dot(a, vb, preferred_element_type=F32)
        b_last = b[HG_C - 1:HG_C, :]
        kdec = jnp.exp(b_last - bk).astype(BF16)
        st_ref[...] = st * jnp.exp(b_last) + jnp.dot(v.T.astype(BF16), kdec, preferred_element_type=F32)
        ms = jnp.mean(o * o, axis=-1, keepdims=True)
        zg = zg_ref[pl.ds(r0, HG_C), :]
        hn_ref[pl.ds(r0, HG_C), :] = o * lax.rsqrt(ms + RMS_EPS) * g * (zg * _sigmoid(zg))
        return carry

    lax.fori_loop(0, n_chunks, chunk, 0, unroll=8)

    @pl.when(t == pl.num_programs(2) - 1)
    def _fin():
        sfin_ref[...] = st_ref[...].T


def _hgrn_prompt(hq, hk, hv, lf, zg, g_hg, tb=1024):
    bsz, t, _ = hq.shape
    seq = pl.BlockSpec((None, tb, HG_HEAD_DIM), lambda b, h, i: (b, i, h))
    return pl.pallas_call(
        functools.partial(_hgrn_kernel, n_chunks=tb // HG_C),
        grid=(bsz, HG_HEADS, t // tb),
        in_specs=[seq, seq, seq, seq, seq,
                  pl.BlockSpec((1, HG_HEAD_DIM), lambda b, h, i: (0, h))],
        out_specs=[seq,
                   pl.BlockSpec((None, None, HG_HEAD_DIM, HG_HEAD_DIM), lambda b, h, i: (b, h, 0, 0))],
        out_shape=[jax.ShapeDtypeStruct((bsz, t, HG_WIDTH), F32),
                   jax.ShapeDtypeStruct((bsz, HG_HEADS, HG_HEAD_DIM, HG_HEAD_DIM), F32)],
        scratch_shapes=[pltpu.VMEM((HG_HEAD_DIM, HG_HEAD_DIM), F32)],
        compiler_params=_cparams(("parallel", "parallel", "arbitrary")),
        name="hgrn_prompt",
    )(hq, hk, hv, lf, zg, g_hg.reshape(1, HG_WIDTH))


def _hgrn_step_kernel(q_ref, k_ref, v_ref, lf_ref, zg_ref, g_ref, s_ref, hn_ref, snew_ref):
    row = slice(None)
    zeros = jnp.zeros((HG_HEAD_DIM - 3, HG_HEAD_DIM), F32)
    for h in range(HG_HEADS):
        cs = slice(h * HG_HEAD_DIM, (h + 1) * HG_HEAD_DIM)
        q, k, v = q_ref[row, cs], k_ref[row, cs], v_ref[row, cs]
        f = jnp.exp(lf_ref[row, cs])
        cols = jnp.concatenate([f, k, q, zeros], axis=0).T
        s_new = cols[:, 0:1] * s_ref[h] + cols[:, 1:2] * v
        snew_ref[h] = s_new
        o = jnp.sum(cols[:, 2:3] * s_new, axis=0, keepdims=True)
        ms = jnp.mean(o * o, axis=-1, keepdims=True)
        zg = zg_ref[row, cs]
        hn_ref[row, cs] = o * lax.rsqrt(ms + RMS_EPS) * g_ref[:, cs] * (zg * _sigmoid(zg))


def _hgrn_step(hq, hk, hv, lf, zg, g_hg, state):
    bsz = hq.shape[0]
    one = pl.BlockSpec((None, 1, HG_WIDTH), lambda b: (b, 0, 0))
    st = pl.BlockSpec((None, HG_HEADS, HG_HEAD_DIM, HG_HEAD_DIM), lambda b: (b, 0, 0, 0))
    r3 = lambda a: a.reshape(bsz, 1, HG_WIDTH)
    hn, s_new = pl.pallas_call(
        _hgrn_step_kernel,
        grid=(bsz,),
        in_specs=[one, one, one, one, one, pl.BlockSpec((1, HG_WIDTH), lambda b: (0, 0)), st],
        out_specs=[one, st],
        out_shape=[jax.ShapeDtypeStruct((bsz, 1, HG_WIDTH), F32),
                   jax.ShapeDtypeStruct(state.shape, F32)],
        compiler_params=_cparams(("parallel",)),
        name="hgrn_step",
    )(r3(hq), r3(hk), r3(hv), r3(lf), r3(zg), g_hg.reshape(1, HG_WIDTH), state)
    return hn.reshape(bsz, HG_WIDTH), s_new


ATT_N = 128
ATT_SUPER = 2048
ATT_G = 8


def _attn_prompt_kernel(q_ref, k_ref, v_ref, o_ref, osc, lsc, *, seq_len):
    lane = lax.broadcasted_iota(jnp.int32, (ATT_N, LANES), 1)
    rowi = lax.broadcasted_iota(jnp.int32, (ATT_N, LANES), 0)
    head0 = lane < ATT_HEAD_DIM
    kidx = lax.broadcasted_iota(jnp.int32, (ATT_N, 2 * ATT_N), 1)
    qidx = lax.broadcasted_iota(jnp.int32, (ATT_N, 2 * ATT_N), 0)
    band = jnp.logical_and(kidx >= qidx, kidx <= qidx + ATT_N)
    in_prev = kidx < ATT_N
    scale = ATT_HEAD_DIM ** -0.5

    bqk = (((2,), (2,)), ((0,), (0,)))
    bkd = (((2,), (1,)), ((0,), (0,)))

    def do_group(p, d, base, g, gsz):
        span = ATT_N * d
        rows_g = gsz * ATT_N
        if d == 1:
            off = g * rows_g
            start = base + off

            def cur(ref):
                return ref[pl.ds(start, rows_g), :].reshape(gsz, ATT_N, LANES)

            def prv(ref, c):
                before = ref[pl.ds(jnp.maximum(start - ATT_N, 0), ATT_N), :].astype(BF16)
                return jnp.concatenate([before[None], c[:gsz - 1]], axis=0)

            bidx = lax.broadcasted_iota(jnp.int32, (gsz, ATT_N, 2 * ATT_N), 0)
            seen = jnp.logical_and(band, jnp.logical_not(
                jnp.logical_and(jnp.logical_and(bidx == 0, in_prev), start == 0)))
        else:
            per_blk = d // gsz
            off = (g // per_blk) * span + (g % per_blk) * gsz
            start = base + off
            prev = jnp.maximum(start - span, 0)

            def cur(ref):
                return jnp.stack([ref[pl.ds(start + r, ATT_N, stride=d), :] for r in range(gsz)])

            def prv(ref, c):
                return jnp.stack([ref[pl.ds(prev + r, ATT_N, stride=d), :] for r in range(gsz)]).astype(BF16)

            seen = jnp.logical_and(band, jnp.logical_not(jnp.logical_and(in_prev, start < span)))
        q = cur(q_ref) * scale
        kc = cur(k_ref).astype(BF16)
        vc = cur(v_ref).astype(BF16)
        kk = jnp.concatenate([prv(k_ref, kc), kc], axis=1)
        vv = jnp.concatenate([prv(v_ref, vc), vc], axis=1)
        res = []
        for hm in (head0, jnp.logical_not(head0)):
            qh = jnp.where(hm, q, 0.0).astype(BF16)
            s = lax.dot_general(qh, kk, bqk, preferred_element_type=F32)
            s = jnp.where(seen, s, NEG)
            m = jnp.max(s, axis=-1, keepdims=True)
            pr = jnp.exp(s - m)
            den = jnp.sum(pr, axis=-1, keepdims=True)
            o = lax.dot_general(pr.astype(BF16), vv, bkd, preferred_element_type=F32)
            res.append((o / den, m + jnp.log(den)))
        o = jnp.where(head0, res[0][0], res[1][0])
        lse = jnp.where(head0, res[0][1], res[1][1])
        if d == 1:
            osc[p, pl.ds(off, rows_g), :] = o.reshape(rows_g, LANES)
            lsc[p, pl.ds(off, rows_g), :] = lse.reshape(rows_g, LANES)
        else:
            for r in range(gsz):
                osc[p, pl.ds(off + r, ATT_N, stride=d), :] = o[r]
                lsc[p, pl.ds(off + r, ATT_N, stride=d), :] = lse[r]

    def superblock(sb, carry):
        base = sb * ATT_SUPER

        def groups(it, c2):
            for p, (w, d) in enumerate(DILATED_PATTERNS):
                gsz = ATT_G if d == 1 else min(ATT_G, d)
                for sub in range(ATT_G // gsz):
                    do_group(p, d, base, it * (ATT_G // gsz) + sub, gsz)
            return c2

        lax.fori_loop(0, ATT_SUPER // (ATT_G * ATT_N), groups, 0)

        piece = 256

        def merge(j, c2):
            r = pl.ds(pl.multiple_of(j * piece, piece), piece)
            ls = [lsc[p, r, :] for p in range(len(DILATED_PATTERNS))]
            mx = jnp.maximum(jnp.maximum(ls[0], ls[1]), ls[2])
            ws = [jnp.exp(l - mx) for l in ls]
            num = ws[0] * osc[0, r, :] + ws[1] * osc[1, r, :] + ws[2] * osc[2, r, :]
            o_ref[pl.ds(pl.multiple_of(base + j * piece, piece), piece), :] = num / (ws[0] + ws[1] + ws[2])
            return c2

        lax.fori_loop(0, ATT_SUPER // piece, merge, 0)
        return carry

    lax.fori_loop(0, seq_len // ATT_SUPER, superblock, 0)


def _attn_prompt(aq, ak, av):
    bsz, t, _ = aq.shape
    spec = pl.BlockSpec((None, t, LANES), lambda b, p: (b, 0, p))
    n_pat = len(DILATED_PATTERNS)
    return pl.pallas_call(
        functools.partial(_attn_prompt_kernel, seq_len=t),
        grid=(bsz, ATT_WIDTH // LANES),
        in_specs=[spec, spec, spec],
        out_specs=spec,
        out_shape=jax.ShapeDtypeStruct((bsz, t, ATT_WIDTH), F32),
        scratch_shapes=[pltpu.VMEM((n_pat, ATT_SUPER, LANES), F32),
                        pltpu.VMEM((n_pat, ATT_SUPER, LANES), F32)],
        compiler_params=_cparams(("parallel", "parallel")),
        name="attn_prompt",
    )(aq, ak, av)


def _attn_step_kernel(q_ref, kn_ref, vn_ref, ck_ref, cv_ref, o_ref, nk_ref, nv_ref):
    win = ck_ref.shape[-1]
    kt, vt = ck_ref[...], cv_ref[...]
    q = q_ref[...] * (ATT_HEAD_DIM ** -0.5)
    kn, vn = kn_ref[...], vn_ref[...]
    s_all = jnp.sum(kt * q, axis=1, keepdims=True)
    s_new = jnp.sum(kn * q, axis=1, keepdims=True)
    dist = win - lax.broadcasted_iota(jnp.int32, (1, 1, win), 2)

    ps, pnews, lses = [], [], []
    for w, d in DILATED_PATTERNS:
        on_stride = (dist & (d - 1)) == 0 if d & (d - 1) == 0 else dist % d == 0
        valid = jnp.logical_and(dist <= w, on_stride)
        sm = jnp.where(valid, s_all, NEG)
        m = jnp.maximum(jnp.max(sm, axis=-1, keepdims=True), s_new)
        p = jnp.exp(sm - m)
        pn = jnp.exp(s_new - m)
        den = jnp.sum(p, axis=-1, keepdims=True) + pn
        ps.append(p / den)
        pnews.append(pn / den)
        lses.append(m + jnp.log(den))
    mx = jnp.maximum(jnp.maximum(lses[0], lses[1]), lses[2])
    ws = [jnp.exp(l - mx) for l in lses]
    wsum = ws[0] + ws[1] + ws[2]
    p_tot = (ws[0] * ps[0] + ws[1] * ps[1] + ws[2] * ps[2]) / wsum
    pn_tot = (ws[0] * pnews[0] + ws[1] * pnews[1] + ws[2] * pnews[2]) / wsum
    o_ref[...] = jnp.sum(vt * p_tot, axis=-1, keepdims=True) + pn_tot * vn

    last = lax.broadcasted_iota(jnp.int32, (1, 1, win), 2) == win - 1
    nk_ref[...] = jnp.where(last, kn, pltpu.roll(kt, win - 1, 2))
    nv_ref[...] = jnp.where(last, vn, pltpu.roll(vt, win - 1, 2))


def _attn_step(aq, ak, av, cache_k, cache_v):
    bsz, _, _, win = cache_k.shape
    one = pl.BlockSpec((None, ATT_HEADS, ATT_HEAD_DIM, 1), lambda b: (b, 0, 0, 0))
    cache = pl.BlockSpec((None, ATT_HEADS, ATT_HEAD_DIM, win), lambda b: (b, 0, 0, 0))
    col = lambda a: a.reshape(bsz, ATT_HEADS, ATT_HEAD_DIM, 1)
    att, new_k, new_v = pl.pallas_call(
        _attn_step_kernel,
        grid=(bsz,),
        in_specs=[one, one, one, cache, cache],
        out_specs=[one, cache, cache],
        out_shape=[jax.ShapeDtypeStruct((bsz, ATT_HEADS, ATT_HEAD_DIM, 1), F32),
                   jax.ShapeDtypeStruct(cache_k.shape, F32),
                   jax.ShapeDtypeStruct(cache_v.shape, F32)],
        compiler_params=_cparams(("parallel",)),
        name="attn_step",
    )(col(aq), col(ak), col(av), cache_k, cache_v)
    return att.reshape(bsz, ATT_WIDTH), new_k, new_v


ROUTE_TM = 256
SUBLANES = 8
ROW_TILE = D_MODEL // LANES
assert ROW_TILE == SUBLANES


def _store_row_tiles(ref, val, lead=()):
    n = val.shape[0]
    for j in range(ROW_TILE):
        ref[lead + (pl.ds(j, n, stride=ROW_TILE), slice(None))] = val[:, j * LANES:(j + 1) * LANES]


def _load_row_tiles(ref, n, lead=()):
    return jnp.concatenate([ref[lead + (pl.ds(j, n, stride=ROW_TILE), slice(None))] for j in range(ROW_TILE)],
                           axis=1)


def _row_tile(ref, r, n=1, lead=()):
    start = r * ROW_TILE if isinstance(r, int) else pl.multiple_of(r * ROW_TILE, ROW_TILE)
    return ref.at[lead + (pl.ds(start, n * ROW_TILE),)]


def _outproj_kernel(hn_ref, att_ref, x_ref, hn_s_ref, att_s_ref, x_s_ref, w_ref, g_ref, wr_ref,
                    xmid_ref, h2_ref, route_ref, cnt_ref, *, n_sample):
    is_prompt = pl.program_id(0) < pl.num_programs(0) - 1
    hn = jnp.where(is_prompt, hn_ref[...], hn_s_ref[...])
    att = jnp.where(is_prompt, att_ref[...], att_s_ref[...])
    y = (jnp.dot(hn.astype(BF16), w_ref[0:HG_WIDTH, :], preferred_element_type=F32)
         + jnp.dot(att.astype(BF16), w_ref[HG_WIDTH:, :], preferred_element_type=F32))
    xm = jnp.where(is_prompt, x_ref[...], x_s_ref[...]) + y
    xmid_ref[...] = xm
    ms = jnp.mean(xm * xm, axis=-1, keepdims=True)
    h2 = xm * lax.rsqrt(ms + RMS_EPS) * g_ref[...]
    _store_row_tiles(h2_ref, h2)
    h2_hi = h2.astype(BF16)
    h2_lo = (h2 - h2_hi.astype(F32)).astype(BF16)
    lg = (jnp.dot(h2_hi, wr_ref[0], preferred_element_type=F32)
          + (jnp.dot(h2_lo, wr_ref[0], preferred_element_type=F32)
             + jnp.dot(h2_hi, wr_ref[1], preferred_element_type=F32)))
    lane = lax.broadcasted_iota(jnp.int32, lg.shape, 1).astype(F32)
    big = float(LANES)
    gmask = lane < N_GROUPS
    lgg = jnp.where(gmask, lg, NEG)
    mg = jnp.max(lgg, axis=-1, keepdims=True)
    gi = jnp.min(jnp.where(lgg == mg, lane, big), axis=-1, keepdims=True)
    p_grp = 1.0 / jnp.sum(jnp.exp(lgg - mg), axis=-1, keepdims=True)
    lo = N_GROUPS + gi * EXPERTS_PER_GROUP
    emask = jnp.logical_and(lane >= lo, lane < lo + EXPERTS_PER_GROUP)
    le1 = jnp.where(emask, lg, NEG)
    m1 = jnp.max(le1, axis=-1, keepdims=True)
    i1 = jnp.min(jnp.where(le1 == m1, lane, big), axis=-1, keepdims=True)
    le2 = jnp.where(lane == i1, NEG, le1)
    m2 = jnp.max(le2, axis=-1, keepdims=True)
    i2 = jnp.min(jnp.where(le2 == m2, lane, big), axis=-1, keepdims=True)
    r = jnp.exp(m2 - m1)
    g1 = p_grp / (1.0 + r)
    g2 = p_grp * r / (1.0 + r)
    e1, e2 = i1 - N_GROUPS, i2 - N_GROUPS
    route_ref[...] = jnp.where(lane == 0, e1,
                               jnp.where(lane == 1, e2,
                                         jnp.where(lane == 2, g1, jnp.where(lane == 3, g2, 0.0))))

    @pl.when(pl.program_id(0) == 0)
    def _zero():
        cnt_ref[...] = jnp.zeros_like(cnt_ref)

    rows = lax.broadcasted_iota(jnp.int32, lg.shape, 0)
    real = jnp.logical_or(is_prompt, rows < n_sample)
    hit = jnp.logical_and(real, jnp.logical_or(lane == e1, lane == e2))
    cnt_ref[...] += jnp.sum(jnp.where(hit, 1.0, 0.0), axis=0, keepdims=True)


def _outproj(hn_p, att_p, x_p, hn_s, att_s, x_s, w_out_bf, g_ffn, w_router):
    tm = ROUTE_TM
    n_p = x_p.shape[0]
    n_tiles = n_p // tm + 1
    pad = lambda a: jnp.pad(a, ((0, tm - a.shape[0]), (0, 0)))
    row_p = lambda i: (jnp.minimum(i, n_tiles - 2), 0)
    row = lambda i: (i, 0)
    const = lambda i: (0, 0)
    n_rows = n_tiles * tm
    return pl.pallas_call(
        functools.partial(_outproj_kernel, n_sample=x_s.shape[0]),
        grid=(n_tiles,),
        in_specs=[pl.BlockSpec((tm, HG_WIDTH), row_p),
                  pl.BlockSpec((tm, ATT_WIDTH), row_p),
                  pl.BlockSpec((tm, D_MODEL), row_p),
                  pl.BlockSpec((tm, HG_WIDTH), const),
                  pl.BlockSpec((tm, ATT_WIDTH), const),
                  pl.BlockSpec((tm, D_MODEL), const),
                  pl.BlockSpec((D_MODEL, D_MODEL), const),
                  pl.BlockSpec((1, D_MODEL), const),
                  pl.BlockSpec((2, D_MODEL, LANES), lambda i: (0, 0, 0))],
        out_specs=[pl.BlockSpec((tm, D_MODEL), row),
                   pl.BlockSpec((tm * ROW_TILE, LANES), row),
                   pl.BlockSpec((tm, LANES), row),
                   pl.BlockSpec((SUBLANES, LANES), const)],
        out_shape=[jax.ShapeDtypeStruct((n_rows, D_MODEL), F32),
                   jax.ShapeDtypeStruct((n_rows * ROW_TILE, LANES), F32),
                   jax.ShapeDtypeStruct((n_rows, LANES), F32),
                   jax.ShapeDtypeStruct((SUBLANES, LANES), F32)],
        compiler_params=_cparams(("arbitrary",)),
        name="outproj",
    )(hn_p, att_p, x_p, pad(hn_s), pad(att_s), pad(x_s), w_out_bf, g_ffn.reshape(1, D_MODEL), w_router)


def _rank_kernel(route_ref, cnt_ref, slot_ref, meta_ref, base_ref, *, n_tok, blk):
    i = pl.program_id(0)
    tm = route_ref.shape[0]
    lane = lax.broadcasted_iota(jnp.int32, (tm, LANES), 1).astype(F32)
    rowg = i * tm + lax.broadcasted_iota(jnp.int32, (tm, LANES), 0)
    valid = rowg < n_tok
    r = route_ref[...]
    oh0 = jnp.where(jnp.logical_and(valid, lane == r[:, 0:1]), 1.0, 0.0)
    oh1 = jnp.where(jnp.logical_and(valid, lane == r[:, 1:2]), 1.0, 0.0)
    oh = oh0 + oh1

    @pl.when(i == 0)
    def _starts():
        cnt = cnt_ref[0:1, :].astype(jnp.int32)
        shift = blk.bit_length() - 1
        padded = (((cnt + (blk - 1)) >> shift) << shift).astype(F32)
        up = (lax.broadcasted_iota(jnp.int32, (LANES, LANES), 0)
              < lax.broadcasted_iota(jnp.int32, (LANES, LANES), 1)).astype(F32)
        start = jnp.dot(jnp.broadcast_to(padded, (8, LANES)), up, precision=HIGHEST,
                        preferred_element_type=F32)[0:1]
        base_ref[...] = start
        nb = meta_ref.shape[0]
        lane_b = lax.broadcasted_iota(jnp.int32, (nb, LANES), 1)
        blk_start = (lax.broadcasted_iota(jnp.int32, (nb, LANES), 0) * blk).astype(F32)
        ended = jnp.logical_and(start + padded <= blk_start, lane_b < N_EXPERTS)
        be = jnp.minimum(jnp.sum(jnp.where(ended, 1.0, 0.0), axis=-1, keepdims=True), N_EXPERTS - 1.0)
        n_used = jnp.sum(padded, axis=-1, keepdims=True) * (1.0 / blk)
        meta_ref[...] = jnp.where(lane_b == 0, be, jnp.where(lane_b == 1, n_used, 0.0)).astype(jnp.int32)

    before = (lax.broadcasted_iota(jnp.int32, (tm, tm), 1)
              < lax.broadcasted_iota(jnp.int32, (tm, tm), 0)).astype(BF16)
    pre = jnp.dot(before, oh.astype(BF16), preferred_element_type=F32) + base_ref[...]
    s0 = jnp.sum(oh0 * pre, axis=-1, keepdims=True)
    s1 = jnp.sum(oh1 * pre, axis=-1, keepdims=True)
    slot_ref[...] = jnp.where(lane == 0, s0, jnp.where(lane == 1, s1, 0.0)).astype(jnp.int32)
    base_ref[...] += jnp.sum(oh, axis=0, keepdims=True)


def _rank(route, counts, n_tok, blk, nblk):
    n_rows = route.shape[0]
    nb = (nblk + 7) // 8 * 8
    slots, meta = pl.pallas_call(
        functools.partial(_rank_kernel, n_tok=n_tok, blk=blk),
        grid=(n_rows // ROUTE_TM,),
        in_specs=[pl.BlockSpec((ROUTE_TM, LANES), lambda i: (i, 0)),
                  pl.BlockSpec(counts.shape, lambda i: (0, 0))],
        out_specs=[pl.BlockSpec((ROUTE_TM, LANES), lambda i: (i, 0)),
                   pl.BlockSpec((nb, LANES), lambda i: (0, 0))],
        out_shape=[jax.ShapeDtypeStruct((n_rows, LANES), jnp.int32),
                   jax.ShapeDtypeStruct((nb, LANES), jnp.int32)],
        scratch_shapes=[pltpu.VMEM((1, LANES), F32)],
        compiler_params=_cparams(("arbitrary",)),
        name="rank",
    )(route, counts)
    return slots[:n_tok, :TOP_K_INNER].reshape(-1), meta[:nblk, 0], meta[0:1, 1]


DMA_UNROLL = 8


def _dispatch_kernel(slot_ref, be_ref, nu_ref, h2_ref, xb_out, zbuf, sem, zsem, *, n_tok):
    i = pl.program_id(0)
    tm = h2_ref.shape[0] // ROW_TILE
    tail = n_tok % tm

    @pl.when(i == 0)
    def _zero_padding():
        blk = zbuf.shape[0] // ROW_TILE
        n_blocks = be_ref.shape[0]
        zbuf[...] = jnp.zeros_like(zbuf)

        def ends_expert(j):
            nxt = be_ref[jnp.minimum(j + 1, n_blocks - 1)]
            return jnp.logical_or(j >= nu_ref[0] - 1, be_ref[j] != nxt)

        def issue(j, c):
            @pl.when(ends_expert(j))
            def _():
                pltpu.make_async_copy(zbuf, _row_tile(xb_out, j * blk, blk), zsem).start()
            return c

        def drain(j, c):
            @pl.when(ends_expert(j))
            def _():
                pltpu.make_async_copy(zbuf, _row_tile(xb_out, 0, blk), zsem).wait()
            return c

        lax.fori_loop(0, n_blocks, issue, 0)
        lax.fori_loop(0, n_blocks, drain, 0)

    def push(rows):
        def body(r, c):
            a = (i * tm + r) * TOP_K_INNER
            for k in range(TOP_K_INNER):
                pltpu.make_async_copy(_row_tile(h2_ref, r), _row_tile(xb_out, slot_ref[a + k]), sem).start(
                    priority=k % 2)
            return c
        lax.fori_loop(0, rows, body, 0, unroll=DMA_UNROLL)
        for k in range(TOP_K_INNER):
            pltpu.make_async_copy(_row_tile(h2_ref, 0, rows), _row_tile(xb_out, 0, rows), sem).wait()

    last = pl.num_programs(0) - 1
    if tail == 0:
        push(tm)
    else:
        @pl.when(i < last)
        def _full():
            push(tm)

        @pl.when(i == last)
        def _tail():
            push(tail)


def _dispatch(slot_flat, blk_expert, n_used, h2, n_tok, blk):
    n_tiles = h2.shape[0] // (ROUTE_TM * ROW_TILE)
    tm = ROUTE_TM * max(g for g in range(1, 9) if n_tiles % g == 0)
    n_slots = blk_expert.shape[0] * blk
    grid_spec = pltpu.PrefetchScalarGridSpec(
        num_scalar_prefetch=3,
        grid=(h2.shape[0] // (tm * ROW_TILE),),
        in_specs=[pl.BlockSpec((tm * ROW_TILE, LANES), lambda i, s, be, nu: (i, 0))],
        out_specs=pl.BlockSpec(memory_space=pl.ANY),
        scratch_shapes=[pltpu.VMEM((blk * ROW_TILE, LANES), F32),
                        pltpu.SemaphoreType.DMA(()),
                        pltpu.SemaphoreType.DMA(())],
    )
    return pl.pallas_call(
        functools.partial(_dispatch_kernel, n_tok=n_tok),
        grid_spec=grid_spec,
        out_shape=jax.ShapeDtypeStruct((n_slots * ROW_TILE, LANES), F32),
        compiler_params=_cparams(("arbitrary",)),
        name="dispatch",
    )(slot_flat, blk_expert, n_used, h2)


def _expert_kernel(be_ref, nu_ref, x_ref, wg_hbm, wu_hbm, wd_hbm, y_ref,
                   wg_f, wu_f, wd_f, sem, wgb, wub, wdb, cur_ref):
    i = pl.program_id(0)
    n_used = nu_ref[0]
    e = be_ref[i]
    e_prev = be_ref[jnp.maximum(i - 1, 0)]

    def fetch(ex, s):
        return [pltpu.make_async_copy(hbm.at[ex], buf.at[s], sem.at[s])
                for hbm, buf in ((wg_hbm, wg_f), (wu_hbm, wu_f), (wd_hbm, wd_f))]

    @pl.when(i == 0)
    def _first():
        cur_ref[0] = 0
        for c in fetch(e, 0):
            c.start()

    @pl.when(jnp.logical_and(jnp.logical_or(i == 0, e != e_prev), i < n_used))
    def _new_expert():
        s = cur_ref[0]
        j = lax.while_loop(lambda j: jnp.logical_and(j < n_used, be_ref[jnp.minimum(j, n_used - 1)] == e),
                           lambda j: j + 1, i + 1)

        @pl.when(j < n_used)
        def _prefetch():
            for c in fetch(be_ref[j], 1 - s):
                c.start()

        for c in fetch(e, s):
            c.wait()
        wgb[...] = wg_f[s].astype(BF16)
        wub[...] = wu_f[s].astype(BF16)
        wdb[...] = wd_f[s].astype(BF16)
        cur_ref[0] = 1 - s

    @pl.when(i < n_used)
    def _run():
        x = _load_row_tiles(x_ref, x_ref.shape[0] // ROW_TILE).astype(BF16)
        a = jnp.dot(x, wgb[...], preferred_element_type=F32)
        u = jnp.dot(x, wub[...], preferred_element_type=F32)
        mid = (a * _sigmoid(a) * u).astype(BF16)
        _store_row_tiles(y_ref, jnp.dot(mid, wdb[...], preferred_element_type=F32))

    @pl.when(i >= nu_ref[0])
    def _skip():
        y_ref[...] = jnp.zeros_like(y_ref)


def _experts(xb, blk_expert, n_used, w_g, w_u, w_d, blk):
    nblk = blk_expert.shape[0]
    hbm = pl.BlockSpec(memory_space=pl.ANY)
    grid_spec = pltpu.PrefetchScalarGridSpec(
        num_scalar_prefetch=2,
        grid=(nblk,),
        in_specs=[pl.BlockSpec((blk * ROW_TILE, LANES), lambda i, be, nu: (jnp.minimum(i, nu[0] - 1), 0)),
                  hbm, hbm, hbm],
        out_specs=pl.BlockSpec((blk * ROW_TILE, LANES), lambda i, be, nu: (i, 0)),
        scratch_shapes=[pltpu.VMEM((2, D_MODEL, D_FF_EXPERT), F32),
                        pltpu.VMEM((2, D_MODEL, D_FF_EXPERT), F32),
                        pltpu.VMEM((2, D_FF_EXPERT, D_MODEL), F32),
                        pltpu.SemaphoreType.DMA((2,)),
                        pltpu.VMEM((D_MODEL, D_FF_EXPERT), BF16),
                        pltpu.VMEM((D_MODEL, D_FF_EXPERT), BF16),
                        pltpu.VMEM((D_FF_EXPERT, D_MODEL), BF16),
                        pltpu.SMEM((1,), jnp.int32)],
    )
    return pl.pallas_call(
        _expert_kernel,
        grid_spec=grid_spec,
        out_shape=jax.ShapeDtypeStruct((nblk * blk * ROW_TILE, LANES), F32),
        compiler_params=_cparams(("arbitrary",)),
        name="experts",
    )(blk_expert, n_used, xb, w_g, w_u, w_d)


def _final_kernel(slot_ref, x_ref, route_ref, g_ref, yb_hbm, o_ref, ybuf, sem, *, tok0):
    i = pl.program_id(0)
    tm = x_ref.shape[0]

    def gather(j, buf):
        def body(r, c):
            a = (tok0 + j * tm + r) * TOP_K_INNER
            for k in range(TOP_K_INNER):
                pltpu.make_async_copy(_row_tile(yb_hbm, slot_ref[a + k]), _row_tile(ybuf, r, lead=(buf, k)),
                                      sem.at[buf]).start(priority=k % 2)
            return c
        lax.fori_loop(0, tm, body, 0, unroll=DMA_UNROLL)

    @pl.when(i == 0)
    def _first():
        gather(0, 0)

    @pl.when(i + 1 < pl.num_programs(0))
    def _next():
        gather(i + 1, (i + 1) % 2)

    buf = i % 2
    for k in range(TOP_K_INNER):
        pltpu.make_async_copy(_row_tile(yb_hbm, 0, tm), ybuf.at[buf, k], sem.at[buf]).wait()
    route = route_ref[...]
    y0 = _load_row_tiles(ybuf, tm, lead=(buf, 0))
    y1 = _load_row_tiles(ybuf, tm, lead=(buf, 1))
    x = x_ref[...] + (y0 * route[:, 2:3] + y1 * route[:, 3:4])
    ms = jnp.mean(x * x, axis=-1, keepdims=True)
    o_ref[...] = x * lax.rsqrt(ms + RMS_EPS) * g_ref[...]


def _final(slot_flat, xmid, route, g_final, yb, tok0, n_out, tm):
    blk0 = tok0 // tm
    grid_spec = pltpu.PrefetchScalarGridSpec(
        num_scalar_prefetch=1,
        grid=(n_out // tm,),
        in_specs=[pl.BlockSpec((tm, D_MODEL), lambda i, s: (i + blk0, 0)),
                  pl.BlockSpec((tm, LANES), lambda i, s: (i + blk0, 0)),
                  pl.BlockSpec((1, D_MODEL), lambda i, s: (0, 0)),
                  pl.BlockSpec(memory_space=pl.ANY)],
        out_specs=pl.BlockSpec((tm, D_MODEL), lambda i, s: (i, 0)),
        scratch_shapes=[pltpu.VMEM((2, TOP_K_INNER, tm * ROW_TILE, LANES), F32),
                        pltpu.SemaphoreType.DMA((2,))],
    )
    return pl.pallas_call(
        functools.partial(_final_kernel, tok0=tok0),
        grid_spec=grid_spec,
        out_shape=jax.ShapeDtypeStruct((n_out, D_MODEL), F32),
        compiler_params=_cparams(("arbitrary",)),
        name="final",
    )(slot_flat, xmid, route, g_final.reshape(1, D_MODEL), yb)


def kernel(x_prompt, x_sample, cache_attn_k, cache_attn_v, state_hgrn, w_in, w_out, hg_lb_logits,
           hg_norm_g, norm_mix_g, norm_ffn_g, norm_final_g, w_route_group, w_route_expert,
           w_expert_gate, w_expert_up, w_expert_down):
    bp, tp, _ = x_prompt.shape
    bs = x_sample.shape[0]
    l = 0
    w_in_bf = w_in[l].astype(BF16)
    w_out_bf = w_out[l].astype(BF16)
    w_router = jnp.concatenate(
        [w_route_group[l],
         jnp.transpose(w_route_expert[l], (1, 0, 2)).reshape(D_MODEL, N_EXPERTS),
         jnp.zeros((D_MODEL, LANES - N_GROUPS - N_EXPERTS), F32)], axis=-1)
    w_router_hi = w_router.astype(BF16)
    w_router = jnp.stack([w_router_hi, (w_router - w_router_hi.astype(F32)).astype(BF16)])

    n_p = bp * tp
    xp = x_prompt.reshape(n_p, D_MODEL)
    pos_p = jnp.arange(tp, dtype=jnp.int32)
    hq, hk, hv, lf, zg, aq, ak, av = _inproj(xp, norm_mix_g[l], w_in_bf, hg_lb_logits, pos_p, 256)
    seq3 = lambda a: a.reshape(bp, tp, HG_WIDTH)
    hn_p, s_fin = _hgrn_prompt(seq3(hq), seq3(hk), seq3(hv), seq3(lf), seq3(zg), hg_norm_g[l])
    att_p = _attn_prompt(seq3(aq), seq3(ak), seq3(av))
    keep = min(MAX_WINDOW, tp)
    heads = lambda a: a.reshape(1, bp, keep, ATT_HEADS, ATT_HEAD_DIM)
    new_k_p = heads(seq3(ak)[:, tp - keep:])
    new_v_p = heads(seq3(av)[:, tp - keep:])

    xs = x_sample.reshape(bs, D_MODEL)
    pos_s = jnp.full((bs,), PAST_LEN, jnp.int32)
    hq, hk, hv, lf, zg, aq, ak, av = _inproj(xs, norm_mix_g[l], w_in_bf, hg_lb_logits, pos_s, bs)
    hn_s, s_new = _hgrn_step(hq, hk, hv, lf, zg, hg_norm_g[l], state_hgrn[l])
    feat = lambda a: jnp.transpose(a, (0, 2, 3, 1))
    att_s, new_k_s, new_v_s = _attn_step(aq, ak, av, feat(cache_attn_k[l]), feat(cache_attn_v[l]))
    cache5 = lambda a: jnp.transpose(a, (0, 3, 1, 2))[None]

    assert n_p % ROUTE_TM == 0 and bs <= ROUTE_TM
    n_tok = n_p + bs
    xmid, h2, route, counts = _outproj(hn_p.reshape(n_p, HG_WIDTH), att_p.reshape(n_p, ATT_WIDTH), xp,
                                       hn_s, att_s, xs, w_out_bf, norm_ffn_g[l], w_router)
    blk = MOE_BLOCK
    nblk = (n_tok * TOP_K_INNER + N_EXPERTS * (blk - 1)) // blk + 1
    slot_flat, blk_expert, n_used = _rank(route, counts, n_tok, blk, nblk)
    xb = _dispatch(slot_flat, blk_expert, n_used, h2, n_tok, blk)
    yb = _experts(xb, blk_expert, n_used, w_expert_gate[l], w_expert_up[l], w_expert_down[l], blk)
    y_prompt = _final(slot_flat, xmid, route, norm_final_g, yb, 0, n_p, 256)
    y_sample = _final(slot_flat, xmid, route, norm_final_g, yb, n_p, bs, bs)

    return (y_prompt.reshape(bp, tp, D_MODEL), y_sample.reshape(bs, 1, D_MODEL),
            new_k_p, new_v_p, s_fin[None], cache5(new_k_s), cache5(new_v_s), s_new[None])
```

```python
import functools

import jax
import jax.numpy as jnp
from jax import lax
from jax.experimental import pallas as pl
from jax.experimental.pallas import tpu as pltpu

F32 = jnp.float32
BF16 = jnp.bfloat16

D_MODEL = 1024
HG_WIDTH = 512
HG_HEAD_DIM = 128
HG_HEADS = 4
ATT_WIDTH = 512
ATT_HEAD_DIM = 64
ATT_HEADS = 8
ROPE_DIM = 16
ROPE_THETA = 500000.0
DILATED_PATTERNS = ((128, 1), (512, 4), (2048, 16))
MAX_WINDOW = 2048
PAST_LEN = 16384
N_GROUPS = 8
EXPERTS_PER_GROUP = 8
N_EXPERTS = 64
TOP_K_INNER = 2
D_FF_EXPERT = 512
MOE_BLOCK = 256
IN_COLS = 4 * HG_WIDTH + 3 * ATT_WIDTH
RMS_EPS = 1e-6

LANES = 128
SUBLANES = 8
VMEM_LIMIT = 56 * 1024 * 1024
NEG = -1e30
HIGHEST = lax.Precision.HIGHEST
NT_DIMS = (((1,), (1,)), ((), ()))


def _sigmoid(z):
    return 1.0 / (1.0 + jnp.exp(-z))


def _cparams(sem):
    return pltpu.CompilerParams(dimension_semantics=sem, vmem_limit_bytes=VMEM_LIMIT)


def _inproj_kernel(x_ref, g_ref, w_ref, lbl_ref, cos_ref, sa_ref, sb_ref,
                   hq_ref, hk_ref, hv_ref, lf_ref, zg_ref, aq_ref, ak_ref, av_ref):
    x = x_ref[...]
    ms = jnp.mean(x * x, axis=-1, keepdims=True)
    h = (x * lax.rsqrt(ms + RMS_EPS) * g_ref[...]).astype(BF16)

    def mm(c0):
        return jnp.dot(h, w_ref[:, c0:c0 + HG_WIDTH], preferred_element_type=F32)

    lbl = lbl_ref[...]
    le = jnp.exp(lbl - jnp.max(lbl, axis=0, keepdims=True))
    lb = le[0:1, :] / jnp.sum(le, axis=0, keepdims=True)

    zq = mm(0)
    hq_ref[...] = zq * _sigmoid(zq)
    zf = mm(HG_WIDTH)
    f = lb + (1.0 - lb) * _sigmoid(zf)
    hk_ref[...] = 1.0 - f
    lf_ref[...] = jnp.log(f)
    hv_ref[...] = mm(2 * HG_WIDTH)
    zg_ref[...] = mm(3 * HG_WIDTH)

    cos, sa, sb = cos_ref[...], sa_ref[...], sb_ref[...]

    def rope(a, out_ref):
        for j in range(ATT_WIDTH // LANES):
            xj = a[:, j * LANES:(j + 1) * LANES]
            up = pltpu.roll(xj, LANES - ROPE_DIM // 2, 1)
            dn = pltpu.roll(xj, ROPE_DIM // 2, 1)
            out_ref[:, j * LANES:(j + 1) * LANES] = xj * cos + up * sa + dn * sb

    rope(mm(4 * HG_WIDTH), aq_ref)
    rope(mm(4 * HG_WIDTH + ATT_WIDTH), ak_ref)
    av_ref[...] = mm(4 * HG_WIDTH + 2 * ATT_WIDTH)


def _rope_tables(pos):
    half = ROPE_DIM // 2
    c = jnp.arange(LANES) % ATT_HEAD_DIM
    inv_freq = ROPE_THETA ** (-(c % half).astype(F32) / half)
    ang = pos.astype(F32)[:, None] * inv_freq[None, :]
    cos, sin = jnp.cos(ang), jnp.sin(ang)
    return (jnp.where(c < ROPE_DIM, cos, 1.0),
            jnp.where(c < half, -sin, 0.0),
            jnp.where(jnp.logical_and(c >= half, c < ROPE_DIM), sin, 0.0))


def _inproj(x2d, g, w_bf, lb_logits, pos, tm):
    m = x2d.shape[0]
    cos, sa, sb = _rope_tables(pos)
    row = lambda i: (i, 0)
    seq_tiles = pos.shape[0] // tm
    row_pos = lambda i: (i % seq_tiles, 0)
    const = lambda i: (0, 0)
    outs = [jax.ShapeDtypeStruct((m, HG_WIDTH), F32)] * 8
    return pl.pallas_call(
        _inproj_kernel,
        grid=(m // tm,),
        in_specs=[pl.BlockSpec((tm, D_MODEL), row),
                  pl.BlockSpec((1, D_MODEL), const),
                  pl.BlockSpec((D_MODEL, IN_COLS), const),
                  pl.BlockSpec(lb_logits.shape, const),
                  pl.BlockSpec((tm, LANES), row_pos),
                  pl.BlockSpec((tm, LANES), row_pos),
                  pl.BlockSpec((tm, LANES), row_pos)],
        out_specs=[pl.BlockSpec((tm, HG_WIDTH), row)] * 8,
        out_shape=outs,
        compiler_params=_cparams(("parallel",)),
        name="inproj",
    )(x2d, g.reshape(1, D_MODEL), w_bf, lb_logits, cos, sa, sb)


HG_C = 128
HG_SB = 16


def _hgrn_kernel(q_ref, k_ref, v_ref, lf_ref, zg_ref, g_ref, hn_ref, sfin_ref, st_ref, *, n_chunks):
    t = pl.program_id(2)

    @pl.when(t == 0)
    def _init():
        st_ref[...] = jnp.zeros_like(st_ref)

    ri = lax.broadcasted_iota(jnp.int32, (HG_C, HG_C), 0)
    ci = lax.broadcasted_iota(jnp.int32, (HG_C, HG_C), 1)
    ltri = (ri >= ci).astype(BF16)
    ones_b = jnp.ones((LANES, LANES), BF16)
    n_sb = HG_C // HG_SB
    row_sb = lax.broadcasted_iota(jnp.int32, (n_sb, HG_SB, LANES), 1)
    col_sb = lax.broadcasted_iota(jnp.int32, (n_sb, HG_SB, HG_C), 2)
    lo_sb = lax.broadcasted_iota(jnp.int32, (n_sb, HG_SB, HG_C), 0) * HG_SB
    g = g_ref[...]

    def chunk(c, carry):
        r0 = pl.multiple_of(c * HG_C, HG_C)
        q = q_ref[pl.ds(r0, HG_C), :]
        k = k_ref[pl.ds(r0, HG_C), :]
        v = v_ref[pl.ds(r0, HG_C), :]
        lf = lf_ref[pl.ds(r0, HG_C), :]
        lf_hi = lf.astype(BF16)
        lf_r = lf - lf_hi.astype(F32)
        lf_mid = lf_r.astype(BF16)
        lf_lo = (lf_r - lf_mid.astype(F32)).astype(BF16)
        b = (jnp.dot(ltri, lf_hi, preferred_element_type=F32)
             + (jnp.dot(ltri, lf_mid, preferred_element_type=F32)
                + jnp.dot(ltri, lf_lo, preferred_element_type=F32)))
        st = st_ref[...]
        vb = v.astype(BF16)
        qb = (q * jnp.exp(b)).astype(BF16)
        o_inter = lax.dot_general(qb, st.astype(BF16), NT_DIMS, preferred_element_type=F32)
        b3, q3, v3 = (a.reshape(n_sb, HG_SB, LANES) for a in (b, q, v))
        bk = b - jnp.log(k)
        bk3 = bk.reshape(n_sb, HG_SB, LANES)
        ps, t_lo, offs = [], [], [0]
        for s in range(HG_SB):
            lo = (s // SUBLANES) * SUBLANES
            d = jnp.where(row_sb[:, lo:] >= s, b3[:, lo:] - bk3[:, s:s + 1, :], NEG)
            ps.append(q3[:, lo:] * jnp.exp(d))
            t_lo.append(lo)
            offs.append(offs[-1] + HG_SB - lo)
        n_rows = offs[-1]
        p_all = jnp.concatenate(ps, axis=1).reshape(n_sb * n_rows, LANES).astype(BF16)
        r_all = jnp.dot(p_all, ones_b, preferred_element_type=F32)
        r_all = r_all.reshape(n_sb, n_rows, LANES)
        o3 = o_inter.reshape(n_sb, HG_SB, LANES)
        tiles = [o3[:, j * SUBLANES:(j + 1) * SUBLANES] for j in range(HG_SB // SUBLANES)]
        for s in range(HG_SB):
            for j in range(t_lo[s] // SUBLANES, HG_SB // SUBLANES):
                r0_ = offs[s] + j * SUBLANES - t_lo[s]
                tiles[j] = tiles[j] + r_all[:, r0_:r0_ + SUBLANES, :] * v3[:, s:s + 1, :]
        o3 = jnp.concatenate(tiles, axis=1)
        b_ref = jnp.concatenate([b3[0:1, 0:1], b3[:n_sb - 1, HG_SB - 1:HG_SB]], axis=0)
        qs = (q3 * jnp.exp(jnp.minimum(b3 - b_ref, 0.0))).astype(BF16)
        ks = jnp.exp(jnp.minimum(b_ref - bk[None], 0.0)).astype(BF16)
        a = lax.dot_general(qs, ks, (((2,), (2,)), ((0,), (0,))), preferred_element_type=F32)
        a = jnp.where(col_sb < lo_sb, a, 0.0).astype(BF16).reshape(HG_C, HG_C)
        o = o3.reshape(HG_C, LANES) + jnp.dot(a, vb, preferred_element_type=F32)
        b_last = b[HG_C - 1:HG_C, :]
        kdec = jnp.exp(b_last - bk).astype(BF16)
        st_ref[...] = st * jnp.exp(b_last) + jnp.dot(v.T.astype(BF16), kdec, preferred_element_type=F32)
        ms = jnp.mean(o * o, axis=-1, keepdims=True)
        zg = zg_ref[pl.ds(r0, HG_C), :]
        hn_ref[pl.ds(r0, HG_C), :] = o * lax.rsqrt(ms + RMS_EPS) * g * (zg * _sigmoid(zg))
        return carry

    lax.fori_loop(0, n_chunks, chunk, 0, unroll=8)

    @pl.when(t == pl.num_programs(2) - 1)
    def _fin():
        sfin_ref[...] = st_ref[...].T


def _hgrn_prompt(hq, hk, hv, lf, zg, g_hg, tb=1024):
    bsz, t, _ = hq.shape
    seq = pl.BlockSpec((None, tb, HG_HEAD_DIM), lambda b, h, i: (b, i, h))
    return pl.pallas_call(
        functools.partial(_hgrn_kernel, n_chunks=tb // HG_C),
        grid=(bsz, HG_HEADS, t // tb),
        in_specs=[seq, seq, seq, seq, seq,
                  pl.BlockSpec((1, HG_HEAD_DIM), lambda b, h, i: (0, h))],
        out_specs=[seq,
                   pl.BlockSpec((None, None, HG_HEAD_DIM, HG_HEAD_DIM), lambda b, h, i: (b, h, 0, 0))],
        out_shape=[jax.ShapeDtypeStruct((bsz, t, HG_WIDTH), F32),
                   jax.ShapeDtypeStruct((bsz, HG_HEADS, HG_HEAD_DIM, HG_HEAD_DIM), F32)],
        scratch_shapes=[pltpu.VMEM((HG_HEAD_DIM, HG_HEAD_DIM), F32)],
        compiler_params=_cparams(("parallel", "parallel", "arbitrary")),
        name="hgrn_prompt",
    )(hq, hk, hv, lf, zg, g_hg.reshape(1, HG_WIDTH))


def _hgrn_step_kernel(q_ref, k_ref, v_ref, lf_ref, zg_ref, g_ref, s_ref, hn_ref, snew_ref):
    row = slice(None)
    zeros = jnp.zeros((HG_HEAD_DIM - 3, HG_HEAD_DIM), F32)
    for h in range(HG_HEADS):
        cs = slice(h * HG_HEAD_DIM, (h + 1) * HG_HEAD_DIM)
        q, k, v = q_ref[row, cs], k_ref[row, cs], v_ref[row, cs]
        f = jnp.exp(lf_ref[row, cs])
        cols = jnp.concatenate([f, k, q, zeros], axis=0).T
        s_new = cols[:, 0:1] * s_ref[h] + cols[:, 1:2] * v
        snew_ref[h] = s_new
        o = jnp.sum(cols[:, 2:3] * s_new, axis=0, keepdims=True)
        ms = jnp.mean(o * o, axis=-1, keepdims=True)
        zg = zg_ref[row, cs]
        hn_ref[row, cs] = o * lax.rsqrt(ms + RMS_EPS) * g_ref[:, cs] * (zg * _sigmoid(zg))


def _hgrn_step(hq, hk, hv, lf, zg, g_hg, state):
    bsz = hq.shape[0]
    one = pl.BlockSpec((None, 1, HG_WIDTH), lambda b: (b, 0, 0))
    st = pl.BlockSpec((None, HG_HEADS, HG_HEAD_DIM, HG_HEAD_DIM), lambda b: (b, 0, 0, 0))
    r3 = lambda a: a.reshape(bsz, 1, HG_WIDTH)
    hn, s_new = pl.pallas_call(
        _hgrn_step_kernel,
        grid=(bsz,),
        in_specs=[one, one, one, one, one, pl.BlockSpec((1, HG_WIDTH), lambda b: (0, 0)), st],
        out_specs=[one, st],
        out_shape=[jax.ShapeDtypeStruct((bsz, 1, HG_WIDTH), F32),
                   jax.ShapeDtypeStruct(state.shape, F32)],
        compiler_params=_cparams(("parallel",)),
        name="hgrn_step",
    )(r3(hq), r3(hk), r3(hv), r3(lf), r3(zg), g_hg.reshape(1, HG_WIDTH), state)
    return hn.reshape(bsz, HG_WIDTH), s_new


ATT_N = 128
ATT_SUPER = 2048
ATT_G = 8


def _attn_prompt_kernel(q_ref, k_ref, v_ref, o_ref, osc, lsc, *, seq_len):
    lane = lax.broadcasted_iota(jnp.int32, (ATT_N, LANES), 1)
    head0 = lane < ATT_HEAD_DIM
    kidx = lax.broadcasted_iota(jnp.int32, (ATT_N, 2 * ATT_N), 1)
    qidx = lax.broadcasted_iota(jnp.int32, (ATT_N, 2 * ATT_N), 0)
    band = jnp.logical_and(kidx >= qidx, kidx <= qidx + ATT_N)
    in_prev = kidx < ATT_N
    scale = ATT_HEAD_DIM ** -0.5

    bqk = (((2,), (2,)), ((0,), (0,)))
    bkd = (((2,), (1,)), ((0,), (0,)))

    def do_group(p, d, base, g, gsz):
        span = ATT_N * d
        rows_g = gsz * ATT_N
        if d == 1:
            off = g * rows_g
            start = base + off

            def cur(ref):
                return ref[pl.ds(start, rows_g), :].reshape(gsz, ATT_N, LANES)

            def prv(ref, c):
                before = ref[pl.ds(jnp.maximum(start - ATT_N, 0), ATT_N), :].astype(BF16)
                return jnp.concatenate([before[None], c[:gsz - 1]], axis=0)

            bidx = lax.broadcasted_iota(jnp.int32, (gsz, ATT_N, 2 * ATT_N), 0)
            seen = jnp.logical_and(band, jnp.logical_not(
                jnp.logical_and(jnp.logical_and(bidx == 0, in_prev), start == 0)))
        else:
            per_blk = d // gsz
            off = (g // per_blk) * span + (g % per_blk) * gsz
            start = base + off
            prev = jnp.maximum(start - span, 0)

            def cur(ref):
                return jnp.stack([ref[pl.ds(start + r, ATT_N, stride=d), :] for r in range(gsz)])

            def prv(ref, c):
                return jnp.stack([ref[pl.ds(prev + r, ATT_N, stride=d), :] for r in range(gsz)]).astype(BF16)

            seen = jnp.logical_and(band, jnp.logical_not(jnp.logical_and(in_prev, start < span)))
        q = cur(q_ref) * scale
        kc = cur(k_ref).astype(BF16)
        vc = cur(v_ref).astype(BF16)
        kk = jnp.concatenate([prv(k_ref, kc), kc], axis=1)
        vv = jnp.concatenate([prv(v_ref, vc), vc], axis=1)
        res = []
        for hm in (head0, jnp.logical_not(head0)):
            qh = jnp.where(hm, q, 0.0).astype(BF16)
            s = lax.dot_general(qh, kk, bqk, preferred_element_type=F32)
            s = jnp.where(seen, s, NEG)
            m = jnp.max(s, axis=-1, keepdims=True)
            pr = jnp.exp(s - m)
            den = jnp.sum(pr, axis=-1, keepdims=True)
            o = lax.dot_general(pr.astype(BF16), vv, bkd, preferred_element_type=F32)
            res.append((o / den, m + jnp.log(den)))
        o = jnp.where(head0, res[0][0], res[1][0])
        lse = jnp.where(head0, res[0][1], res[1][1])
        if d == 1:
            osc[p, pl.ds(off, rows_g), :] = o.reshape(rows_g, LANES)
            lsc[p, pl.ds(off, rows_g), :] = lse.reshape(rows_g, LANES)
        else:
            for r in range(gsz):
                osc[p, pl.ds(off + r, ATT_N, stride=d), :] = o[r]
                lsc[p, pl.ds(off + r, ATT_N, stride=d), :] = lse[r]

    def superblock(sb, carry):
        base = sb * ATT_SUPER

        def groups(it, c2):
            for p, (w, d) in enumerate(DILATED_PATTERNS):
                gsz = ATT_G if d == 1 else min(ATT_G, d)
                for sub in range(ATT_G // gsz):
                    do_group(p, d, base, it * (ATT_G // gsz) + sub, gsz)
            return c2

        lax.fori_loop(0, ATT_SUPER // (ATT_G * ATT_N), groups, 0)

        piece = 256

        def merge(j, c2):
            r = pl.ds(pl.multiple_of(j * piece, piece), piece)
            ls = [lsc[p, r, :] for p in range(len(DILATED_PATTERNS))]
            mx = jnp.maximum(jnp.maximum(ls[0], ls[1]), ls[2])
            ws = [jnp.exp(l - mx) for l in ls]
            num = ws[0] * osc[0, r, :] + ws[1] * osc[1, r, :] + ws[2] * osc[2, r, :]
            o_ref[pl.ds(pl.multiple_of(base + j * piece, piece), piece), :] = num / (ws[0] + ws[1] + ws[2])
            return c2

        lax.fori_loop(0, ATT_SUPER // piece, merge, 0)
        return carry

    lax.fori_loop(0, seq_len // ATT_SUPER, superblock, 0)


def _attn_prompt(aq, ak, av):
    bsz, t, _ = aq.shape
    spec = pl.BlockSpec((None, t, LANES), lambda b, p: (b, 0, p))
    n_pat = len(DILATED_PATTERNS)
    return pl.pallas_call(
        functools.partial(_attn_prompt_kernel, seq_len=t),
        grid=(bsz, ATT_WIDTH // LANES),
        in_specs=[spec, spec, spec],
        out_specs=spec,
        out_shape=jax.ShapeDtypeStruct((bsz, t, ATT_WIDTH), F32),
        scratch_shapes=[pltpu.VMEM((n_pat, ATT_SUPER, LANES), F32),
                        pltpu.VMEM((n_pat, ATT_SUPER, LANES), F32)],
        compiler_params=_cparams(("parallel", "parallel")),
        name="attn_prompt",
    )(aq, ak, av)


def _attn_step_kernel(q_ref, kn_ref, vn_ref, ck_ref, cv_ref, o_ref, nk_ref, nv_ref):
    win = ck_ref.shape[-1]
    kt, vt = ck_ref[...], cv_ref[...]
    chan = lax.broadcasted_iota(jnp.int32, (ATT_HEAD_DIM, ATT_WIDTH), 1)
    feat = lax.broadcasted_iota(jnp.int32, (ATT_HEAD_DIM, ATT_WIDTH), 0)
    picks = [chan == h * ATT_HEAD_DIM + feat for h in range(ATT_HEADS)]

    def to_cols(row):
        return jnp.stack([jnp.sum(jnp.where(m, row, 0.0), axis=1, keepdims=True) for m in picks])

    q = to_cols(q_ref[...]) * (ATT_HEAD_DIM ** -0.5)
    kn, vn = to_cols(kn_ref[...]), to_cols(vn_ref[...])
    s_all = jnp.sum(kt * q, axis=1, keepdims=True)
    s_new = jnp.sum(kn * q, axis=1, keepdims=True)
    dist = win - lax.broadcasted_iota(jnp.int32, (1, 1, win), 2)

    ps, pnews, lses = [], [], []
    for w, d in DILATED_PATTERNS:
        on_stride = (dist & (d - 1)) == 0 if d & (d - 1) == 0 else dist % d == 0
        valid = jnp.logical_and(dist <= w, on_stride)
        sm = jnp.where(valid, s_all, NEG)
        m = jnp.maximum(jnp.max(sm, axis=-1, keepdims=True), s_new)
        p = jnp.exp(sm - m)
        pn = jnp.exp(s_new - m)
        den = jnp.sum(p, axis=-1, keepdims=True) + pn
        ps.append(p / den)
        pnews.append(pn / den)
        lses.append(m + jnp.log(den))
    mx = jnp.maximum(jnp.maximum(lses[0], lses[1]), lses[2])
    ws = [jnp.exp(l - mx) for l in lses]
    wsum = ws[0] + ws[1] + ws[2]
    p_tot = (ws[0] * ps[0] + ws[1] * ps[1] + ws[2] * ps[2]) / wsum
    pn_tot = (ws[0] * pnews[0] + ws[1] * pnews[1] + ws[2] * pnews[2]) / wsum
    o = jnp.sum(vt * p_tot, axis=-1, keepdims=True) + pn_tot * vn
    o_ref[...] = sum(jnp.sum(jnp.where(m, o[h], 0.0), axis=0, keepdims=True) for h, m in enumerate(picks))

    last = lax.broadcasted_iota(jnp.int32, (1, 1, win), 2) == win - 1
    nk_ref[...] = jnp.where(last, kn, pltpu.roll(kt, win - 1, 2))
    nv_ref[...] = jnp.where(last, vn, pltpu.roll(vt, win - 1, 2))


def _attn_step(aq, ak, av, cache_k, cache_v):
    bsz, _, _, win = cache_k.shape
    one = pl.BlockSpec((None, 1, ATT_WIDTH), lambda b: (b, 0, 0))
    cache = pl.BlockSpec((None, ATT_HEADS, ATT_HEAD_DIM, win), lambda b: (b, 0, 0, 0))
    row = lambda a: a.reshape(bsz, 1, ATT_WIDTH)
    att, new_k, new_v = pl.pallas_call(
        _attn_step_kernel,
        grid=(bsz,),
        in_specs=[one, one, one, cache, cache],
        out_specs=[one, cache, cache],
        out_shape=[jax.ShapeDtypeStruct((bsz, 1, ATT_WIDTH), F32),
                   jax.ShapeDtypeStruct(cache_k.shape, F32),
                   jax.ShapeDtypeStruct(cache_v.shape, F32)],
        compiler_params=_cparams(("parallel",)),
        name="attn_step",
    )(row(aq), row(ak), row(av), cache_k, cache_v)
    return att.reshape(bsz, ATT_WIDTH), new_k, new_v


ROUTE_TM = 256
ROW_TILE = D_MODEL // LANES
assert ROW_TILE == SUBLANES


def _store_row_tiles(ref, val, lead=()):
    n = val.shape[0]
    for j in range(ROW_TILE):
        ref[lead + (pl.ds(j, n, stride=ROW_TILE), slice(None))] = val[:, j * LANES:(j + 1) * LANES]


def _load_row_tiles(ref, n, lead=()):
    return jnp.concatenate([ref[lead + (pl.ds(j, n, stride=ROW_TILE), slice(None))] for j in range(ROW_TILE)],
                           axis=1)


def _row_tile(ref, r, n=1, lead=()):
    start = r * ROW_TILE if isinstance(r, int) else pl.multiple_of(r * ROW_TILE, ROW_TILE)
    return ref.at[lead + (pl.ds(start, n * ROW_TILE),)]


def _outproj_kernel(hn_ref, att_ref, x_ref, hn_s_ref, att_s_ref, x_s_ref, w_ref, g_ref, wr_ref,
                    xmid_ref, h2_ref, route_ref, cnt_ref, *, n_sample):
    is_prompt = pl.program_id(0) < pl.num_programs(0) - 1
    hn = jnp.where(is_prompt, hn_ref[...], hn_s_ref[...])
    att = jnp.where(is_prompt, att_ref[...], att_s_ref[...])
    y = (jnp.dot(hn.astype(BF16), w_ref[0:HG_WIDTH, :], preferred_element_type=F32)
         + jnp.dot(att.astype(BF16), w_ref[HG_WIDTH:, :], preferred_element_type=F32))
    xm = jnp.where(is_prompt, x_ref[...], x_s_ref[...]) + y
    xmid_ref[...] = xm
    ms = jnp.mean(xm * xm, axis=-1, keepdims=True)
    h2 = xm * lax.rsqrt(ms + RMS_EPS) * g_ref[...]
    _store_row_tiles(h2_ref, h2)
    h2_hi = h2.astype(BF16)
    h2_lo = (h2 - h2_hi.astype(F32)).astype(BF16)
    lg = (jnp.dot(h2_hi, wr_ref[0], preferred_element_type=F32)
          + (jnp.dot(h2_lo, wr_ref[0], preferred_element_type=F32)
             + jnp.dot(h2_hi, wr_ref[1], preferred_element_type=F32)))
    lane = lax.broadcasted_iota(jnp.int32, lg.shape, 1).astype(F32)
    big = float(LANES)
    gmask = lane < N_GROUPS
    lgg = jnp.where(gmask, lg, NEG)
    mg = jnp.max(lgg, axis=-1, keepdims=True)
    gi = jnp.min(jnp.where(lgg == mg, lane, big), axis=-1, keepdims=True)
    p_grp = 1.0 / jnp.sum(jnp.exp(lgg - mg), axis=-1, keepdims=True)
    lo = N_GROUPS + gi * EXPERTS_PER_GROUP
    emask = jnp.logical_and(lane >= lo, lane < lo + EXPERTS_PER_GROUP)
    le1 = jnp.where(emask, lg, NEG)
    m1 = jnp.max(le1, axis=-1, keepdims=True)
    i1 = jnp.min(jnp.where(le1 == m1, lane, big), axis=-1, keepdims=True)
    le2 = jnp.where(lane == i1, NEG, le1)
    m2 = jnp.max(le2, axis=-1, keepdims=True)
    i2 = jnp.min(jnp.where(le2 == m2, lane, big), axis=-1, keepdims=True)
    r = jnp.exp(m2 - m1)
    g1 = p_grp / (1.0 + r)
    g2 = p_grp * r / (1.0 + r)
    e1, e2 = i1 - N_GROUPS, i2 - N_GROUPS
    route_ref[...] = jnp.where(lane == 0, e1,
                               jnp.where(lane == 1, e2,
                                         jnp.where(lane == 2, g1, jnp.where(lane == 3, g2, 0.0))))

    @pl.when(pl.program_id(0) == 0)
    def _zero():
        cnt_ref[...] = jnp.zeros_like(cnt_ref)

    rows = lax.broadcasted_iota(jnp.int32, lg.shape, 0)
    real = jnp.logical_or(is_prompt, rows < n_sample)
    hit = jnp.logical_and(real, jnp.logical_or(lane == e1, lane == e2))
    cnt_ref[...] += jnp.sum(jnp.where(hit, 1.0, 0.0), axis=0, keepdims=True)


def _outproj(hn_p, att_p, x_p, hn_s, att_s, x_s, w_out_bf, g_ffn, w_router):
    tm = ROUTE_TM
    n_p = x_p.shape[0]
    n_tiles = n_p // tm + 1
    pad = lambda a: jnp.pad(a, ((0, tm - a.shape[0]), (0, 0)))
    row_p = lambda i: (jnp.minimum(i, n_tiles - 2), 0)
    row = lambda i: (i, 0)
    const = lambda i: (0, 0)
    n_rows = n_tiles * tm
    return pl.pallas_call(
        functools.partial(_outproj_kernel, n_sample=x_s.shape[0]),
        grid=(n_tiles,),
        in_specs=[pl.BlockSpec((tm, HG_WIDTH), row_p),
                  pl.BlockSpec((tm, ATT_WIDTH), row_p),
                  pl.BlockSpec((tm, D_MODEL), row_p),
                  pl.BlockSpec((tm, HG_WIDTH), const),
                  pl.BlockSpec((tm, ATT_WIDTH), const),
                  pl.BlockSpec((tm, D_MODEL), const),
                  pl.BlockSpec((D_MODEL, D_MODEL), const),
                  pl.BlockSpec((1, D_MODEL), const),
                  pl.BlockSpec((2, D_MODEL, LANES), lambda i: (0, 0, 0))],
        out_specs=[pl.BlockSpec((tm, D_MODEL), row),
                   pl.BlockSpec((tm * ROW_TILE, LANES), row),
                   pl.BlockSpec((tm, LANES), row),
                   pl.BlockSpec((SUBLANES, LANES), const)],
        out_shape=[jax.ShapeDtypeStruct((n_rows, D_MODEL), F32),
                   jax.ShapeDtypeStruct((n_rows * ROW_TILE, LANES), F32),
                   jax.ShapeDtypeStruct((n_rows, LANES), F32),
                   jax.ShapeDtypeStruct((SUBLANES, LANES), F32)],
        compiler_params=_cparams(("arbitrary",)),
        name="outproj",
    )(hn_p, att_p, x_p, pad(hn_s), pad(att_s), pad(x_s), w_out_bf, g_ffn.reshape(1, D_MODEL), w_router)


def _rank_kernel(route_ref, cnt_ref, slot_ref, meta_ref, base_ref, *, n_tok, blk):
    i = pl.program_id(0)
    tm = route_ref.shape[0]
    lane = lax.broadcasted_iota(jnp.int32, (tm, LANES), 1).astype(F32)
    rowg = i * tm + lax.broadcasted_iota(jnp.int32, (tm, LANES), 0)
    valid = rowg < n_tok
    r = route_ref[...]
    oh0 = jnp.where(jnp.logical_and(valid, lane == r[:, 0:1]), 1.0, 0.0)
    oh1 = jnp.where(jnp.logical_and(valid, lane == r[:, 1:2]), 1.0, 0.0)
    oh = oh0 + oh1

    @pl.when(i == 0)
    def _starts():
        cnt = cnt_ref[0:1, :].astype(jnp.int32)
        shift = blk.bit_length() - 1
        padded = (((cnt + (blk - 1)) >> shift) << shift).astype(F32)
        up = (lax.broadcasted_iota(jnp.int32, (LANES, LANES), 0)
              < lax.broadcasted_iota(jnp.int32, (LANES, LANES), 1)).astype(F32)
        start = jnp.dot(jnp.broadcast_to(padded, (8, LANES)), up, precision=HIGHEST,
                        preferred_element_type=F32)[0:1]
        base_ref[...] = start
        nb = meta_ref.shape[0]
        lane_b = lax.broadcasted_iota(jnp.int32, (nb, LANES), 1)
        blk_start = (lax.broadcasted_iota(jnp.int32, (nb, LANES), 0) * blk).astype(F32)
        ended = jnp.logical_and(start + padded <= blk_start, lane_b < N_EXPERTS)
        be = jnp.minimum(jnp.sum(jnp.where(ended, 1.0, 0.0), axis=-1, keepdims=True), N_EXPERTS - 1.0)
        n_used = jnp.sum(padded, axis=-1, keepdims=True) * (1.0 / blk)
        meta_ref[...] = jnp.where(lane_b == 0, be, jnp.where(lane_b == 1, n_used, 0.0)).astype(jnp.int32)

    before = (lax.broadcasted_iota(jnp.int32, (tm, tm), 1)
              < lax.broadcasted_iota(jnp.int32, (tm, tm), 0)).astype(BF16)
    pre = jnp.dot(before, oh.astype(BF16), preferred_element_type=F32) + base_ref[...]
    s0 = jnp.sum(oh0 * pre, axis=-1, keepdims=True)
    s1 = jnp.sum(oh1 * pre, axis=-1, keepdims=True)
    slot_ref[...] = jnp.where(lane == 0, s0, jnp.where(lane == 1, s1, 0.0)).astype(jnp.int32)
    base_ref[...] += jnp.sum(oh, axis=0, keepdims=True)


def _rank(route, counts, n_tok, blk, nblk):
    n_rows = route.shape[0]
    nb = (nblk + 7) // 8 * 8
    slots, meta = pl.pallas_call(
        functools.partial(_rank_kernel, n_tok=n_tok, blk=blk),
        grid=(n_rows // ROUTE_TM,),
        in_specs=[pl.BlockSpec((ROUTE_TM, LANES), lambda i: (i, 0)),
                  pl.BlockSpec(counts.shape, lambda i: (0, 0))],
        out_specs=[pl.BlockSpec((ROUTE_TM, LANES), lambda i: (i, 0)),
                   pl.BlockSpec((nb, LANES), lambda i: (0, 0))],
        out_shape=[jax.ShapeDtypeStruct((n_rows, LANES), jnp.int32),
                   jax.ShapeDtypeStruct((nb, LANES), jnp.int32)],
        scratch_shapes=[pltpu.VMEM((1, LANES), F32)],
        compiler_params=_cparams(("arbitrary",)),
        name="rank",
    )(route, counts)
    return slots[:n_tok, :TOP_K_INNER].reshape(-1), meta[:nblk, 0], meta[0:1, 1]


DMA_UNROLL = 8


def _dispatch_kernel(slot_ref, be_ref, nu_ref, h2_ref, xb_out, zbuf, sem, zsem, *, n_tok):
    i = pl.program_id(0)
    tm = h2_ref.shape[0] // ROW_TILE
    tail = n_tok % tm

    @pl.when(i == 0)
    def _zero_padding():
        blk = zbuf.shape[0] // ROW_TILE
        n_blocks = be_ref.shape[0]
        zbuf[...] = jnp.zeros_like(zbuf)

        def ends_expert(j):
            nxt = be_ref[jnp.minimum(j + 1, n_blocks - 1)]
            return jnp.logical_or(j >= nu_ref[0] - 1, be_ref[j] != nxt)

        def issue(j, c):
            @pl.when(ends_expert(j))
            def _():
                pltpu.make_async_copy(zbuf, _row_tile(xb_out, j * blk, blk), zsem).start()
            return c

        def drain(j, c):
            @pl.when(ends_expert(j))
            def _():
                pltpu.make_async_copy(zbuf, _row_tile(xb_out, 0, blk), zsem).wait()
            return c

        lax.fori_loop(0, n_blocks, issue, 0)
        lax.fori_loop(0, n_blocks, drain, 0)

    def push(rows):
        def body(r, c):
            a = (i * tm + r) * TOP_K_INNER
            for k in range(TOP_K_INNER):
                pltpu.make_async_copy(_row_tile(h2_ref, r), _row_tile(xb_out, slot_ref[a + k]), sem).start(
                    priority=k % 2)
            return c
        lax.fori_loop(0, rows, body, 0, unroll=DMA_UNROLL)
        for k in range(TOP_K_INNER):
            pltpu.make_async_copy(_row_tile(h2_ref, 0, rows), _row_tile(xb_out, 0, rows), sem).wait()

    last = pl.num_programs(0) - 1
    if tail == 0:
        push(tm)
    else:
        @pl.when(i < last)
        def _full():
            push(tm)

        @pl.when(i == last)
        def _tail():
            push(tail)


def _dispatch(slot_flat, blk_expert, n_used, h2, n_tok, blk):
    n_tiles = h2.shape[0] // (ROUTE_TM * ROW_TILE)
    tm = ROUTE_TM * max(g for g in range(1, 9) if n_tiles % g == 0)
    n_slots = blk_expert.shape[0] * blk
    grid_spec = pltpu.PrefetchScalarGridSpec(
        num_scalar_prefetch=3,
        grid=(h2.shape[0] // (tm * ROW_TILE),),
        in_specs=[pl.BlockSpec((tm * ROW_TILE, LANES), lambda i, s, be, nu: (i, 0))],
        out_specs=pl.BlockSpec(memory_space=pl.ANY),
        scratch_shapes=[pltpu.VMEM((blk * ROW_TILE, LANES), F32),
                        pltpu.SemaphoreType.DMA(()),
                        pltpu.SemaphoreType.DMA(())],
    )
    return pl.pallas_call(
        functools.partial(_dispatch_kernel, n_tok=n_tok),
        grid_spec=grid_spec,
        out_shape=jax.ShapeDtypeStruct((n_slots * ROW_TILE, LANES), F32),
        compiler_params=_cparams(("arbitrary",)),
        name="dispatch",
    )(slot_flat, blk_expert, n_used, h2)


def _expert_kernel(be_ref, nu_ref, x_ref, wg_hbm, wu_hbm, wd_hbm, y_ref,
                   wg_f, wu_f, wd_f, sem, wgb, wub, wdb, cur_ref):
    i = pl.program_id(0)
    n_used = nu_ref[0]
    e = be_ref[i]
    e_prev = be_ref[jnp.maximum(i - 1, 0)]

    def fetch(ex, s):
        return [pltpu.make_async_copy(hbm.at[ex], buf.at[s], sem.at[s])
                for hbm, buf in ((wg_hbm, wg_f), (wu_hbm, wu_f), (wd_hbm, wd_f))]

    @pl.when(i == 0)
    def _first():
        cur_ref[0] = 0
        for c in fetch(e, 0):
            c.start()

    @pl.when(jnp.logical_and(jnp.logical_or(i == 0, e != e_prev), i < n_used))
    def _new_expert():
        s = cur_ref[0]
        j = lax.while_loop(lambda j: jnp.logical_and(j < n_used, be_ref[jnp.minimum(j, n_used - 1)] == e),
                           lambda j: j + 1, i + 1)

        @pl.when(j < n_used)
        def _prefetch():
            for c in fetch(be_ref[j], 1 - s):
                c.start()

        for c in fetch(e, s):
            c.wait()
        wgb[...] = wg_f[s].astype(BF16)
        wub[...] = wu_f[s].astype(BF16)
        wdb[...] = wd_f[s].astype(BF16)
        cur_ref[0] = 1 - s

    @pl.when(i < n_used)
    def _run():
        x = _load_row_tiles(x_ref, x_ref.shape[0] // ROW_TILE).astype(BF16)
        a = jnp.dot(x, wgb[...], preferred_element_type=F32)
        u = jnp.dot(x, wub[...], preferred_element_type=F32)
        mid = (a * _sigmoid(a) * u).astype(BF16)
        _store_row_tiles(y_ref, jnp.dot(mid, wdb[...], preferred_element_type=F32))

    @pl.when(i >= nu_ref[0])
    def _skip():
        y_ref[...] = jnp.zeros_like(y_ref)


def _experts(xb, blk_expert, n_used, w_g, w_u, w_d, blk):
    nblk = blk_expert.shape[0]
    hbm = pl.BlockSpec(memory_space=pl.ANY)
    grid_spec = pltpu.PrefetchScalarGridSpec(
        num_scalar_prefetch=2,
        grid=(nblk,),
        in_specs=[pl.BlockSpec((blk * ROW_TILE, LANES), lambda i, be, nu: (jnp.minimum(i, nu[0] - 1), 0)),
                  hbm, hbm, hbm],
        out_specs=pl.BlockSpec((blk * ROW_TILE, LANES), lambda i, be, nu: (i, 0)),
        scratch_shapes=[pltpu.VMEM((2, D_MODEL, D_FF_EXPERT), F32),
                        pltpu.VMEM((2, D_MODEL, D_FF_EXPERT), F32),
                        pltpu.VMEM((2, D_FF_EXPERT, D_MODEL), F32),
                        pltpu.SemaphoreType.DMA((2,)),
                        pltpu.VMEM((D_MODEL, D_FF_EXPERT), BF16),
                        pltpu.VMEM((D_MODEL, D_FF_EXPERT), BF16),
                        pltpu.VMEM((D_FF_EXPERT, D_MODEL), BF16),
                        pltpu.SMEM((1,), jnp.int32)],
    )
    return pl.pallas_call(
        _expert_kernel,
        grid_spec=grid_spec,
        out_shape=jax.ShapeDtypeStruct((nblk * blk * ROW_TILE, LANES), F32),
        compiler_params=_cparams(("arbitrary",)),
        name="experts",
    )(blk_expert, n_used, xb, w_g, w_u, w_d)


def _final_kernel(slot_ref, x_ref, route_ref, g_ref, yb_hbm, o_ref, ybuf, sem, *, tok0):
    i = pl.program_id(0)
    tm = x_ref.shape[0]

    def gather(j, buf):
        def body(r, c):
            a = (tok0 + j * tm + r) * TOP_K_INNER
            for k in range(TOP_K_INNER):
                pltpu.make_async_copy(_row_tile(yb_hbm, slot_ref[a + k]), _row_tile(ybuf, r, lead=(buf, k)),
                                      sem.at[buf]).start(priority=k % 2)
            return c
        lax.fori_loop(0, tm, body, 0, unroll=DMA_UNROLL)

    @pl.when(i == 0)
    def _first():
        gather(0, 0)

    @pl.when(i + 1 < pl.num_programs(0))
    def _next():
        gather(i + 1, (i + 1) % 2)

    buf = i % 2
    for k in range(TOP_K_INNER):
        pltpu.make_async_copy(_row_tile(yb_hbm, 0, tm), ybuf.at[buf, k], sem.at[buf]).wait()
    route = route_ref[...]
    y0 = _load_row_tiles(ybuf, tm, lead=(buf, 0))
    y1 = _load_row_tiles(ybuf, tm, lead=(buf, 1))
    x = x_ref[...] + (y0 * route[:, 2:3] + y1 * route[:, 3:4])
    ms = jnp.mean(x * x, axis=-1, keepdims=True)
    o_ref[...] = x * lax.rsqrt(ms + RMS_EPS) * g_ref[...]


def _final(slot_flat, xmid, route, g_final, yb, tok0, n_out, tm):
    blk0 = tok0 // tm
    grid_spec = pltpu.PrefetchScalarGridSpec(
        num_scalar_prefetch=1,
        grid=(n_out // tm,),
        in_specs=[pl.BlockSpec((tm, D_MODEL), lambda i, s: (i + blk0, 0)),
                  pl.BlockSpec((tm, LANES), lambda i, s: (i + blk0, 0)),
                  pl.BlockSpec((1, D_MODEL), lambda i, s: (0, 0)),
                  pl.BlockSpec(memory_space=pl.ANY)],
        out_specs=pl.BlockSpec((tm, D_MODEL), lambda i, s: (i, 0)),
        scratch_shapes=[pltpu.VMEM((2, TOP_K_INNER, tm * ROW_TILE, LANES), F32),
                        pltpu.SemaphoreType.DMA((2,))],
    )
    return pl.pallas_call(
        functools.partial(_final_kernel, tok0=tok0),
        grid_spec=grid_spec,
        out_shape=jax.ShapeDtypeStruct((n_out, D_MODEL), F32),
        compiler_params=_cparams(("arbitrary",)),
        name="final",
    )(slot_flat, xmid, route, g_final.reshape(1, D_MODEL), yb)


def kernel(x_prompt, x_sample, cache_attn_k, cache_attn_v, state_hgrn, w_in, w_out, hg_lb_logits,
           hg_norm_g, norm_mix_g, norm_ffn_g, norm_final_g, w_route_group, w_route_expert,
           w_expert_gate, w_expert_up, w_expert_down):
    bp, tp, _ = x_prompt.shape
    bs = x_sample.shape[0]
    l = 0
    w_in_bf = w_in[l].astype(BF16)
    w_out_bf = w_out[l].astype(BF16)
    w_router = jnp.concatenate(
        [w_route_group[l],
         jnp.transpose(w_route_expert[l], (1, 0, 2)).reshape(D_MODEL, N_EXPERTS),
         jnp.zeros((D_MODEL, LANES - N_GROUPS - N_EXPERTS), F32)], axis=-1)
    w_router_hi = w_router.astype(BF16)
    w_router = jnp.stack([w_router_hi, (w_router - w_router_hi.astype(F32)).astype(BF16)])

    n_p = bp * tp
    xp = x_prompt.reshape(n_p, D_MODEL)
    pos_p = jnp.arange(tp, dtype=jnp.int32)
    hq, hk, hv, lf, zg, aq, ak, av = _inproj(xp, norm_mix_g[l], w_in_bf, hg_lb_logits, pos_p, 256)
    seq3 = lambda a: a.reshape(bp, tp, HG_WIDTH)
    hn_p, s_fin = _hgrn_prompt(seq3(hq), seq3(hk), seq3(hv), seq3(lf), seq3(zg), hg_norm_g[l])
    att_p = _attn_prompt(seq3(aq), seq3(ak), seq3(av))
    keep = min(MAX_WINDOW, tp)
    heads = lambda a: a.reshape(1, bp, keep, ATT_HEADS, ATT_HEAD_DIM)
    new_k_p = heads(seq3(ak)[:, tp - keep:])
    new_v_p = heads(seq3(av)[:, tp - keep:])

    xs = x_sample.reshape(bs, D_MODEL)
    pos_s = jnp.full((bs,), PAST_LEN, jnp.int32)
    hq, hk, hv, lf, zg, aq, ak, av = _inproj(xs, norm_mix_g[l], w_in_bf, hg_lb_logits, pos_s, bs)
    hn_s, s_new = _hgrn_step(hq, hk, hv, lf, zg, hg_norm_g[l], state_hgrn[l])
    feat = lambda a: jnp.transpose(a, (0, 2, 3, 1))
    att_s, new_k_s, new_v_s = _attn_step(aq, ak, av, feat(cache_attn_k[l]), feat(cache_attn_v[l]))
    cache5 = lambda a: jnp.transpose(a, (0, 3, 1, 2))[None]

    assert n_p % ROUTE_TM == 0 and bs <= ROUTE_TM
    n_tok = n_p + bs
    xmid, h2, route, counts = _outproj(hn_p.reshape(n_p, HG_WIDTH), att_p.reshape(n_p, ATT_WIDTH), xp,
                                       hn_s, att_s, xs, w_out_bf, norm_ffn_g[l], w_router)
    blk = MOE_BLOCK
    nblk = (n_tok * TOP_K_INNER + N_EXPERTS * (blk - 1)) // blk + 1
    slot_flat, blk_expert, n_used = _rank(route, counts, n_tok, blk, nblk)
    xb = _dispatch(slot_flat, blk_expert, n_used, h2, n_tok, blk)
    yb = _experts(xb, blk_expert, n_used, w_expert_gate[l], w_expert_up[l], w_expert_down[l], blk)
    y_prompt = _final(slot_flat, xmid, route, norm_final_g, yb, 0, n_p, 256)
    y_sample = _final(slot_flat, xmid, route, norm_final_g, yb, n_p, bs, bs)

    return (y_prompt.reshape(bp, tp, D_MODEL), y_sample.reshape(bs, 1, D_MODEL),
            new_k_p, new_v_p, s_fin[None], cache5(new_k_s), cache5(new_v_s), s_new[None])
```

```python
import functools

import jax
import jax.numpy as jnp
from jax import lax
from jax.experimental import pallas as pl
from jax.experimental.pallas import tpu as pltpu

F32 = jnp.float32
BF16 = jnp.bfloat16

D_MODEL = 1024
HG_WIDTH = 512
HG_HEAD_DIM = 128
HG_HEADS = 4
ATT_WIDTH = 512
ATT_HEAD_DIM = 64
ATT_HEADS = 8
ROPE_DIM = 16
ROPE_THETA = 500000.0
DILATED_PATTERNS = ((128, 1), (512, 4), (2048, 16))
MAX_WINDOW = 2048
PAST_LEN = 16384
N_GROUPS = 8
EXPERTS_PER_GROUP = 8
N_EXPERTS = 64
TOP_K_INNER = 2
D_FF_EXPERT = 512
MOE_BLOCK = 256
IN_COLS = 4 * HG_WIDTH + 3 * ATT_WIDTH
RMS_EPS = 1e-6

LANES = 128
SUBLANES = 8
VMEM_LIMIT = 56 * 1024 * 1024
NEG = -1e30
HIGHEST = lax.Precision.HIGHEST
NT_DIMS = (((1,), (1,)), ((), ()))


def _sigmoid(z):
    return 1.0 / (1.0 + jnp.exp(-z))


def _cparams(sem):
    return pltpu.CompilerParams(dimension_semantics=sem, vmem_limit_bytes=VMEM_LIMIT)


def _inproj_kernel(x_ref, g_ref, w_ref, lbl_ref, cos_ref, sa_ref, sb_ref,
                   hq_ref, hk_ref, hv_ref, lf_ref, zg_ref, aq_ref, ak_ref, av_ref):
    x = x_ref[...]
    ms = jnp.mean(x * x, axis=-1, keepdims=True)
    h = (x * lax.rsqrt(ms + RMS_EPS) * g_ref[...]).astype(BF16)

    def mm(c0):
        return jnp.dot(h, w_ref[:, c0:c0 + HG_WIDTH], preferred_element_type=F32)

    lbl = lbl_ref[...]
    le = jnp.exp(lbl - jnp.max(lbl, axis=0, keepdims=True))
    lb = le[0:1, :] / jnp.sum(le, axis=0, keepdims=True)

    zq = mm(0)
    hq_ref[...] = zq * _sigmoid(zq)
    zf = mm(HG_WIDTH)
    f = lb + (1.0 - lb) * _sigmoid(zf)
    hk_ref[...] = 1.0 - f
    lf_ref[...] = jnp.log(f)
    hv_ref[...] = mm(2 * HG_WIDTH)
    zg_ref[...] = mm(3 * HG_WIDTH)

    cos, sa, sb = cos_ref[...], sa_ref[...], sb_ref[...]

    def rope(a, out_ref):
        for j in range(ATT_WIDTH // LANES):
            xj = a[:, j * LANES:(j + 1) * LANES]
            up = pltpu.roll(xj, LANES - ROPE_DIM // 2, 1)
            dn = pltpu.roll(xj, ROPE_DIM // 2, 1)
            out_ref[:, j * LANES:(j + 1) * LANES] = xj * cos + up * sa + dn * sb

    rope(mm(4 * HG_WIDTH), aq_ref)
    rope(mm(4 * HG_WIDTH + ATT_WIDTH), ak_ref)
    av_ref[...] = mm(4 * HG_WIDTH + 2 * ATT_WIDTH)


def _rope_tables(pos):
    half = ROPE_DIM // 2
    c = jnp.arange(LANES) % ATT_HEAD_DIM
    inv_freq = ROPE_THETA ** (-(c % half).astype(F32) / half)
    ang = pos.astype(F32)[:, None] * inv_freq[None, :]
    cos, sin = jnp.cos(ang), jnp.sin(ang)
    return (jnp.where(c < ROPE_DIM, cos, 1.0),
            jnp.where(c < half, -sin, 0.0),
            jnp.where(jnp.logical_and(c >= half, c < ROPE_DIM), sin, 0.0))


def _inproj(x2d, g, w_bf, lb_logits, pos, tm):
    m = x2d.shape[0]
    cos, sa, sb = _rope_tables(pos)
    row = lambda i: (i, 0)
    seq_tiles = pos.shape[0] // tm
    row_pos = lambda i: (i % seq_tiles, 0)
    const = lambda i: (0, 0)
    outs = [jax.ShapeDtypeStruct((m, HG_WIDTH), F32)] * 8
    return pl.pallas_call(
        _inproj_kernel,
        grid=(m // tm,),
        in_specs=[pl.BlockSpec((tm, D_MODEL), row),
                  pl.BlockSpec((1, D_MODEL), const),
                  pl.BlockSpec((D_MODEL, IN_COLS), const),
                  pl.BlockSpec(lb_logits.shape, const),
                  pl.BlockSpec((tm, LANES), row_pos),
                  pl.BlockSpec((tm, LANES), row_pos),
                  pl.BlockSpec((tm, LANES), row_pos)],
        out_specs=[pl.BlockSpec((tm, HG_WIDTH), row)] * 8,
        out_shape=outs,
        compiler_params=_cparams(("parallel",)),
        name="inproj",
    )(x2d, g.reshape(1, D_MODEL), w_bf, lb_logits, cos, sa, sb)


HG_C = 128
HG_SB = 16


def _hgrn_kernel(q_ref, k_ref, v_ref, lf_ref, zg_ref, g_ref, hn_ref, sfin_ref, st_ref, *, n_chunks):
    t = pl.program_id(2)

    @pl.when(t == 0)
    def _init():
        st_ref[...] = jnp.zeros_like(st_ref)

    ri = lax.broadcasted_iota(jnp.int32, (HG_C, HG_C), 0)
    ci = lax.broadcasted_iota(jnp.int32, (HG_C, HG_C), 1)
    ltri = (ri >= ci).astype(BF16)
    ones_b = jnp.ones((LANES, LANES), BF16)
    n_sb = HG_C // HG_SB
    row_sb = lax.broadcasted_iota(jnp.int32, (n_sb, HG_SB, LANES), 1)
    col_sb = lax.broadcasted_iota(jnp.int32, (n_sb, HG_SB, HG_C), 2)
    lo_sb = lax.broadcasted_iota(jnp.int32, (n_sb, HG_SB, HG_C), 0) * HG_SB
    g = g_ref[...]

    def chunk(c, carry):
        r0 = pl.multiple_of(c * HG_C, HG_C)
        q = q_ref[pl.ds(r0, HG_C), :]
        k = k_ref[pl.ds(r0, HG_C), :]
        v = v_ref[pl.ds(r0, HG_C), :]
        lf = lf_ref[pl.ds(r0, HG_C), :]
        lf_hi = lf.astype(BF16)
        lf_r = lf - lf_hi.astype(F32)
        lf_mid = lf_r.astype(BF16)
        lf_lo = (lf_r - lf_mid.astype(F32)).astype(BF16)
        b = (jnp.dot(ltri, lf_hi, preferred_element_type=F32)
             + (jnp.dot(ltri, lf_mid, preferred_element_type=F32)
                + jnp.dot(ltri, lf_lo, preferred_element_type=F32)))
        st = st_ref[...]
        vb = v.astype(BF16)
        qb = (q * jnp.exp(b)).astype(BF16)
        o_inter = lax.dot_general(qb, st.astype(BF16), NT_DIMS, preferred_element_type=F32)
        b3, q3, v3 = (a.reshape(n_sb, HG_SB, LANES) for a in (b, q, v))
        bk = b - jnp.log(k)
        bk3 = bk.reshape(n_sb, HG_SB, LANES)
        ps, t_lo, offs = [], [], [0]
        for s in range(HG_SB):
            lo = (s // SUBLANES) * SUBLANES
            d = jnp.where(row_sb[:, lo:] >= s, b3[:, lo:] - bk3[:, s:s + 1, :], NEG)
            ps.append(q3[:, lo:] * jnp.exp(d))
            t_lo.append(lo)
            offs.append(offs[-1] + HG_SB - lo)
        n_rows = offs[-1]
        p_all = jnp.concatenate(ps, axis=1).reshape(n_sb * n_rows, LANES).astype(BF16)
        r_all = jnp.dot(p_all, ones_b, preferred_element_type=F32)
        r_all = r_all.reshape(n_sb, n_rows, LANES)
        o3 = o_inter.reshape(n_sb, HG_SB, LANES)
        tiles = [o3[:, j * SUBLANES:(j + 1) * SUBLANES] for j in range(HG_SB // SUBLANES)]
        for s in range(HG_SB):
            for j in range(t_lo[s] // SUBLANES, HG_SB // SUBLANES):
                r0_ = offs[s] + j * SUBLANES - t_lo[s]
                tiles[j] = tiles[j] + r_all[:, r0_:r0_ + SUBLANES, :] * v3[:, s:s + 1, :]
        o3 = jnp.concatenate(tiles, axis=1)
        b_ref = jnp.concatenate([b3[0:1, 0:1], b3[:n_sb - 1, HG_SB - 1:HG_SB]], axis=0)
        qs = (q3 * jnp.exp(jnp.minimum(b3 - b_ref, 0.0))).astype(BF16)
        ks = jnp.exp(jnp.minimum(b_ref - bk[None], 0.0)).astype(BF16)
        a = lax.dot_general(qs, ks, (((2,), (2,)), ((0,), (0,))), preferred_element_type=F32)
        a = jnp.where(col_sb < lo_sb, a, 0.0).astype(BF16).reshape(HG_C, HG_C)
        o = o3.reshape(HG_C, LANES) + jnp.dot(a, vb, preferred_element_type=F32)
        b_last = b[HG_C - 1:HG_C, :]
        kdec = jnp.exp(b_last - bk).astype(BF16)
        st_ref[...] = st * jnp.exp(b_last) + jnp.dot(v.T.astype(BF16), kdec, preferred_element_type=F32)
        ms = jnp.mean(o * o, axis=-1, keepdims=True)
        zg = zg_ref[pl.ds(r0, HG_C), :]
        hn_ref[pl.ds(r0, HG_C), :] = o * lax.rsqrt(ms + RMS_EPS) * g * (zg * _sigmoid(zg))
        return carry

    lax.fori_loop(0, n_chunks, chunk, 0, unroll=8)

    @pl.when(t == pl.num_programs(2) - 1)
    def _fin():
        sfin_ref[...] = st_ref[...].T


def _hgrn_prompt(hq, hk, hv, lf, zg, g_hg, tb=1024):
    bsz, t, _ = hq.shape
    seq = pl.BlockSpec((None, tb, HG_HEAD_DIM), lambda b, h, i: (b, i, h))
    return pl.pallas_call(
        functools.partial(_hgrn_kernel, n_chunks=tb // HG_C),
        grid=(bsz, HG_HEADS, t // tb),
        in_specs=[seq, seq, seq, seq, seq,
                  pl.BlockSpec((1, HG_HEAD_DIM), lambda b, h, i: (0, h))],
        out_specs=[seq,
                   pl.BlockSpec((None, None, HG_HEAD_DIM, HG_HEAD_DIM), lambda b, h, i: (b, h, 0, 0))],
        out_shape=[jax.ShapeDtypeStruct((bsz, t, HG_WIDTH), F32),
                   jax.ShapeDtypeStruct((bsz, HG_HEADS, HG_HEAD_DIM, HG_HEAD_DIM), F32)],
        scratch_shapes=[pltpu.VMEM((HG_HEAD_DIM, HG_HEAD_DIM), F32)],
        compiler_params=_cparams(("parallel", "parallel", "arbitrary")),
        name="hgrn_prompt",
    )(hq, hk, hv, lf, zg, g_hg.reshape(1, HG_WIDTH))


def _hgrn_step_kernel(q_ref, k_ref, v_ref, lf_ref, zg_ref, g_ref, s_ref, hn_ref, snew_ref):
    row = slice(None)
    zeros = jnp.zeros((HG_HEAD_DIM - 3, HG_HEAD_DIM), F32)
    for h in range(HG_HEADS):
        cs = slice(h * HG_HEAD_DIM, (h + 1) * HG_HEAD_DIM)
        q, k, v = q_ref[row, cs], k_ref[row, cs], v_ref[row, cs]
        f = jnp.exp(lf_ref[row, cs])
        cols = jnp.concatenate([f, k, q, zeros], axis=0).T
        s_new = cols[:, 0:1] * s_ref[h] + cols[:, 1:2] * v
        snew_ref[h] = s_new
        o = jnp.sum(cols[:, 2:3] * s_new, axis=0, keepdims=True)
        ms = jnp.mean(o * o, axis=-1, keepdims=True)
        zg = zg_ref[row, cs]
        hn_ref[row, cs] = o * lax.rsqrt(ms + RMS_EPS) * g_ref[:, cs] * (zg * _sigmoid(zg))


def _hgrn_step(hq, hk, hv, lf, zg, g_hg, state):
    bsz = hq.shape[0]
    one = pl.BlockSpec((None, 1, HG_WIDTH), lambda b: (b, 0, 0))
    st = pl.BlockSpec((None, HG_HEADS, HG_HEAD_DIM, HG_HEAD_DIM), lambda b: (b, 0, 0, 0))
    r3 = lambda a: a.reshape(bsz, 1, HG_WIDTH)
    hn, s_new = pl.pallas_call(
        _hgrn_step_kernel,
        grid=(bsz,),
        in_specs=[one, one, one, one, one, pl.BlockSpec((1, HG_WIDTH), lambda b: (0, 0)), st],
        out_specs=[one, st],
        out_shape=[jax.ShapeDtypeStruct((bsz, 1, HG_WIDTH), F32),
                   jax.ShapeDtypeStruct(state.shape, F32)],
        compiler_params=_cparams(("parallel",)),
        name="hgrn_step",
    )(r3(hq), r3(hk), r3(hv), r3(lf), r3(zg), g_hg.reshape(1, HG_WIDTH), state)
    return hn.reshape(bsz, HG_WIDTH), s_new


ATT_N = 128
ATT_SUPER = 2048
ATT_G = 8


def _attn_prompt_kernel(q_ref, k_ref, v_ref, o_ref, osc, lsc, *, seq_len):
    lane = lax.broadcasted_iota(jnp.int32, (ATT_N, LANES), 1)
    head0 = lane < ATT_HEAD_DIM
    kidx = lax.broadcasted_iota(jnp.int32, (ATT_N, 2 * ATT_N), 1)
    qidx = lax.broadcasted_iota(jnp.int32, (ATT_N, 2 * ATT_N), 0)
    band = jnp.logical_and(kidx >= qidx, kidx <= qidx + ATT_N)
    in_prev = kidx < ATT_N
    scale = ATT_HEAD_DIM ** -0.5

    bqk = (((2,), (2,)), ((0,), (0,)))
    bkd = (((2,), (1,)), ((0,), (0,)))

    def do_group(p, d, base, g, gsz):
        span = ATT_N * d
        rows_g = gsz * ATT_N
        if d == 1:
            off = g * rows_g
            start = base + off

            def cur(ref):
                return ref[pl.ds(start, rows_g), :].reshape(gsz, ATT_N, LANES)

            def prv(ref, c):
                before = ref[pl.ds(jnp.maximum(start - ATT_N, 0), ATT_N), :].astype(BF16)
                return jnp.concatenate([before[None], c[:gsz - 1]], axis=0)

            bidx = lax.broadcasted_iota(jnp.int32, (gsz, ATT_N, 2 * ATT_N), 0)
            seen = jnp.logical_and(band, jnp.logical_not(
                jnp.logical_and(jnp.logical_and(bidx == 0, in_prev), start == 0)))
        else:
            per_blk = d // gsz
            off = (g // per_blk) * span + (g % per_blk) * gsz
            start = base + off
            prev = jnp.maximum(start - span, 0)

            def cur(ref):
                return jnp.stack([ref[pl.ds(start + r, ATT_N, stride=d), :] for r in range(gsz)])

            def prv(ref, c):
                return jnp.stack([ref[pl.ds(prev + r, ATT_N, stride=d), :] for r in range(gsz)]).astype(BF16)

            seen = jnp.logical_and(band, jnp.logical_not(jnp.logical_and(in_prev, start < span)))
        q = cur(q_ref) * scale
        kc = cur(k_ref).astype(BF16)
        vc = cur(v_ref).astype(BF16)
        kk = jnp.concatenate([prv(k_ref, kc), kc], axis=1)
        vv = jnp.concatenate([prv(v_ref, vc), vc], axis=1)
        res = []
        for hm in (head0, jnp.logical_not(head0)):
            qh = jnp.where(hm, q, 0.0).astype(BF16)
            s = lax.dot_general(qh, kk, bqk, preferred_element_type=F32)
            s = jnp.where(seen, s, NEG)
            m = jnp.max(s, axis=-1, keepdims=True)
            pr = jnp.exp(s - m)
            den = jnp.sum(pr, axis=-1, keepdims=True)
            o = lax.dot_general(pr.astype(BF16), vv, bkd, preferred_element_type=F32)
            res.append((o / den, m + jnp.log(den)))
        o = jnp.where(head0, res[0][0], res[1][0])
        lse = jnp.where(head0, res[0][1], res[1][1])
        if d == 1:
            osc[p, pl.ds(off, rows_g), :] = o.reshape(rows_g, LANES)
            lsc[p, pl.ds(off, rows_g), :] = lse.reshape(rows_g, LANES)
        else:
            for r in range(gsz):
                osc[p, pl.ds(off + r, ATT_N, stride=d), :] = o[r]
                lsc[p, pl.ds(off + r, ATT_N, stride=d), :] = lse[r]

    def superblock(sb, carry):
        base = sb * ATT_SUPER

        def groups(it, c2):
            for p, (w, d) in enumerate(DILATED_PATTERNS):
                gsz = ATT_G if d == 1 else min(ATT_G, d)
                for sub in range(ATT_G // gsz):
                    do_group(p, d, base, it * (ATT_G // gsz) + sub, gsz)
            return c2

        lax.fori_loop(0, ATT_SUPER // (ATT_G * ATT_N), groups, 0)

        piece = 256

        def merge(j, c2):
            r = pl.ds(pl.multiple_of(j * piece, piece), piece)
            ls = [lsc[p, r, :] for p in range(len(DILATED_PATTERNS))]
            mx = jnp.maximum(jnp.maximum(ls[0], ls[1]), ls[2])
            ws = [jnp.exp(l - mx) for l in ls]
            num = ws[0] * osc[0, r, :] + ws[1] * osc[1, r, :] + ws[2] * osc[2, r, :]
            o_ref[pl.ds(pl.multiple_of(base + j * piece, piece), piece), :] = num / (ws[0] + ws[1] + ws[2])
            return c2

        lax.fori_loop(0, ATT_SUPER // piece, merge, 0)
        return carry

    lax.fori_loop(0, seq_len // ATT_SUPER, superblock, 0)


def _attn_prompt(aq, ak, av):
    bsz, t, _ = aq.shape
    spec = pl.BlockSpec((None, t, LANES), lambda b, p: (b, 0, p))
    n_pat = len(DILATED_PATTERNS)
    return pl.pallas_call(
        functools.partial(_attn_prompt_kernel, seq_len=t),
        grid=(bsz, ATT_WIDTH // LANES),
        in_specs=[spec, spec, spec],
        out_specs=spec,
        out_shape=jax.ShapeDtypeStruct((bsz, t, ATT_WIDTH), F32),
        scratch_shapes=[pltpu.VMEM((n_pat, ATT_SUPER, LANES), F32),
                        pltpu.VMEM((n_pat, ATT_SUPER, LANES), F32)],
        compiler_params=_cparams(("parallel", "parallel")),
        name="attn_prompt",
    )(aq, ak, av)


def _attn_step_kernel(q_ref, kn_ref, vn_ref, ck_ref, cv_ref, o_ref, nk_ref, nv_ref):
    win = ck_ref.shape[-1]
    kt, vt = ck_ref[...], cv_ref[...]
    chan = lax.broadcasted_iota(jnp.int32, (ATT_HEAD_DIM, ATT_WIDTH), 1)
    feat = lax.broadcasted_iota(jnp.int32, (ATT_HEAD_DIM, ATT_WIDTH), 0)
    picks = [chan == h * ATT_HEAD_DIM + feat for h in range(ATT_HEADS)]

    def to_cols(row):
        return jnp.stack([jnp.sum(jnp.where(m, row, 0.0), axis=1, keepdims=True) for m in picks])

    q = to_cols(q_ref[...]) * (ATT_HEAD_DIM ** -0.5)
    kn, vn = to_cols(kn_ref[...]), to_cols(vn_ref[...])
    s_all = jnp.sum(kt * q, axis=1, keepdims=True)
    s_new = jnp.sum(kn * q, axis=1, keepdims=True)
    dist = win - lax.broadcasted_iota(jnp.int32, (1, 1, win), 2)

    ps, pnews, lses = [], [], []
    for w, d in DILATED_PATTERNS:
        on_stride = (dist & (d - 1)) == 0 if d & (d - 1) == 0 else dist % d == 0
        valid = jnp.logical_and(dist <= w, on_stride)
        sm = jnp.where(valid, s_all, NEG)
        m = jnp.maximum(jnp.max(sm, axis=-1, keepdims=True), s_new)
        p = jnp.exp(sm - m)
        pn = jnp.exp(s_new - m)
        den = jnp.sum(p, axis=-1, keepdims=True) + pn
        ps.append(p / den)
        pnews.append(pn / den)
        lses.append(m + jnp.log(den))
    mx = jnp.maximum(jnp.maximum(lses[0], lses[1]), lses[2])
    ws = [jnp.exp(l - mx) for l in lses]
    wsum = ws[0] + ws[1] + ws[2]
    p_tot = (ws[0] * ps[0] + ws[1] * ps[1] + ws[2] * ps[2]) / wsum
    pn_tot = (ws[0] * pnews[0] + ws[1] * pnews[1] + ws[2] * pnews[2]) / wsum
    o = jnp.sum(vt * p_tot, axis=-1, keepdims=True) + pn_tot * vn
    o_ref[...] = sum(jnp.sum(jnp.where(m, o[h], 0.0), axis=0, keepdims=True) for h, m in enumerate(picks))

    last = lax.broadcasted_iota(jnp.int32, (1, 1, win), 2) == win - 1
    nk_ref[...] = jnp.where(last, kn, pltpu.roll(kt, win - 1, 2))
    nv_ref[...] = jnp.where(last, vn, pltpu.roll(vt, win - 1, 2))


def _attn_step(aq, ak, av, cache_k, cache_v):
    bsz, _, _, win = cache_k.shape
    one = pl.BlockSpec((None, 1, ATT_WIDTH), lambda b: (b, 0, 0))
    cache = pl.BlockSpec((None, ATT_HEADS, ATT_HEAD_DIM, win), lambda b: (b, 0, 0, 0))
    row = lambda a: a.reshape(bsz, 1, ATT_WIDTH)
    att, new_k, new_v = pl.pallas_call(
        _attn_step_kernel,
        grid=(bsz,),
        in_specs=[one, one, one, cache, cache],
        out_specs=[one, cache, cache],
        out_shape=[jax.ShapeDtypeStruct((bsz, 1, ATT_WIDTH), F32),
                   jax.ShapeDtypeStruct(cache_k.shape, F32),
                   jax.ShapeDtypeStruct(cache_v.shape, F32)],
        compiler_params=_cparams(("parallel",)),
        name="attn_step",
    )(row(aq), row(ak), row(av), cache_k, cache_v)
    return att.reshape(bsz, ATT_WIDTH), new_k, new_v


ROUTE_TM = 256
ROW_TILE = D_MODEL // LANES
assert ROW_TILE == SUBLANES


def _store_row_tiles(ref, val, lead=()):
    n = val.shape[0]
    for j in range(ROW_TILE):
        ref[lead + (pl.ds(j, n, stride=ROW_TILE), slice(None))] = val[:, j * LANES:(j + 1) * LANES]


def _load_row_tiles(ref, n, lead=()):
    return jnp.concatenate([ref[lead + (pl.ds(j, n, stride=ROW_TILE), slice(None))] for j in range(ROW_TILE)],
                           axis=1)


def _row_tile(ref, r, n=1, lead=()):
    start = r * ROW_TILE if isinstance(r, int) else pl.multiple_of(r * ROW_TILE, ROW_TILE)
    return ref.at[lead + (pl.ds(start, n * ROW_TILE),)]


def _outproj_kernel(hn_ref, att_ref, x_ref, hn_s_ref, att_s_ref, x_s_ref, w_ref, g_ref, wr_ref,
                    xmid_ref, h2_ref, route_ref, cnt_ref, *, n_sample):
    is_prompt = pl.program_id(0) < pl.num_programs(0) - 1
    hn = jnp.where(is_prompt, hn_ref[...], hn_s_ref[...])
    att = jnp.where(is_prompt, att_ref[...], att_s_ref[...])
    y = (jnp.dot(hn.astype(BF16), w_ref[0:HG_WIDTH, :], preferred_element_type=F32)
         + jnp.dot(att.astype(BF16), w_ref[HG_WIDTH:, :], preferred_element_type=F32))
    xm = jnp.where(is_prompt, x_ref[...], x_s_ref[...]) + y
    xmid_ref[...] = xm
    ms = jnp.mean(xm * xm, axis=-1, keepdims=True)
    h2 = xm * lax.rsqrt(ms + RMS_EPS) * g_ref[...]
    _store_row_tiles(h2_ref, h2)
    h2_hi = h2.astype(BF16)
    h2_lo = (h2 - h2_hi.astype(F32)).astype(BF16)
    lg = (jnp.dot(h2_hi, wr_ref[0], preferred_element_type=F32)
          + (jnp.dot(h2_lo, wr_ref[0], preferred_element_type=F32)
             + jnp.dot(h2_hi, wr_ref[1], preferred_element_type=F32)))
    lane = lax.broadcasted_iota(jnp.int32, lg.shape, 1).astype(F32)
    big = float(LANES)
    gmask = lane < N_GROUPS
    lgg = jnp.where(gmask, lg, NEG)
    mg = jnp.max(lgg, axis=-1, keepdims=True)
    gi = jnp.min(jnp.where(lgg == mg, lane, big), axis=-1, keepdims=True)
    p_grp = 1.0 / jnp.sum(jnp.exp(lgg - mg), axis=-1, keepdims=True)
    lo = N_GROUPS + gi * EXPERTS_PER_GROUP
    emask = jnp.logical_and(lane >= lo, lane < lo + EXPERTS_PER_GROUP)
    le1 = jnp.where(emask, lg, NEG)
    m1 = jnp.max(le1, axis=-1, keepdims=True)
    i1 = jnp.min(jnp.where(le1 == m1, lane, big), axis=-1, keepdims=True)
    le2 = jnp.where(lane == i1, NEG, le1)
    m2 = jnp.max(le2, axis=-1, keepdims=True)
    i2 = jnp.min(jnp.where(le2 == m2, lane, big), axis=-1, keepdims=True)
    r = jnp.exp(m2 - m1)
    g1 = p_grp / (1.0 + r)
    g2 = p_grp * r / (1.0 + r)
    e1, e2 = i1 - N_GROUPS, i2 - N_GROUPS
    route_ref[...] = jnp.where(lane == 0, e1,
                               jnp.where(lane == 1, e2,
                                         jnp.where(lane == 2, g1, jnp.where(lane == 3, g2, 0.0))))

    @pl.when(pl.program_id(0) == 0)
    def _zero():
        cnt_ref[...] = jnp.zeros_like(cnt_ref)

    rows = lax.broadcasted_iota(jnp.int32, lg.shape, 0)
    real = jnp.logical_or(is_prompt, rows < n_sample)
    hit = jnp.logical_and(real, jnp.logical_or(lane == e1, lane == e2))
    cnt_ref[...] += jnp.sum(jnp.where(hit, 1.0, 0.0), axis=0, keepdims=True)


def _outproj(hn_p, att_p, x_p, hn_s, att_s, x_s, w_out_bf, g_ffn, w_router):
    tm = ROUTE_TM
    n_p = x_p.shape[0]
    n_tiles = n_p // tm + 1
    pad = lambda a: jnp.pad(a, ((0, tm - a.shape[0]), (0, 0)))
    row_p = lambda i: (jnp.minimum(i, n_tiles - 2), 0)
    row = lambda i: (i, 0)
    const = lambda i: (0, 0)
    n_rows = n_tiles * tm
    return pl.pallas_call(
        functools.partial(_outproj_kernel, n_sample=x_s.shape[0]),
        grid=(n_tiles,),
        in_specs=[pl.BlockSpec((tm, HG_WIDTH), row_p),
                  pl.BlockSpec((tm, ATT_WIDTH), row_p),
                  pl.BlockSpec((tm, D_MODEL), row_p),
                  pl.BlockSpec((tm, HG_WIDTH), const),
                  pl.BlockSpec((tm, ATT_WIDTH), const),
                  pl.BlockSpec((tm, D_MODEL), const),
                  pl.BlockSpec((D_MODEL, D_MODEL), const),
                  pl.BlockSpec((1, D_MODEL), const),
                  pl.BlockSpec((2, D_MODEL, LANES), lambda i: (0, 0, 0))],
        out_specs=[pl.BlockSpec((tm, D_MODEL), row),
                   pl.BlockSpec((tm * ROW_TILE, LANES), row),
                   pl.BlockSpec((tm, LANES), row),
                   pl.BlockSpec((SUBLANES, LANES), const)],
        out_shape=[jax.ShapeDtypeStruct((n_rows, D_MODEL), F32),
                   jax.ShapeDtypeStruct((n_rows * ROW_TILE, LANES), F32),
                   jax.ShapeDtypeStruct((n_rows, LANES), F32),
                   jax.ShapeDtypeStruct((SUBLANES, LANES), F32)],
        compiler_params=_cparams(("arbitrary",)),
        name="outproj",
    )(hn_p, att_p, x_p, pad(hn_s), pad(att_s), pad(x_s), w_out_bf, g_ffn.reshape(1, D_MODEL), w_router)


def _rank_kernel(route_ref, cnt_ref, slot_ref, meta_ref, base_ref, *, n_tok, blk):
    i = pl.program_id(0)
    tm = route_ref.shape[0]
    lane = lax.broadcasted_iota(jnp.int32, (tm, LANES), 1).astype(F32)
    rowg = i * tm + lax.broadcasted_iota(jnp.int32, (tm, LANES), 0)
    valid = rowg < n_tok
    r = route_ref[...]
    oh0 = jnp.where(jnp.logical_and(valid, lane == r[:, 0:1]), 1.0, 0.0)
    oh1 = jnp.where(jnp.logical_and(valid, lane == r[:, 1:2]), 1.0, 0.0)
    oh = oh0 + oh1

    @pl.when(i == 0)
    def _starts():
        cnt = cnt_ref[0:1, :].astype(jnp.int32)
        shift = blk.bit_length() - 1
        padded = (((cnt + (blk - 1)) >> shift) << shift).astype(F32)
        up = (lax.broadcasted_iota(jnp.int32, (LANES, LANES), 0)
              < lax.broadcasted_iota(jnp.int32, (LANES, LANES), 1)).astype(F32)
        start = jnp.dot(jnp.broadcast_to(padded, (8, LANES)), up, precision=HIGHEST,
                        preferred_element_type=F32)[0:1]
        base_ref[...] = start
        nb = meta_ref.shape[0]
        lane_b = lax.broadcasted_iota(jnp.int32, (nb, LANES), 1)
        blk_start = (lax.broadcasted_iota(jnp.int32, (nb, LANES), 0) * blk).astype(F32)
        ended = jnp.logical_and(start + padded <= blk_start, lane_b < N_EXPERTS)
        be = jnp.minimum(jnp.sum(jnp.where(ended, 1.0, 0.0), axis=-1, keepdims=True), N_EXPERTS - 1.0)
        n_used = jnp.sum(padded, axis=-1, keepdims=True) * (1.0 / blk)
        meta_ref[...] = jnp.where(lane_b == 0, be, jnp.where(lane_b == 1, n_used, 0.0)).astype(jnp.int32)

    before = (lax.broadcasted_iota(jnp.int32, (tm, tm), 1)
              < lax.broadcasted_iota(jnp.int32, (tm, tm), 0)).astype(BF16)
    pre = jnp.dot(before, oh.astype(BF16), preferred_element_type=F32) + base_ref[...]
    s0 = jnp.sum(oh0 * pre, axis=-1, keepdims=True)
    s1 = jnp.sum(oh1 * pre, axis=-1, keepdims=True)
    slot_ref[...] = jnp.where(lane == 0, s0, jnp.where(lane == 1, s1, 0.0)).astype(jnp.int32)
    base_ref[...] += jnp.sum(oh, axis=0, keepdims=True)


def _rank(route, counts, n_tok, blk, nblk):
    n_rows = route.shape[0]
    nb = (nblk + 7) // 8 * 8
    slots, meta = pl.pallas_call(
        functools.partial(_rank_kernel, n_tok=n_tok, blk=blk),
        grid=(n_rows // ROUTE_TM,),
        in_specs=[pl.BlockSpec((ROUTE_TM, LANES), lambda i: (i, 0)),
                  pl.BlockSpec(counts.shape, lambda i: (0, 0))],
        out_specs=[pl.BlockSpec((ROUTE_TM, LANES), lambda i: (i, 0)),
                   pl.BlockSpec((nb, LANES), lambda i: (0, 0))],
        out_shape=[jax.ShapeDtypeStruct((n_rows, LANES), jnp.int32),
                   jax.ShapeDtypeStruct((nb, LANES), jnp.int32)],
        scratch_shapes=[pltpu.VMEM((1, LANES), F32)],
        compiler_params=_cparams(("arbitrary",)),
        name="rank",
    )(route, counts)
    return slots[:n_tok, :TOP_K_INNER].reshape(-1), meta[:nblk, 0], meta[0:1, 1]


DMA_UNROLL = 8


def _dispatch_kernel(slot_ref, be_ref, nu_ref, h2_ref, xb_out, zbuf, sem, zsem, *, n_tok):
    i = pl.program_id(0)
    tm = h2_ref.shape[0] // ROW_TILE
    tail = n_tok % tm

    @pl.when(i == 0)
    def _zero_padding():
        blk = zbuf.shape[0] // ROW_TILE
        n_blocks = be_ref.shape[0]
        zbuf[...] = jnp.zeros_like(zbuf)

        def ends_expert(j):
            nxt = be_ref[jnp.minimum(j + 1, n_blocks - 1)]
            return jnp.logical_or(j >= nu_ref[0] - 1, be_ref[j] != nxt)

        def issue(j, c):
            @pl.when(ends_expert(j))
            def _():
                pltpu.make_async_copy(zbuf, _row_tile(xb_out, j * blk, blk), zsem).start()
            return c

        def drain(j, c):
            @pl.when(ends_expert(j))
            def _():
                pltpu.make_async_copy(zbuf, _row_tile(xb_out, 0, blk), zsem).wait()
            return c

        lax.fori_loop(0, n_blocks, issue, 0)
        lax.fori_loop(0, n_blocks, drain, 0)

    def push(rows):
        def body(r, c):
            a = (i * tm + r) * TOP_K_INNER
            for k in range(TOP_K_INNER):
                pltpu.make_async_copy(_row_tile(h2_ref, r), _row_tile(xb_out, slot_ref[a + k]), sem).start(
                    priority=k % 2)
            return c
        lax.fori_loop(0, rows, body, 0, unroll=DMA_UNROLL)
        for k in range(TOP_K_INNER):
            pltpu.make_async_copy(_row_tile(h2_ref, 0, rows), _row_tile(xb_out, 0, rows), sem).wait()

    last = pl.num_programs(0) - 1
    if tail == 0:
        push(tm)
    else:
        @pl.when(i < last)
        def _full():
            push(tm)

        @pl.when(i == last)
        def _tail():
            push(tail)


def _dispatch(slot_flat, blk_expert, n_used, h2, n_tok, blk):
    n_tiles = h2.shape[0] // (ROUTE_TM * ROW_TILE)
    tm = ROUTE_TM * max(g for g in range(1, 9) if n_tiles % g == 0)
    n_slots = blk_expert.shape[0] * blk
    grid_spec = pltpu.PrefetchScalarGridSpec(
        num_scalar_prefetch=3,
        grid=(h2.shape[0] // (tm * ROW_TILE),),
        in_specs=[pl.BlockSpec((tm * ROW_TILE, LANES), lambda i, s, be, nu: (i, 0))],
        out_specs=pl.BlockSpec(memory_space=pl.ANY),
        scratch_shapes=[pltpu.VMEM((blk * ROW_TILE, LANES), F32),
                        pltpu.SemaphoreType.DMA(()),
                        pltpu.SemaphoreType.DMA(())],
    )
    return pl.pallas_call(
        functools.partial(_dispatch_kernel, n_tok=n_tok),
        grid_spec=grid_spec,
        out_shape=jax.ShapeDtypeStruct((n_slots * ROW_TILE, LANES), F32),
        compiler_params=_cparams(("arbitrary",)),
        name="dispatch",
    )(slot_flat, blk_expert, n_used, h2)


def _expert_kernel(be_ref, nu_ref, x_ref, wg_hbm, wu_hbm, wd_hbm, y_ref,
                   wg_f, wu_f, wd_f, sem, wgb, wub, wdb, cur_ref):
    i = pl.program_id(0)
    n_used = nu_ref[0]
    e = be_ref[i]
    e_prev = be_ref[jnp.maximum(i - 1, 0)]

    def fetch(ex, s):
        return [pltpu.make_async_copy(hbm.at[ex], buf.at[s], sem.at[s])
                for hbm, buf in ((wg_hbm, wg_f), (wu_hbm, wu_f), (wd_hbm, wd_f))]

    @pl.when(i == 0)
    def _first():
        cur_ref[0] = 0
        for c in fetch(e, 0):
            c.start()

    @pl.when(jnp.logical_and(jnp.logical_or(i == 0, e != e_prev), i < n_used))
    def _new_expert():
        s = cur_ref[0]
        j = lax.while_loop(lambda j: jnp.logical_and(j < n_used, be_ref[jnp.minimum(j, n_used - 1)] == e),
                           lambda j: j + 1, i + 1)

        @pl.when(j < n_used)
        def _prefetch():
            for c in fetch(be_ref[j], 1 - s):
                c.start()

        for c in fetch(e, s):
            c.wait()
        wgb[...] = wg_f[s].astype(BF16)
        wub[...] = wu_f[s].astype(BF16)
        wdb[...] = wd_f[s].astype(BF16)
        cur_ref[0] = 1 - s

    @pl.when(i < n_used)
    def _run():
        x = _load_row_tiles(x_ref, x_ref.shape[0] // ROW_TILE).astype(BF16)
        a = jnp.dot(x, wgb[...], preferred_element_type=F32)
        u = jnp.dot(x, wub[...], preferred_element_type=F32)
        mid = (a * _sigmoid(a) * u).astype(BF16)
        _store_row_tiles(y_ref, jnp.dot(mid, wdb[...], preferred_element_type=F32))

    @pl.when(i >= nu_ref[0])
    def _skip():
        y_ref[...] = jnp.zeros_like(y_ref)


def _experts(xb, blk_expert, n_used, w_g, w_u, w_d, blk):
    nblk = blk_expert.shape[0]
    hbm = pl.BlockSpec(memory_space=pl.ANY)
    grid_spec = pltpu.PrefetchScalarGridSpec(
        num_scalar_prefetch=2,
        grid=(nblk,),
        in_specs=[pl.BlockSpec((blk * ROW_TILE, LANES), lambda i, be, nu: (jnp.minimum(i, nu[0] - 1), 0)),
                  hbm, hbm, hbm],
        out_specs=pl.BlockSpec((blk * ROW_TILE, LANES), lambda i, be, nu: (i, 0)),
        scratch_shapes=[pltpu.VMEM((2, D_MODEL, D_FF_EXPERT), F32),
                        pltpu.VMEM((2, D_MODEL, D_FF_EXPERT), F32),
                        pltpu.VMEM((2, D_FF_EXPERT, D_MODEL), F32),
                        pltpu.SemaphoreType.DMA((2,)),
                        pltpu.VMEM((D_MODEL, D_FF_EXPERT), BF16),
                        pltpu.VMEM((D_MODEL, D_FF_EXPERT), BF16),
                        pltpu.VMEM((D_FF_EXPERT, D_MODEL), BF16),
                        pltpu.SMEM((1,), jnp.int32)],
    )
    return pl.pallas_call(
        _expert_kernel,
        grid_spec=grid_spec,
        out_shape=jax.ShapeDtypeStruct((nblk * blk * ROW_TILE, LANES), F32),
        compiler_params=_cparams(("arbitrary",)),
        name="experts",
    )(blk_expert, n_used, xb, w_g, w_u, w_d)


def _final_kernel(slot_ref, x_ref, route_ref, g_ref, yb_hbm, o_ref, ybuf, sem, *, tok0):
    i = pl.program_id(0)
    tm = x_ref.shape[0]

    def gather(j, buf):
        def body(r, c):
            a = (tok0 + j * tm + r) * TOP_K_INNER
            for k in range(TOP_K_INNER):
                pltpu.make_async_copy(_row_tile(yb_hbm, slot_ref[a + k]), _row_tile(ybuf, r, lead=(buf, k)),
                                      sem.at[buf]).start(priority=k % 2)
            return c
        lax.fori_loop(0, tm, body, 0, unroll=DMA_UNROLL)

    @pl.when(i == 0)
    def _first():
        gather(0, 0)

    @pl.when(i + 1 < pl.num_programs(0))
    def _next():
        gather(i + 1, (i + 1) % 2)

    buf = i % 2
    for k in range(TOP_K_INNER):
        pltpu.make_async_copy(_row_tile(yb_hbm, 0, tm), ybuf.at[buf, k], sem.at[buf]).wait()
    route = route_ref[...]
    y0 = _load_row_tiles(ybuf, tm, lead=(buf, 0))
    y1 = _load_row_tiles(ybuf, tm, lead=(buf, 1))
    x = x_ref[...] + (y0 * route[:, 2:3] + y1 * route[:, 3:4])
    ms = jnp.mean(x * x, axis=-1, keepdims=True)
    o_ref[...] = x * lax.rsqrt(ms + RMS_EPS) * g_ref[...]


def _final(slot_flat, xmid, route, g_final, yb, tok0, n_out, tm):
    blk0 = tok0 // tm
    grid_spec = pltpu.PrefetchScalarGridSpec(
        num_scalar_prefetch=1,
        grid=(n_out // tm,),
        in_specs=[pl.BlockSpec((tm, D_MODEL), lambda i, s: (i + blk0, 0)),
                  pl.BlockSpec((tm, LANES), lambda i, s: (i + blk0, 0)),
                  pl.BlockSpec((1, D_MODEL), lambda i, s: (0, 0)),
                  pl.BlockSpec(memory_space=pl.ANY)],
        out_specs=pl.BlockSpec((tm, D_MODEL), lambda i, s: (i, 0)),
        scratch_shapes=[pltpu.VMEM((2, TOP_K_INNER, tm * ROW_TILE, LANES), F32),
                        pltpu.SemaphoreType.DMA((2,))],
    )
    return pl.pallas_call(
        functools.partial(_final_kernel, tok0=tok0),
        grid_spec=grid_spec,
        out_shape=jax.ShapeDtypeStruct((n_out, D_MODEL), F32),
        compiler_params=_cparams(("arbitrary",)),
        name="final",
    )(slot_flat, xmid, route, g_final.reshape(1, D_MODEL), yb)


def kernel(x_prompt, x_sample, cache_attn_k, cache_attn_v, state_hgrn, w_in, w_out, hg_lb_logits,
           hg_norm_g, norm_mix_g, norm_ffn_g, norm_final_g, w_route_group, w_route_expert,
           w_expert_gate, w_expert_up, w_expert_down):
    bp, tp, _ = x_prompt.shape
    bs = x_sample.shape[0]
    l = 0
    w_in_bf = w_in[l].astype(BF16)
    w_out_bf = w_out[l].astype(BF16)
    w_router = jnp.concatenate(
        [w_route_group[l],
         jnp.transpose(w_route_expert[l], (1, 0, 2)).reshape(D_MODEL, N_EXPERTS),
         jnp.zeros((D_MODEL, LANES - N_GROUPS - N_EXPERTS), F32)], axis=-1)
    w_router_hi = w_router.astype(BF16)
    w_router = jnp.stack([w_router_hi, (w_router - w_router_hi.astype(F32)).astype(BF16)])

    n_p = bp * tp
    xp = x_prompt.reshape(n_p, D_MODEL)
    pos_p = jnp.arange(tp, dtype=jnp.int32)
    hq, hk, hv, lf, zg, aq, ak, av = _inproj(xp, norm_mix_g[l], w_in_bf, hg_lb_logits, pos_p, 512)
    seq3 = lambda a: a.reshape(bp, tp, HG_WIDTH)
    hn_p, s_fin = _hgrn_prompt(seq3(hq), seq3(hk), seq3(hv), seq3(lf), seq3(zg), hg_norm_g[l])
    att_p = _attn_prompt(seq3(aq), seq3(ak), seq3(av))
    keep = min(MAX_WINDOW, tp)
    heads = lambda a: a.reshape(1, bp, keep, ATT_HEADS, ATT_HEAD_DIM)
    new_k_p = heads(seq3(ak)[:, tp - keep:])
    new_v_p = heads(seq3(av)[:, tp - keep:])

    xs = x_sample.reshape(bs, D_MODEL)
    pos_s = jnp.full((bs,), PAST_LEN, jnp.int32)
    hq, hk, hv, lf, zg, aq, ak, av = _inproj(xs, norm_mix_g[l], w_in_bf, hg_lb_logits, pos_s, bs)
    hn_s, s_new = _hgrn_step(hq, hk, hv, lf, zg, hg_norm_g[l], state_hgrn[l])
    feat = lambda a: jnp.transpose(a, (0, 2, 3, 1))
    att_s, new_k_s, new_v_s = _attn_step(aq, ak, av, feat(cache_attn_k[l]), feat(cache_attn_v[l]))
    cache5 = lambda a: jnp.transpose(a, (0, 3, 1, 2))[None]

    assert n_p % ROUTE_TM == 0 and bs <= ROUTE_TM
    n_tok = n_p + bs
    xmid, h2, route, counts = _outproj(hn_p.reshape(n_p, HG_WIDTH), att_p.reshape(n_p, ATT_WIDTH), xp,
                                       hn_s, att_s, xs, w_out_bf, norm_ffn_g[l], w_router)
    blk = MOE_BLOCK
    nblk = (n_tok * TOP_K_INNER + N_EXPERTS * (blk - 1)) // blk + 1
    slot_flat, blk_expert, n_used = _rank(route, counts, n_tok, blk, nblk)
    xb = _dispatch(slot_flat, blk_expert, n_used, h2, n_tok, blk)
    yb = _experts(xb, blk_expert, n_used, w_expert_gate[l], w_expert_up[l], w_expert_down[l], blk)
    y_prompt = _final(slot_flat, xmid, route, norm_final_g, yb, 0, n_p, 512)
    y_sample = _final(slot_flat, xmid, route, norm_final_g, yb, n_p, bs, bs)

    return (y_prompt.reshape(bp, tp, D_MODEL), y_sample.reshape(bs, 1, D_MODEL),
            new_k_p, new_v_p, s_fin[None], cache5(new_k_s), cache5(new_v_s), s_new[None])
```

```python
import functools

import jax
import jax.numpy as jnp
from jax import lax
from jax.experimental import pallas as pl
from jax.experimental.pallas import tpu as pltpu

F32 = jnp.float32
BF16 = jnp.bfloat16

D_MODEL = 1024
HG_WIDTH = 512
HG_HEAD_DIM = 128
HG_HEADS = 4
ATT_WIDTH = 512
ATT_HEAD_DIM = 64
ATT_HEADS = 8
ROPE_DIM = 16
ROPE_THETA = 500000.0
DILATED_PATTERNS = ((128, 1), (512, 4), (2048, 16))
MAX_WINDOW = 2048
PAST_LEN = 16384
N_GROUPS = 8
EXPERTS_PER_GROUP = 8
N_EXPERTS = 64
TOP_K_INNER = 2
D_FF_EXPERT = 512
MOE_BLOCK = 256
IN_COLS = 4 * HG_WIDTH + 3 * ATT_WIDTH
RMS_EPS = 1e-6

LANES = 128
SUBLANES = 8
VMEM_LIMIT = 56 * 1024 * 1024
NEG = -1e30
HIGHEST = lax.Precision.HIGHEST
NT_DIMS = (((1,), (1,)), ((), ()))


def _sigmoid(z):
    return 1.0 / (1.0 + jnp.exp(-z))


def _cparams(sem):
    return pltpu.CompilerParams(dimension_semantics=sem, vmem_limit_bytes=VMEM_LIMIT)


def _inproj_kernel(x_ref, g_ref, w_ref, lbl_ref, cos_ref, sa_ref, sb_ref,
                   hq_ref, hk_ref, hv_ref, lf_ref, zg_ref, aq_ref, ak_ref, av_ref):
    x = x_ref[...]
    ms = jnp.mean(x * x, axis=-1, keepdims=True)
    h = (x * lax.rsqrt(ms + RMS_EPS) * g_ref[...]).astype(BF16)

    def mm(c0):
        return jnp.dot(h, w_ref[:, c0:c0 + HG_WIDTH], preferred_element_type=F32)

    lbl = lbl_ref[...]
    le = jnp.exp(lbl - jnp.max(lbl, axis=0, keepdims=True))
    lb = le[0:1, :] / jnp.sum(le, axis=0, keepdims=True)

    zq = mm(0)
    hq_ref[...] = zq * _sigmoid(zq)
    zf = mm(HG_WIDTH)
    f = lb + (1.0 - lb) * _sigmoid(zf)
    hk_ref[...] = 1.0 - f
    lf_ref[...] = jnp.log(f)
    hv_ref[...] = mm(2 * HG_WIDTH)
    zg_ref[...] = mm(3 * HG_WIDTH)

    cos, sa, sb = cos_ref[...], sa_ref[...], sb_ref[...]

    def rope(a, out_ref):
        for j in range(ATT_WIDTH // LANES):
            xj = a[:, j * LANES:(j + 1) * LANES]
            up = pltpu.roll(xj, LANES - ROPE_DIM // 2, 1)
            dn = pltpu.roll(xj, ROPE_DIM // 2, 1)
            out_ref[:, j * LANES:(j + 1) * LANES] = xj * cos + up * sa + dn * sb

    rope(mm(4 * HG_WIDTH), aq_ref)
    rope(mm(4 * HG_WIDTH + ATT_WIDTH), ak_ref)
    av_ref[...] = mm(4 * HG_WIDTH + 2 * ATT_WIDTH)


def _rope_tables(pos):
    half = ROPE_DIM // 2
    c = jnp.arange(LANES) % ATT_HEAD_DIM
    inv_freq = ROPE_THETA ** (-(c % half).astype(F32) / half)
    ang = pos.astype(F32)[:, None] * inv_freq[None, :]
    cos, sin = jnp.cos(ang), jnp.sin(ang)
    return (jnp.where(c < ROPE_DIM, cos, 1.0),
            jnp.where(c < half, -sin, 0.0),
            jnp.where(jnp.logical_and(c >= half, c < ROPE_DIM), sin, 0.0))


def _inproj(x2d, g, w_bf, lb_logits, pos, tm):
    m = x2d.shape[0]
    cos, sa, sb = _rope_tables(pos)
    row = lambda i: (i, 0)
    seq_tiles = pos.shape[0] // tm
    row_pos = lambda i: (i % seq_tiles, 0)
    const = lambda i: (0, 0)
    outs = [jax.ShapeDtypeStruct((m, HG_WIDTH), F32)] * 8
    return pl.pallas_call(
        _inproj_kernel,
        grid=(m // tm,),
        in_specs=[pl.BlockSpec((tm, D_MODEL), row),
                  pl.BlockSpec((1, D_MODEL), const),
                  pl.BlockSpec((D_MODEL, IN_COLS), const),
                  pl.BlockSpec(lb_logits.shape, const),
                  pl.BlockSpec((tm, LANES), row_pos),
                  pl.BlockSpec((tm, LANES), row_pos),
                  pl.BlockSpec((tm, LANES), row_pos)],
        out_specs=[pl.BlockSpec((tm, HG_WIDTH), row)] * 8,
        out_shape=outs,
        compiler_params=_cparams(("parallel",)),
        name="inproj",
    )(x2d, g.reshape(1, D_MODEL), w_bf, lb_logits, cos, sa, sb)


HG_C = 128
HG_SB = 16


def _hgrn_kernel(q_ref, k_ref, v_ref, lf_ref, zg_ref, g_ref, hn_ref, sfin_ref, st_ref, *, n_chunks):
    t = pl.program_id(2)

    @pl.when(t == 0)
    def _init():
        st_ref[...] = jnp.zeros_like(st_ref)

    ri = lax.broadcasted_iota(jnp.int32, (HG_C, HG_C), 0)
    ci = lax.broadcasted_iota(jnp.int32, (HG_C, HG_C), 1)
    ltri = (ri >= ci).astype(BF16)
    ones_b = jnp.ones((LANES, LANES), BF16)
    n_sb = HG_C // HG_SB
    row_sb = lax.broadcasted_iota(jnp.int32, (n_sb, HG_SB, LANES), 1)
    col_sb = lax.broadcasted_iota(jnp.int32, (n_sb, HG_SB, HG_C), 2)
    lo_sb = lax.broadcasted_iota(jnp.int32, (n_sb, HG_SB, HG_C), 0) * HG_SB
    g = g_ref[...]

    def chunk(c, carry):
        r0 = pl.multiple_of(c * HG_C, HG_C)
        q = q_ref[pl.ds(r0, HG_C), :]
        k = k_ref[pl.ds(r0, HG_C), :]
        v = v_ref[pl.ds(r0, HG_C), :]
        lf = lf_ref[pl.ds(r0, HG_C), :]
        lf_hi = lf.astype(BF16)
        lf_r = lf - lf_hi.astype(F32)
        lf_mid = lf_r.astype(BF16)
        lf_lo = (lf_r - lf_mid.astype(F32)).astype(BF16)
        b = (jnp.dot(ltri, lf_hi, preferred_element_type=F32)
             + (jnp.dot(ltri, lf_mid, preferred_element_type=F32)
                + jnp.dot(ltri, lf_lo, preferred_element_type=F32)))
        st = st_ref[...]
        vb = v.astype(BF16)
        qb = (q * jnp.exp(b)).astype(BF16)
        o_inter = lax.dot_general(qb, st.astype(BF16), NT_DIMS, preferred_element_type=F32)
        b3, q3, v3 = (a.reshape(n_sb, HG_SB, LANES) for a in (b, q, v))
        bk = b - jnp.log(k)
        bk3 = bk.reshape(n_sb, HG_SB, LANES)
        ps, t_lo, offs = [], [], [0]
        for s in range(HG_SB):
            lo = (s // SUBLANES) * SUBLANES
            d = jnp.where(row_sb[:, lo:] >= s, b3[:, lo:] - bk3[:, s:s + 1, :], NEG)
            ps.append(q3[:, lo:] * jnp.exp(d))
            t_lo.append(lo)
            offs.append(offs[-1] + HG_SB - lo)
        n_rows = offs[-1]
        p_all = jnp.concatenate(ps, axis=1).reshape(n_sb * n_rows, LANES).astype(BF16)
        r_all = jnp.dot(p_all, ones_b, preferred_element_type=F32)
        r_all = r_all.reshape(n_sb, n_rows, LANES)
        o3 = o_inter.reshape(n_sb, HG_SB, LANES)
        tiles = [o3[:, j * SUBLANES:(j + 1) * SUBLANES] for j in range(HG_SB // SUBLANES)]
        for s in range(HG_SB):
            for j in range(t_lo[s] // SUBLANES, HG_SB // SUBLANES):
                r0_ = offs[s] + j * SUBLANES - t_lo[s]
                tiles[j] = tiles[j] + r_all[:, r0_:r0_ + SUBLANES, :] * v3[:, s:s + 1, :]
        o3 = jnp.concatenate(tiles, axis=1)
        b_ref = jnp.concatenate([b3[0:1, 0:1], b3[:n_sb - 1, HG_SB - 1:HG_SB]], axis=0)
        qs = (q3 * jnp.exp(jnp.minimum(b3 - b_ref, 0.0))).astype(BF16)
        ks = jnp.exp(jnp.minimum(b_ref - bk[None], 0.0)).astype(BF16)
        a = lax.dot_general(qs, ks, (((2,), (2,)), ((0,), (0,))), preferred_element_type=F32)
        a = jnp.where(col_sb < lo_sb, a, 0.0).astype(BF16).reshape(HG_C, HG_C)
        o = o3.reshape(HG_C, LANES) + jnp.dot(a, vb, preferred_element_type=F32)
        b_last = b[HG_C - 1:HG_C, :]
        kdec = jnp.exp(b_last - bk).astype(BF16)
        st_ref[...] = st * jnp.exp(b_last) + jnp.dot(v.T.astype(BF16), kdec, preferred_element_type=F32)
        ms = jnp.mean(o * o, axis=-1, keepdims=True)
        zg = zg_ref[pl.ds(r0, HG_C), :]
        hn_ref[pl.ds(r0, HG_C), :] = o * lax.rsqrt(ms + RMS_EPS) * g * (zg * _sigmoid(zg))
        return carry

    lax.fori_loop(0, n_chunks, chunk, 0, unroll=8)

    @pl.when(t == pl.num_programs(2) - 1)
    def _fin():
        sfin_ref[...] = st_ref[...].T


def _hgrn_prompt(hq, hk, hv, lf, zg, g_hg, tb=1024):
    bsz, t, _ = hq.shape
    seq = pl.BlockSpec((None, tb, HG_HEAD_DIM), lambda b, h, i: (b, i, h))
    return pl.pallas_call(
        functools.partial(_hgrn_kernel, n_chunks=tb // HG_C),
        grid=(bsz, HG_HEADS, t // tb),
        in_specs=[seq, seq, seq, seq, seq,
                  pl.BlockSpec((1, HG_HEAD_DIM), lambda b, h, i: (0, h))],
        out_specs=[seq,
                   pl.BlockSpec((None, None, HG_HEAD_DIM, HG_HEAD_DIM), lambda b, h, i: (b, h, 0, 0))],
        out_shape=[jax.ShapeDtypeStruct((bsz, t, HG_WIDTH), F32),
                   jax.ShapeDtypeStruct((bsz, HG_HEADS, HG_HEAD_DIM, HG_HEAD_DIM), F32)],
        scratch_shapes=[pltpu.VMEM((HG_HEAD_DIM, HG_HEAD_DIM), F32)],
        compiler_params=_cparams(("parallel", "parallel", "arbitrary")),
        name="hgrn_prompt",
    )(hq, hk, hv, lf, zg, g_hg.reshape(1, HG_WIDTH))


def _hgrn_step_kernel(q_ref, k_ref, v_ref, lf_ref, zg_ref, g_ref, s_ref, hn_ref, snew_ref):
    row = slice(None)
    zeros = jnp.zeros((HG_HEAD_DIM - 3, HG_HEAD_DIM), F32)
    for h in range(HG_HEADS):
        cs = slice(h * HG_HEAD_DIM, (h + 1) * HG_HEAD_DIM)
        q, k, v = q_ref[row, cs], k_ref[row, cs], v_ref[row, cs]
        f = jnp.exp(lf_ref[row, cs])
        cols = jnp.concatenate([f, k, q, zeros], axis=0).T
        s_new = cols[:, 0:1] * s_ref[h] + cols[:, 1:2] * v
        snew_ref[h] = s_new
        o = jnp.sum(cols[:, 2:3] * s_new, axis=0, keepdims=True)
        ms = jnp.mean(o * o, axis=-1, keepdims=True)
        zg = zg_ref[row, cs]
        hn_ref[row, cs] = o * lax.rsqrt(ms + RMS_EPS) * g_ref[:, cs] * (zg * _sigmoid(zg))


def _hgrn_step(hq, hk, hv, lf, zg, g_hg, state):
    bsz = hq.shape[0]
    one = pl.BlockSpec((None, 1, HG_WIDTH), lambda b: (b, 0, 0))
    st = pl.BlockSpec((None, HG_HEADS, HG_HEAD_DIM, HG_HEAD_DIM), lambda b: (b, 0, 0, 0))
    r3 = lambda a: a.reshape(bsz, 1, HG_WIDTH)
    hn, s_new = pl.pallas_call(
        _hgrn_step_kernel,
        grid=(bsz,),
        in_specs=[one, one, one, one, one, pl.BlockSpec((1, HG_WIDTH), lambda b: (0, 0)), st],
        out_specs=[one, st],
        out_shape=[jax.ShapeDtypeStruct((bsz, 1, HG_WIDTH), F32),
                   jax.ShapeDtypeStruct(state.shape, F32)],
        compiler_params=_cparams(("parallel",)),
        name="hgrn_step",
    )(r3(hq), r3(hk), r3(hv), r3(lf), r3(zg), g_hg.reshape(1, HG_WIDTH), state)
    return hn.reshape(bsz, HG_WIDTH), s_new


ATT_N = 128
ATT_SUPER = 2048
ATT_G = 8


def _attn_prompt_kernel(q_ref, k_ref, v_ref, o_ref, osc, lsc, *, seq_len):
    lane = lax.broadcasted_iota(jnp.int32, (ATT_N, LANES), 1)
    head0 = lane < ATT_HEAD_DIM
    kidx = lax.broadcasted_iota(jnp.int32, (ATT_N, 2 * ATT_N), 1)
    qidx = lax.broadcasted_iota(jnp.int32, (ATT_N, 2 * ATT_N), 0)
    band = jnp.logical_and(kidx >= qidx, kidx <= qidx + ATT_N)
    in_prev = kidx < ATT_N
    scale = ATT_HEAD_DIM ** -0.5

    bqk = (((2,), (2,)), ((0,), (0,)))
    bkd = (((2,), (1,)), ((0,), (0,)))

    def do_group(p, d, base, g, gsz):
        span = ATT_N * d
        rows_g = gsz * ATT_N
        if d == 1:
            off = g * rows_g
            start = base + off

            def cur(ref):
                return ref[pl.ds(start, rows_g), :].reshape(gsz, ATT_N, LANES)

            def prv(ref, c):
                before = ref[pl.ds(jnp.maximum(start - ATT_N, 0), ATT_N), :].astype(BF16)
                return jnp.concatenate([before[None], c[:gsz - 1]], axis=0)

            bidx = lax.broadcasted_iota(jnp.int32, (gsz, ATT_N, 2 * ATT_N), 0)
            seen = jnp.logical_and(band, jnp.logical_not(
                jnp.logical_and(jnp.logical_and(bidx == 0, in_prev), start == 0)))
        else:
            per_blk = d // gsz
            off = (g // per_blk) * span + (g % per_blk) * gsz
            start = base + off
            prev = jnp.maximum(start - span, 0)

            def cur(ref):
                return jnp.stack([ref[pl.ds(start + r, ATT_N, stride=d), :] for r in range(gsz)])

            def prv(ref, c):
                return jnp.stack([ref[pl.ds(prev + r, ATT_N, stride=d), :] for r in range(gsz)]).astype(BF16)

            seen = jnp.logical_and(band, jnp.logical_not(jnp.logical_and(in_prev, start < span)))
        q = cur(q_ref) * scale
        kc = cur(k_ref).astype(BF16)
        vc = cur(v_ref).astype(BF16)
        kk = jnp.concatenate([prv(k_ref, kc), kc], axis=1)
        vv = jnp.concatenate([prv(v_ref, vc), vc], axis=1)
        res = []
        for hm in (head0, jnp.logical_not(head0)):
            qh = jnp.where(hm, q, 0.0).astype(BF16)
            s = lax.dot_general(qh, kk, bqk, preferred_element_type=F32)
            s = jnp.where(seen, s, NEG)
            m = jnp.max(s, axis=-1, keepdims=True)
            pr = jnp.exp(s - m)
            den = jnp.sum(pr, axis=-1, keepdims=True)
            o = lax.dot_general(pr.astype(BF16), vv, bkd, preferred_element_type=F32)
            res.append((o / den, m + jnp.log(den)))
        o = jnp.where(head0, res[0][0], res[1][0])
        lse = jnp.where(head0, res[0][1], res[1][1])
        if d == 1:
            osc[p, pl.ds(off, rows_g), :] = o.reshape(rows_g, LANES)
            lsc[p, pl.ds(off, rows_g), :] = lse.reshape(rows_g, LANES)
        else:
            for r in range(gsz):
                osc[p, pl.ds(off + r, ATT_N, stride=d), :] = o[r]
                lsc[p, pl.ds(off + r, ATT_N, stride=d), :] = lse[r]

    def superblock(sb, carry):
        base = sb * ATT_SUPER

        def groups(it, c2):
            for p, (w, d) in enumerate(DILATED_PATTERNS):
                gsz = ATT_G if d == 1 else min(ATT_G, d)
                for sub in range(ATT_G // gsz):
                    do_group(p, d, base, it * (ATT_G // gsz) + sub, gsz)
            return c2

        lax.fori_loop(0, ATT_SUPER // (ATT_G * ATT_N), groups, 0)

        piece = 256

        def merge(j, c2):
            r = pl.ds(pl.multiple_of(j * piece, piece), piece)
            ls = [lsc[p, r, :] for p in range(len(DILATED_PATTERNS))]
            mx = jnp.maximum(jnp.maximum(ls[0], ls[1]), ls[2])
            ws = [jnp.exp(l - mx) for l in ls]
            num = ws[0] * osc[0, r, :] + ws[1] * osc[1, r, :] + ws[2] * osc[2, r, :]
            o_ref[pl.ds(pl.multiple_of(base + j * piece, piece), piece), :] = num / (ws[0] + ws[1] + ws[2])
            return c2

        lax.fori_loop(0, ATT_SUPER // piece, merge, 0)
        return carry

    lax.fori_loop(0, seq_len // ATT_SUPER, superblock, 0)


def _attn_prompt(aq, ak, av):
    bsz, t, _ = aq.shape
    spec = pl.BlockSpec((None, t, LANES), lambda b, p: (b, 0, p))
    n_pat = len(DILATED_PATTERNS)
    return pl.pallas_call(
        functools.partial(_attn_prompt_kernel, seq_len=t),
        grid=(bsz, ATT_WIDTH // LANES),
        in_specs=[spec, spec, spec],
        out_specs=spec,
        out_shape=jax.ShapeDtypeStruct((bsz, t, ATT_WIDTH), F32),
        scratch_shapes=[pltpu.VMEM((n_pat, ATT_SUPER, LANES), F32),
                        pltpu.VMEM((n_pat, ATT_SUPER, LANES), F32)],
        compiler_params=_cparams(("parallel", "parallel")),
        name="attn_prompt",
    )(aq, ak, av)


def _attn_step_kernel(q_ref, kn_ref, vn_ref, ck_ref, cv_ref, o_ref, nk_ref, nv_ref):
    win = ck_ref.shape[-1]
    kt, vt = ck_ref[...], cv_ref[...]
    chan = lax.broadcasted_iota(jnp.int32, (ATT_HEAD_DIM, ATT_WIDTH), 1)
    feat = lax.broadcasted_iota(jnp.int32, (ATT_HEAD_DIM, ATT_WIDTH), 0)
    picks = [chan == h * ATT_HEAD_DIM + feat for h in range(ATT_HEADS)]

    def to_cols(row):
        return jnp.stack([jnp.sum(jnp.where(m, row, 0.0), axis=1, keepdims=True) for m in picks])

    q = to_cols(q_ref[...]) * (ATT_HEAD_DIM ** -0.5)
    kn, vn = to_cols(kn_ref[...]), to_cols(vn_ref[...])
    s_all = jnp.sum(kt * q, axis=1, keepdims=True)
    s_new = jnp.sum(kn * q, axis=1, keepdims=True)
    dist = win - lax.broadcasted_iota(jnp.int32, (1, 1, win), 2)

    ps, pnews, lses = [], [], []
    for w, d in DILATED_PATTERNS:
        on_stride = (dist & (d - 1)) == 0 if d & (d - 1) == 0 else dist % d == 0
        valid = jnp.logical_and(dist <= w, on_stride)
        sm = jnp.where(valid, s_all, NEG)
        m = jnp.maximum(jnp.max(sm, axis=-1, keepdims=True), s_new)
        p = jnp.exp(sm - m)
        pn = jnp.exp(s_new - m)
        den = jnp.sum(p, axis=-1, keepdims=True) + pn
        ps.append(p / den)
        pnews.append(pn / den)
        lses.append(m + jnp.log(den))
    mx = jnp.maximum(jnp.maximum(lses[0], lses[1]), lses[2])
    ws = [jnp.exp(l - mx) for l in lses]
    wsum = ws[0] + ws[1] + ws[2]
    p_tot = (ws[0] * ps[0] + ws[1] * ps[1] + ws[2] * ps[2]) / wsum
    pn_tot = (ws[0] * pnews[0] + ws[1] * pnews[1] + ws[2] * pnews[2]) / wsum
    o = jnp.sum(vt * p_tot, axis=-1, keepdims=True) + pn_tot * vn
    o_ref[...] = sum(jnp.sum(jnp.where(m, o[h], 0.0), axis=0, keepdims=True) for h, m in enumerate(picks))

    last = lax.broadcasted_iota(jnp.int32, (1, 1, win), 2) == win - 1
    nk_ref[...] = jnp.where(last, kn, pltpu.roll(kt, win - 1, 2))
    nv_ref[...] = jnp.where(last, vn, pltpu.roll(vt, win - 1, 2))


def _attn_step(aq, ak, av, cache_k, cache_v):
    bsz, _, _, win = cache_k.shape
    one = pl.BlockSpec((None, 1, ATT_WIDTH), lambda b: (b, 0, 0))
    cache = pl.BlockSpec((None, ATT_HEADS, ATT_HEAD_DIM, win), lambda b: (b, 0, 0, 0))
    row = lambda a: a.reshape(bsz, 1, ATT_WIDTH)
    att, new_k, new_v = pl.pallas_call(
        _attn_step_kernel,
        grid=(bsz,),
        in_specs=[one, one, one, cache, cache],
        out_specs=[one, cache, cache],
        out_shape=[jax.ShapeDtypeStruct((bsz, 1, ATT_WIDTH), F32),
                   jax.ShapeDtypeStruct(cache_k.shape, F32),
                   jax.ShapeDtypeStruct(cache_v.shape, F32)],
        compiler_params=_cparams(("parallel",)),
        name="attn_step",
    )(row(aq), row(ak), row(av), cache_k, cache_v)
    return att.reshape(bsz, ATT_WIDTH), new_k, new_v


ROUTE_TM = 256
ROW_TILE = D_MODEL // LANES
assert ROW_TILE == SUBLANES


def _store_row_tiles(ref, val, lead=()):
    n = val.shape[0]
    for j in range(ROW_TILE):
        ref[lead + (pl.ds(j, n, stride=ROW_TILE), slice(None))] = val[:, j * LANES:(j + 1) * LANES]


def _load_row_tiles(ref, n, lead=()):
    return jnp.concatenate([ref[lead + (pl.ds(j, n, stride=ROW_TILE), slice(None))] for j in range(ROW_TILE)],
                           axis=1)


def _row_tile(ref, r, n=1, lead=()):
    start = r * ROW_TILE if isinstance(r, int) else pl.multiple_of(r * ROW_TILE, ROW_TILE)
    return ref.at[lead + (pl.ds(start, n * ROW_TILE),)]


def _outproj_kernel(hn_ref, att_ref, x_ref, hn_s_ref, att_s_ref, x_s_ref, w_ref, g_ref, wr_ref,
                    xmid_ref, h2_ref, route_ref, cnt_ref, *, n_sample):
    is_prompt = pl.program_id(0) < pl.num_programs(0) - 1
    hn = jnp.where(is_prompt, hn_ref[...], hn_s_ref[...])
    att = jnp.where(is_prompt, att_ref[...], att_s_ref[...])
    y = (jnp.dot(hn.astype(BF16), w_ref[0:HG_WIDTH, :], preferred_element_type=F32)
         + jnp.dot(att.astype(BF16), w_ref[HG_WIDTH:, :], preferred_element_type=F32))
    xm = jnp.where(is_prompt, x_ref[...], x_s_ref[...]) + y
    xmid_ref[...] = xm
    ms = jnp.mean(xm * xm, axis=-1, keepdims=True)
    h2 = xm * lax.rsqrt(ms + RMS_EPS) * g_ref[...]
    _store_row_tiles(h2_ref, h2)
    h2_hi = h2.astype(BF16)
    h2_lo = (h2 - h2_hi.astype(F32)).astype(BF16)
    lg = (jnp.dot(h2_hi, wr_ref[0], preferred_element_type=F32)
          + (jnp.dot(h2_lo, wr_ref[0], preferred_element_type=F32)
             + jnp.dot(h2_hi, wr_ref[1], preferred_element_type=F32)))
    lane = lax.broadcasted_iota(jnp.int32, lg.shape, 1).astype(F32)
    big = float(LANES)
    gmask = lane < N_GROUPS
    lgg = jnp.where(gmask, lg, NEG)
    mg = jnp.max(lgg, axis=-1, keepdims=True)
    gi = jnp.min(jnp.where(lgg == mg, lane, big), axis=-1, keepdims=True)
    p_grp = 1.0 / jnp.sum(jnp.exp(lgg - mg), axis=-1, keepdims=True)
    lo = N_GROUPS + gi * EXPERTS_PER_GROUP
    emask = jnp.logical_and(lane >= lo, lane < lo + EXPERTS_PER_GROUP)
    le1 = jnp.where(emask, lg, NEG)
    m1 = jnp.max(le1, axis=-1, keepdims=True)
    i1 = jnp.min(jnp.where(le1 == m1, lane, big), axis=-1, keepdims=True)
    le2 = jnp.where(lane == i1, NEG, le1)
    m2 = jnp.max(le2, axis=-1, keepdims=True)
    i2 = jnp.min(jnp.where(le2 == m2, lane, big), axis=-1, keepdims=True)
    r = jnp.exp(m2 - m1)
    g1 = p_grp / (1.0 + r)
    g2 = p_grp * r / (1.0 + r)
    e1, e2 = i1 - N_GROUPS, i2 - N_GROUPS
    route_ref[...] = jnp.where(lane == 0, e1,
                               jnp.where(lane == 1, e2,
                                         jnp.where(lane == 2, g1, jnp.where(lane == 3, g2, 0.0))))

    @pl.when(pl.program_id(0) == 0)
    def _zero():
        cnt_ref[...] = jnp.zeros_like(cnt_ref)

    rows = lax.broadcasted_iota(jnp.int32, lg.shape, 0)
    real = jnp.logical_or(is_prompt, rows < n_sample)
    hit = jnp.logical_and(real, jnp.logical_or(lane == e1, lane == e2))
    cnt_ref[...] += jnp.sum(jnp.where(hit, 1.0, 0.0), axis=0, keepdims=True)


def _outproj(hn_p, att_p, x_p, hn_s, att_s, x_s, w_out_bf, g_ffn, w_router):
    tm = ROUTE_TM
    n_p = x_p.shape[0]
    n_tiles = n_p // tm + 1
    pad = lambda a: jnp.pad(a, ((0, tm - a.shape[0]), (0, 0)))
    row_p = lambda i: (jnp.minimum(i, n_tiles - 2), 0)
    row = lambda i: (i, 0)
    const = lambda i: (0, 0)
    n_rows = n_tiles * tm
    return pl.pallas_call(
        functools.partial(_outproj_kernel, n_sample=x_s.shape[0]),
        grid=(n_tiles,),
        in_specs=[pl.BlockSpec((tm, HG_WIDTH), row_p),
                  pl.BlockSpec((tm, ATT_WIDTH), row_p),
                  pl.BlockSpec((tm, D_MODEL), row_p),
                  pl.BlockSpec((tm, HG_WIDTH), const),
                  pl.BlockSpec((tm, ATT_WIDTH), const),
                  pl.BlockSpec((tm, D_MODEL), const),
                  pl.BlockSpec((D_MODEL, D_MODEL), const),
                  pl.BlockSpec((1, D_MODEL), const),
                  pl.BlockSpec((2, D_MODEL, LANES), lambda i: (0, 0, 0))],
        out_specs=[pl.BlockSpec((tm, D_MODEL), row),
                   pl.BlockSpec((tm * ROW_TILE, LANES), row),
                   pl.BlockSpec((tm, LANES), row),
                   pl.BlockSpec((SUBLANES, LANES), const)],
        out_shape=[jax.ShapeDtypeStruct((n_rows, D_MODEL), F32),
                   jax.ShapeDtypeStruct((n_rows * ROW_TILE, LANES), F32),
                   jax.ShapeDtypeStruct((n_rows, LANES), F32),
                   jax.ShapeDtypeStruct((SUBLANES, LANES), F32)],
        compiler_params=_cparams(("arbitrary",)),
        name="outproj",
    )(hn_p, att_p, x_p, pad(hn_s), pad(att_s), pad(x_s), w_out_bf, g_ffn.reshape(1, D_MODEL), w_router)


def _rank_kernel(route_ref, cnt_ref, slot_ref, meta_ref, base_ref, *, n_tok, blk):
    i = pl.program_id(0)
    tm = route_ref.shape[0]
    lane = lax.broadcasted_iota(jnp.int32, (tm, LANES), 1).astype(F32)
    rowg = i * tm + lax.broadcasted_iota(jnp.int32, (tm, LANES), 0)
    valid = rowg < n_tok
    r = route_ref[...]
    oh0 = jnp.where(jnp.logical_and(valid, lane == r[:, 0:1]), 1.0, 0.0)
    oh1 = jnp.where(jnp.logical_and(valid, lane == r[:, 1:2]), 1.0, 0.0)
    oh = oh0 + oh1

    @pl.when(i == 0)
    def _starts():
        cnt = cnt_ref[0:1, :].astype(jnp.int32)
        shift = blk.bit_length() - 1
        padded = (((cnt + (blk - 1)) >> shift) << shift).astype(F32)
        up = (lax.broadcasted_iota(jnp.int32, (LANES, LANES), 0)
              < lax.broadcasted_iota(jnp.int32, (LANES, LANES), 1)).astype(F32)
        start = jnp.dot(jnp.broadcast_to(padded, (8, LANES)), up, precision=HIGHEST,
                        preferred_element_type=F32)[0:1]
        base_ref[...] = start
        nb = meta_ref.shape[0]
        lane_b = lax.broadcasted_iota(jnp.int32, (nb, LANES), 1)
        blk_start = (lax.broadcasted_iota(jnp.int32, (nb, LANES), 0) * blk).astype(F32)
        ended = jnp.logical_and(start + padded <= blk_start, lane_b < N_EXPERTS)
        be = jnp.minimum(jnp.sum(jnp.where(ended, 1.0, 0.0), axis=-1, keepdims=True), N_EXPERTS - 1.0)
        n_used = jnp.sum(padded, axis=-1, keepdims=True) * (1.0 / blk)
        meta_ref[...] = jnp.where(lane_b == 0, be, jnp.where(lane_b == 1, n_used, 0.0)).astype(jnp.int32)

    before = (lax.broadcasted_iota(jnp.int32, (tm, tm), 1)
              < lax.broadcasted_iota(jnp.int32, (tm, tm), 0)).astype(BF16)
    pre = jnp.dot(before, oh.astype(BF16), preferred_element_type=F32) + base_ref[...]
    s0 = jnp.sum(oh0 * pre, axis=-1, keepdims=True)
    s1 = jnp.sum(oh1 * pre, axis=-1, keepdims=True)
    slot_ref[...] = jnp.where(lane == 0, s0, jnp.where(lane == 1, s1, 0.0)).astype(jnp.int32)
    base_ref[...] += jnp.sum(oh, axis=0, keepdims=True)


def _rank(route, counts, n_tok, blk, nblk):
    n_rows = route.shape[0]
    nb = (nblk + 7) // 8 * 8
    slots, meta = pl.pallas_call(
        functools.partial(_rank_kernel, n_tok=n_tok, blk=blk),
        grid=(n_rows // ROUTE_TM,),
        in_specs=[pl.BlockSpec((ROUTE_TM, LANES), lambda i: (i, 0)),
                  pl.BlockSpec(counts.shape, lambda i: (0, 0))],
        out_specs=[pl.BlockSpec((ROUTE_TM, LANES), lambda i: (i, 0)),
                   pl.BlockSpec((nb, LANES), lambda i: (0, 0))],
        out_shape=[jax.ShapeDtypeStruct((n_rows, LANES), jnp.int32),
                   jax.ShapeDtypeStruct((nb, LANES), jnp.int32)],
        scratch_shapes=[pltpu.VMEM((1, LANES), F32)],
        compiler_params=_cparams(("arbitrary",)),
        name="rank",
    )(route, counts)
    return slots[:n_tok, :TOP_K_INNER].reshape(-1), meta[:nblk, 0], meta[0:1, 1]


DMA_UNROLL = 8


def _dispatch_kernel(slot_ref, be_ref, nu_ref, h2_ref, xb_out, zbuf, sem, zsem, *, n_tok):
    i = pl.program_id(0)
    tm = h2_ref.shape[0] // ROW_TILE
    tail = n_tok % tm

    @pl.when(i == 0)
    def _zero_padding():
        blk = zbuf.shape[0] // ROW_TILE
        n_blocks = be_ref.shape[0]
        zbuf[...] = jnp.zeros_like(zbuf)

        def ends_expert(j):
            nxt = be_ref[jnp.minimum(j + 1, n_blocks - 1)]
            return jnp.logical_or(j >= nu_ref[0] - 1, be_ref[j] != nxt)

        def issue(j, c):
            @pl.when(ends_expert(j))
            def _():
                pltpu.make_async_copy(zbuf, _row_tile(xb_out, j * blk, blk), zsem).start()
            return c

        def drain(j, c):
            @pl.when(ends_expert(j))
            def _():
                pltpu.make_async_copy(zbuf, _row_tile(xb_out, 0, blk), zsem).wait()
            return c

        lax.fori_loop(0, n_blocks, issue, 0)
        lax.fori_loop(0, n_blocks, drain, 0)

    def push(rows):
        def body(r, c):
            a = (i * tm + r) * TOP_K_INNER
            for k in range(TOP_K_INNER):
                pltpu.make_async_copy(_row_tile(h2_ref, r), _row_tile(xb_out, slot_ref[a + k]), sem).start(
                    priority=k % 2)
            return c
        lax.fori_loop(0, rows, body, 0, unroll=DMA_UNROLL)
        for k in range(TOP_K_INNER):
            pltpu.make_async_copy(_row_tile(h2_ref, 0, rows), _row_tile(xb_out, 0, rows), sem).wait()

    last = pl.num_programs(0) - 1
    if tail == 0:
        push(tm)
    else:
        @pl.when(i < last)
        def _full():
            push(tm)

        @pl.when(i == last)
        def _tail():
            push(tail)


def _dispatch(slot_flat, blk_expert, n_used, h2, n_tok, blk):
    n_tiles = h2.shape[0] // (ROUTE_TM * ROW_TILE)
    tm = ROUTE_TM * max(g for g in range(1, 9) if n_tiles % g == 0)
    n_slots = blk_expert.shape[0] * blk
    grid_spec = pltpu.PrefetchScalarGridSpec(
        num_scalar_prefetch=3,
        grid=(h2.shape[0] // (tm * ROW_TILE),),
        in_specs=[pl.BlockSpec((tm * ROW_TILE, LANES), lambda i, s, be, nu: (i, 0))],
        out_specs=pl.BlockSpec(memory_space=pl.ANY),
        scratch_shapes=[pltpu.VMEM((blk * ROW_TILE, LANES), F32),
                        pltpu.SemaphoreType.DMA(()),
                        pltpu.SemaphoreType.DMA(())],
    )
    return pl.pallas_call(
        functools.partial(_dispatch_kernel, n_tok=n_tok),
        grid_spec=grid_spec,
        out_shape=jax.ShapeDtypeStruct((n_slots * ROW_TILE, LANES), F32),
        compiler_params=_cparams(("arbitrary",)),
        name="dispatch",
    )(slot_flat, blk_expert, n_used, h2)


def _expert_kernel(be_ref, nu_ref, x_ref, wg_hbm, wu_hbm, wd_hbm, y_ref,
                   wg_f, wu_f, wd_f, sem, wgb, wub, wdb, cur_ref):
    i = pl.program_id(0)
    n_used = nu_ref[0]
    e = be_ref[i]
    e_prev = be_ref[jnp.maximum(i - 1, 0)]

    def fetch(ex, s):
        return [pltpu.make_async_copy(hbm.at[ex], buf.at[s], sem.at[s])
                for hbm, buf in ((wg_hbm, wg_f), (wu_hbm, wu_f), (wd_hbm, wd_f))]

    @pl.when(i == 0)
    def _first():
        cur_ref[0] = 0
        for c in fetch(e, 0):
            c.start()

    @pl.when(jnp.logical_and(jnp.logical_or(i == 0, e != e_prev), i < n_used))
    def _new_expert():
        s = cur_ref[0]
        j = lax.while_loop(lambda j: jnp.logical_and(j < n_used, be_ref[jnp.minimum(j, n_used - 1)] == e),
                           lambda j: j + 1, i + 1)

        @pl.when(j < n_used)
        def _prefetch():
            for c in fetch(be_ref[j], 1 - s):
                c.start()

        for c in fetch(e, s):
            c.wait()
        wgb[...] = wg_f[s].astype(BF16)
        wub[...] = wu_f[s].astype(BF16)
        wdb[...] = wd_f[s].astype(BF16)
        cur_ref[0] = 1 - s

    @pl.when(i < n_used)
    def _run():
        x = _load_row_tiles(x_ref, x_ref.shape[0] // ROW_TILE).astype(BF16)
        half = D_FF_EXPERT // 2
        y = None
        for c0 in (0, half):
            a = jnp.dot(x, wgb[:, c0:c0 + half], preferred_element_type=F32)
            u = jnp.dot(x, wub[:, c0:c0 + half], preferred_element_type=F32)
            mid = (a * _sigmoid(a) * u).astype(BF16)
            part = jnp.dot(mid, wdb[c0:c0 + half, :], preferred_element_type=F32)
            y = part if y is None else y + part
        _store_row_tiles(y_ref, y)

    @pl.when(i >= nu_ref[0])
    def _skip():
        y_ref[...] = jnp.zeros_like(y_ref)


def _experts(xb, blk_expert, n_used, w_g, w_u, w_d, blk):
    nblk = blk_expert.shape[0]
    hbm = pl.BlockSpec(memory_space=pl.ANY)
    grid_spec = pltpu.PrefetchScalarGridSpec(
        num_scalar_prefetch=2,
        grid=(nblk,),
        in_specs=[pl.BlockSpec((blk * ROW_TILE, LANES), lambda i, be, nu: (jnp.minimum(i, nu[0] - 1), 0)),
                  hbm, hbm, hbm],
        out_specs=pl.BlockSpec((blk * ROW_TILE, LANES), lambda i, be, nu: (i, 0)),
        scratch_shapes=[pltpu.VMEM((2, D_MODEL, D_FF_EXPERT), F32),
                        pltpu.VMEM((2, D_MODEL, D_FF_EXPERT), F32),
                        pltpu.VMEM((2, D_FF_EXPERT, D_MODEL), F32),
                        pltpu.SemaphoreType.DMA((2,)),
                        pltpu.VMEM((D_MODEL, D_FF_EXPERT), BF16),
                        pltpu.VMEM((D_MODEL, D_FF_EXPERT), BF16),
                        pltpu.VMEM((D_FF_EXPERT, D_MODEL), BF16),
                        pltpu.SMEM((1,), jnp.int32)],
    )
    return pl.pallas_call(
        _expert_kernel,
        grid_spec=grid_spec,
        out_shape=jax.ShapeDtypeStruct((nblk * blk * ROW_TILE, LANES), F32),
        compiler_params=_cparams(("arbitrary",)),
        name="experts",
    )(blk_expert, n_used, xb, w_g, w_u, w_d)


def _final_kernel(slot_ref, x_ref, route_ref, g_ref, yb_hbm, o_ref, ybuf, sem, *, tok0):
    i = pl.program_id(0)
    tm = x_ref.shape[0]

    def gather(j, buf):
        def body(r, c):
            a = (tok0 + j * tm + r) * TOP_K_INNER
            for k in range(TOP_K_INNER):
                pltpu.make_async_copy(_row_tile(yb_hbm, slot_ref[a + k]), _row_tile(ybuf, r, lead=(buf, k)),
                                      sem.at[buf]).start(priority=k % 2)
            return c
        lax.fori_loop(0, tm, body, 0, unroll=DMA_UNROLL)

    @pl.when(i == 0)
    def _first():
        gather(0, 0)

    @pl.when(i + 1 < pl.num_programs(0))
    def _next():
        gather(i + 1, (i + 1) % 2)

    buf = i % 2
    for k in range(TOP_K_INNER):
        pltpu.make_async_copy(_row_tile(yb_hbm, 0, tm), ybuf.at[buf, k], sem.at[buf]).wait()
    route = route_ref[...]
    y0 = _load_row_tiles(ybuf, tm, lead=(buf, 0))
    y1 = _load_row_tiles(ybuf, tm, lead=(buf, 1))
    x = x_ref[...] + (y0 * route[:, 2:3] + y1 * route[:, 3:4])
    ms = jnp.mean(x * x, axis=-1, keepdims=True)
    o_ref[...] = x * lax.rsqrt(ms + RMS_EPS) * g_ref[...]


def _final(slot_flat, xmid, route, g_final, yb, tok0, n_out, tm):
    blk0 = tok0 // tm
    grid_spec = pltpu.PrefetchScalarGridSpec(
        num_scalar_prefetch=1,
        grid=(n_out // tm,),
        in_specs=[pl.BlockSpec((tm, D_MODEL), lambda i, s: (i + blk0, 0)),
                  pl.BlockSpec((tm, LANES), lambda i, s: (i + blk0, 0)),
                  pl.BlockSpec((1, D_MODEL), lambda i, s: (0, 0)),
                  pl.BlockSpec(memory_space=pl.ANY)],
        out_specs=pl.BlockSpec((tm, D_MODEL), lambda i, s: (i, 0)),
        scratch_shapes=[pltpu.VMEM((2, TOP_K_INNER, tm * ROW_TILE, LANES), F32),
                        pltpu.SemaphoreType.DMA((2,))],
    )
    return pl.pallas_call(
        functools.partial(_final_kernel, tok0=tok0),
        grid_spec=grid_spec,
        out_shape=jax.ShapeDtypeStruct((n_out, D_MODEL), F32),
        compiler_params=_cparams(("arbitrary",)),
        name="final",
    )(slot_flat, xmid, route, g_final.reshape(1, D_MODEL), yb)


def kernel(x_prompt, x_sample, cache_attn_k, cache_attn_v, state_hgrn, w_in, w_out, hg_lb_logits,
           hg_norm_g, norm_mix_g, norm_ffn_g, norm_final_g, w_route_group, w_route_expert,
           w_expert_gate, w_expert_up, w_expert_down):
    bp, tp, _ = x_prompt.shape
    bs = x_sample.shape[0]
    l = 0
    w_in_bf = w_in[l].astype(BF16)
    w_out_bf = w_out[l].astype(BF16)
    w_router = jnp.concatenate(
        [w_route_group[l],
         jnp.transpose(w_route_expert[l], (1, 0, 2)).reshape(D_MODEL, N_EXPERTS),
         jnp.zeros((D_MODEL, LANES - N_GROUPS - N_EXPERTS), F32)], axis=-1)
    w_router_hi = w_router.astype(BF16)
    w_router = jnp.stack([w_router_hi, (w_router - w_router_hi.astype(F32)).astype(BF16)])

    n_p = bp * tp
    xp = x_prompt.reshape(n_p, D_MODEL)
    pos_p = jnp.arange(tp, dtype=jnp.int32)
    hq, hk, hv, lf, zg, aq, ak, av = _inproj(xp, norm_mix_g[l], w_in_bf, hg_lb_logits, pos_p, 512)
    seq3 = lambda a: a.reshape(bp, tp, HG_WIDTH)
    hn_p, s_fin = _hgrn_prompt(seq3(hq), seq3(hk), seq3(hv), seq3(lf), seq3(zg), hg_norm_g[l])
    att_p = _attn_prompt(seq3(aq), seq3(ak), seq3(av))
    keep = min(MAX_WINDOW, tp)
    heads = lambda a: a.reshape(1, bp, keep, ATT_HEADS, ATT_HEAD_DIM)
    new_k_p = heads(seq3(ak)[:, tp - keep:])
    new_v_p = heads(seq3(av)[:, tp - keep:])

    xs = x_sample.reshape(bs, D_MODEL)
    pos_s = jnp.full((bs,), PAST_LEN, jnp.int32)
    hq, hk, hv, lf, zg, aq, ak, av = _inproj(xs, norm_mix_g[l], w_in_bf, hg_lb_logits, pos_s, bs)
    hn_s, s_new = _hgrn_step(hq, hk, hv, lf, zg, hg_norm_g[l], state_hgrn[l])
    feat = lambda a: jnp.transpose(a, (0, 2, 3, 1))
    att_s, new_k_s, new_v_s = _attn_step(aq, ak, av, feat(cache_attn_k[l]), feat(cache_attn_v[l]))
    cache5 = lambda a: jnp.transpose(a, (0, 3, 1, 2))[None]

    assert n_p % ROUTE_TM == 0 and bs <= ROUTE_TM
    n_tok = n_p + bs
    xmid, h2, route, counts = _outproj(hn_p.reshape(n_p, HG_WIDTH), att_p.reshape(n_p, ATT_WIDTH), xp,
                                       hn_s, att_s, xs, w_out_bf, norm_ffn_g[l], w_router)
    blk = MOE_BLOCK
    nblk = (n_tok * TOP_K_INNER + N_EXPERTS * (blk - 1)) // blk + 1
    slot_flat, blk_expert, n_used = _rank(route, counts, n_tok, blk, nblk)
    xb = _dispatch(slot_flat, blk_expert, n_used, h2, n_tok, blk)
    yb = _experts(xb, blk_expert, n_used, w_expert_gate[l], w_expert_up[l], w_expert_down[l], blk)
    y_prompt = _final(slot_flat, xmid, route, norm_final_g, yb, 0, n_p, 512)
    y_sample = _final(slot_flat, xmid, route, norm_final_g, yb, n_p, bs, bs)

    return (y_prompt.reshape(bp, tp, D_MODEL), y_sample.reshape(bs, 1, D_MODEL),
            new_k_p, new_v_p, s_fin[None], cache5(new_k_s), cache5(new_v_s), s_new[None])
```
